```python
import math
import jax, jax.numpy as jnp
from jax import lax
import numpy as np

D_MODEL = 1024
BATCH = 1
SEQ = 16384
DEPTH = 1
DEC_BATCH = 8
DEC_SEQ = 32
PAST_LEN = 1024

CHUNK = 64
BAND_PREV = 8
BAND = (BAND_PREV + 1) * CHUNK
REL_CLIP = 128
H_A = 8
HD_A = 64
H_B = 4
HD_B = 64
ROT_DIM = HD_B // 4
ROPE_THETA = 500000.0
N_MEM = 256
H_M = 4
HD_M = D_MODEL // H_M
D_FF = 2816
CONV_W = 3
Q_BLOCK = 128
LN_EPS = 1e-5
DEEPNORM_ALPHA = (2.0 * DEPTH) ** 0.25
DEEPNORM_BETA = (8.0 * DEPTH) ** -0.25
A_WIDTH = H_A * HD_A
B_WIDTH = H_B * 2 * HD_B
MIX_WIDTH = A_WIDTH + B_WIDTH
QKV_DIM = 3 * A_WIDTH + 3 * B_WIDTH

kernel_name = "hybrid_chunkband_diffattn_stream_step"


def layer_norm(x, g, b):
    xf = x.astype(jnp.float32)
    mu = jnp.mean(xf, -1, keepdims=True)
    var = jnp.mean(jnp.square(xf - mu), -1, keepdims=True)
    return ((xf - mu) * lax.rsqrt(var + LN_EPS) * g.astype(jnp.float32) + b.astype(jnp.float32)).astype(x.dtype)


def partial_rope(x, pos):
    half = ROT_DIM // 2
    inv = ROPE_THETA ** (-jnp.arange(0, ROT_DIM, 2, dtype=jnp.float32) / ROT_DIM)
    ang = pos.astype(jnp.float32)[:, None] * inv[None, :]
    shp = (1, pos.shape[0]) + (1,) * (x.ndim - 3) + (half,)
    cos, sin = jnp.cos(ang).reshape(shp), jnp.sin(ang).reshape(shp)
    xr = x[..., :ROT_DIM].astype(jnp.float32)
    x1, x2 = xr[..., :half], xr[..., half:]
    rot = jnp.concatenate([x1 * cos - x2 * sin, x2 * cos + x1 * sin], -1)
    return jnp.concatenate([rot.astype(x.dtype), x[..., ROT_DIM:]], -1)


def project_groups(h, w_qkv, pos):
    B, S, _ = h.shape
    z = h @ w_qkv
    qa, ka, va, qb, kb, vb = jnp.split(z, [A_WIDTH, 2 * A_WIDTH, 3 * A_WIDTH, 3 * A_WIDTH + B_WIDTH, 3 * A_WIDTH + 2 * B_WIDTH], axis=-1)
    qa = qa.reshape(B, S, H_A, HD_A)
    ka = ka.reshape(B, S, H_A, HD_A)
    va = va.reshape(B, S, H_A, HD_A)
    qb = partial_rope(qb.reshape(B, S, H_B, 2, HD_B), pos)
    kb = partial_rope(kb.reshape(B, S, H_B, 2, HD_B), pos)
    vb = vb.reshape(B, S, H_B, 2 * HD_B)
    return qa, ka, va, qb, kb, vb


def rel_bias_lookup(rel_bias, rel):
    return rel_bias[:, jnp.clip(rel, -REL_CLIP, REL_CLIP) + REL_CLIP].astype(jnp.float32)


def chunk_band_attn_prompt(q, k, v, rel_bias):
    B, S, H, D = q.shape
    nc = S // CHUNK
    qc = q.reshape(B, nc, CHUNK, H, D)
    pad = ((0, 0), (BAND_PREV * CHUNK, 0), (0, 0), (0, 0))
    kp = jnp.pad(k, pad).reshape(B, nc + BAND_PREV, CHUNK, H, D)
    vp = jnp.pad(v, pad).reshape(B, nc + BAND_PREV, CHUNK, H, D)
    idx = jnp.arange(nc)[:, None] + jnp.arange(BAND_PREV + 1)[None, :]
    kb = kp[:, idx].reshape(B, nc, BAND, H, D)
    vb = vp[:, idx].reshape(B, nc, BAND, H, D)
    s = jnp.einsum('bcqhd,bckhd->bhcqk', qc, kb, preferred_element_type=jnp.float32) * (D ** -0.5)
    koff = jnp.arange(BAND) - BAND_PREV * CHUNK
    rel = jnp.arange(CHUNK)[:, None] - koff[None, :]
    s = s + rel_bias_lookup(rel_bias, rel)[None, :, None]
    valid = (jnp.arange(nc)[:, None] * CHUNK + koff[None, :]) >= 0
    s = jnp.where(valid[None, None, :, None, :], s, -jnp.inf)
    p = jax.nn.softmax(s, axis=-1)
    o = jnp.einsum('bhcqk,bckhd->bcqhd', p, vb.astype(jnp.float32))
    return o.reshape(B, S, H, D).astype(q.dtype)


def chunk_band_attn_sample(q, k_new, v_new, k_past, v_past, rel_bias):
    P, T, D = k_past.shape[1], q.shape[1], q.shape[-1]
    k = jnp.concatenate([k_past, k_new], 1)
    v = jnp.concatenate([v_past, v_new], 1)
    s = jnp.einsum('bqhd,bkhd->bhqk', q, k, preferred_element_type=jnp.float32) * (D ** -0.5)
    koff = jnp.concatenate([jnp.arange(P) - P, jnp.arange(T)])
    rel = jnp.arange(T)[:, None] - koff[None, :]
    s = s + rel_bias_lookup(rel_bias, rel)[None]
    p = jax.nn.softmax(s, axis=-1)
    return jnp.einsum('bhqk,bkhd->bqhd', p, v.astype(jnp.float32)).astype(q.dtype)


def diff_attn_block(q, k, v, q_chunk, k_chunk, lam, lam_init, subln_g):
    s = jnp.einsum('bqhmd,bkhmd->bmhqk', q, k, preferred_element_type=jnp.float32) * (HD_B ** -0.5)
    vis = k_chunk[None, :] <= q_chunk[:, None]
    s = jnp.where(vis, s, -jnp.inf)
    p = jax.nn.softmax(s, axis=-1)
    a = p[:, 0] - lam * p[:, 1]
    o = jnp.einsum('bhqk,bkhe->bqhe', a, v.astype(jnp.float32))
    o = o * lax.rsqrt(jnp.mean(o * o, -1, keepdims=True) + LN_EPS) * subln_g.astype(jnp.float32) * (1.0 - lam_init)
    return o.astype(v.dtype)


def diff_attn_prompt(q, k, v, lam, lam_init, subln_g):
    B, S = q.shape[:2]
    nb = S // Q_BLOCK
    chunk_id = jnp.arange(S) // CHUNK
    qs = jnp.moveaxis(q.reshape(B, nb, Q_BLOCK, H_B, 2, HD_B), 1, 0)
    qc = chunk_id.reshape(nb, Q_BLOCK)
    o = lax.map(lambda a: diff_attn_block(a[0], k, v, a[1], chunk_id, lam, lam_init, subln_g), (qs, qc))
    return jnp.moveaxis(o, 0, 1).reshape(B, S, H_B, 2 * HD_B)


def diff_attn_sample(q, k_new, v_new, k_past, v_past, lam, lam_init, subln_g):
    P, T = k_past.shape[1], q.shape[1]
    k = jnp.concatenate([k_past, k_new], 1)
    v = jnp.concatenate([v_past, v_new], 1)
    k_chunk = jnp.concatenate([jnp.arange(P), P + jnp.arange(T)]) // CHUNK
    q_chunk = (P + jnp.arange(T)) // CHUNK
    return diff_attn_block(q, k, v, q_chunk, k_chunk, lam, lam_init, subln_g)


def merge_groups(oa, ob, w_o):
    B, S = oa.shape[:2]
    return jnp.concatenate([oa.reshape(B, S, A_WIDTH), ob.reshape(B, S, B_WIDTH)], -1) @ w_o


def mem_kv(mem, w_mk, w_mv):
    B = mem.shape[0]
    return (mem @ w_mk).reshape(B, N_MEM, H_M, HD_M), (mem @ w_mv).reshape(B, N_MEM, H_M, HD_M)


def mem_attn(x, mk, mv, w_mq, w_mo):
    B, S, _ = x.shape
    q = (x @ w_mq).reshape(B, S, H_M, HD_M)
    s = jnp.einsum('bqhd,bkhd->bhqk', q, mk, preferred_element_type=jnp.float32) * (HD_M ** -0.5)
    p = jax.nn.softmax(s, axis=-1)
    o = jnp.einsum('bhqk,bkhd->bqhd', p, mv.astype(jnp.float32)).astype(x.dtype)
    return o.reshape(B, S, D_MODEL) @ w_mo


def conv_ffn(x, conv_past, w_up, conv_w, conv_b, w_down):
    S = x.shape[1]
    u = jnp.concatenate([conv_past, x @ w_up], 1)
    c = sum(u[:, j:j + S] * conv_w[j] for j in range(CONV_W)) + conv_b
    gate, val = jnp.split(c, 2, axis=-1)
    return (jax.nn.silu(gate) * val) @ w_down, u[:, -(CONV_W - 1):]


def layer_tail(x, mix, mk, mv, conv_past, ln1_g, ln1_b, w_mq, w_mo, ln2_g, ln2_b, w_up, conv_w, conv_b, w_down, ln3_g, ln3_b):
    x = layer_norm(DEEPNORM_ALPHA * x + mix, ln1_g, ln1_b)
    x = layer_norm(DEEPNORM_ALPHA * x + mem_attn(x, mk, mv, w_mq, w_mo), ln2_g, ln2_b)
    f, conv_new = conv_ffn(x, conv_past, w_up, conv_w, conv_b, w_down)
    x = layer_norm(DEEPNORM_ALPHA * x + f, ln3_g, ln3_b)
    return x, conv_new


def setup_inputs(seed: int = 0) -> dict:
    key = jax.random.key(seed)
    ks = jax.random.split(key, 40)
    a_cache = min(BAND_PREV * CHUNK, PAST_LEN)
    n = lambda i, shape, scale=1.0: jax.random.normal(ks[i], shape, jnp.float32) * scale
    L = DEPTH
    return {
        "x_prompt": n(0, (BATCH, SEQ, D_MODEL)),
        "x_sample": n(1, (DEC_BATCH, DEC_SEQ, D_MODEL)),
        "mem_prompt": n(2, (BATCH, N_MEM, D_MODEL)),
        "cache_a_k": n(3, (L, DEC_BATCH, a_cache, H_A, HD_A)),
        "cache_a_v": n(4, (L, DEC_BATCH, a_cache, H_A, HD_A)),
        "cache_b_k": n(5, (L, DEC_BATCH, PAST_LEN, H_B, 2, HD_B)),
        "cache_b_v": n(6, (L, DEC_BATCH, PAST_LEN, H_B, 2 * HD_B)),
        "cache_mem_k": n(7, (L, DEC_BATCH, N_MEM, H_M, HD_M)),
        "cache_mem_v": n(8, (L, DEC_BATCH, N_MEM, H_M, HD_M)),
        "state_conv": n(9, (L, DEC_BATCH, CONV_W - 1, 2 * D_FF)),
        "w_qkv": n(10, (L, D_MODEL, QKV_DIM), D_MODEL ** -0.5),
        "rel_bias": n(11, (L, H_A, 2 * REL_CLIP + 1), 0.2),
        "lambda_q1": n(12, (L, HD_B), 0.1),
        "lambda_k1": n(13, (L, HD_B), 0.1),
        "lambda_q2": n(14, (L, HD_B), 0.1),
        "lambda_k2": n(15, (L, HD_B), 0.1),
        "subln_g": 1.0 + n(16, (L, 2 * HD_B), 0.02),
        "w_o": n(17, (L, MIX_WIDTH, D_MODEL), MIX_WIDTH ** -0.5 * DEEPNORM_BETA),
        "ln1_g": 1.0 + n(18, (L, D_MODEL), 0.02),
        "ln1_b": n(19, (L, D_MODEL), 0.02),
        "w_mq": n(20, (L, D_MODEL, D_MODEL), D_MODEL ** -0.5),
        "w_mk": n(21, (L, D_MODEL, D_MODEL), D_MODEL ** -0.5),
        "w_mv": n(22, (L, D_MODEL, D_MODEL), D_MODEL ** -0.5),
        "w_mo": n(23, (L, D_MODEL, D_MODEL), D_MODEL ** -0.5 * DEEPNORM_BETA),
        "ln2_g": 1.0 + n(24, (L, D_MODEL), 0.02),
        "ln2_b": n(25, (L, D_MODEL), 0.02),
        "w_up": n(26, (L, D_MODEL, 2 * D_FF), D_MODEL ** -0.5),
        "conv_w": n(27, (L, CONV_W, 2 * D_FF), CONV_W ** -0.5),
        "conv_b": n(28, (L, 2 * D_FF), 0.02),
        "w_down": n(29, (L, D_FF, D_MODEL), D_FF ** -0.5 * DEEPNORM_BETA),
        "ln3_g": 1.0 + n(30, (L, D_MODEL), 0.02),
        "ln3_b": n(31, (L, D_MODEL), 0.02),
    }


def reference(x_prompt, x_sample, mem_prompt, cache_a_k, cache_a_v, cache_b_k, cache_b_v, cache_mem_k, cache_mem_v, state_conv,
              w_qkv, rel_bias, lambda_q1, lambda_k1, lambda_q2, lambda_k2, subln_g, w_o, ln1_g, ln1_b,
              w_mq, w_mk, w_mv, w_mo, ln2_g, ln2_b, w_up, conv_w, conv_b, w_down, ln3_g, ln3_b):
    xp, xs = x_prompt, x_sample
    B, S = xp.shape[:2]
    Bd, T = xs.shape[:2]
    P = cache_b_k.shape[2]
    pos_p = jnp.arange(S)
    pos_s = P + jnp.arange(T)
    keep = min(BAND_PREV * CHUNK, S)
    akp, avp, bkp, bvp, mkp, mvp, cvp = [], [], [], [], [], [], []
    aks, avs, bks, bvs, cvs = [], [], [], [], []
    for l in range(DEPTH):
        lam_init = 0.8 - 0.6 * math.exp(-0.3 * l)
        lam = (jnp.exp(jnp.sum(lambda_q1[l].astype(jnp.float32) * lambda_k1[l].astype(jnp.float32)))
               - jnp.exp(jnp.sum(lambda_q2[l].astype(jnp.float32) * lambda_k2[l].astype(jnp.float32))) + lam_init)
        tail = (ln1_g[l], ln1_b[l], w_mq[l], w_mo[l], ln2_g[l], ln2_b[l], w_up[l], conv_w[l], conv_b[l], w_down[l], ln3_g[l], ln3_b[l])
        qa, ka, va, qb, kb, vb = project_groups(xp, w_qkv[l], pos_p)
        oa = chunk_band_attn_prompt(qa, ka, va, rel_bias[l])
        ob = diff_attn_prompt(qb, kb, vb, lam, lam_init, subln_g[l])
        mk, mv = mem_kv(mem_prompt, w_mk[l], w_mv[l])
        conv0 = jnp.zeros((B, CONV_W - 1, 2 * D_FF), xp.dtype)
        xp, conv_p = layer_tail(xp, merge_groups(oa, ob, w_o[l]), mk, mv, conv0, *tail)
        akp.append(ka[:, S - keep:]); avp.append(va[:, S - keep:])
        bkp.append(kb); bvp.append(vb); mkp.append(mk); mvp.append(mv); cvp.append(conv_p)
        qa, ka, va, qb, kb, vb = project_groups(xs, w_qkv[l], pos_s)
        oa = chunk_band_attn_sample(qa, ka, va, cache_a_k[l], cache_a_v[l], rel_bias[l])
        ob = diff_attn_sample(qb, kb, vb, cache_b_k[l], cache_b_v[l], lam, lam_init, subln_g[l])
        xs, conv_s = layer_tail(xs, merge_groups(oa, ob, w_o[l]), cache_mem_k[l], cache_mem_v[l], state_conv[l], *tail)
        aks.append(ka); avs.append(va); bks.append(kb); bvs.append(vb); cvs.append(conv_s)
    st = lambda xs_: jnp.stack(xs_, 0)
    return (xp, xs, st(akp), st(avp), st(bkp), st(bvp), st(mkp), st(mvp), st(cvp), st(aks), st(avs), st(bks), st(bvs), st(cvs))
```

```python
import functools
import math

import numpy as np
import jax
import jax.numpy as jnp
from jax import lax
from jax.experimental import pallas as pl
from jax.experimental.pallas import tpu as pltpu

F32 = jnp.float32
BF16 = jnp.bfloat16

D_MODEL = 1024
CHUNK = 64
BAND_PREV = 8
REL_CLIP = 128
H_A = 8
HD_A = 64
H_B = 4
HD_B = 64
ROT_DIM = HD_B // 4
ROPE_THETA = 500000.0
N_MEM = 256
H_M = 4
HD_M = D_MODEL // H_M
D_FF = 2816
CONV_W = 3
LN_EPS = 1e-5
DEPTH = 1
DEEPNORM_ALPHA = (2.0 * DEPTH) ** 0.25
A_WIDTH = H_A * HD_A
B_WIDTH = H_B * 2 * HD_B
LAM_INIT = 0.8 - 0.6 * math.exp(-0.3 * 0)

LANES = 128
SUBLANES = 8
NEG = -1e30

QKV_TM = 512
A_TQ = 256
B_TQ = 256
B_TK = 512
TAIL_TM = 256
FF_BLOCK = 1408

_NT = (((1,), (1,)), ((), ()))


def _dot(a, b):
    return jnp.dot(a, b, preferred_element_type=F32)


def _dot_nt(a, b):
    return lax.dot_general(a, b, _NT, preferred_element_type=F32)


def _const_spec(shape):
    nd = len(shape)
    return pl.BlockSpec(shape, lambda *_: (0,) * nd, pipeline_mode=pl.Buffered(1))


def _arbitrary(n):
    return pltpu.CompilerParams(dimension_semantics=("arbitrary",) * n)


def _rope(z, cos, sin):
    lane = lax.broadcasted_iota(jnp.int32, (z.shape[0], LANES), 1)
    first_half = (lane % HD_B) < (ROT_DIM // 2)
    outs = []
    for c in range(z.shape[1] // LANES):
        zc = z[:, c * LANES:(c + 1) * LANES]
        partner = jnp.where(first_half,
                            pltpu.roll(zc, LANES - ROT_DIM // 2, axis=1),
                            pltpu.roll(zc, ROT_DIM // 2, axis=1))
        outs.append(zc * cos + partner * sin)
    return jnp.concatenate(outs, axis=1)


def _qkv_kernel(x_ref, w_ref, cos_ref, sin_ref,
                qa_ref, ka_ref, va_ref, qb_ref, kb_ref, vb_ref,
                kaf_ref, vaf_ref, kbf_ref, vbf_ref):
    xb = x_ref[...].astype(BF16)
    cos = cos_ref[...]
    sin = sin_ref[...]

    def section(i):
        return _dot(xb, w_ref[:, i * A_WIDTH:(i + 1) * A_WIDTH])

    qa = section(0)
    qa_ref[...] = (qa * (HD_A ** -0.5)).astype(BF16)
    ka = section(1)
    ka_ref[...] = ka.astype(BF16)
    kaf_ref[...] = ka
    va = section(2)
    va_ref[...] = va.astype(BF16)
    vaf_ref[...] = va
    qb = _rope(section(3), cos, sin)
    qb_ref[...] = (qb * (HD_B ** -0.5)).astype(BF16)
    kb = _rope(section(4), cos, sin)
    kb_ref[...] = kb.astype(BF16)
    kbf_ref[...] = kb
    vb = section(5)
    vb_ref[...] = vb.astype(BF16)
    vbf_ref[...] = vb


def _rope_tables(pos):
    half = ROT_DIM // 2
    inv = ROPE_THETA ** (-jnp.arange(0, ROT_DIM, 2, dtype=F32) / ROT_DIM)
    ang = pos.astype(F32)[:, None] * inv[None, :]
    cos, sin = jnp.cos(ang), jnp.sin(ang)
    n = pos.shape[0]
    rest = HD_B - ROT_DIM
    cos64 = jnp.concatenate([cos, cos, jnp.ones((n, rest), F32)], axis=1)
    sin64 = jnp.concatenate([-sin, sin, jnp.zeros((n, rest), F32)], axis=1)
    reps = LANES // HD_B
    return jnp.tile(cos64, (1, reps)), jnp.tile(sin64, (1, reps))


def _qkv(x, w_bf, pos, tm):
    rows = x.shape[0]
    nt = rows // tm
    cos, sin = _rope_tables(pos)
    row_spec = lambda w: pl.BlockSpec((tm, w), lambda i: (i, 0))
    last_spec = pl.BlockSpec((tm, A_WIDTH), lambda i: (0, 0))
    bf = jax.ShapeDtypeStruct((rows, A_WIDTH), BF16)
    return pl.pallas_call(
        _qkv_kernel,
        grid=(nt,),
        in_specs=[row_spec(D_MODEL), _const_spec(w_bf.shape), row_spec(LANES), row_spec(LANES)],
        out_specs=[row_spec(A_WIDTH)] * 6 + [last_spec, last_spec, row_spec(B_WIDTH), row_spec(B_WIDTH)],
        out_shape=[bf] * 6 + [jax.ShapeDtypeStruct((tm, A_WIDTH), F32)] * 2
                  + [jax.ShapeDtypeStruct((rows, B_WIDTH), F32)] * 2,
        compiler_params=_arbitrary(1),
        name="qkv_proj",
    )(x, w_bf, cos, sin)


def _band_bias(rel_bias):
    q = np.arange(A_TQ)[:, None]
    k = np.arange(3 * A_TQ)[None, :] - 2 * A_TQ
    idx = np.clip(q - k, -REL_CLIP, REL_CLIP) + REL_CLIP
    qc = q // CHUNK
    kc = np.floor_divide(k, CHUNK)
    band = (kc >= qc - BAND_PREV) & (kc <= qc)
    return jnp.where(band[None], rel_bias.astype(F32)[:, idx], NEG)


def _pair_queries(q2):
    lane = lax.broadcasted_iota(jnp.int32, q2.shape, 1)
    lo = lane < (LANES // 2)
    zero = jnp.zeros_like(q2)
    return jnp.concatenate([jnp.where(lo, q2, zero), jnp.where(lo, zero, q2)], axis=0), lo


def _attn_a_kernel(q_ref, k2_ref, k1_ref, k0_ref, v2_ref, v1_ref, v0_ref, bias_ref, o_ref):
    i = pl.program_id(0)
    kidx = lax.broadcasted_iota(jnp.int32, (1, 3 * A_TQ), 1)
    kvalid = (kidx + (i - 2) * A_TQ) >= 0
    for p in range(H_A // 2):
        sl = slice(p * LANES, (p + 1) * LANES)
        qz, lo = _pair_queries(q_ref[:, sl])
        kc = jnp.concatenate([k2_ref[:, sl], k1_ref[:, sl], k0_ref[:, sl]], axis=0)
        vc = jnp.concatenate([v2_ref[:, sl], v1_ref[:, sl], v0_ref[:, sl]], axis=0)
        s = _dot_nt(qz, kc)
        b = jnp.concatenate([bias_ref[2 * p], bias_ref[2 * p + 1]], axis=0)
        s = jnp.where(kvalid, s + b, NEG)
        m = jnp.max(s, axis=-1, keepdims=True)
        e = jnp.exp(s - m)
        l = jnp.sum(e, axis=-1, keepdims=True)
        o = _dot(e.astype(BF16), vc) / l
        o_ref[:, sl] = jnp.where(lo, o[:A_TQ], o[A_TQ:]).astype(BF16)


def _attn_a_prompt(qa, ka, va, bias):
    rows = qa.shape[0]
    cur = pl.BlockSpec((A_TQ, A_WIDTH), lambda i: (i, 0))
    prev1 = pl.BlockSpec((A_TQ, A_WIDTH), lambda i: (jnp.maximum(i - 1, 0), 0))
    prev2 = pl.BlockSpec((A_TQ, A_WIDTH), lambda i: (jnp.maximum(i - 2, 0), 0))
    return pl.pallas_call(
        _attn_a_kernel,
        grid=(rows // A_TQ,),
        in_specs=[cur, prev2, prev1, cur, prev2, prev1, cur, _const_spec(bias.shape)],
        out_specs=cur,
        out_shape=jax.ShapeDtypeStruct((rows, A_WIDTH), BF16),
        compiler_params=_arbitrary(1),
        name="attn_a_prompt",
    )(qa, ka, ka, ka, va, va, va, bias)


def _attn_a_sample_kernel(q_ref, kn_ref, vn_ref, kc_ref, vc_ref, bias_ref, o_ref):
    t = q_ref.shape[0]
    past = kc_ref.shape[1]
    kc = kc_ref[0].astype(BF16)
    vc = vc_ref[0].astype(BF16)
    for p in range(H_A // 2):
        sl = slice(p * LANES, (p + 1) * LANES)
        qz, lo = _pair_queries(q_ref[:, sl])
        b_past = jnp.concatenate([bias_ref[2 * p, 0:t, 0:past], bias_ref[2 * p + 1, 0:t, 0:past]], axis=0)
        b_new = jnp.concatenate([bias_ref[2 * p, 0:t, past:past + t],
                                 bias_ref[2 * p + 1, 0:t, past:past + t]], axis=0)
        s_past = _dot_nt(qz, kc[:, sl]) + b_past
        s_new = _dot_nt(qz, kn_ref[:, sl]) + b_new
        m = jnp.maximum(jnp.max(s_past, axis=-1, keepdims=True), jnp.max(s_new, axis=-1, keepdims=True))
        e_past = jnp.exp(s_past - m)
        e_new = jnp.exp(s_new - m)
        l = jnp.sum(e_past, axis=-1, keepdims=True) + jnp.sum(e_new, axis=-1, keepdims=True)
        o = (_dot(e_past.astype(BF16), vc[:, sl]) + _dot(e_new.astype(BF16), vn_ref[:, sl])) / l
        o_ref[:, sl] = jnp.where(lo, o[:t], o[t:]).astype(BF16)


def _attn_a_sample(qa, ka, va, cache_k, cache_v, bias, t):
    rows = qa.shape[0]
    streams, past, width = cache_k.shape
    assert past == 2 * A_TQ and t <= CHUNK and past + t <= 3 * A_TQ
    row = pl.BlockSpec((t, A_WIDTH), lambda s: (s, 0))
    cache = pl.BlockSpec((1, past, width), lambda s: (s, 0, 0))
    return pl.pallas_call(
        _attn_a_sample_kernel,
        grid=(streams,),
        in_specs=[row, row, row, cache, cache, _const_spec(bias.shape)],
        out_specs=row,
        out_shape=jax.ShapeDtypeStruct((rows, A_WIDTH), BF16),
        compiler_params=_arbitrary(1),
        name="attn_a_sample",
    )(qa, ka, va, cache_k, cache_v, bias)


def _lambda(lamp_ref):
    lp = lamp_ref[...]
    a = jnp.sum(lp[0:1] * lp[1:2], axis=-1, keepdims=True)
    b = jnp.sum(lp[2:3] * lp[3:4], axis=-1, keepdims=True)
    return jnp.exp(a) - jnp.exp(b) + LAM_INIT


def _diff_finish(o0, o1, lam, g):
    o = o0 - lam * o1
    o = o * lax.rsqrt(jnp.mean(o * o, axis=-1, keepdims=True) + LN_EPS) * g * (1.0 - LAM_INIT)
    return o.astype(BF16)


def _attn_b_kernel(q_ref, k_ref, v_ref, lamp_ref, g_ref, o_ref, m_ref, l_ref, acc_ref):
    qi = pl.program_id(1)
    qz, _ = _pair_queries(q_ref[...])
    m_ref[...] = jnp.full(m_ref.shape, NEG, F32)
    l_ref[...] = jnp.zeros(l_ref.shape, F32)
    acc_ref[...] = jnp.zeros(acc_ref.shape, F32)

    def step(j, masked):
        start = pl.multiple_of(j * B_TK, B_TK)
        s = _dot_nt(qz, k_ref[pl.ds(start, B_TK), :])
        if masked:
            row = lax.broadcasted_iota(jnp.int32, s.shape, 0) % B_TQ
            kpos = start + lax.broadcasted_iota(jnp.int32, s.shape, 1)
            limit = ((qi * B_TQ + row) // CHUNK + 1) * CHUNK
            s = jnp.where(kpos < limit, s, NEG)
        m_old = m_ref[...]
        m_new = jnp.maximum(m_old, jnp.max(s, axis=-1, keepdims=True))
        alpha = jnp.exp(m_old - m_new)
        p = jnp.exp(s - m_new)
        l_ref[...] = alpha * l_ref[...] + jnp.sum(p, axis=-1, keepdims=True)
        acc_ref[...] = alpha * acc_ref[...] + _dot(p.astype(BF16), v_ref[pl.ds(start, B_TK), :])
        m_ref[...] = m_new

    n_full = (qi * B_TQ) // B_TK

    def body(j, carry):
        step(j, False)
        return carry

    lax.fori_loop(0, n_full, body, 0)
    step(n_full, True)

    o = acc_ref[...] / l_ref[...]
    o_ref[...] = _diff_finish(o[:B_TQ], o[B_TQ:], _lambda(lamp_ref), g_ref[...])


def _attn_b_prompt(qb, kb, vb, lamp, g):
    rows = qb.shape[0]
    assert B_TK % B_TQ == 0 and rows % B_TK == 0
    qspec = pl.BlockSpec((B_TQ, LANES), lambda h, i: (i, h))
    kvspec = pl.BlockSpec((rows, LANES), lambda h, i: (0, h))
    return pl.pallas_call(
        _attn_b_kernel,
        grid=(H_B, rows // B_TQ),
        in_specs=[qspec, kvspec, kvspec, _const_spec(lamp.shape), _const_spec(g.shape)],
        out_specs=qspec,
        out_shape=jax.ShapeDtypeStruct((rows, B_WIDTH), BF16),
        scratch_shapes=[pltpu.VMEM((2 * B_TQ, 1), F32), pltpu.VMEM((2 * B_TQ, 1), F32),
                        pltpu.VMEM((2 * B_TQ, LANES), F32)],
        compiler_params=_arbitrary(2),
        name="attn_b_prompt",
    )(qb, kb, vb, lamp, g)


def _attn_b_sample_kernel(q_ref, kn_ref, vn_ref, kc_ref, vc_ref, lamp_ref, g_ref, o_ref):
    t = q_ref.shape[0]
    lam = _lambda(lamp_ref)
    g = g_ref[...]
    for h in range(H_B):
        sl = slice(h * LANES, (h + 1) * LANES)
        qz, _ = _pair_queries(q_ref[:, sl])
        s_past = _dot_nt(qz, kc_ref[0, :, sl].astype(BF16))
        s_new = _dot_nt(qz, kn_ref[:, sl])
        m = jnp.maximum(jnp.max(s_past, axis=-1, keepdims=True), jnp.max(s_new, axis=-1, keepdims=True))
        e_past = jnp.exp(s_past - m)
        e_new = jnp.exp(s_new - m)
        l = jnp.sum(e_past, axis=-1, keepdims=True) + jnp.sum(e_new, axis=-1, keepdims=True)
        o = (_dot(e_past.astype(BF16), vc_ref[0, :, sl].astype(BF16))
             + _dot(e_new.astype(BF16), vn_ref[:, sl])) / l
        o_ref[:, sl] = _diff_finish(o[:t], o[t:], lam, g)


def _attn_b_sample(qb, kb, vb, cache_k, cache_v, lamp, g, t):
    rows = qb.shape[0]
    streams, past, width = cache_k.shape
    assert past % CHUNK == 0 and t <= CHUNK
    row = pl.BlockSpec((t, B_WIDTH), lambda s: (s, 0))
    cache = pl.BlockSpec((1, past, width), lambda s: (s, 0, 0))
    return pl.pallas_call(
        _attn_b_sample_kernel,
        grid=(streams,),
        in_specs=[row, row, row, cache, cache, _const_spec(lamp.shape), _const_spec(g.shape)],
        out_specs=row,
        out_shape=jax.ShapeDtypeStruct((rows, B_WIDTH), BF16),
        compiler_params=_arbitrary(1),
        name="attn_b_sample",
    )(qb, kb, vb, cache_k, cache_v, lamp, g)


def _mem_kv_kernel(x_ref, wk_ref, wv_ref, kf_ref, vf_ref):
    xb = x_ref[...].astype(BF16)
    kf_ref[...] = _dot(xb, wk_ref[...])
    vf_ref[...] = _dot(xb, wv_ref[...])


def _mem_kv(mem, wk_bf, wv_bf):
    f = jax.ShapeDtypeStruct(mem.shape, F32)
    return pl.pallas_call(
        _mem_kv_kernel,
        out_shape=[f, f],
        name="mem_kv",
    )(mem, wk_bf, wv_bf)


def _layer_norm(x, g, b):
    mu = jnp.mean(x, axis=-1, keepdims=True)
    xc = x - mu
    var = jnp.mean(xc * xc, axis=-1, keepdims=True)
    return xc * lax.rsqrt(var + LN_EPS) * g + b


def _tail_kernel(x_ref, oa_ref, ob_ref, mk_ref, mv_ref, cin_ref,
                 wo_ref, ln1g_ref, ln1b_ref, wmq_ref, wmo_ref, ln2g_ref, ln2b_ref,
                 wup_ref, cw_ref, cb_ref, wdn_ref, ln3g_ref, ln3b_ref,
                 y_ref, carry_ref, *, seg):
    tm = x_ref.shape[0]
    n_seg = tm // seg

    @pl.when(pl.program_id(0) == 0)
    def _():
        carry_ref[...] = cin_ref[...]

    x = x_ref[...]
    mix = _dot(oa_ref[...], wo_ref[0:A_WIDTH, :]) + _dot(ob_ref[...], wo_ref[A_WIDTH:A_WIDTH + B_WIDTH, :])
    x1 = _layer_norm(DEEPNORM_ALPHA * x + mix, ln1g_ref[...], ln1b_ref[...])

    qm = (_dot(x1.astype(BF16), wmq_ref[...]) * (HD_M ** -0.5)).astype(BF16)

    def mem_attend(q, mk, mv):
        heads = []
        for h in range(H_M):
            sl = slice(h * HD_M, (h + 1) * HD_M)
            s = _dot_nt(q[:, sl], mk[:, sl])
            e = jnp.exp(s - jnp.max(s, axis=-1, keepdims=True))
            l = jnp.sum(e, axis=-1, keepdims=True)
            heads.append((_dot(e.astype(BF16), mv[:, sl]) / l).astype(BF16))
        return jnp.concatenate(heads, axis=1)

    if mk_ref.shape[0] == 1:
        om = mem_attend(qm, mk_ref[0].astype(BF16), mv_ref[0].astype(BF16))
    else:
        om = jnp.concatenate([mem_attend(qm[sg * seg:(sg + 1) * seg],
                                         mk_ref[sg].astype(BF16), mv_ref[sg].astype(BF16))
                              for sg in range(n_seg)], axis=0)
    mo = _dot(om, wmo_ref[...])
    x2 = _layer_norm(DEEPNORM_ALPHA * x1 + mo, ln2g_ref[...], ln2b_ref[...])
    x2b = x2.astype(BF16)

    row = lax.broadcasted_iota(jnp.int32, (seg, FF_BLOCK), 0)

    def conv_block(col):
        cols = slice(col, col + FF_BLOCK)
        u = _dot(x2b, wup_ref[:, cols])
        w0, w1, w2 = cw_ref[0:1, cols], cw_ref[1:2, cols], cw_ref[2:3, cols]
        outs = []
        for sg in range(n_seg):
            us = u[sg * seg:(sg + 1) * seg]
            p2 = carry_ref[SUBLANES * sg + 6:SUBLANES * sg + 7, cols]
            p1 = carry_ref[SUBLANES * sg + 7:SUBLANES * sg + 8, cols]
            u1 = jnp.where(row == 0, p1, pltpu.roll(us, 1, axis=0))
            u2 = jnp.where(row == 0, p2, jnp.where(row == 1, p1, pltpu.roll(us, 2, axis=0)))
            outs.append(u2 * w0 + u1 * w1 + us * w2 + cb_ref[0:1, cols])
            carry_ref[SUBLANES * sg:SUBLANES * (sg + 1), cols] = us[seg - SUBLANES:seg]
        return outs[0] if n_seg == 1 else jnp.concatenate(outs, axis=0)

    f = jnp.zeros((tm, D_MODEL), F32)
    for c0 in range(0, D_FF, FF_BLOCK):
        gate = conv_block(c0)
        val = conv_block(D_FF + c0)
        hid = gate * (1.0 / (1.0 + jnp.exp(-gate))) * val
        f = f + _dot(hid.astype(BF16), wdn_ref[c0:c0 + FF_BLOCK, :])
    y_ref[...] = _layer_norm(DEEPNORM_ALPHA * x2 + f, ln3g_ref[...], ln3b_ref[...])


def _tail(x, oa, ob, mk, mv, conv_in, params, tm, seg):
    rows = x.shape[0]
    n_seg = tm // seg
    assert conv_in.shape == (SUBLANES * n_seg, 2 * D_FF)
    assert n_seg == 1 or rows == tm
    assert mk.shape[0] in (1, n_seg)
    row_spec = lambda w: pl.BlockSpec((tm, w), lambda i: (i, 0))
    carry_spec = pl.BlockSpec(conv_in.shape, lambda i: (0, 0))
    return pl.pallas_call(
        functools.partial(_tail_kernel, seg=seg),
        grid=(rows // tm,),
        in_specs=[row_spec(D_MODEL), row_spec(A_WIDTH), row_spec(B_WIDTH),
                  _const_spec(mk.shape), _const_spec(mv.shape),
                  _const_spec(conv_in.shape)] + [_const_spec(p.shape) for p in params],
        out_specs=[row_spec(D_MODEL), carry_spec],
        out_shape=[jax.ShapeDtypeStruct((rows, D_MODEL), F32),
                   jax.ShapeDtypeStruct(conv_in.shape, F32)],
        compiler_params=_arbitrary(1),
        name="tail",
    )(x, oa, ob, mk, mv, conv_in, *params)


def kernel(x_prompt, x_sample, mem_prompt, cache_a_k, cache_a_v, cache_b_k, cache_b_v, cache_mem_k, cache_mem_v,
           state_conv, w_qkv, rel_bias, lambda_q1, lambda_k1, lambda_q2, lambda_k2, subln_g, w_o, ln1_g, ln1_b,
           w_mq, w_mk, w_mv, w_mo, ln2_g, ln2_b, w_up, conv_w, conv_b, w_down, ln3_g, ln3_b):
    assert w_qkv.shape[0] == DEPTH == 1
    batch, seq, _ = x_prompt.shape
    streams, t_new, _ = x_sample.shape
    past_b = cache_b_k.shape[2]
    assert batch == 1 and seq % QKV_TM == 0 and QKV_TM == BAND_PREV * CHUNK

    bf = lambda w: w[0].astype(BF16)
    row = lambda v: v[0].astype(F32)[None, :]
    w_qkv_bf = bf(w_qkv)
    tail_params = (bf(w_o), row(ln1_g), row(ln1_b), bf(w_mq), bf(w_mo), row(ln2_g), row(ln2_b),
                   bf(w_up), conv_w[0].astype(F32), row(conv_b), bf(w_down), row(ln3_g), row(ln3_b))
    lamp = jnp.stack([lambda_q1[0], lambda_k1[0], lambda_q2[0], lambda_k2[0]]).astype(F32)
    g = row(subln_g)
    bias = _band_bias(rel_bias[0])

    xp = x_prompt.reshape(seq, D_MODEL)
    qa, ka, va, qb, kb, vb, ka_f, va_f, kb_f, vb_f = _qkv(xp, w_qkv_bf, jnp.arange(seq), QKV_TM)
    oa = _attn_a_prompt(qa, ka, va, bias)
    ob = _attn_b_prompt(qb, kb, vb, lamp, g)
    mk_p, mv_p = _mem_kv(mem_prompt.reshape(N_MEM, D_MODEL), bf(w_mk), bf(w_mv))
    conv0 = jnp.zeros((SUBLANES, 2 * D_FF), F32)
    y_p, conv_p = _tail(xp, oa, ob, mk_p[None], mv_p[None], conv0, tail_params, TAIL_TM, TAIL_TM)

    rows_s = streams * t_new
    xs = x_sample.reshape(rows_s, D_MODEL)
    pos_s = jnp.tile(past_b + jnp.arange(t_new), streams)
    qa_s, ka_s, va_s, qb_s, kb_s, vb_s, ka_sf, va_sf, kb_sf, vb_sf = _qkv(xs, w_qkv_bf, pos_s, rows_s)
    oa_s = _attn_a_sample(qa_s, ka_s, va_s,
                          cache_a_k[0].reshape(streams, -1, A_WIDTH), cache_a_v[0].reshape(streams, -1, A_WIDTH),
                          bias, t_new)
    ob_s = _attn_b_sample(qb_s, kb_s, vb_s,
                          cache_b_k[0].reshape(streams, past_b, B_WIDTH),
                          cache_b_v[0].reshape(streams, past_b, B_WIDTH), lamp, g, t_new)
    conv_in_s = jnp.pad(state_conv[0].astype(F32), ((0, 0), (SUBLANES - (CONV_W - 1), 0), (0, 0)))
    y_s, conv_s = _tail(xs, oa_s, ob_s,
                        cache_mem_k[0].reshape(streams, N_MEM, D_MODEL).astype(BF16),
                        cache_mem_v[0].reshape(streams, N_MEM, D_MODEL).astype(BF16),
                        conv_in_s.reshape(streams * SUBLANES, 2 * D_FF), tail_params, rows_s, t_new)

    keep = CONV_W - 1
    return (
        y_p.reshape(batch, seq, D_MODEL),
        y_s.reshape(streams, t_new, D_MODEL),
        ka_f.reshape(1, batch, QKV_TM, H_A, HD_A),
        va_f.reshape(1, batch, QKV_TM, H_A, HD_A),
        kb_f.reshape(1, batch, seq, H_B, 2, HD_B),
        vb_f.reshape(1, batch, seq, H_B, 2 * HD_B),
        mk_p.reshape(1, batch, N_MEM, H_M, HD_M),
        mv_p.reshape(1, batch, N_MEM, H_M, HD_M),
        conv_p[SUBLANES - keep:].reshape(1, batch, keep, 2 * D_FF),
        ka_sf.reshape(1, streams, t_new, H_A, HD_A),
        va_sf.reshape(1, streams, t_new, H_A, HD_A),
        kb_sf.reshape(1, streams, t_new, H_B, 2, HD_B),
        vb_sf.reshape(1, streams, t_new, H_B, 2 * HD_B),
        conv_s.reshape(streams, SUBLANES, 2 * D_FF)[:, SUBLANES - keep:].reshape(1, streams, keep, 2 * D_FF),
    )
```

```python
import functools
import math

import numpy as np
import jax
import jax.numpy as jnp
from jax import lax
from jax.experimental import pallas as pl
from jax.experimental.pallas import tpu as pltpu

F32 = jnp.float32
BF16 = jnp.bfloat16

D_MODEL = 1024
CHUNK = 64
BAND_PREV = 8
REL_CLIP = 128
H_A = 8
HD_A = 64
H_B = 4
HD_B = 64
ROT_DIM = HD_B // 4
ROPE_THETA = 500000.0
N_MEM = 256
H_M = 4
HD_M = D_MODEL // H_M
D_FF = 2816
CONV_W = 3
LN_EPS = 1e-5
DEPTH = 1
DEEPNORM_ALPHA = (2.0 * DEPTH) ** 0.25
A_WIDTH = H_A * HD_A
B_WIDTH = H_B * 2 * HD_B
LAM_INIT = 0.8 - 0.6 * math.exp(-0.3 * 0)

LANES = 128
SUBLANES = 8
NEG = -1e30
LOG2_E = math.log2(math.e)

QKV_TM = 512
A_TQ = 256
B_TQ = 512
B_QW = 256
B_HEADS = 2
B_LAG = 4
B_TK = 512
V_ROWS = LANES + 16
BIAS_SPAN = 1024
TAIL_TM = 256
FF_BLOCK = 1408

_NT = (((1,), (1,)), ((), ()))


def _dot(a, b):
    return jnp.dot(a, b, preferred_element_type=F32)


def _dot_nt(a, b):
    return lax.dot_general(a, b, _NT, preferred_element_type=F32)


def _const_spec(shape):
    nd = len(shape)
    return pl.BlockSpec(shape, lambda *_: (0,) * nd, pipeline_mode=pl.Buffered(1))


def _arbitrary(n):
    return pltpu.CompilerParams(dimension_semantics=("arbitrary",) * n)


def _rope(z, cos, sin):
    lane = lax.broadcasted_iota(jnp.int32, (z.shape[0], LANES), 1)
    first_half = (lane % HD_B) < (ROT_DIM // 2)
    outs = []
    for c in range(z.shape[1] // LANES):
        zc = z[:, c * LANES:(c + 1) * LANES]
        partner = jnp.where(first_half,
                            pltpu.roll(zc, LANES - ROT_DIM // 2, axis=1),
                            pltpu.roll(zc, ROT_DIM // 2, axis=1))
        outs.append(zc * cos + partner * sin)
    return jnp.concatenate(outs, axis=1)


def _qkv_kernel(x_ref, w_ref, cos_ref, sin_ref,
                qa_ref, ka_ref, va_ref, qb_ref, kb_ref, vb_ref,
                kaf_ref, vaf_ref, kbf_ref, vbf_ref, *, transposed_b):
    xb = x_ref[...].astype(BF16)
    cos = cos_ref[...]
    sin = sin_ref[...]
    tm = xb.shape[0]

    def section(i):
        return _dot(xb, w_ref[:, i * A_WIDTH:(i + 1) * A_WIDTH])

    qa = section(0)
    qa_ref[...] = (qa * (HD_A ** -0.5)).astype(BF16)
    ka = section(1)
    ka_ref[...] = ka.astype(BF16)
    kaf_ref[...] = ka
    va = section(2)
    va_ref[...] = va.astype(BF16)
    vaf_ref[...] = va
    qb = _rope(section(3), cos, sin) * (HD_B ** -0.5 * (LOG2_E if transposed_b else 1.0))
    kb = _rope(section(4), cos, sin)
    kb_ref[...] = kb.astype(BF16)
    kbf_ref[...] = kb
    vb = section(5)
    vbf_ref[...] = vb
    if transposed_b:
        ones = (lax.broadcasted_iota(jnp.int32, (V_ROWS - LANES, tm), 0) == 0).astype(BF16)
        for h in range(H_B):
            sl = slice(h * LANES, (h + 1) * LANES)
            qb_ref[h] = qb[:, sl].T.astype(BF16)
            vb_ref[h, 0, 0:LANES, :] = vb[:, sl].T.astype(BF16)
            vb_ref[h, 0, LANES:V_ROWS, :] = ones
    else:
        qb_ref[...] = qb.astype(BF16)
        vb_ref[...] = vb.astype(BF16)


def _rope_tables(pos):
    half = ROT_DIM // 2
    inv = ROPE_THETA ** (-jnp.arange(0, ROT_DIM, 2, dtype=F32) / ROT_DIM)
    ang = pos.astype(F32)[:, None] * inv[None, :]
    cos, sin = jnp.cos(ang), jnp.sin(ang)
    n = pos.shape[0]
    rest = HD_B - ROT_DIM
    cos64 = jnp.concatenate([cos, cos, jnp.ones((n, rest), F32)], axis=1)
    sin64 = jnp.concatenate([-sin, sin, jnp.zeros((n, rest), F32)], axis=1)
    reps = LANES // HD_B
    return jnp.tile(cos64, (1, reps)), jnp.tile(sin64, (1, reps))


def _qkv(x, w_bf, pos, tm, transposed_b):
    rows = x.shape[0]
    nt = rows // tm
    cos, sin = _rope_tables(pos)
    row_spec = lambda w: pl.BlockSpec((tm, w), lambda i: (i, 0))
    last_spec = pl.BlockSpec((tm, A_WIDTH), lambda i: (0, 0))
    bf = jax.ShapeDtypeStruct((rows, A_WIDTH), BF16)
    if transposed_b:
        assert tm == B_TK
        qb_spec = pl.BlockSpec((H_B, LANES, tm), lambda i: (0, 0, i))
        qb_shape = jax.ShapeDtypeStruct((H_B, LANES, rows), BF16)
        vb_spec = pl.BlockSpec((H_B, 1, V_ROWS, tm), lambda i: (0, i, 0, 0))
        vb_shape = jax.ShapeDtypeStruct((H_B, nt, V_ROWS, tm), BF16)
    else:
        qb_spec = vb_spec = row_spec(B_WIDTH)
        qb_shape = vb_shape = bf
    return pl.pallas_call(
        functools.partial(_qkv_kernel, transposed_b=transposed_b),
        grid=(nt,),
        in_specs=[row_spec(D_MODEL), _const_spec(w_bf.shape), row_spec(LANES), row_spec(LANES)],
        out_specs=[row_spec(A_WIDTH)] * 3 + [qb_spec, row_spec(B_WIDTH), vb_spec]
                  + [last_spec, last_spec, row_spec(B_WIDTH), row_spec(B_WIDTH)],
        out_shape=[bf] * 3 + [qb_shape, bf, vb_shape] + [jax.ShapeDtypeStruct((tm, A_WIDTH), F32)] * 2
                  + [jax.ShapeDtypeStruct((rows, B_WIDTH), F32)] * 2,
        compiler_params=_arbitrary(1),
        name="qkv_proj",
    )(x, w_bf, cos, sin)


def _bias_row(rel_bias):
    table = rel_bias.astype(F32)
    far = table[:, 2 * REL_CLIP:]
    near = table[:, :1]
    n_far = 2 * A_TQ - REL_CLIP
    n_near = 3 * A_TQ - (n_far + 2 * REL_CLIP + 1)
    h = table.shape[0]
    return jnp.concatenate([jnp.broadcast_to(far, (h, n_far)), table[:, ::-1],
                            jnp.broadcast_to(near, (h, n_near)),
                            jnp.broadcast_to(far, (h, BIAS_SPAN - 3 * A_TQ))], axis=1)


def _build_band_bias(row_ref, bias_ref):
    _, rows, cols = bias_ref.shape
    q = lax.broadcasted_iota(jnp.int32, (rows, cols), 0)
    k = lax.broadcasted_iota(jnp.int32, (rows, cols), 1)
    qc = q // CHUNK
    kc = k // CHUNK - (2 * A_TQ) // CHUNK
    band = (kc >= qc - BAND_PREV) & (kc <= qc)
    for h in range(H_A):
        r = jnp.broadcast_to(row_ref[h:h + 1, :], (rows, BIAS_SPAN))
        toeplitz = pltpu.roll(r, 0, axis=1, stride=1, stride_axis=0)
        bias_ref[h] = jnp.where(band, toeplitz[:, :cols], NEG)


def _pair_queries(q2):
    lane = lax.broadcasted_iota(jnp.int32, q2.shape, 1)
    lo = lane < (LANES // 2)
    zero = jnp.zeros_like(q2)
    return jnp.concatenate([jnp.where(lo, q2, zero), jnp.where(lo, zero, q2)], axis=0), lo


def _attn_a_kernel(q_ref, k2_ref, k1_ref, k0_ref, v2_ref, v1_ref, v0_ref, row_ref, o_ref, bias_ref):
    i = pl.program_id(0)

    @pl.when(i == 0)
    def _():
        _build_band_bias(row_ref, bias_ref)

    kidx = lax.broadcasted_iota(jnp.int32, (1, 3 * A_TQ), 1)
    kvalid = (kidx + (i - 2) * A_TQ) >= 0
    for p in range(H_A // 2):
        sl = slice(p * LANES, (p + 1) * LANES)
        qz, lo = _pair_queries(q_ref[:, sl])
        kc = jnp.concatenate([k2_ref[:, sl], k1_ref[:, sl], k0_ref[:, sl]], axis=0)
        vc = jnp.concatenate([v2_ref[:, sl], v1_ref[:, sl], v0_ref[:, sl]], axis=0)
        s = _dot_nt(qz, kc)
        b = jnp.concatenate([bias_ref[2 * p], bias_ref[2 * p + 1]], axis=0)
        s = jnp.where(kvalid, s + b, NEG)
        m = jnp.max(s, axis=-1, keepdims=True)
        e = jnp.exp(s - m)
        l = jnp.sum(e, axis=-1, keepdims=True)
        o = _dot(e.astype(BF16), vc) / l
        o_ref[:, sl] = jnp.where(lo, o[:A_TQ], o[A_TQ:]).astype(BF16)


def _attn_a_prompt(qa, ka, va, bias_row):
    rows = qa.shape[0]
    cur = pl.BlockSpec((A_TQ, A_WIDTH), lambda i: (i, 0))
    prev1 = pl.BlockSpec((A_TQ, A_WIDTH), lambda i: (jnp.maximum(i - 1, 0), 0))
    prev2 = pl.BlockSpec((A_TQ, A_WIDTH), lambda i: (jnp.maximum(i - 2, 0), 0))
    return pl.pallas_call(
        _attn_a_kernel,
        grid=(rows // A_TQ,),
        in_specs=[cur, prev2, prev1, cur, prev2, prev1, cur, _const_spec(bias_row.shape)],
        out_specs=cur,
        out_shape=jax.ShapeDtypeStruct((rows, A_WIDTH), BF16),
        scratch_shapes=[pltpu.VMEM((H_A, A_TQ, 3 * A_TQ), F32)],
        compiler_params=_arbitrary(1),
        name="attn_a_prompt",
    )(qa, ka, ka, ka, va, va, va, bias_row)


def _attn_a_sample_kernel(q_ref, kn_ref, vn_ref, kc_ref, vc_ref, row_ref, o_ref, bias_ref):
    t = q_ref.shape[0]
    past = kc_ref.shape[1]

    @pl.when(pl.program_id(0) == 0)
    def _():
        _build_band_bias(row_ref, bias_ref)

    kc = kc_ref[0].astype(BF16)
    vc = vc_ref[0].astype(BF16)
    for p in range(H_A // 2):
        sl = slice(p * LANES, (p + 1) * LANES)
        qz, lo = _pair_queries(q_ref[:, sl])
        b_past = jnp.concatenate([bias_ref[2 * p, 0:t, 0:past], bias_ref[2 * p + 1, 0:t, 0:past]], axis=0)
        b_new = jnp.concatenate([bias_ref[2 * p, 0:t, past:past + t],
                                 bias_ref[2 * p + 1, 0:t, past:past + t]], axis=0)
        s_past = _dot_nt(qz, kc[:, sl]) + b_past
        s_new = _dot_nt(qz, kn_ref[:, sl]) + b_new
        m = jnp.maximum(jnp.max(s_past, axis=-1, keepdims=True), jnp.max(s_new, axis=-1, keepdims=True))
        e_past = jnp.exp(s_past - m)
        e_new = jnp.exp(s_new - m)
        l = jnp.sum(e_past, axis=-1, keepdims=True) + jnp.sum(e_new, axis=-1, keepdims=True)
        o = (_dot(e_past.astype(BF16), vc[:, sl]) + _dot(e_new.astype(BF16), vn_ref[:, sl])) / l
        o_ref[:, sl] = jnp.where(lo, o[:t], o[t:]).astype(BF16)


def _attn_a_sample(qa, ka, va, cache_k, cache_v, bias_row, t):
    rows = qa.shape[0]
    streams, past, width = cache_k.shape
    assert past == 2 * A_TQ and t <= CHUNK and t % SUBLANES == 0
    row = pl.BlockSpec((t, A_WIDTH), lambda s: (s, 0))
    cache = pl.BlockSpec((1, past, width), lambda s: (s, 0, 0))
    return pl.pallas_call(
        _attn_a_sample_kernel,
        grid=(streams,),
        in_specs=[row, row, row, cache, cache, _const_spec(bias_row.shape)],
        out_specs=row,
        out_shape=jax.ShapeDtypeStruct((rows, A_WIDTH), BF16),
        scratch_shapes=[pltpu.VMEM((H_A, t, past + LANES), F32)],
        compiler_params=_arbitrary(1),
        name="attn_a_sample",
    )(qa, ka, va, cache_k, cache_v, bias_row)


def _lambda(lamp_ref):
    lp = lamp_ref[...]
    a = jnp.sum(lp[0:1] * lp[1:2], axis=-1, keepdims=True)
    b = jnp.sum(lp[2:3] * lp[3:4], axis=-1, keepdims=True)
    return jnp.exp(a) - jnp.exp(b) + LAM_INIT


def _diff_finish(o0, o1, lam, g):
    o = o0 - lam * o1
    o = o * lax.rsqrt(jnp.mean(o * o, axis=-1, keepdims=True) + LN_EPS) * g * (1.0 - LAM_INIT)
    return o.astype(BF16)


def _attn_b_kernel(qt_ref, k_ref, vt_ref, lamp_ref, gcol_ref, o_ref, acc_ref):
    qi = pl.program_id(1)
    dim = lax.broadcasted_iota(jnp.int32, (LANES, B_QW), 0)
    zero = jnp.zeros((LANES, B_QW), BF16)
    chains = [(h, part, mp) for h in range(B_HEADS) for part in range(B_TQ // B_QW) for mp in range(2)]
    qz = []
    for h, part, mp in chains:
        qt = qt_ref[h, :, part * B_QW:(part + 1) * B_QW]
        qz.append(jnp.where(dim < HD_B if mp == 0 else dim >= HD_B, qt, zero))
    acc_ref[...] = jnp.zeros(acc_ref.shape, F32)

    def step(j, ms, masked):
        start = pl.multiple_of(j * B_TK, B_TK)
        kpos = start + lax.broadcasted_iota(jnp.int32, (B_TK, B_QW), 0)
        lane = lax.broadcasted_iota(jnp.int32, (B_TK, B_QW), 1)
        n = len(chains)
        s, p, alpha, out = [None] * n, [None] * n, [None] * n, [None] * n

        def scores(c):
            h, part, _ = chains[c]
            s[c] = _dot(k_ref[pl.ds(start, B_TK), h * LANES:(h + 1) * LANES], qz[c])
            if masked:
                qpos = qi * B_TQ + part * B_QW + lane
                s[c] = jnp.where(kpos < (qpos // CHUNK + 1) * CHUNK, s[c], NEG)

        def softmax(c):
            out[c] = jnp.maximum(ms[c], jnp.max(s[c], axis=0, keepdims=True))
            alpha[c] = jnp.exp2(ms[c] - out[c])
            p[c] = jnp.exp2(s[c] - out[c]).astype(BF16)

        def values(c):
            acc_ref[c] = alpha[c] * acc_ref[c] + _dot(vt_ref[chains[c][0], j], p[c])

        for c in range(n + B_LAG):
            if c < n:
                scores(c)
            if 0 <= c - 1 < n:
                softmax(c - 1)
            if c >= B_LAG:
                values(c - B_LAG)
        return tuple(out)

    n_full = (qi * B_TQ) // B_TK
    m_init = tuple(jnp.full((1, B_QW), NEG, F32) for _ in chains)
    ms = lax.fori_loop(0, n_full, lambda j, ms: step(j, ms, False), m_init)
    step(n_full, ms, True)

    lam = _lambda(lamp_ref)
    for c in range(0, len(chains), 2):
        h, part, _ = chains[c]
        a0 = acc_ref[c]
        a1 = acc_ref[c + 1]
        o = a0[0:LANES] / a0[LANES:LANES + 1] - lam * (a1[0:LANES] / a1[LANES:LANES + 1])
        o = o * lax.rsqrt(jnp.mean(o * o, axis=0, keepdims=True) + LN_EPS) * gcol_ref[...] * (1.0 - LAM_INIT)
        o_ref[part * B_QW:(part + 1) * B_QW, h * LANES:(h + 1) * LANES] = o.T.astype(BF16)


def _attn_b_prompt(qbt, kb, vbt, lamp, gcol):
    rows = kb.shape[0]
    assert B_TK == B_TQ and rows % B_TK == 0 and B_QW % CHUNK == 0 and H_B % B_HEADS == 0
    n_chains = B_HEADS * (B_TQ // B_QW) * 2
    single = pl.Buffered(1)
    return pl.pallas_call(
        _attn_b_kernel,
        grid=(H_B // B_HEADS, rows // B_TQ),
        in_specs=[pl.BlockSpec((B_HEADS, LANES, B_TQ), lambda h, i: (h, 0, i)),
                  pl.BlockSpec((rows, B_HEADS * LANES), lambda h, i: (0, h), pipeline_mode=single),
                  pl.BlockSpec((B_HEADS,) + vbt.shape[1:], lambda h, i: (h, 0, 0, 0), pipeline_mode=single),
                  _const_spec(lamp.shape), _const_spec(gcol.shape)],
        out_specs=pl.BlockSpec((B_TQ, B_HEADS * LANES), lambda h, i: (i, h)),
        out_shape=jax.ShapeDtypeStruct((rows, B_WIDTH), BF16),
        scratch_shapes=[pltpu.VMEM((n_chains, V_ROWS, B_QW), F32)],
        compiler_params=_arbitrary(2),
        name="attn_b_prompt",
    )(qbt, kb, vbt, lamp, gcol)


def _attn_b_sample_kernel(q_ref, kn_ref, vn_ref, kc_ref, vc_ref, lamp_ref, g_ref, o_ref):
    t = q_ref.shape[0]
    lam = _lambda(lamp_ref)
    g = g_ref[...]
    for h in range(H_B):
        sl = slice(h * LANES, (h + 1) * LANES)
        qz, _ = _pair_queries(q_ref[:, sl])
        s_past = _dot_nt(qz, kc_ref[0, :, sl].astype(BF16))
        s_new = _dot_nt(qz, kn_ref[:, sl])
        m = jnp.maximum(jnp.max(s_past, axis=-1, keepdims=True), jnp.max(s_new, axis=-1, keepdims=True))
        e_past = jnp.exp(s_past - m)
        e_new = jnp.exp(s_new - m)
        l = jnp.sum(e_past, axis=-1, keepdims=True) + jnp.sum(e_new, axis=-1, keepdims=True)
        o = (_dot(e_past.astype(BF16), vc_ref[0, :, sl].astype(BF16))
             + _dot(e_new.astype(BF16), vn_ref[:, sl])) / l
        o_ref[:, sl] = _diff_finish(o[:t], o[t:], lam, g)


def _attn_b_sample(qb, kb, vb, cache_k, cache_v, lamp, g, t):
    rows = qb.shape[0]
    streams, past, width = cache_k.shape
    assert past % CHUNK == 0 and t <= CHUNK
    row = pl.BlockSpec((t, B_WIDTH), lambda s: (s, 0))
    cache = pl.BlockSpec((1, past, width), lambda s: (s, 0, 0))
    return pl.pallas_call(
        _attn_b_sample_kernel,
        grid=(streams,),
        in_specs=[row, row, row, cache, cache, _const_spec(lamp.shape), _const_spec(g.shape)],
        out_specs=row,
        out_shape=jax.ShapeDtypeStruct((rows, B_WIDTH), BF16),
        compiler_params=_arbitrary(1),
        name="attn_b_sample",
    )(qb, kb, vb, cache_k, cache_v, lamp, g)


def _mem_kv_kernel(x_ref, wk_ref, wv_ref, kf_ref, vf_ref):
    xb = x_ref[...].astype(BF16)
    kf_ref[...] = _dot(xb, wk_ref[...])
    vf_ref[...] = _dot(xb, wv_ref[...])


def _mem_kv(mem, wk_bf, wv_bf):
    f = jax.ShapeDtypeStruct(mem.shape, F32)
    return pl.pallas_call(
        _mem_kv_kernel,
        out_shape=[f, f],
        name="mem_kv",
    )(mem, wk_bf, wv_bf)


def _layer_norm(x, g, b):
    mu = jnp.mean(x, axis=-1, keepdims=True)
    xc = x - mu
    var = jnp.mean(xc * xc, axis=-1, keepdims=True)
    return xc * lax.rsqrt(var + LN_EPS) * g + b


def _tail_kernel(x_ref, oa_ref, ob_ref, mk_ref, mv_ref, cin_ref,
                 wo_ref, ln1g_ref, ln1b_ref, wmq_ref, wmo_ref, ln2g_ref, ln2b_ref,
                 wup_ref, cw_ref, cb_ref, wdn_ref, ln3g_ref, ln3b_ref,
                 y_ref, carry_ref, *, seg):
    tm = x_ref.shape[0]
    n_seg = tm // seg

    @pl.when(pl.program_id(0) == 0)
    def _():
        carry_ref[...] = cin_ref[...]

    x = x_ref[...]
    mix = _dot(oa_ref[...], wo_ref[0:A_WIDTH, :]) + _dot(ob_ref[...], wo_ref[A_WIDTH:A_WIDTH + B_WIDTH, :])
    x1 = _layer_norm(DEEPNORM_ALPHA * x + mix, ln1g_ref[...], ln1b_ref[...])

    qm = (_dot(x1.astype(BF16), wmq_ref[...]) * (HD_M ** -0.5)).astype(BF16)

    def mem_attend(q, mk, mv):
        heads = []
        for h in range(H_M):
            sl = slice(h * HD_M, (h + 1) * HD_M)
            s = _dot_nt(q[:, sl], mk[:, sl])
            e = jnp.exp(s - jnp.max(s, axis=-1, keepdims=True))
            l = jnp.sum(e, axis=-1, keepdims=True)
            heads.append((_dot(e.astype(BF16), mv[:, sl]) / l).astype(BF16))
        return jnp.concatenate(heads, axis=1)

    if mk_ref.shape[0] == 1:
        om = mem_attend(qm, mk_ref[0].astype(BF16), mv_ref[0].astype(BF16))
    else:
        om = jnp.concatenate([mem_attend(qm[sg * seg:(sg + 1) * seg],
                                         mk_ref[sg].astype(BF16), mv_ref[sg].astype(BF16))
                              for sg in range(n_seg)], axis=0)
    mo = _dot(om, wmo_ref[...])
    x2 = _layer_norm(DEEPNORM_ALPHA * x1 + mo, ln2g_ref[...], ln2b_ref[...])
    x2b = x2.astype(BF16)

    row = lax.broadcasted_iota(jnp.int32, (seg, FF_BLOCK), 0)

    def conv_block(col):
        cols = slice(col, col + FF_BLOCK)
        u = _dot(x2b, wup_ref[:, cols])
        w0, w1, w2 = cw_ref[0:1, cols], cw_ref[1:2, cols], cw_ref[2:3, cols]
        outs = []
        for sg in range(n_seg):
            us = u[sg * seg:(sg + 1) * seg]
            p2 = carry_ref[SUBLANES * sg + 6:SUBLANES * sg + 7, cols]
            p1 = carry_ref[SUBLANES * sg + 7:SUBLANES * sg + 8, cols]
            u1 = jnp.where(row == 0, p1, pltpu.roll(us, 1, axis=0))
            u2 = jnp.where(row == 0, p2, jnp.where(row == 1, p1, pltpu.roll(us, 2, axis=0)))
            outs.append(u2 * w0 + u1 * w1 + us * w2 + cb_ref[0:1, cols])
            carry_ref[SUBLANES * sg:SUBLANES * (sg + 1), cols] = us[seg - SUBLANES:seg]
        return outs[0] if n_seg == 1 else jnp.concatenate(outs, axis=0)

    f = jnp.zeros((tm, D_MODEL), F32)
    for c0 in range(0, D_FF, FF_BLOCK):
        gate = conv_block(c0)
        val = conv_block(D_FF + c0)
        hid = gate * (1.0 / (1.0 + jnp.exp(-gate))) * val
        f = f + _dot(hid.astype(BF16), wdn_ref[c0:c0 + FF_BLOCK, :])
    y_ref[...] = _layer_norm(DEEPNORM_ALPHA * x2 + f, ln3g_ref[...], ln3b_ref[...])


def _tail(x, oa, ob, mk, mv, conv_in, params, tm, seg):
    rows = x.shape[0]
    n_seg = tm // seg
    assert conv_in.shape == (SUBLANES * n_seg, 2 * D_FF)
    assert n_seg == 1 or rows == tm
    assert mk.shape[0] in (1, n_seg)
    row_spec = lambda w: pl.BlockSpec((tm, w), lambda i: (i, 0))
    carry_spec = pl.BlockSpec(conv_in.shape, lambda i: (0, 0))
    return pl.pallas_call(
        functools.partial(_tail_kernel, seg=seg),
        grid=(rows // tm,),
        in_specs=[row_spec(D_MODEL), row_spec(A_WIDTH), row_spec(B_WIDTH),
                  _const_spec(mk.shape), _const_spec(mv.shape),
                  _const_spec(conv_in.shape)] + [_const_spec(p.shape) for p in params],
        out_specs=[row_spec(D_MODEL), carry_spec],
        out_shape=[jax.ShapeDtypeStruct((rows, D_MODEL), F32),
                   jax.ShapeDtypeStruct(conv_in.shape, F32)],
        compiler_params=_arbitrary(1),
        name="tail",
    )(x, oa, ob, mk, mv, conv_in, *params)


def kernel(x_prompt, x_sample, mem_prompt, cache_a_k, cache_a_v, cache_b_k, cache_b_v, cache_mem_k, cache_mem_v,
           state_conv, w_qkv, rel_bias, lambda_q1, lambda_k1, lambda_q2, lambda_k2, subln_g, w_o, ln1_g, ln1_b,
           w_mq, w_mk, w_mv, w_mo, ln2_g, ln2_b, w_up, conv_w, conv_b, w_down, ln3_g, ln3_b):
    assert w_qkv.shape[0] == DEPTH == 1
    batch, seq, _ = x_prompt.shape
    streams, t_new, _ = x_sample.shape
    past_b = cache_b_k.shape[2]
    assert batch == 1 and seq % QKV_TM == 0 and QKV_TM == BAND_PREV * CHUNK

    bf = lambda w: w[0].astype(BF16)
    row = lambda v: v[0].astype(F32)[None, :]
    w_qkv_bf = bf(w_qkv)
    tail_params = (bf(w_o), row(ln1_g), row(ln1_b), bf(w_mq), bf(w_mo), row(ln2_g), row(ln2_b),
                   bf(w_up), conv_w[0].astype(F32), row(conv_b), bf(w_down), row(ln3_g), row(ln3_b))
    lamp = jnp.stack([lambda_q1[0], lambda_k1[0], lambda_q2[0], lambda_k2[0]]).astype(F32)
    g = row(subln_g)
    gcol = jnp.broadcast_to(subln_g[0].astype(F32)[:, None], (2 * HD_B, B_QW))
    bias = _bias_row(rel_bias[0])

    xp = x_prompt.reshape(seq, D_MODEL)
    qa, ka, va, qbt, kb, vbt, ka_f, va_f, kb_f, vb_f = _qkv(xp, w_qkv_bf, jnp.arange(seq), QKV_TM, True)
    oa = _attn_a_prompt(qa, ka, va, bias)
    ob = _attn_b_prompt(qbt, kb, vbt, lamp, gcol)
    mk_p, mv_p = _mem_kv(mem_prompt.reshape(N_MEM, D_MODEL), bf(w_mk), bf(w_mv))
    conv0 = jnp.zeros((SUBLANES, 2 * D_FF), F32)
    y_p, conv_p = _tail(xp, oa, ob, mk_p[None], mv_p[None], conv0, tail_params, TAIL_TM, TAIL_TM)

    rows_s = streams * t_new
    xs = x_sample.reshape(rows_s, D_MODEL)
    pos_s = jnp.tile(past_b + jnp.arange(t_new), streams)
    qa_s, ka_s, va_s, qb_s, kb_s, vb_s, ka_sf, va_sf, kb_sf, vb_sf = _qkv(xs, w_qkv_bf, pos_s, rows_s, False)
    oa_s = _attn_a_sample(qa_s, ka_s, va_s,
                          cache_a_k[0].reshape(streams, -1, A_WIDTH), cache_a_v[0].reshape(streams, -1, A_WIDTH),
                          bias, t_new)
    ob_s = _attn_b_sample(qb_s, kb_s, vb_s,
                          cache_b_k[0].reshape(streams, past_b, B_WIDTH),
                          cache_b_v[0].reshape(streams, past_b, B_WIDTH), lamp, g, t_new)
    conv_in_s = jnp.pad(state_conv[0].astype(F32), ((0, 0), (SUBLANES - (CONV_W - 1), 0), (0, 0)))
    y_s, conv_s = _tail(xs, oa_s, ob_s,
                        cache_mem_k[0].reshape(streams, N_MEM, D_MODEL).astype(BF16),
                        cache_mem_v[0].reshape(streams, N_MEM, D_MODEL).astype(BF16),
                        conv_in_s.reshape(streams * SUBLANES, 2 * D_FF), tail_params, rows_s, t_new)

    keep = CONV_W - 1
    return (
        y_p.reshape(batch, seq, D_MODEL),
        y_s.reshape(streams, t_new, D_MODEL),
        ka_f.reshape(1, batch, QKV_TM, H_A, HD_A),
        va_f.reshape(1, batch, QKV_TM, H_A, HD_A),
        kb_f.reshape(1, batch, seq, H_B, 2, HD_B),
        vb_f.reshape(1, batch, seq, H_B, 2 * HD_B),
        mk_p.reshape(1, batch, N_MEM, H_M, HD_M),
        mv_p.reshape(1, batch, N_MEM, H_M, HD_M),
        conv_p[SUBLANES - keep:].reshape(1, batch, keep, 2 * D_FF),
        ka_sf.reshape(1, streams, t_new, H_A, HD_A),
        va_sf.reshape(1, streams, t_new, H_A, HD_A),
        kb_sf.reshape(1, streams, t_new, H_B, 2, HD_B),
        vb_sf.reshape(1, streams, t_new, H_B, 2 * HD_B),
        conv_s.reshape(streams, SUBLANES, 2 * D_FF)[:, SUBLANES - keep:].reshape(1, streams, keep, 2 * D_FF),
    )
```

```python
import functools
import math

import numpy as np
import jax
import jax.numpy as jnp
from jax import lax
from jax.experimental import pallas as pl
from jax.experimental.pallas import tpu as pltpu

F32 = jnp.float32
BF16 = jnp.bfloat16

D_MODEL = 1024
CHUNK = 64
BAND_PREV = 8
REL_CLIP = 128
H_A = 8
HD_A = 64
H_B = 4
HD_B = 64
ROT_DIM = HD_B // 4
ROPE_THETA = 500000.0
N_MEM = 256
H_M = 4
HD_M = D_MODEL // H_M
D_FF = 2816
CONV_W = 3
LN_EPS = 1e-5
DEPTH = 1
DEEPNORM_ALPHA = (2.0 * DEPTH) ** 0.25
A_WIDTH = H_A * HD_A
B_WIDTH = H_B * 2 * HD_B
LAM_INIT = 0.8 - 0.6 * math.exp(-0.3 * 0)

LANES = 128
SUBLANES = 8
NEG = -1e30
LOG2_E = math.log2(math.e)

QKV_TM = 512
A_TQ = 256
B_TQ = 1024
B_QW = 256
B_TK = 512
V_ROWS = LANES + 16
BIAS_SPAN = 1024
TAIL_TM = 256
FF_BLOCK = 1408

_NT = (((1,), (1,)), ((), ()))


def _dot(a, b):
    return jnp.dot(a, b, preferred_element_type=F32)


def _dot_nt(a, b):
    return lax.dot_general(a, b, _NT, preferred_element_type=F32)


def _const_spec(shape):
    nd = len(shape)
    return pl.BlockSpec(shape, lambda *_: (0,) * nd, pipeline_mode=pl.Buffered(1))


def _arbitrary(n):
    return pltpu.CompilerParams(dimension_semantics=("arbitrary",) * n)


def _rope(z, cos, sin):
    lane = lax.broadcasted_iota(jnp.int32, (z.shape[0], LANES), 1)
    first_half = (lane % HD_B) < (ROT_DIM // 2)
    outs = []
    for c in range(z.shape[1] // LANES):
        zc = z[:, c * LANES:(c + 1) * LANES]
        partner = jnp.where(first_half,
                            pltpu.roll(zc, LANES - ROT_DIM // 2, axis=1),
                            pltpu.roll(zc, ROT_DIM // 2, axis=1))
        outs.append(zc * cos + partner * sin)
    return jnp.concatenate(outs, axis=1)


def _qkv_kernel(x_ref, w_ref, cos_ref, sin_ref,
                qa_ref, ka_ref, va_ref, qb_ref, kb_ref, vb_ref,
                kaf_ref, vaf_ref, kbf_ref, vbf_ref, *, transposed_b):
    xb = x_ref[...].astype(BF16)
    cos = cos_ref[...]
    sin = sin_ref[...]
    tm = xb.shape[0]

    def section(i):
        return _dot(xb, w_ref[:, i * A_WIDTH:(i + 1) * A_WIDTH])

    qa = section(0)
    qa_ref[...] = (qa * (HD_A ** -0.5)).astype(BF16)
    ka = section(1)
    ka_ref[...] = ka.astype(BF16)
    kaf_ref[...] = ka
    va = section(2)
    va_ref[...] = va.astype(BF16)
    vaf_ref[...] = va
    qb = _rope(section(3), cos, sin) * (HD_B ** -0.5 * (LOG2_E if transposed_b else 1.0))
    kb = _rope(section(4), cos, sin)
    kb_ref[...] = kb.astype(BF16)
    kbf_ref[...] = kb
    vb = section(5)
    vbf_ref[...] = vb
    if transposed_b:
        ones = (lax.broadcasted_iota(jnp.int32, (V_ROWS - LANES, tm), 0) == 0).astype(BF16)
        for h in range(H_B):
            sl = slice(h * LANES, (h + 1) * LANES)
            qb_ref[h] = qb[:, sl].T.astype(BF16)
            vb_ref[h, 0, 0:LANES, :] = vb[:, sl].T.astype(BF16)
            vb_ref[h, 0, LANES:V_ROWS, :] = ones
    else:
        qb_ref[...] = qb.astype(BF16)
        vb_ref[...] = vb.astype(BF16)


def _rope_tables(pos):
    half = ROT_DIM // 2
    inv = ROPE_THETA ** (-jnp.arange(0, ROT_DIM, 2, dtype=F32) / ROT_DIM)
    ang = pos.astype(F32)[:, None] * inv[None, :]
    cos, sin = jnp.cos(ang), jnp.sin(ang)
    n = pos.shape[0]
    rest = HD_B - ROT_DIM
    cos64 = jnp.concatenate([cos, cos, jnp.ones((n, rest), F32)], axis=1)
    sin64 = jnp.concatenate([-sin, sin, jnp.zeros((n, rest), F32)], axis=1)
    reps = LANES // HD_B
    return jnp.tile(cos64, (1, reps)), jnp.tile(sin64, (1, reps))


def _qkv(x, w_bf, pos, tm, transposed_b):
    rows = x.shape[0]
    nt = rows // tm
    cos, sin = _rope_tables(pos)
    row_spec = lambda w: pl.BlockSpec((tm, w), lambda i: (i, 0))
    last_spec = pl.BlockSpec((tm, A_WIDTH), lambda i: (0, 0))
    bf = jax.ShapeDtypeStruct((rows, A_WIDTH), BF16)
    if transposed_b:
        assert tm == B_TK
        qb_spec = pl.BlockSpec((H_B, LANES, tm), lambda i: (0, 0, i))
        qb_shape = jax.ShapeDtypeStruct((H_B, LANES, rows), BF16)
        vb_spec = pl.BlockSpec((H_B, 1, V_ROWS, tm), lambda i: (0, i, 0, 0))
        vb_shape = jax.ShapeDtypeStruct((H_B, nt, V_ROWS, tm), BF16)
    else:
        qb_spec = vb_spec = row_spec(B_WIDTH)
        qb_shape = vb_shape = bf
    return pl.pallas_call(
        functools.partial(_qkv_kernel, transposed_b=transposed_b),
        grid=(nt,),
        in_specs=[row_spec(D_MODEL), _const_spec(w_bf.shape), row_spec(LANES), row_spec(LANES)],
        out_specs=[row_spec(A_WIDTH)] * 3 + [qb_spec, row_spec(B_WIDTH), vb_spec]
                  + [last_spec, last_spec, row_spec(B_WIDTH), row_spec(B_WIDTH)],
        out_shape=[bf] * 3 + [qb_shape, bf, vb_shape] + [jax.ShapeDtypeStruct((tm, A_WIDTH), F32)] * 2
                  + [jax.ShapeDtypeStruct((rows, B_WIDTH), F32)] * 2,
        compiler_params=_arbitrary(1),
        name="qkv_proj",
    )(x, w_bf, cos, sin)


def _bias_row(rel_bias):
    table = rel_bias.astype(F32)
    far = table[:, 2 * REL_CLIP:]
    near = table[:, :1]
    n_far = 2 * A_TQ - REL_CLIP
    n_near = 3 * A_TQ - (n_far + 2 * REL_CLIP + 1)
    h = table.shape[0]
    return jnp.concatenate([jnp.broadcast_to(far, (h, n_far)), table[:, ::-1],
                            jnp.broadcast_to(near, (h, n_near)),
                            jnp.broadcast_to(far, (h, BIAS_SPAN - 3 * A_TQ))], axis=1)


def _build_band_bias(row_ref, bias_ref):
    _, rows, cols = bias_ref.shape
    q = lax.broadcasted_iota(jnp.int32, (rows, cols), 0)
    k = lax.broadcasted_iota(jnp.int32, (rows, cols), 1)
    qc = q // CHUNK
    kc = k // CHUNK - (2 * A_TQ) // CHUNK
    band = (kc >= qc - BAND_PREV) & (kc <= qc)
    for h in range(H_A):
        r = jnp.broadcast_to(row_ref[h:h + 1, :], (rows, BIAS_SPAN))
        toeplitz = pltpu.roll(r, 0, axis=1, stride=1, stride_axis=0)
        bias_ref[h] = jnp.where(band, toeplitz[:, :cols], NEG)


def _pair_queries(q2):
    lane = lax.broadcasted_iota(jnp.int32, q2.shape, 1)
    lo = lane < (LANES // 2)
    zero = jnp.zeros_like(q2)
    return jnp.concatenate([jnp.where(lo, q2, zero), jnp.where(lo, zero, q2)], axis=0), lo


def _attn_a_kernel(q_ref, k2_ref, k1_ref, k0_ref, v2_ref, v1_ref, v0_ref, row_ref, o_ref, bias_ref):
    i = pl.program_id(0)

    @pl.when(i == 0)
    def _():
        _build_band_bias(row_ref, bias_ref)

    kidx = lax.broadcasted_iota(jnp.int32, (1, 3 * A_TQ), 1)
    kvalid = (kidx + (i - 2) * A_TQ) >= 0
    for p in range(H_A // 2):
        sl = slice(p * LANES, (p + 1) * LANES)
        qz, lo = _pair_queries(q_ref[:, sl])
        kc = jnp.concatenate([k2_ref[:, sl], k1_ref[:, sl], k0_ref[:, sl]], axis=0)
        vc = jnp.concatenate([v2_ref[:, sl], v1_ref[:, sl], v0_ref[:, sl]], axis=0)
        s = _dot_nt(qz, kc)
        b = jnp.concatenate([bias_ref[2 * p], bias_ref[2 * p + 1]], axis=0)
        s = jnp.where(kvalid, s + b, NEG)
        m = jnp.max(s, axis=-1, keepdims=True)
        e = jnp.exp(s - m)
        l = jnp.sum(e, axis=-1, keepdims=True)
        o = _dot(e.astype(BF16), vc) / l
        o_ref[:, sl] = jnp.where(lo, o[:A_TQ], o[A_TQ:]).astype(BF16)


def _attn_a_prompt(qa, ka, va, bias_row):
    rows = qa.shape[0]
    cur = pl.BlockSpec((A_TQ, A_WIDTH), lambda i: (i, 0))
    prev1 = pl.BlockSpec((A_TQ, A_WIDTH), lambda i: (jnp.maximum(i - 1, 0), 0))
    prev2 = pl.BlockSpec((A_TQ, A_WIDTH), lambda i: (jnp.maximum(i - 2, 0), 0))
    return pl.pallas_call(
        _attn_a_kernel,
        grid=(rows // A_TQ,),
        in_specs=[cur, prev2, prev1, cur, prev2, prev1, cur, _const_spec(bias_row.shape)],
        out_specs=cur,
        out_shape=jax.ShapeDtypeStruct((rows, A_WIDTH), BF16),
        scratch_shapes=[pltpu.VMEM((H_A, A_TQ, 3 * A_TQ), F32)],
        compiler_params=_arbitrary(1),
        name="attn_a_prompt",
    )(qa, ka, ka, ka, va, va, va, bias_row)


def _attn_a_sample_kernel(q_ref, kn_ref, vn_ref, kc_ref, vc_ref, row_ref, o_ref, bias_ref):
    t = q_ref.shape[0]
    past = kc_ref.shape[1]

    @pl.when(pl.program_id(0) == 0)
    def _():
        _build_band_bias(row_ref, bias_ref)

    kc = kc_ref[0].astype(BF16)
    vc = vc_ref[0].astype(BF16)
    for p in range(H_A // 2):
        sl = slice(p * LANES, (p + 1) * LANES)
        qz, lo = _pair_queries(q_ref[:, sl])
        b_past = jnp.concatenate([bias_ref[2 * p, 0:t, 0:past], bias_ref[2 * p + 1, 0:t, 0:past]], axis=0)
        b_new = jnp.concatenate([bias_ref[2 * p, 0:t, past:past + t],
                                 bias_ref[2 * p + 1, 0:t, past:past + t]], axis=0)
        s_past = _dot_nt(qz, kc[:, sl]) + b_past
        s_new = _dot_nt(qz, kn_ref[:, sl]) + b_new
        m = jnp.maximum(jnp.max(s_past, axis=-1, keepdims=True), jnp.max(s_new, axis=-1, keepdims=True))
        e_past = jnp.exp(s_past - m)
        e_new = jnp.exp(s_new - m)
        l = jnp.sum(e_past, axis=-1, keepdims=True) + jnp.sum(e_new, axis=-1, keepdims=True)
        o = (_dot(e_past.astype(BF16), vc[:, sl]) + _dot(e_new.astype(BF16), vn_ref[:, sl])) / l
        o_ref[:, sl] = jnp.where(lo, o[:t], o[t:]).astype(BF16)


def _attn_a_sample(qa, ka, va, cache_k, cache_v, bias_row, t):
    rows = qa.shape[0]
    streams, past, width = cache_k.shape
    assert past == 2 * A_TQ and t <= CHUNK and t % SUBLANES == 0
    row = pl.BlockSpec((t, A_WIDTH), lambda s: (s, 0))
    cache = pl.BlockSpec((1, past, width), lambda s: (s, 0, 0))
    return pl.pallas_call(
        _attn_a_sample_kernel,
        grid=(streams,),
        in_specs=[row, row, row, cache, cache, _const_spec(bias_row.shape)],
        out_specs=row,
        out_shape=jax.ShapeDtypeStruct((rows, A_WIDTH), BF16),
        scratch_shapes=[pltpu.VMEM((H_A, t, past + LANES), F32)],
        compiler_params=_arbitrary(1),
        name="attn_a_sample",
    )(qa, ka, va, cache_k, cache_v, bias_row)


def _lambda(lamp_ref):
    lp = lamp_ref[...]
    a = jnp.sum(lp[0:1] * lp[1:2], axis=-1, keepdims=True)
    b = jnp.sum(lp[2:3] * lp[3:4], axis=-1, keepdims=True)
    return jnp.exp(a) - jnp.exp(b) + LAM_INIT


def _diff_finish(o0, o1, lam, g):
    o = o0 - lam * o1
    o = o * lax.rsqrt(jnp.mean(o * o, axis=-1, keepdims=True) + LN_EPS) * g * (1.0 - LAM_INIT)
    return o.astype(BF16)


def _attn_b_kernel(qt_ref, k_ref, vt_ref, lamp_ref, gcol_ref, o_ref, acc_ref, p_ref, alpha_ref):
    qi = pl.program_id(1)
    n_parts = B_TQ // B_QW
    dim = lax.broadcasted_iota(jnp.int32, (LANES, B_QW), 0)
    zero = jnp.zeros((LANES, B_QW), BF16)
    chains = [(part, mp) for part in range(n_parts) for mp in range(2)]
    qz = []
    for part, mp in chains:
        qt = qt_ref[0, :, part * B_QW:(part + 1) * B_QW]
        qz.append(jnp.where(dim < HD_B if mp == 0 else dim >= HD_B, qt, zero))
    acc_ref[...] = jnp.zeros(acc_ref.shape, F32)
    p_ref[1] = jnp.zeros(p_ref.shape[1:], BF16)
    alpha_ref[1] = jnp.ones(alpha_ref.shape[1:], F32)

    def pending_values(c, j_prev, buf):
        acc_ref[c] = alpha_ref[buf, c] * acc_ref[c] + _dot(vt_ref[0, j_prev], p_ref[buf, c])

    def step(j, ms, buf, parts, diag_parts, pending_parts):
        start = pl.multiple_of(j * B_TK, B_TK)
        j_prev = jnp.maximum(j - 1, 0)
        kpos = start + lax.broadcasted_iota(jnp.int32, (B_TK, B_QW), 0)
        lane = lax.broadcasted_iota(jnp.int32, (B_TK, B_QW), 1)
        out = list(ms)
        for c, (part, _) in enumerate(chains):
            if part in parts:
                s = _dot(k_ref[pl.ds(start, B_TK), :], qz[c])
                if part in diag_parts:
                    qpos = qi * B_TQ + part * B_QW + lane
                    s = jnp.where(kpos < (qpos // CHUNK + 1) * CHUNK, s, NEG)
                out[c] = jnp.maximum(ms[c], jnp.max(s, axis=0, keepdims=True))
                alpha_ref[buf, c] = jnp.exp2(ms[c] - out[c])
                p_ref[buf, c] = jnp.exp2(s - out[c]).astype(BF16)
            if part in pending_parts:
                pending_values(c, j_prev, 1 - buf)
        return tuple(out)

    every = tuple(range(n_parts))
    low, high = every[:n_parts // 2], every[n_parts // 2:]

    def pair(i, ms):
        ms = step(2 * i, ms, 0, every, (), every)
        return step(2 * i + 1, ms, 1, every, (), every)

    m_init = tuple(jnp.full((1, B_QW), NEG, F32) for _ in chains)
    ms = lax.fori_loop(0, qi, pair, m_init)
    ms = step(2 * qi, ms, 0, every, low, every)
    step(2 * qi + 1, ms, 1, high, high, every)
    for c, (part, _) in enumerate(chains):
        if part in high:
            pending_values(c, 2 * qi + 1, 1)

    lam = _lambda(lamp_ref)
    for part in range(n_parts):
        a0 = acc_ref[2 * part]
        a1 = acc_ref[2 * part + 1]
        o = a0[0:LANES] / a0[LANES:LANES + 1] - lam * (a1[0:LANES] / a1[LANES:LANES + 1])
        o = o * lax.rsqrt(jnp.mean(o * o, axis=0, keepdims=True) + LN_EPS) * gcol_ref[...] * (1.0 - LAM_INIT)
        o_ref[part * B_QW:(part + 1) * B_QW, :] = o.T.astype(BF16)


def _attn_b_prompt(qbt, kb, vbt, lamp, gcol):
    rows = kb.shape[0]
    assert rows % B_TQ == 0 and B_TQ == 2 * B_TK and B_TK == 2 * B_QW and B_QW % CHUNK == 0
    n_chains = (B_TQ // B_QW) * 2
    return pl.pallas_call(
        _attn_b_kernel,
        grid=(H_B, rows // B_TQ),
        in_specs=[pl.BlockSpec((1, LANES, B_TQ), lambda h, i: (h, 0, i)),
                  pl.BlockSpec((rows, LANES), lambda h, i: (0, h)),
                  pl.BlockSpec((1,) + vbt.shape[1:], lambda h, i: (h, 0, 0, 0)),
                  _const_spec(lamp.shape), _const_spec(gcol.shape)],
        out_specs=pl.BlockSpec((B_TQ, LANES), lambda h, i: (i, h)),
        out_shape=jax.ShapeDtypeStruct((rows, B_WIDTH), BF16),
        scratch_shapes=[pltpu.VMEM((n_chains, V_ROWS, B_QW), F32),
                        pltpu.VMEM((2, n_chains, B_TK, B_QW), BF16), pltpu.VMEM((2, n_chains, 1, B_QW), F32)],
        compiler_params=_arbitrary(2),
        name="attn_b_prompt",
    )(qbt, kb, vbt, lamp, gcol)


def _attn_b_sample_kernel(q_ref, kn_ref, vn_ref, kc_ref, vc_ref, lamp_ref, g_ref, o_ref):
    t = q_ref.shape[0]
    lam = _lambda(lamp_ref)
    g = g_ref[...]
    for h in range(H_B):
        sl = slice(h * LANES, (h + 1) * LANES)
        qz, _ = _pair_queries(q_ref[:, sl])
        s_past = _dot_nt(qz, kc_ref[0, :, sl].astype(BF16))
        s_new = _dot_nt(qz, kn_ref[:, sl])
        m = jnp.maximum(jnp.max(s_past, axis=-1, keepdims=True), jnp.max(s_new, axis=-1, keepdims=True))
        e_past = jnp.exp(s_past - m)
        e_new = jnp.exp(s_new - m)
        l = jnp.sum(e_past, axis=-1, keepdims=True) + jnp.sum(e_new, axis=-1, keepdims=True)
        o = (_dot(e_past.astype(BF16), vc_ref[0, :, sl].astype(BF16))
             + _dot(e_new.astype(BF16), vn_ref[:, sl])) / l
        o_ref[:, sl] = _diff_finish(o[:t], o[t:], lam, g)


def _attn_b_sample(qb, kb, vb, cache_k, cache_v, lamp, g, t):
    rows = qb.shape[0]
    streams, past, width = cache_k.shape
    assert past % CHUNK == 0 and t <= CHUNK
    row = pl.BlockSpec((t, B_WIDTH), lambda s: (s, 0))
    cache = pl.BlockSpec((1, past, width), lambda s: (s, 0, 0))
    return pl.pallas_call(
        _attn_b_sample_kernel,
        grid=(streams,),
        in_specs=[row, row, row, cache, cache, _const_spec(lamp.shape), _const_spec(g.shape)],
        out_specs=row,
        out_shape=jax.ShapeDtypeStruct((rows, B_WIDTH), BF16),
        compiler_params=_arbitrary(1),
        name="attn_b_sample",
    )(qb, kb, vb, cache_k, cache_v, lamp, g)


def _mem_kv_kernel(x_ref, wk_ref, wv_ref, kf_ref, vf_ref):
    xb = x_ref[...].astype(BF16)
    kf_ref[...] = _dot(xb, wk_ref[...])
    vf_ref[...] = _dot(xb, wv_ref[...])


def _mem_kv(mem, wk_bf, wv_bf):
    f = jax.ShapeDtypeStruct(mem.shape, F32)
    return pl.pallas_call(
        _mem_kv_kernel,
        out_shape=[f, f],
        name="mem_kv",
    )(mem, wk_bf, wv_bf)


def _layer_norm(x, g, b):
    mu = jnp.mean(x, axis=-1, keepdims=True)
    xc = x - mu
    var = jnp.mean(xc * xc, axis=-1, keepdims=True)
    return xc * lax.rsqrt(var + LN_EPS) * g + b


def _tail_kernel(x_ref, oa_ref, ob_ref, mk_ref, mv_ref, cin_ref,
                 wo_ref, ln1g_ref, ln1b_ref, wmq_ref, wmo_ref, ln2g_ref, ln2b_ref,
                 wup_ref, cw_ref, cb_ref, wdn_ref, ln3g_ref, ln3b_ref,
                 y_ref, carry_ref, *, seg):
    tm = x_ref.shape[0]
    n_seg = tm // seg

    @pl.when(pl.program_id(0) == 0)
    def _():
        carry_ref[...] = cin_ref[...]

    x = x_ref[...]
    mix = _dot(oa_ref[...], wo_ref[0:A_WIDTH, :]) + _dot(ob_ref[...], wo_ref[A_WIDTH:A_WIDTH + B_WIDTH, :])
    x1 = _layer_norm(DEEPNORM_ALPHA * x + mix, ln1g_ref[...], ln1b_ref[...])

    qm = (_dot(x1.astype(BF16), wmq_ref[...]) * (HD_M ** -0.5)).astype(BF16)

    def mem_attend(q, mk, mv):
        heads = []
        for h in range(H_M):
            sl = slice(h * HD_M, (h + 1) * HD_M)
            s = _dot_nt(q[:, sl], mk[:, sl])
            e = jnp.exp(s - jnp.max(s, axis=-1, keepdims=True))
            l = jnp.sum(e, axis=-1, keepdims=True)
            heads.append((_dot(e.astype(BF16), mv[:, sl]) / l).astype(BF16))
        return jnp.concatenate(heads, axis=1)

    if mk_ref.shape[0] == 1:
        om = mem_attend(qm, mk_ref[0].astype(BF16), mv_ref[0].astype(BF16))
    else:
        om = jnp.concatenate([mem_attend(qm[sg * seg:(sg + 1) * seg],
                                         mk_ref[sg].astype(BF16), mv_ref[sg].astype(BF16))
                              for sg in range(n_seg)], axis=0)
    mo = _dot(om, wmo_ref[...])
    x2 = _layer_norm(DEEPNORM_ALPHA * x1 + mo, ln2g_ref[...], ln2b_ref[...])
    x2b = x2.astype(BF16)

    row = lax.broadcasted_iota(jnp.int32, (seg, FF_BLOCK), 0)

    def conv_block(col):
        cols = slice(col, col + FF_BLOCK)
        u = _dot(x2b, wup_ref[:, cols])
        w0, w1, w2 = cw_ref[0:1, cols], cw_ref[1:2, cols], cw_ref[2:3, cols]
        outs = []
        for sg in range(n_seg):
            us = u[sg * seg:(sg + 1) * seg]
            p2 = carry_ref[SUBLANES * sg + 6:SUBLANES * sg + 7, cols]
            p1 = carry_ref[SUBLANES * sg + 7:SUBLANES * sg + 8, cols]
            u1 = jnp.where(row == 0, p1, pltpu.roll(us, 1, axis=0))
            u2 = jnp.where(row == 0, p2, jnp.where(row == 1, p1, pltpu.roll(us, 2, axis=0)))
            outs.append(u2 * w0 + u1 * w1 + us * w2 + cb_ref[0:1, cols])
            carry_ref[SUBLANES * sg:SUBLANES * (sg + 1), cols] = us[seg - SUBLANES:seg]
        return outs[0] if n_seg == 1 else jnp.concatenate(outs, axis=0)

    f = jnp.zeros((tm, D_MODEL), F32)
    for c0 in range(0, D_FF, FF_BLOCK):
        gate = conv_block(c0)
        val = conv_block(D_FF + c0)
        hid = gate * (1.0 / (1.0 + jnp.exp(-gate))) * val
        f = f + _dot(hid.astype(BF16), wdn_ref[c0:c0 + FF_BLOCK, :])
    y_ref[...] = _layer_norm(DEEPNORM_ALPHA * x2 + f, ln3g_ref[...], ln3b_ref[...])


def _tail(x, oa, ob, mk, mv, conv_in, params, tm, seg):
    rows = x.shape[0]
    n_seg = tm // seg
    assert conv_in.shape == (SUBLANES * n_seg, 2 * D_FF)
    assert n_seg == 1 or rows == tm
    assert mk.shape[0] in (1, n_seg)
    row_spec = lambda w: pl.BlockSpec((tm, w), lambda i: (i, 0))
    carry_spec = pl.BlockSpec(conv_in.shape, lambda i: (0, 0))
    return pl.pallas_call(
        functools.partial(_tail_kernel, seg=seg),
        grid=(rows // tm,),
        in_specs=[row_spec(D_MODEL), row_spec(A_WIDTH), row_spec(B_WIDTH),
                  _const_spec(mk.shape), _const_spec(mv.shape),
                  _const_spec(conv_in.shape)] + [_const_spec(p.shape) for p in params],
        out_specs=[row_spec(D_MODEL), carry_spec],
        out_shape=[jax.ShapeDtypeStruct((rows, D_MODEL), F32),
                   jax.ShapeDtypeStruct(conv_in.shape, F32)],
        compiler_params=_arbitrary(1),
        name="tail",
    )(x, oa, ob, mk, mv, conv_in, *params)


def kernel(x_prompt, x_sample, mem_prompt, cache_a_k, cache_a_v, cache_b_k, cache_b_v, cache_mem_k, cache_mem_v,
           state_conv, w_qkv, rel_bias, lambda_q1, lambda_k1, lambda_q2, lambda_k2, subln_g, w_o, ln1_g, ln1_b,
           w_mq, w_mk, w_mv, w_mo, ln2_g, ln2_b, w_up, conv_w, conv_b, w_down, ln3_g, ln3_b):
    assert w_qkv.shape[0] == DEPTH == 1
    batch, seq, _ = x_prompt.shape
    streams, t_new, _ = x_sample.shape
    past_b = cache_b_k.shape[2]
    assert batch == 1 and seq % QKV_TM == 0 and QKV_TM == BAND_PREV * CHUNK

    bf = lambda w: w[0].astype(BF16)
    row = lambda v: v[0].astype(F32)[None, :]
    w_qkv_bf = bf(w_qkv)
    tail_params = (bf(w_o), row(ln1_g), row(ln1_b), bf(w_mq), bf(w_mo), row(ln2_g), row(ln2_b),
                   bf(w_up), conv_w[0].astype(F32), row(conv_b), bf(w_down), row(ln3_g), row(ln3_b))
    lamp = jnp.stack([lambda_q1[0], lambda_k1[0], lambda_q2[0], lambda_k2[0]]).astype(F32)
    g = row(subln_g)
    gcol = jnp.broadcast_to(subln_g[0].astype(F32)[:, None], (2 * HD_B, B_QW))
    bias = _bias_row(rel_bias[0])

    xp = x_prompt.reshape(seq, D_MODEL)
    qa, ka, va, qbt, kb, vbt, ka_f, va_f, kb_f, vb_f = _qkv(xp, w_qkv_bf, jnp.arange(seq), QKV_TM, True)
    oa = _attn_a_prompt(qa, ka, va, bias)
    ob = _attn_b_prompt(qbt, kb, vbt, lamp, gcol)
    mk_p, mv_p = _mem_kv(mem_prompt.reshape(N_MEM, D_MODEL), bf(w_mk), bf(w_mv))
    conv0 = jnp.zeros((SUBLANES, 2 * D_FF), F32)
    y_p, conv_p = _tail(xp, oa, ob, mk_p[None], mv_p[None], conv0, tail_params, TAIL_TM, TAIL_TM)

    rows_s = streams * t_new
    xs = x_sample.reshape(rows_s, D_MODEL)
    pos_s = jnp.tile(past_b + jnp.arange(t_new), streams)
    qa_s, ka_s, va_s, qb_s, kb_s, vb_s, ka_sf, va_sf, kb_sf, vb_sf = _qkv(xs, w_qkv_bf, pos_s, rows_s, False)
    oa_s = _attn_a_sample(qa_s, ka_s, va_s,
                          cache_a_k[0].reshape(streams, -1, A_WIDTH), cache_a_v[0].reshape(streams, -1, A_WIDTH),
                          bias, t_new)
    ob_s = _attn_b_sample(qb_s, kb_s, vb_s,
                          cache_b_k[0].reshape(streams, past_b, B_WIDTH),
                          cache_b_v[0].reshape(streams, past_b, B_WIDTH), lamp, g, t_new)
    conv_in_s = jnp.pad(state_conv[0].astype(F32), ((0, 0), (SUBLANES - (CONV_W - 1), 0), (0, 0)))
    y_s, conv_s = _tail(xs, oa_s, ob_s,
                        cache_mem_k[0].reshape(streams, N_MEM, D_MODEL).astype(BF16),
                        cache_mem_v[0].reshape(streams, N_MEM, D_MODEL).astype(BF16),
                        conv_in_s.reshape(streams * SUBLANES, 2 * D_FF), tail_params, rows_s, t_new)

    keep = CONV_W - 1
    return (
        y_p.reshape(batch, seq, D_MODEL),
        y_s.reshape(streams, t_new, D_MODEL),
        ka_f.reshape(1, batch, QKV_TM, H_A, HD_A),
        va_f.reshape(1, batch, QKV_TM, H_A, HD_A),
        kb_f.reshape(1, batch, seq, H_B, 2, HD_B),
        vb_f.reshape(1, batch, seq, H_B, 2 * HD_B),
        mk_p.reshape(1, batch, N_MEM, H_M, HD_M),
        mv_p.reshape(1, batch, N_MEM, H_M, HD_M),
        conv_p[SUBLANES - keep:].reshape(1, batch, keep, 2 * D_FF),
        ka_sf.reshape(1, streams, t_new, H_A, HD_A),
        va_sf.reshape(1, streams, t_new, H_A, HD_A),
        kb_sf.reshape(1, streams, t_new, H_B, 2, HD_B),
        vb_sf.reshape(1, streams, t_new, H_B, 2 * HD_B),
        conv_s.reshape(streams, SUBLANES, 2 * D_FF)[:, SUBLANES - keep:].reshape(1, streams, keep, 2 * D_FF),
    )
```

```python
import functools
import math

import numpy as np
import jax
import jax.numpy as jnp
from jax import lax
from jax.experimental import pallas as pl
from jax.experimental.pallas import tpu as pltpu

F32 = jnp.float32
BF16 = jnp.bfloat16

D_MODEL = 1024
CHUNK = 64
BAND_PREV = 8
REL_CLIP = 128
H_A = 8
HD_A = 64
H_B = 4
HD_B = 64
ROT_DIM = HD_B // 4
ROPE_THETA = 500000.0
N_MEM = 256
H_M = 4
HD_M = D_MODEL // H_M
D_FF = 2816
CONV_W = 3
LN_EPS = 1e-5
DEPTH = 1
DEEPNORM_ALPHA = (2.0 * DEPTH) ** 0.25
A_WIDTH = H_A * HD_A
B_WIDTH = H_B * 2 * HD_B
LAM_INIT = 0.8 - 0.6 * math.exp(-0.3 * 0)

LANES = 128
SUBLANES = 8
NEG = -1e30
LOG2_E = math.log2(math.e)

QKV_TM = 512
A_TQ = 256
B_TQ = 1024
B_QW = 256
B_TK = 512
V_ROWS = LANES + 16
BIAS_SPAN = 1024
TAIL_TM = 512
TAIL_SUB = 256
MXU_DIM = 256
FF_BLOCKS = (4 * MXU_DIM, 4 * MXU_DIM, D_FF - 8 * MXU_DIM)
assert sum(FF_BLOCKS) == D_FF and all(w % MXU_DIM == 0 for w in FF_BLOCKS)

_NT = (((1,), (1,)), ((), ()))


def _dot(a, b):
    return jnp.dot(a, b, preferred_element_type=F32)


def _dot_nt(a, b):
    return lax.dot_general(a, b, _NT, preferred_element_type=F32)


def _const_spec(shape):
    nd = len(shape)
    return pl.BlockSpec(shape, lambda *_: (0,) * nd, pipeline_mode=pl.Buffered(1))


def _arbitrary(n):
    return pltpu.CompilerParams(dimension_semantics=("arbitrary",) * n)


def _rope(z, cos, sin):
    lane = lax.broadcasted_iota(jnp.int32, (z.shape[0], LANES), 1)
    first_half = (lane % HD_B) < (ROT_DIM // 2)
    outs = []
    for c in range(z.shape[1] // LANES):
        zc = z[:, c * LANES:(c + 1) * LANES]
        partner = jnp.where(first_half,
                            pltpu.roll(zc, LANES - ROT_DIM // 2, axis=1),
                            pltpu.roll(zc, ROT_DIM // 2, axis=1))
        outs.append(zc * cos + partner * sin)
    return jnp.concatenate(outs, axis=1)


def _qkv_kernel(x_ref, w_ref, cos_ref, sin_ref,
                qa_ref, ka_ref, va_ref, qb_ref, kb_ref, vb_ref,
                kaf_ref, vaf_ref, kbf_ref, vbf_ref, *, transposed_b):
    xb = x_ref[...].astype(BF16)
    cos = cos_ref[...]
    sin = sin_ref[...]
    tm = xb.shape[0]

    def section(i):
        return _dot(xb, w_ref[:, i * A_WIDTH:(i + 1) * A_WIDTH])

    qa = section(0)
    qa_ref[...] = (qa * (HD_A ** -0.5 * LOG2_E)).astype(BF16)
    ka = section(1)
    ka_ref[...] = ka.astype(BF16)
    kaf_ref[...] = ka
    va = section(2)
    va_ref[...] = va.astype(BF16)
    vaf_ref[...] = va
    qb = _rope(section(3), cos, sin) * (HD_B ** -0.5 * (LOG2_E if transposed_b else 1.0))
    kb = _rope(section(4), cos, sin)
    kb_ref[...] = kb.astype(BF16)
    kbf_ref[...] = kb
    vb = section(5)
    vbf_ref[...] = vb
    if transposed_b:
        ones = (lax.broadcasted_iota(jnp.int32, (V_ROWS - LANES, tm), 0) == 0).astype(BF16)
        for h in range(H_B):
            sl = slice(h * LANES, (h + 1) * LANES)
            qb_ref[h] = qb[:, sl].T.astype(BF16)
            vb_ref[h, 0, 0:LANES, :] = vb[:, sl].T.astype(BF16)
            vb_ref[h, 0, LANES:V_ROWS, :] = ones
    else:
        qb_ref[...] = qb.astype(BF16)
        vb_ref[...] = vb.astype(BF16)


def _rope_tables(pos):
    half = ROT_DIM // 2
    inv = ROPE_THETA ** (-jnp.arange(0, ROT_DIM, 2, dtype=F32) / ROT_DIM)
    ang = pos.astype(F32)[:, None] * inv[None, :]
    cos, sin = jnp.cos(ang), jnp.sin(ang)
    n = pos.shape[0]
    rest = HD_B - ROT_DIM
    cos64 = jnp.concatenate([cos, cos, jnp.ones((n, rest), F32)], axis=1)
    sin64 = jnp.concatenate([-sin, sin, jnp.zeros((n, rest), F32)], axis=1)
    reps = LANES // HD_B
    return jnp.tile(cos64, (1, reps)), jnp.tile(sin64, (1, reps))


def _qkv(x, w_bf, pos, tm, transposed_b):
    rows = x.shape[0]
    nt = rows // tm
    cos, sin = _rope_tables(pos)
    row_spec = lambda w: pl.BlockSpec((tm, w), lambda i: (i, 0))
    last_spec = pl.BlockSpec((tm, A_WIDTH), lambda i: (0, 0))
    bf = jax.ShapeDtypeStruct((rows, A_WIDTH), BF16)
    if transposed_b:
        assert tm == B_TK
        qb_spec = pl.BlockSpec((H_B, LANES, tm), lambda i: (0, 0, i))
        qb_shape = jax.ShapeDtypeStruct((H_B, LANES, rows), BF16)
        vb_spec = pl.BlockSpec((H_B, 1, V_ROWS, tm), lambda i: (0, i, 0, 0))
        vb_shape = jax.ShapeDtypeStruct((H_B, nt, V_ROWS, tm), BF16)
    else:
        qb_spec = vb_spec = row_spec(B_WIDTH)
        qb_shape = vb_shape = bf
    return pl.pallas_call(
        functools.partial(_qkv_kernel, transposed_b=transposed_b),
        grid=(nt,),
        in_specs=[row_spec(D_MODEL), _const_spec(w_bf.shape), row_spec(LANES), row_spec(LANES)],
        out_specs=[row_spec(A_WIDTH)] * 3 + [qb_spec, row_spec(B_WIDTH), vb_spec]
                  + [last_spec, last_spec, row_spec(B_WIDTH), row_spec(B_WIDTH)],
        out_shape=[bf] * 3 + [qb_shape, bf, vb_shape] + [jax.ShapeDtypeStruct((tm, A_WIDTH), F32)] * 2
                  + [jax.ShapeDtypeStruct((rows, B_WIDTH), F32)] * 2,
        compiler_params=_arbitrary(1),
        name="qkv_proj",
    )(x, w_bf, cos, sin)


def _bias_row(rel_bias):
    table = rel_bias.astype(F32)
    far = table[:, 2 * REL_CLIP:]
    near = table[:, :1]
    n_far = 2 * A_TQ - REL_CLIP
    n_near = 3 * A_TQ - (n_far + 2 * REL_CLIP + 1)
    h = table.shape[0]
    return jnp.concatenate([jnp.broadcast_to(far, (h, n_far)), table[:, ::-1],
                            jnp.broadcast_to(near, (h, n_near)),
                            jnp.broadcast_to(far, (h, BIAS_SPAN - 3 * A_TQ))], axis=1)


def _build_band_bias(row_ref, bias_ref):
    _, rows, cols = bias_ref.shape
    q = lax.broadcasted_iota(jnp.int32, (rows, cols), 0)
    k = lax.broadcasted_iota(jnp.int32, (rows, cols), 1)
    qc = q // CHUNK
    kc = k // CHUNK - (2 * A_TQ) // CHUNK
    band = (kc >= qc - BAND_PREV) & (kc <= qc)
    for h in range(H_A):
        r = jnp.broadcast_to(row_ref[h:h + 1, :], (rows, BIAS_SPAN))
        toeplitz = pltpu.roll(r, 0, axis=1, stride=1, stride_axis=0)
        bias_ref[h] = jnp.where(band, toeplitz[:, :cols] * LOG2_E, NEG)


def _pair_queries(q2):
    lane = lax.broadcasted_iota(jnp.int32, q2.shape, 1)
    lo = lane < (LANES // 2)
    zero = jnp.zeros_like(q2)
    return jnp.concatenate([jnp.where(lo, q2, zero), jnp.where(lo, zero, q2)], axis=0), lo


def _attn_a_kernel(q_ref, k2_ref, k1_ref, k0_ref, v2_ref, v1_ref, v0_ref, row_ref, o_ref, bias_ref):
    i = pl.program_id(0)

    @pl.when(i == 0)
    def _():
        _build_band_bias(row_ref, bias_ref)

    kidx = lax.broadcasted_iota(jnp.int32, (1, 3 * A_TQ), 1)
    kvalid = (kidx + (i - 2) * A_TQ) >= 0
    ones = jnp.ones((3 * A_TQ, LANES), BF16)
    n_pairs = H_A // 2
    s, lo = [None] * n_pairs, [None] * n_pairs

    def scores(p):
        sl = slice(p * LANES, (p + 1) * LANES)
        qz, lo[p] = _pair_queries(q_ref[:, sl])
        kc = jnp.concatenate([k2_ref[:, sl], k1_ref[:, sl], k0_ref[:, sl]], axis=0)
        b = jnp.concatenate([bias_ref[2 * p], bias_ref[2 * p + 1]], axis=0)
        s[p] = jnp.where(kvalid, _dot_nt(qz, kc) + b, NEG)

    def values(p):
        sl = slice(p * LANES, (p + 1) * LANES)
        e = jnp.exp2(s[p] - jnp.max(s[p], axis=-1, keepdims=True)).astype(BF16)
        vc = jnp.concatenate([v2_ref[:, sl], v1_ref[:, sl], v0_ref[:, sl]], axis=0)
        r = _dot(e, jnp.concatenate([vc, ones], axis=1))
        o = r[:, :LANES] / r[:, LANES:LANES + 1]
        o_ref[:, sl] = jnp.where(lo[p], o[:A_TQ], o[A_TQ:]).astype(BF16)

    for p in range(n_pairs + 1):
        if p < n_pairs:
            scores(p)
        if p >= 1:
            values(p - 1)


def _attn_a_prompt(qa, ka, va, bias_row):
    rows = qa.shape[0]
    cur = pl.BlockSpec((A_TQ, A_WIDTH), lambda i: (i, 0))
    prev1 = pl.BlockSpec((A_TQ, A_WIDTH), lambda i: (jnp.maximum(i - 1, 0), 0))
    prev2 = pl.BlockSpec((A_TQ, A_WIDTH), lambda i: (jnp.maximum(i - 2, 0), 0))
    return pl.pallas_call(
        _attn_a_kernel,
        grid=(rows // A_TQ,),
        in_specs=[cur, prev2, prev1, cur, prev2, prev1, cur, _const_spec(bias_row.shape)],
        out_specs=cur,
        out_shape=jax.ShapeDtypeStruct((rows, A_WIDTH), BF16),
        scratch_shapes=[pltpu.VMEM((H_A, A_TQ, 3 * A_TQ), F32)],
        compiler_params=_arbitrary(1),
        name="attn_a_prompt",
    )(qa, ka, ka, ka, va, va, va, bias_row)


def _attn_a_sample_kernel(q_ref, kn_ref, vn_ref, kc_ref, vc_ref, row_ref, o_ref, bias_ref):
    t = q_ref.shape[0]
    past = kc_ref.shape[1]

    @pl.when(pl.program_id(0) == 0)
    def _():
        _build_band_bias(row_ref, bias_ref)

    kc = kc_ref[0].astype(BF16)
    vc = vc_ref[0].astype(BF16)
    for p in range(H_A // 2):
        sl = slice(p * LANES, (p + 1) * LANES)
        qz, lo = _pair_queries(q_ref[:, sl])
        b_past = jnp.concatenate([bias_ref[2 * p, 0:t, 0:past], bias_ref[2 * p + 1, 0:t, 0:past]], axis=0)
        b_new = jnp.concatenate([bias_ref[2 * p, 0:t, past:past + t],
                                 bias_ref[2 * p + 1, 0:t, past:past + t]], axis=0)
        s_past = _dot_nt(qz, kc[:, sl]) + b_past
        s_new = _dot_nt(qz, kn_ref[:, sl]) + b_new
        m = jnp.maximum(jnp.max(s_past, axis=-1, keepdims=True), jnp.max(s_new, axis=-1, keepdims=True))
        e_past = jnp.exp2(s_past - m)
        e_new = jnp.exp2(s_new - m)
        l = jnp.sum(e_past, axis=-1, keepdims=True) + jnp.sum(e_new, axis=-1, keepdims=True)
        o = (_dot(e_past.astype(BF16), vc[:, sl]) + _dot(e_new.astype(BF16), vn_ref[:, sl])) / l
        o_ref[:, sl] = jnp.where(lo, o[:t], o[t:]).astype(BF16)


def _attn_a_sample(qa, ka, va, cache_k, cache_v, bias_row, t):
    rows = qa.shape[0]
    streams, past, width = cache_k.shape
    assert past == 2 * A_TQ and t <= CHUNK and t % SUBLANES == 0
    row = pl.BlockSpec((t, A_WIDTH), lambda s: (s, 0))
    cache = pl.BlockSpec((1, past, width), lambda s: (s, 0, 0))
    return pl.pallas_call(
        _attn_a_sample_kernel,
        grid=(streams,),
        in_specs=[row, row, row, cache, cache, _const_spec(bias_row.shape)],
        out_specs=row,
        out_shape=jax.ShapeDtypeStruct((rows, A_WIDTH), BF16),
        scratch_shapes=[pltpu.VMEM((H_A, t, past + LANES), F32)],
        compiler_params=_arbitrary(1),
        name="attn_a_sample",
    )(qa, ka, va, cache_k, cache_v, bias_row)


def _lambda(lamp_ref):
    lp = lamp_ref[...]
    a = jnp.sum(lp[0:1] * lp[1:2], axis=-1, keepdims=True)
    b = jnp.sum(lp[2:3] * lp[3:4], axis=-1, keepdims=True)
    return jnp.exp(a) - jnp.exp(b) + LAM_INIT


def _diff_finish(o0, o1, lam, g):
    o = o0 - lam * o1
    o = o * lax.rsqrt(jnp.mean(o * o, axis=-1, keepdims=True) + LN_EPS) * g * (1.0 - LAM_INIT)
    return o.astype(BF16)


def _attn_b_kernel(qt_ref, k_ref, vt_ref, lamp_ref, gcol_ref, o_ref, acc_ref, p_ref, alpha_ref):
    qi = pl.program_id(1)
    n_parts = B_TQ // B_QW
    dim = lax.broadcasted_iota(jnp.int32, (LANES, B_QW), 0)
    zero = jnp.zeros((LANES, B_QW), BF16)
    chains = [(part, mp) for part in range(n_parts) for mp in range(2)]
    qz = []
    for part, mp in chains:
        qt = qt_ref[0, :, part * B_QW:(part + 1) * B_QW]
        qz.append(jnp.where(dim < HD_B if mp == 0 else dim >= HD_B, qt, zero))
    acc_ref[...] = jnp.zeros(acc_ref.shape, F32)
    p_ref[1] = jnp.zeros(p_ref.shape[1:], BF16)
    alpha_ref[1] = jnp.ones(alpha_ref.shape[1:], F32)

    def pending_values(c, j_prev, buf):
        acc_ref[c] = alpha_ref[buf, c] * acc_ref[c] + _dot(vt_ref[0, j_prev], p_ref[buf, c])

    def step(j, ms, buf, parts, diag_parts, pending_parts):
        start = pl.multiple_of(j * B_TK, B_TK)
        j_prev = jnp.maximum(j - 1, 0)
        kpos = start + lax.broadcasted_iota(jnp.int32, (B_TK, B_QW), 0)
        lane = lax.broadcasted_iota(jnp.int32, (B_TK, B_QW), 1)
        out = list(ms)
        for c, (part, _) in enumerate(chains):
            if part in parts:
                s = _dot(k_ref[pl.ds(start, B_TK), :], qz[c])
                if part in diag_parts:
                    qpos = qi * B_TQ + part * B_QW + lane
                    s = jnp.where(kpos < (qpos // CHUNK + 1) * CHUNK, s, NEG)
                out[c] = jnp.maximum(ms[c], jnp.max(s, axis=0, keepdims=True))
                alpha_ref[buf, c] = jnp.exp2(ms[c] - out[c])
                p_ref[buf, c] = jnp.exp2(s - out[c]).astype(BF16)
            if part in pending_parts:
                pending_values(c, j_prev, 1 - buf)
        return tuple(out)

    every = tuple(range(n_parts))
    low, high = every[:n_parts // 2], every[n_parts // 2:]

    def pair(i, ms):
        ms = step(2 * i, ms, 0, every, (), every)
        return step(2 * i + 1, ms, 1, every, (), every)

    m_init = tuple(jnp.full((1, B_QW), NEG, F32) for _ in chains)
    ms = lax.fori_loop(0, qi, pair, m_init)
    ms = step(2 * qi, ms, 0, every, low, every)
    step(2 * qi + 1, ms, 1, high, high, every)
    for c, (part, _) in enumerate(chains):
        if part in high:
            pending_values(c, 2 * qi + 1, 1)

    lam = _lambda(lamp_ref)
    for part in range(n_parts):
        a0 = acc_ref[2 * part]
        a1 = acc_ref[2 * part + 1]
        o = a0[0:LANES] / a0[LANES:LANES + 1] - lam * (a1[0:LANES] / a1[LANES:LANES + 1])
        o = o * lax.rsqrt(jnp.mean(o * o, axis=0, keepdims=True) + LN_EPS) * gcol_ref[...] * (1.0 - LAM_INIT)
        o_ref[part * B_QW:(part + 1) * B_QW, :] = o.T.astype(BF16)


def _attn_b_prompt(qbt, kb, vbt, lamp, gcol):
    rows = kb.shape[0]
    assert rows % B_TQ == 0 and B_TQ == 2 * B_TK and B_TK == 2 * B_QW and B_QW % CHUNK == 0
    n_chains = (B_TQ // B_QW) * 2
    return pl.pallas_call(
        _attn_b_kernel,
        grid=(H_B, rows // B_TQ),
        in_specs=[pl.BlockSpec((1, LANES, B_TQ), lambda h, i: (h, 0, i)),
                  pl.BlockSpec((rows, LANES), lambda h, i: (0, h)),
                  pl.BlockSpec((1,) + vbt.shape[1:], lambda h, i: (h, 0, 0, 0)),
                  _const_spec(lamp.shape), _const_spec(gcol.shape)],
        out_specs=pl.BlockSpec((B_TQ, LANES), lambda h, i: (i, h)),
        out_shape=jax.ShapeDtypeStruct((rows, B_WIDTH), BF16),
        scratch_shapes=[pltpu.VMEM((n_chains, V_ROWS, B_QW), F32),
                        pltpu.VMEM((2, n_chains, B_TK, B_QW), BF16), pltpu.VMEM((2, n_chains, 1, B_QW), F32)],
        compiler_params=_arbitrary(2),
        name="attn_b_prompt",
    )(qbt, kb, vbt, lamp, gcol)


def _attn_b_sample_kernel(q_ref, kn_ref, vn_ref, kc_ref, vc_ref, lamp_ref, g_ref, o_ref):
    t = q_ref.shape[0]
    lam = _lambda(lamp_ref)
    g = g_ref[...]
    for h in range(H_B):
        sl = slice(h * LANES, (h + 1) * LANES)
        qz, _ = _pair_queries(q_ref[:, sl])
        s_past = _dot_nt(qz, kc_ref[0, :, sl].astype(BF16))
        s_new = _dot_nt(qz, kn_ref[:, sl])
        m = jnp.maximum(jnp.max(s_past, axis=-1, keepdims=True), jnp.max(s_new, axis=-1, keepdims=True))
        e_past = jnp.exp(s_past - m)
        e_new = jnp.exp(s_new - m)
        l = jnp.sum(e_past, axis=-1, keepdims=True) + jnp.sum(e_new, axis=-1, keepdims=True)
        o = (_dot(e_past.astype(BF16), vc_ref[0, :, sl].astype(BF16))
             + _dot(e_new.astype(BF16), vn_ref[:, sl])) / l
        o_ref[:, sl] = _diff_finish(o[:t], o[t:], lam, g)


def _attn_b_sample(qb, kb, vb, cache_k, cache_v, lamp, g, t):
    rows = qb.shape[0]
    streams, past, width = cache_k.shape
    assert past % CHUNK == 0 and t <= CHUNK
    row = pl.BlockSpec((t, B_WIDTH), lambda s: (s, 0))
    cache = pl.BlockSpec((1, past, width), lambda s: (s, 0, 0))
    return pl.pallas_call(
        _attn_b_sample_kernel,
        grid=(streams,),
        in_specs=[row, row, row, cache, cache, _const_spec(lamp.shape), _const_spec(g.shape)],
        out_specs=row,
        out_shape=jax.ShapeDtypeStruct((rows, B_WIDTH), BF16),
        compiler_params=_arbitrary(1),
        name="attn_b_sample",
    )(qb, kb, vb, cache_k, cache_v, lamp, g)


def _mem_kv_kernel(x_ref, wk_ref, wv_ref, kf_ref, vf_ref):
    xb = x_ref[...].astype(BF16)
    kf_ref[...] = _dot(xb, wk_ref[...])
    vf_ref[...] = _dot(xb, wv_ref[...])


def _mem_kv(mem, wk_bf, wv_bf):
    f = jax.ShapeDtypeStruct(mem.shape, F32)
    return pl.pallas_call(
        _mem_kv_kernel,
        out_shape=[f, f],
        name="mem_kv",
    )(mem, wk_bf, wv_bf)


def _layer_norm(x, g, b):
    mu = jnp.mean(x, axis=-1, keepdims=True)
    xc = x - mu
    var = jnp.mean(xc * xc, axis=-1, keepdims=True)
    return xc * lax.rsqrt(var + LN_EPS) * g + b


def _tail_kernel(x_ref, oa_ref, ob_ref, mk_ref, mv_ref, cin_ref,
                 wo_ref, ln1g_ref, ln1b_ref, wmq_ref, wmo_ref, ln2g_ref, ln2b_ref,
                 wup_ref, cw_ref, cb_ref, wdn_ref, ln3g_ref, ln3b_ref,
                 y_ref, carry_ref, *, seg, sub):
    tm = x_ref.shape[0]
    n_seg = tm // seg
    n_sub = tm // sub
    assert n_seg == 1 or n_sub == 1
    subs = [slice(i * sub, (i + 1) * sub) for i in range(n_sub)]

    @pl.when(pl.program_id(0) == 0)
    def _():
        carry_ref[...] = cin_ref[...]

    def mem_attend(q, mk, mv):
        heads = []
        for h in range(H_M):
            sl = slice(h * HD_M, (h + 1) * HD_M)
            s = _dot_nt(q[:, sl], mk[:, sl])
            e = jnp.exp(s - jnp.max(s, axis=-1, keepdims=True))
            l = jnp.sum(e, axis=-1, keepdims=True)
            heads.append((_dot(e.astype(BF16), mv[:, sl]) / l).astype(BF16))
        return jnp.concatenate(heads, axis=1)

    def mem_attend_rows(q):
        if mk_ref.shape[0] == 1:
            return mem_attend(q, mk_ref[0].astype(BF16), mv_ref[0].astype(BF16))
        return jnp.concatenate([mem_attend(q[sg * seg:(sg + 1) * seg],
                                           mk_ref[sg].astype(BF16), mv_ref[sg].astype(BF16))
                                for sg in range(n_seg)], axis=0)

    mix = [_dot(oa_ref[sl, :], wo_ref[0:A_WIDTH, :]) + _dot(ob_ref[sl, :], wo_ref[A_WIDTH:A_WIDTH + B_WIDTH, :])
           for sl in subs]
    x1 = [_layer_norm(DEEPNORM_ALPHA * x_ref[sl, :] + m, ln1g_ref[...], ln1b_ref[...]) for sl, m in zip(subs, mix)]
    qm = [(_dot(v.astype(BF16), wmq_ref[...]) * (HD_M ** -0.5)).astype(BF16) for v in x1]
    om = [mem_attend_rows(q) for q in qm]
    mo = [_dot(v, wmo_ref[...]) for v in om]
    x2 = [_layer_norm(DEEPNORM_ALPHA * a + b, ln2g_ref[...], ln2b_ref[...]) for a, b in zip(x1, mo)]
    x2b = [v.astype(BF16) for v in x2]

    def conv(us, col, width):
        cols = slice(col, col + width)
        w0, w1, w2, b = cw_ref[0:1, cols], cw_ref[1:2, cols], cw_ref[2:3, cols], cb_ref[0:1, cols]

        def taps(u, p2, p1):
            row = lax.broadcasted_iota(jnp.int32, u.shape, 0)
            u1 = jnp.where(row == 0, p1, pltpu.roll(u, 1, axis=0))
            u2 = jnp.where(row == 0, p2, jnp.where(row == 1, p1, pltpu.roll(u, 2, axis=0)))
            return u2 * w0 + u1 * w1 + u * w2 + b

        def state(sg):
            base = SUBLANES * sg
            return carry_ref[base + 6:base + 7, cols], carry_ref[base + 7:base + 8, cols]

        if n_seg == 1:
            p2, p1 = state(0)
            outs = []
            for u in us:
                outs.append(taps(u, p2, p1))
                p2, p1 = u[sub - 2:sub - 1], u[sub - 1:sub]
            carry_ref[0:SUBLANES, cols] = us[-1][sub - SUBLANES:sub]
            return outs
        (u,) = us
        outs = []
        for sg in range(n_seg):
            useg = u[sg * seg:(sg + 1) * seg]
            outs.append(taps(useg, *state(sg)))
            carry_ref[SUBLANES * sg:SUBLANES * (sg + 1), cols] = useg[seg - SUBLANES:seg]
        return [jnp.concatenate(outs, axis=0)]

    f = [jnp.zeros((sub, D_MODEL), F32) for _ in subs]
    c0 = 0
    for width in FF_BLOCKS:
        gate = conv([_dot(v, wup_ref[:, c0:c0 + width]) for v in x2b], c0, width)
        val = conv([_dot(v, wup_ref[:, D_FF + c0:D_FF + c0 + width]) for v in x2b], D_FF + c0, width)
        hid = [(g * (1.0 / (1.0 + jnp.exp(-g))) * v).astype(BF16) for g, v in zip(gate, val)]
        f = [a + _dot(h, wdn_ref[c0:c0 + width, :]) for a, h in zip(f, hid)]
        c0 += width
    for sl, a, b in zip(subs, x2, f):
        y_ref[sl, :] = _layer_norm(DEEPNORM_ALPHA * a + b, ln3g_ref[...], ln3b_ref[...])


def _tail(x, oa, ob, mk, mv, conv_in, params, tm, seg, sub):
    rows = x.shape[0]
    n_seg = tm // seg
    assert tm % sub == 0 and sub % SUBLANES == 0
    assert conv_in.shape == (SUBLANES * n_seg, 2 * D_FF)
    assert n_seg == 1 or rows == tm
    assert mk.shape[0] in (1, n_seg)
    row_spec = lambda w: pl.BlockSpec((tm, w), lambda i: (i, 0))
    carry_spec = pl.BlockSpec(conv_in.shape, lambda i: (0, 0))
    return pl.pallas_call(
        functools.partial(_tail_kernel, seg=seg, sub=sub),
        grid=(rows // tm,),
        in_specs=[row_spec(D_MODEL), row_spec(A_WIDTH), row_spec(B_WIDTH),
                  _const_spec(mk.shape), _const_spec(mv.shape),
                  _const_spec(conv_in.shape)] + [_const_spec(p.shape) for p in params],
        out_specs=[row_spec(D_MODEL), carry_spec],
        out_shape=[jax.ShapeDtypeStruct((rows, D_MODEL), F32),
                   jax.ShapeDtypeStruct(conv_in.shape, F32)],
        compiler_params=_arbitrary(1),
        name="tail",
    )(x, oa, ob, mk, mv, conv_in, *params)


def kernel(x_prompt, x_sample, mem_prompt, cache_a_k, cache_a_v, cache_b_k, cache_b_v, cache_mem_k, cache_mem_v,
           state_conv, w_qkv, rel_bias, lambda_q1, lambda_k1, lambda_q2, lambda_k2, subln_g, w_o, ln1_g, ln1_b,
           w_mq, w_mk, w_mv, w_mo, ln2_g, ln2_b, w_up, conv_w, conv_b, w_down, ln3_g, ln3_b):
    assert w_qkv.shape[0] == DEPTH == 1
    batch, seq, _ = x_prompt.shape
    streams, t_new, _ = x_sample.shape
    past_b = cache_b_k.shape[2]
    assert batch == 1 and seq % QKV_TM == 0 and QKV_TM == BAND_PREV * CHUNK

    bf = lambda w: w[0].astype(BF16)
    row = lambda v: v[0].astype(F32)[None, :]
    w_qkv_bf = bf(w_qkv)
    tail_params = (bf(w_o), row(ln1_g), row(ln1_b), bf(w_mq), bf(w_mo), row(ln2_g), row(ln2_b),
                   bf(w_up), conv_w[0].astype(F32), row(conv_b), bf(w_down), row(ln3_g), row(ln3_b))
    lamp = jnp.stack([lambda_q1[0], lambda_k1[0], lambda_q2[0], lambda_k2[0]]).astype(F32)
    g = row(subln_g)
    gcol = jnp.broadcast_to(subln_g[0].astype(F32)[:, None], (2 * HD_B, B_QW))
    bias = _bias_row(rel_bias[0])

    xp = x_prompt.reshape(seq, D_MODEL)
    qa, ka, va, qbt, kb, vbt, ka_f, va_f, kb_f, vb_f = _qkv(xp, w_qkv_bf, jnp.arange(seq), QKV_TM, True)
    oa = _attn_a_prompt(qa, ka, va, bias)
    ob = _attn_b_prompt(qbt, kb, vbt, lamp, gcol)
    mk_p, mv_p = _mem_kv(mem_prompt.reshape(N_MEM, D_MODEL), bf(w_mk), bf(w_mv))
    conv0 = jnp.zeros((SUBLANES, 2 * D_FF), F32)
    y_p, conv_p = _tail(xp, oa, ob, mk_p[None], mv_p[None], conv0, tail_params, TAIL_TM, TAIL_TM, TAIL_SUB)

    rows_s = streams * t_new
    xs = x_sample.reshape(rows_s, D_MODEL)
    pos_s = jnp.tile(past_b + jnp.arange(t_new), streams)
    qa_s, ka_s, va_s, qb_s, kb_s, vb_s, ka_sf, va_sf, kb_sf, vb_sf = _qkv(xs, w_qkv_bf, pos_s, rows_s, False)
    oa_s = _attn_a_sample(qa_s, ka_s, va_s,
                          cache_a_k[0].reshape(streams, -1, A_WIDTH), cache_a_v[0].reshape(streams, -1, A_WIDTH),
                          bias, t_new)
    ob_s = _attn_b_sample(qb_s, kb_s, vb_s,
                          cache_b_k[0].reshape(streams, past_b, B_WIDTH),
                          cache_b_v[0].reshape(streams, past_b, B_WIDTH), lamp, g, t_new)
    conv_in_s = jnp.pad(state_conv[0].astype(F32), ((0, 0), (SUBLANES - (CONV_W - 1), 0), (0, 0)))
    y_s, conv_s = _tail(xs, oa_s, ob_s,
                        cache_mem_k[0].reshape(streams, N_MEM, D_MODEL).astype(BF16),
                        cache_mem_v[0].reshape(streams, N_MEM, D_MODEL).astype(BF16),
                        conv_in_s.reshape(streams * SUBLANES, 2 * D_FF), tail_params, rows_s, t_new, rows_s)

    keep = CONV_W - 1
    return (
        y_p.reshape(batch, seq, D_MODEL),
        y_s.reshape(streams, t_new, D_MODEL),
        ka_f.reshape(1, batch, QKV_TM, H_A, HD_A),
        va_f.reshape(1, batch, QKV_TM, H_A, HD_A),
        kb_f.reshape(1, batch, seq, H_B, 2, HD_B),
        vb_f.reshape(1, batch, seq, H_B, 2 * HD_B),
        mk_p.reshape(1, batch, N_MEM, H_M, HD_M),
        mv_p.reshape(1, batch, N_MEM, H_M, HD_M),
        conv_p[SUBLANES - keep:].reshape(1, batch, keep, 2 * D_FF),
        ka_sf.reshape(1, streams, t_new, H_A, HD_A),
        va_sf.reshape(1, streams, t_new, H_A, HD_A),
        kb_sf.reshape(1, streams, t_new, H_B, 2, HD_B),
        vb_sf.reshape(1, streams, t_new, H_B, 2 * HD_B),
        conv_s.reshape(streams, SUBLANES, 2 * D_FF)[:, SUBLANES - keep:].reshape(1, streams, keep, 2 * D_FF),
    )
```

```python
import functools
import math

import numpy as np
import jax
import jax.numpy as jnp
from jax import lax
from jax.experimental import pallas as pl
from jax.experimental.pallas import tpu as pltpu

F32 = jnp.float32
BF16 = jnp.bfloat16

D_MODEL = 1024
CHUNK = 64
BAND_PREV = 8
REL_CLIP = 128
H_A = 8
HD_A = 64
H_B = 4
HD_B = 64
ROT_DIM = HD_B // 4
ROPE_THETA = 500000.0
N_MEM = 256
H_M = 4
HD_M = D_MODEL // H_M
D_FF = 2816
CONV_W = 3
LN_EPS = 1e-5
DEPTH = 1
DEEPNORM_ALPHA = (2.0 * DEPTH) ** 0.25
A_WIDTH = H_A * HD_A
B_WIDTH = H_B * 2 * HD_B
LAM_INIT = 0.8 - 0.6 * math.exp(-0.3 * 0)

LANES = 128
SUBLANES = 8
NEG = -1e30
LOG2_E = math.log2(math.e)

QKV_TM = 512
A_TQ = 256
B_TQ = 1024
B_QW = 256
B_HEADS = 2
B_TK = 512
V_ROWS = LANES + 16
BIAS_SPAN = 1024
TAIL_TM = 512
TAIL_SUB = 256
MXU_DIM = 256
FF_BLOCKS = (4 * MXU_DIM, 4 * MXU_DIM, D_FF - 8 * MXU_DIM)
assert sum(FF_BLOCKS) == D_FF and all(w % MXU_DIM == 0 for w in FF_BLOCKS)

_NT = (((1,), (1,)), ((), ()))


def _dot(a, b):
    return jnp.dot(a, b, preferred_element_type=F32)


def _dot_nt(a, b):
    return lax.dot_general(a, b, _NT, preferred_element_type=F32)


def _const_spec(shape):
    nd = len(shape)
    return pl.BlockSpec(shape, lambda *_: (0,) * nd, pipeline_mode=pl.Buffered(1))


def _arbitrary(n):
    return pltpu.CompilerParams(dimension_semantics=("arbitrary",) * n)


def _rope(z, cos, sin):
    lane = lax.broadcasted_iota(jnp.int32, (z.shape[0], LANES), 1)
    first_half = (lane % HD_B) < (ROT_DIM // 2)
    outs = []
    for c in range(z.shape[1] // LANES):
        zc = z[:, c * LANES:(c + 1) * LANES]
        partner = jnp.where(first_half,
                            pltpu.roll(zc, LANES - ROT_DIM // 2, axis=1),
                            pltpu.roll(zc, ROT_DIM // 2, axis=1))
        outs.append(zc * cos + partner * sin)
    return jnp.concatenate(outs, axis=1)


def _qkv_kernel(x_ref, w_ref, cos_ref, sin_ref,
                qa_ref, ka_ref, va_ref, qb_ref, kb_ref, vb_ref,
                kaf_ref, vaf_ref, kbf_ref, vbf_ref, *, transposed_b):
    xb = x_ref[...].astype(BF16)
    cos = cos_ref[...]
    sin = sin_ref[...]
    tm = xb.shape[0]

    def section(i):
        return _dot(xb, w_ref[:, i * A_WIDTH:(i + 1) * A_WIDTH])

    qa = section(0)
    qa_ref[...] = (qa * (HD_A ** -0.5 * LOG2_E)).astype(BF16)
    ka = section(1)
    ka_ref[...] = ka.astype(BF16)
    kaf_ref[...] = ka
    va = section(2)
    va_ref[...] = va.astype(BF16)
    vaf_ref[...] = va
    qb = _rope(section(3), cos, sin) * (HD_B ** -0.5 * (LOG2_E if transposed_b else 1.0))
    kb = _rope(section(4), cos, sin)
    kb_ref[...] = kb.astype(BF16)
    kbf_ref[...] = kb
    vb = section(5)
    vbf_ref[...] = vb
    if transposed_b:
        ones = (lax.broadcasted_iota(jnp.int32, (V_ROWS - LANES, tm), 0) == 0).astype(BF16)
        for h in range(H_B):
            sl = slice(h * LANES, (h + 1) * LANES)
            qb_ref[h] = qb[:, sl].T.astype(BF16)
            vb_ref[h, 0, 0:LANES, :] = vb[:, sl].T.astype(BF16)
            vb_ref[h, 0, LANES:V_ROWS, :] = ones
    else:
        qb_ref[...] = qb.astype(BF16)
        vb_ref[...] = vb.astype(BF16)


def _rope_tables(pos):
    half = ROT_DIM // 2
    inv = ROPE_THETA ** (-jnp.arange(0, ROT_DIM, 2, dtype=F32) / ROT_DIM)
    ang = pos.astype(F32)[:, None] * inv[None, :]
    cos, sin = jnp.cos(ang), jnp.sin(ang)
    n = pos.shape[0]
    rest = HD_B - ROT_DIM
    cos64 = jnp.concatenate([cos, cos, jnp.ones((n, rest), F32)], axis=1)
    sin64 = jnp.concatenate([-sin, sin, jnp.zeros((n, rest), F32)], axis=1)
    reps = LANES // HD_B
    return jnp.tile(cos64, (1, reps)), jnp.tile(sin64, (1, reps))


def _qkv(x, w_bf, pos, tm, transposed_b):
    rows = x.shape[0]
    nt = rows // tm
    cos, sin = _rope_tables(pos)
    row_spec = lambda w: pl.BlockSpec((tm, w), lambda i: (i, 0))
    last_spec = pl.BlockSpec((tm, A_WIDTH), lambda i: (0, 0))
    bf = jax.ShapeDtypeStruct((rows, A_WIDTH), BF16)
    if transposed_b:
        assert tm == B_TK
        qb_spec = pl.BlockSpec((H_B, LANES, tm), lambda i: (0, 0, i))
        qb_shape = jax.ShapeDtypeStruct((H_B, LANES, rows), BF16)
        vb_spec = pl.BlockSpec((H_B, 1, V_ROWS, tm), lambda i: (0, i, 0, 0))
        vb_shape = jax.ShapeDtypeStruct((H_B, nt, V_ROWS, tm), BF16)
    else:
        qb_spec = vb_spec = row_spec(B_WIDTH)
        qb_shape = vb_shape = bf
    return pl.pallas_call(
        functools.partial(_qkv_kernel, transposed_b=transposed_b),
        grid=(nt,),
        in_specs=[row_spec(D_MODEL), _const_spec(w_bf.shape), row_spec(LANES), row_spec(LANES)],
        out_specs=[row_spec(A_WIDTH)] * 3 + [qb_spec, row_spec(B_WIDTH), vb_spec]
                  + [last_spec, last_spec, row_spec(B_WIDTH), row_spec(B_WIDTH)],
        out_shape=[bf] * 3 + [qb_shape, bf, vb_shape] + [jax.ShapeDtypeStruct((tm, A_WIDTH), F32)] * 2
                  + [jax.ShapeDtypeStruct((rows, B_WIDTH), F32)] * 2,
        compiler_params=_arbitrary(1),
        name="qkv_proj",
    )(x, w_bf, cos, sin)


def _bias_row(rel_bias):
    table = rel_bias.astype(F32)
    far = table[:, 2 * REL_CLIP:]
    near = table[:, :1]
    n_far = 2 * A_TQ - REL_CLIP
    n_near = 3 * A_TQ - (n_far + 2 * REL_CLIP + 1)
    h = table.shape[0]
    return jnp.concatenate([jnp.broadcast_to(far, (h, n_far)), table[:, ::-1],
                            jnp.broadcast_to(near, (h, n_near)),
                            jnp.broadcast_to(far, (h, BIAS_SPAN - 3 * A_TQ))], axis=1)


def _build_band_bias(row_ref, bias_ref):
    _, rows, cols = bias_ref.shape
    q = lax.broadcasted_iota(jnp.int32, (rows, cols), 0)
    k = lax.broadcasted_iota(jnp.int32, (rows, cols), 1)
    qc = q // CHUNK
    kc = k // CHUNK - (2 * A_TQ) // CHUNK
    band = (kc >= qc - BAND_PREV) & (kc <= qc)
    for h in range(H_A):
        r = jnp.broadcast_to(row_ref[h:h + 1, :], (rows, BIAS_SPAN))
        toeplitz = pltpu.roll(r, 0, axis=1, stride=1, stride_axis=0)
        bias_ref[h] = jnp.where(band, toeplitz[:, :cols] * LOG2_E, NEG)


def _pair_queries(q2):
    lane = lax.broadcasted_iota(jnp.int32, q2.shape, 1)
    lo = lane < (LANES // 2)
    zero = jnp.zeros_like(q2)
    return jnp.concatenate([jnp.where(lo, q2, zero), jnp.where(lo, zero, q2)], axis=0), lo


def _attn_a_kernel(q_ref, k2_ref, k1_ref, k0_ref, v2_ref, v1_ref, v0_ref, row_ref, o_ref, bias_ref):
    i = pl.program_id(0)

    @pl.when(i == 0)
    def _():
        _build_band_bias(row_ref, bias_ref)

    kidx = lax.broadcasted_iota(jnp.int32, (1, 3 * A_TQ), 1)
    kvalid = (kidx + (i - 2) * A_TQ) >= 0
    ones = jnp.ones((3 * A_TQ, LANES), BF16)
    n_pairs = H_A // 2
    s, lo = [None] * n_pairs, [None] * n_pairs

    def scores(p):
        sl = slice(p * LANES, (p + 1) * LANES)
        qz, lo[p] = _pair_queries(q_ref[:, sl])
        kc = jnp.concatenate([k2_ref[:, sl], k1_ref[:, sl], k0_ref[:, sl]], axis=0)
        b = jnp.concatenate([bias_ref[2 * p], bias_ref[2 * p + 1]], axis=0)
        s[p] = jnp.where(kvalid, _dot_nt(qz, kc) + b, NEG)

    def values(p):
        sl = slice(p * LANES, (p + 1) * LANES)
        e = jnp.exp2(s[p] - jnp.max(s[p], axis=-1, keepdims=True)).astype(BF16)
        vc = jnp.concatenate([v2_ref[:, sl], v1_ref[:, sl], v0_ref[:, sl]], axis=0)
        r = _dot(e, jnp.concatenate([vc, ones], axis=1))
        o = r[:, :LANES] / r[:, LANES:LANES + 1]
        o_ref[:, sl] = jnp.where(lo[p], o[:A_TQ], o[A_TQ:]).astype(BF16)

    for p in range(n_pairs + 1):
        if p < n_pairs:
            scores(p)
        if p >= 1:
            values(p - 1)


def _attn_a_prompt(qa, ka, va, bias_row):
    rows = qa.shape[0]
    cur = pl.BlockSpec((A_TQ, A_WIDTH), lambda i: (i, 0))
    prev1 = pl.BlockSpec((A_TQ, A_WIDTH), lambda i: (jnp.maximum(i - 1, 0), 0))
    prev2 = pl.BlockSpec((A_TQ, A_WIDTH), lambda i: (jnp.maximum(i - 2, 0), 0))
    return pl.pallas_call(
        _attn_a_kernel,
        grid=(rows // A_TQ,),
        in_specs=[cur, prev2, prev1, cur, prev2, prev1, cur, _const_spec(bias_row.shape)],
        out_specs=cur,
        out_shape=jax.ShapeDtypeStruct((rows, A_WIDTH), BF16),
        scratch_shapes=[pltpu.VMEM((H_A, A_TQ, 3 * A_TQ), F32)],
        compiler_params=_arbitrary(1),
        name="attn_a_prompt",
    )(qa, ka, ka, ka, va, va, va, bias_row)


def _attn_a_sample_kernel(q_ref, kn_ref, vn_ref, kc_ref, vc_ref, row_ref, o_ref, bias_ref):
    t = q_ref.shape[0]
    past = kc_ref.shape[1]

    @pl.when(pl.program_id(0) == 0)
    def _():
        _build_band_bias(row_ref, bias_ref)

    kc = kc_ref[0].astype(BF16)
    vc = vc_ref[0].astype(BF16)
    for p in range(H_A // 2):
        sl = slice(p * LANES, (p + 1) * LANES)
        qz, lo = _pair_queries(q_ref[:, sl])
        b_past = jnp.concatenate([bias_ref[2 * p, 0:t, 0:past], bias_ref[2 * p + 1, 0:t, 0:past]], axis=0)
        b_new = jnp.concatenate([bias_ref[2 * p, 0:t, past:past + t],
                                 bias_ref[2 * p + 1, 0:t, past:past + t]], axis=0)
        s_past = _dot_nt(qz, kc[:, sl]) + b_past
        s_new = _dot_nt(qz, kn_ref[:, sl]) + b_new
        m = jnp.maximum(jnp.max(s_past, axis=-1, keepdims=True), jnp.max(s_new, axis=-1, keepdims=True))
        e_past = jnp.exp2(s_past - m)
        e_new = jnp.exp2(s_new - m)
        l = jnp.sum(e_past, axis=-1, keepdims=True) + jnp.sum(e_new, axis=-1, keepdims=True)
        o = (_dot(e_past.astype(BF16), vc[:, sl]) + _dot(e_new.astype(BF16), vn_ref[:, sl])) / l
        o_ref[:, sl] = jnp.where(lo, o[:t], o[t:]).astype(BF16)


def _attn_a_sample(qa, ka, va, cache_k, cache_v, bias_row, t):
    rows = qa.shape[0]
    streams, past, width = cache_k.shape
    assert past == 2 * A_TQ and t <= CHUNK and t % SUBLANES == 0
    row = pl.BlockSpec((t, A_WIDTH), lambda s: (s, 0))
    cache = pl.BlockSpec((1, past, width), lambda s: (s, 0, 0))
    return pl.pallas_call(
        _attn_a_sample_kernel,
        grid=(streams,),
        in_specs=[row, row, row, cache, cache, _const_spec(bias_row.shape)],
        out_specs=row,
        out_shape=jax.ShapeDtypeStruct((rows, A_WIDTH), BF16),
        scratch_shapes=[pltpu.VMEM((H_A, t, past + LANES), F32)],
        compiler_params=_arbitrary(1),
        name="attn_a_sample",
    )(qa, ka, va, cache_k, cache_v, bias_row)


def _lambda(lamp_ref):
    lp = lamp_ref[...]
    a = jnp.sum(lp[0:1] * lp[1:2], axis=-1, keepdims=True)
    b = jnp.sum(lp[2:3] * lp[3:4], axis=-1, keepdims=True)
    return jnp.exp(a) - jnp.exp(b) + LAM_INIT


def _diff_finish(o0, o1, lam, g):
    o = o0 - lam * o1
    o = o * lax.rsqrt(jnp.mean(o * o, axis=-1, keepdims=True) + LN_EPS) * g * (1.0 - LAM_INIT)
    return o.astype(BF16)


def _attn_b_kernel(qt_ref, k_ref, vt_ref, lamp_ref, gcol_ref, o_ref, acc_ref, p_ref, alpha_ref):
    qi = pl.program_id(1)
    n_parts = B_TQ // B_QW
    dim = lax.broadcasted_iota(jnp.int32, (LANES, B_QW), 0)
    zero = jnp.zeros((LANES, B_QW), BF16)
    chains = [(part, (h, mp)) for part in range(n_parts) for h in range(B_HEADS) for mp in range(2)]
    qz = []
    for part, (h, mp) in chains:
        qt = qt_ref[h, :, part * B_QW:(part + 1) * B_QW]
        qz.append(jnp.where(dim < HD_B if mp == 0 else dim >= HD_B, qt, zero))
    acc_ref[...] = jnp.zeros(acc_ref.shape, F32)
    p_ref[1] = jnp.zeros(p_ref.shape[1:], BF16)
    alpha_ref[1] = jnp.ones(alpha_ref.shape[1:], F32)

    def pending_values(c, j_prev, buf):
        h = chains[c][1][0]
        acc_ref[c] = alpha_ref[buf, c] * acc_ref[c] + _dot(vt_ref[h, j_prev], p_ref[buf, c])

    def step(j, ms, buf, parts, diag_parts, pending_parts):
        start = pl.multiple_of(j * B_TK, B_TK)
        j_prev = jnp.maximum(j - 1, 0)
        kpos = start + lax.broadcasted_iota(jnp.int32, (B_TK, B_QW), 0)
        lane = lax.broadcasted_iota(jnp.int32, (B_TK, B_QW), 1)
        out = list(ms)
        for c, (part, (h, _)) in enumerate(chains):
            if part in parts:
                s = _dot(k_ref[pl.ds(start, B_TK), h * LANES:(h + 1) * LANES], qz[c])
                if part in diag_parts:
                    qpos = qi * B_TQ + part * B_QW + lane
                    s = jnp.where(kpos < (qpos // CHUNK + 1) * CHUNK, s, NEG)
                out[c] = jnp.maximum(ms[c], jnp.max(s, axis=0, keepdims=True))
                alpha_ref[buf, c] = jnp.exp2(ms[c] - out[c])
                p_ref[buf, c] = jnp.exp2(s - out[c]).astype(BF16)
            if part in pending_parts:
                pending_values(c, j_prev, 1 - buf)
        return tuple(out)

    every = tuple(range(n_parts))
    low, high = every[:n_parts // 2], every[n_parts // 2:]

    def pair(i, ms):
        ms = step(2 * i, ms, 0, every, (), every)
        return step(2 * i + 1, ms, 1, every, (), every)

    m_init = tuple(jnp.full((1, B_QW), NEG, F32) for _ in chains)
    ms = lax.fori_loop(0, qi, pair, m_init)
    ms = step(2 * qi, ms, 0, every, low, every)
    step(2 * qi + 1, ms, 1, high, high, every)
    for c, (part, _) in enumerate(chains):
        if part in high:
            pending_values(c, 2 * qi + 1, 1)

    lam = _lambda(lamp_ref)
    for c in range(0, len(chains), 2):
        part, (h, _) = chains[c]
        a0 = acc_ref[c]
        a1 = acc_ref[c + 1]
        o = a0[0:LANES] / a0[LANES:LANES + 1] - lam * (a1[0:LANES] / a1[LANES:LANES + 1])
        o = o * lax.rsqrt(jnp.mean(o * o, axis=0, keepdims=True) + LN_EPS) * gcol_ref[...] * (1.0 - LAM_INIT)
        o_ref[part * B_QW:(part + 1) * B_QW, h * LANES:(h + 1) * LANES] = o.T.astype(BF16)


def _attn_b_prompt(qbt, kb, vbt, lamp, gcol):
    rows = kb.shape[0]
    assert rows % B_TQ == 0 and B_TQ == 2 * B_TK and B_TK == 2 * B_QW and B_QW % CHUNK == 0
    assert H_B % B_HEADS == 0
    n_chains = (B_TQ // B_QW) * B_HEADS * 2
    single = pl.Buffered(1)
    return pl.pallas_call(
        _attn_b_kernel,
        grid=(H_B // B_HEADS, rows // B_TQ),
        in_specs=[pl.BlockSpec((B_HEADS, LANES, B_TQ), lambda h, i: (h, 0, i)),
                  pl.BlockSpec((rows, B_HEADS * LANES), lambda h, i: (0, h), pipeline_mode=single),
                  pl.BlockSpec((B_HEADS,) + vbt.shape[1:], lambda h, i: (h, 0, 0, 0), pipeline_mode=single),
                  _const_spec(lamp.shape), _const_spec(gcol.shape)],
        out_specs=pl.BlockSpec((B_TQ, B_HEADS * LANES), lambda h, i: (i, h)),
        out_shape=jax.ShapeDtypeStruct((rows, B_WIDTH), BF16),
        scratch_shapes=[pltpu.VMEM((n_chains, V_ROWS, B_QW), F32),
                        pltpu.VMEM((2, n_chains, B_TK, B_QW), BF16), pltpu.VMEM((2, n_chains, 1, B_QW), F32)],
        compiler_params=_arbitrary(2),
        name="attn_b_prompt",
    )(qbt, kb, vbt, lamp, gcol)


def _attn_b_sample_kernel(q_ref, kn_ref, vn_ref, kc_ref, vc_ref, lamp_ref, g_ref, o_ref):
    t = q_ref.shape[0]
    lam = _lambda(lamp_ref)
    g = g_ref[...]
    for h in range(H_B):
        sl = slice(h * LANES, (h + 1) * LANES)
        qz, _ = _pair_queries(q_ref[:, sl])
        s_past = _dot_nt(qz, kc_ref[0, :, sl].astype(BF16))
        s_new = _dot_nt(qz, kn_ref[:, sl])
        m = jnp.maximum(jnp.max(s_past, axis=-1, keepdims=True), jnp.max(s_new, axis=-1, keepdims=True))
        e_past = jnp.exp(s_past - m)
        e_new = jnp.exp(s_new - m)
        l = jnp.sum(e_past, axis=-1, keepdims=True) + jnp.sum(e_new, axis=-1, keepdims=True)
        o = (_dot(e_past.astype(BF16), vc_ref[0, :, sl].astype(BF16))
             + _dot(e_new.astype(BF16), vn_ref[:, sl])) / l
        o_ref[:, sl] = _diff_finish(o[:t], o[t:], lam, g)


def _attn_b_sample(qb, kb, vb, cache_k, cache_v, lamp, g, t):
    rows = qb.shape[0]
    streams, past, width = cache_k.shape
    assert past % CHUNK == 0 and t <= CHUNK
    row = pl.BlockSpec((t, B_WIDTH), lambda s: (s, 0))
    cache = pl.BlockSpec((1, past, width), lambda s: (s, 0, 0))
    return pl.pallas_call(
        _attn_b_sample_kernel,
        grid=(streams,),
        in_specs=[row, row, row, cache, cache, _const_spec(lamp.shape), _const_spec(g.shape)],
        out_specs=row,
        out_shape=jax.ShapeDtypeStruct((rows, B_WIDTH), BF16),
        compiler_params=_arbitrary(1),
        name="attn_b_sample",
    )(qb, kb, vb, cache_k, cache_v, lamp, g)


def _mem_kv_kernel(x_ref, wk_ref, wv_ref, kf_ref, vf_ref):
    xb = x_ref[...].astype(BF16)
    kf_ref[...] = _dot(xb, wk_ref[...])
    vf_ref[...] = _dot(xb, wv_ref[...])


def _mem_kv(mem, wk_bf, wv_bf):
    f = jax.ShapeDtypeStruct(mem.shape, F32)
    return pl.pallas_call(
        _mem_kv_kernel,
        out_shape=[f, f],
        name="mem_kv",
    )(mem, wk_bf, wv_bf)


def _layer_norm(x, g, b):
    mu = jnp.mean(x, axis=-1, keepdims=True)
    xc = x - mu
    var = jnp.mean(xc * xc, axis=-1, keepdims=True)
    return xc * lax.rsqrt(var + LN_EPS) * g + b


def _tail_kernel(x_ref, oa_ref, ob_ref, mk_ref, mv_ref, cin_ref,
                 wo_ref, ln1g_ref, ln1b_ref, wmq_ref, wmo_ref, ln2g_ref, ln2b_ref,
                 wup_ref, cw_ref, cb_ref, wdn_ref, ln3g_ref, ln3b_ref,
                 y_ref, carry_ref, *, seg, sub):
    tm = x_ref.shape[0]
    n_seg = tm // seg
    n_sub = tm // sub
    assert n_seg == 1 or n_sub == 1
    subs = [slice(i * sub, (i + 1) * sub) for i in range(n_sub)]

    @pl.when(pl.program_id(0) == 0)
    def _():
        carry_ref[...] = cin_ref[...]

    def mem_attend(q, mk, mv):
        heads = []
        for h in range(H_M):
            sl = slice(h * HD_M, (h + 1) * HD_M)
            s = _dot_nt(q[:, sl], mk[:, sl])
            e = jnp.exp(s - jnp.max(s, axis=-1, keepdims=True))
            l = jnp.sum(e, axis=-1, keepdims=True)
            heads.append((_dot(e.astype(BF16), mv[:, sl]) / l).astype(BF16))
        return jnp.concatenate(heads, axis=1)

    def mem_attend_rows(q):
        if mk_ref.shape[0] == 1:
            return mem_attend(q, mk_ref[0].astype(BF16), mv_ref[0].astype(BF16))
        return jnp.concatenate([mem_attend(q[sg * seg:(sg + 1) * seg],
                                           mk_ref[sg].astype(BF16), mv_ref[sg].astype(BF16))
                                for sg in range(n_seg)], axis=0)

    mix = [_dot(oa_ref[sl, :], wo_ref[0:A_WIDTH, :]) + _dot(ob_ref[sl, :], wo_ref[A_WIDTH:A_WIDTH + B_WIDTH, :])
           for sl in subs]
    x1 = [_layer_norm(DEEPNORM_ALPHA * x_ref[sl, :] + m, ln1g_ref[...], ln1b_ref[...]) for sl, m in zip(subs, mix)]
    qm = [(_dot(v.astype(BF16), wmq_ref[...]) * (HD_M ** -0.5)).astype(BF16) for v in x1]
    om = [mem_attend_rows(q) for q in qm]
    mo = [_dot(v, wmo_ref[...]) for v in om]
    x2 = [_layer_norm(DEEPNORM_ALPHA * a + b, ln2g_ref[...], ln2b_ref[...]) for a, b in zip(x1, mo)]
    x2b = [v.astype(BF16) for v in x2]

    def conv(us, col, width):
        cols = slice(col, col + width)
        w0, w1, w2, b = cw_ref[0:1, cols], cw_ref[1:2, cols], cw_ref[2:3, cols], cb_ref[0:1, cols]

        def taps(u, p2, p1):
            row = lax.broadcasted_iota(jnp.int32, u.shape, 0)
            u1 = jnp.where(row == 0, p1, pltpu.roll(u, 1, axis=0))
            u2 = jnp.where(row == 0, p2, jnp.where(row == 1, p1, pltpu.roll(u, 2, axis=0)))
            return u2 * w0 + u1 * w1 + u * w2 + b

        def state(sg):
            base = SUBLANES * sg
            return carry_ref[base + 6:base + 7, cols], carry_ref[base + 7:base + 8, cols]

        if n_seg == 1:
            p2, p1 = state(0)
            outs = []
            for u in us:
                outs.append(taps(u, p2, p1))
                p2, p1 = u[sub - 2:sub - 1], u[sub - 1:sub]
            carry_ref[0:SUBLANES, cols] = us[-1][sub - SUBLANES:sub]
            return outs
        (u,) = us
        outs = []
        for sg in range(n_seg):
            useg = u[sg * seg:(sg + 1) * seg]
            outs.append(taps(useg, *state(sg)))
            carry_ref[SUBLANES * sg:SUBLANES * (sg + 1), cols] = useg[seg - SUBLANES:seg]
        return [jnp.concatenate(outs, axis=0)]

    f = [jnp.zeros((sub, D_MODEL), F32) for _ in subs]
    c0 = 0
    for width in FF_BLOCKS:
        gate = conv([_dot(v, wup_ref[:, c0:c0 + width]) for v in x2b], c0, width)
        val = conv([_dot(v, wup_ref[:, D_FF + c0:D_FF + c0 + width]) for v in x2b], D_FF + c0, width)
        hid = [(g * (1.0 / (1.0 + jnp.exp(-g))) * v).astype(BF16) for g, v in zip(gate, val)]
        f = [a + _dot(h, wdn_ref[c0:c0 + width, :]) for a, h in zip(f, hid)]
        c0 += width
    for sl, a, b in zip(subs, x2, f):
        y_ref[sl, :] = _layer_norm(DEEPNORM_ALPHA * a + b, ln3g_ref[...], ln3b_ref[...])


def _tail(x, oa, ob, mk, mv, conv_in, params, tm, seg, sub):
    rows = x.shape[0]
    n_seg = tm // seg
    assert tm % sub == 0 and sub % SUBLANES == 0
    assert conv_in.shape == (SUBLANES * n_seg, 2 * D_FF)
    assert n_seg == 1 or rows == tm
    assert mk.shape[0] in (1, n_seg)
    row_spec = lambda w: pl.BlockSpec((tm, w), lambda i: (i, 0))
    carry_spec = pl.BlockSpec(conv_in.shape, lambda i: (0, 0))
    return pl.pallas_call(
        functools.partial(_tail_kernel, seg=seg, sub=sub),
        grid=(rows // tm,),
        in_specs=[row_spec(D_MODEL), row_spec(A_WIDTH), row_spec(B_WIDTH),
                  _const_spec(mk.shape), _const_spec(mv.shape),
                  _const_spec(conv_in.shape)] + [_const_spec(p.shape) for p in params],
        out_specs=[row_spec(D_MODEL), carry_spec],
        out_shape=[jax.ShapeDtypeStruct((rows, D_MODEL), F32),
                   jax.ShapeDtypeStruct(conv_in.shape, F32)],
        compiler_params=_arbitrary(1),
        name="tail",
    )(x, oa, ob, mk, mv, conv_in, *params)


def kernel(x_prompt, x_sample, mem_prompt, cache_a_k, cache_a_v, cache_b_k, cache_b_v, cache_mem_k, cache_mem_v,
           state_conv, w_qkv, rel_bias, lambda_q1, lambda_k1, lambda_q2, lambda_k2, subln_g, w_o, ln1_g, ln1_b,
           w_mq, w_mk, w_mv, w_mo, ln2_g, ln2_b, w_up, conv_w, conv_b, w_down, ln3_g, ln3_b):
    assert w_qkv.shape[0] == DEPTH == 1
    batch, seq, _ = x_prompt.shape
    streams, t_new, _ = x_sample.shape
    past_b = cache_b_k.shape[2]
    assert batch == 1 and seq % QKV_TM == 0 and QKV_TM == BAND_PREV * CHUNK

    bf = lambda w: w[0].astype(BF16)
    row = lambda v: v[0].astype(F32)[None, :]
    w_qkv_bf = bf(w_qkv)
    tail_params = (bf(w_o), row(ln1_g), row(ln1_b), bf(w_mq), bf(w_mo), row(ln2_g), row(ln2_b),
                   bf(w_up), conv_w[0].astype(F32), row(conv_b), bf(w_down), row(ln3_g), row(ln3_b))
    lamp = jnp.stack([lambda_q1[0], lambda_k1[0], lambda_q2[0], lambda_k2[0]]).astype(F32)
    g = row(subln_g)
    gcol = jnp.broadcast_to(subln_g[0].astype(F32)[:, None], (2 * HD_B, B_QW))
    bias = _bias_row(rel_bias[0])

    xp = x_prompt.reshape(seq, D_MODEL)
    qa, ka, va, qbt, kb, vbt, ka_f, va_f, kb_f, vb_f = _qkv(xp, w_qkv_bf, jnp.arange(seq), QKV_TM, True)
    oa = _attn_a_prompt(qa, ka, va, bias)
    ob = _attn_b_prompt(qbt, kb, vbt, lamp, gcol)
    mk_p, mv_p = _mem_kv(mem_prompt.reshape(N_MEM, D_MODEL), bf(w_mk), bf(w_mv))
    conv0 = jnp.zeros((SUBLANES, 2 * D_FF), F32)
    y_p, conv_p = _tail(xp, oa, ob, mk_p[None], mv_p[None], conv0, tail_params, TAIL_TM, TAIL_TM, TAIL_SUB)

    rows_s = streams * t_new
    xs = x_sample.reshape(rows_s, D_MODEL)
    pos_s = jnp.tile(past_b + jnp.arange(t_new), streams)
    qa_s, ka_s, va_s, qb_s, kb_s, vb_s, ka_sf, va_sf, kb_sf, vb_sf = _qkv(xs, w_qkv_bf, pos_s, rows_s, False)
    oa_s = _attn_a_sample(qa_s, ka_s, va_s,
                          cache_a_k[0].reshape(streams, -1, A_WIDTH), cache_a_v[0].reshape(streams, -1, A_WIDTH),
                          bias, t_new)
    ob_s = _attn_b_sample(qb_s, kb_s, vb_s,
                          cache_b_k[0].reshape(streams, past_b, B_WIDTH),
                          cache_b_v[0].reshape(streams, past_b, B_WIDTH), lamp, g, t_new)
    conv_in_s = jnp.pad(state_conv[0].astype(F32), ((0, 0), (SUBLANES - (CONV_W - 1), 0), (0, 0)))
    y_s, conv_s = _tail(xs, oa_s, ob_s,
                        cache_mem_k[0].reshape(streams, N_MEM, D_MODEL).astype(BF16),
                        cache_mem_v[0].reshape(streams, N_MEM, D_MODEL).astype(BF16),
                        conv_in_s.reshape(streams * SUBLANES, 2 * D_FF), tail_params, rows_s, t_new, rows_s)

    keep = CONV_W - 1
    return (
        y_p.reshape(batch, seq, D_MODEL),
        y_s.reshape(streams, t_new, D_MODEL),
        ka_f.reshape(1, batch, QKV_TM, H_A, HD_A),
        va_f.reshape(1, batch, QKV_TM, H_A, HD_A),
        kb_f.reshape(1, batch, seq, H_B, 2, HD_B),
        vb_f.reshape(1, batch, seq, H_B, 2 * HD_B),
        mk_p.reshape(1, batch, N_MEM, H_M, HD_M),
        mv_p.reshape(1, batch, N_MEM, H_M, HD_M),
        conv_p[SUBLANES - keep:].reshape(1, batch, keep, 2 * D_FF),
        ka_sf.reshape(1, streams, t_new, H_A, HD_A),
        va_sf.reshape(1, streams, t_new, H_A, HD_A),
        kb_sf.reshape(1, streams, t_new, H_B, 2, HD_B),
        vb_sf.reshape(1, streams, t_new, H_B, 2 * HD_B),
        conv_s.reshape(streams, SUBLANES, 2 * D_FF)[:, SUBLANES - keep:].reshape(1, streams, keep, 2 * D_FF),
    )
```

```python
import functools
import math

import numpy as np
import jax
import jax.numpy as jnp
from jax import lax
from jax.experimental import pallas as pl
from jax.experimental.pallas import tpu as pltpu

F32 = jnp.float32
BF16 = jnp.bfloat16

D_MODEL = 1024
CHUNK = 64
BAND_PREV = 8
REL_CLIP = 128
H_A = 8
HD_A = 64
H_B = 4
HD_B = 64
ROT_DIM = HD_B // 4
ROPE_THETA = 500000.0
N_MEM = 256
H_M = 4
HD_M = D_MODEL // H_M
D_FF = 2816
CONV_W = 3
LN_EPS = 1e-5
DEPTH = 1
DEEPNORM_ALPHA = (2.0 * DEPTH) ** 0.25
A_WIDTH = H_A * HD_A
B_WIDTH = H_B * 2 * HD_B
LAM_INIT = 0.8 - 0.6 * math.exp(-0.3 * 0)

LANES = 128
SUBLANES = 8
NEG = -1e30
LOG2_E = math.log2(math.e)

QKV_TM = 512
A_TQ = 256
B_TQ = 1024
B_QW = 256
B_HEADS = 2
B_TK = 512
V_ROWS = LANES + 16
BIAS_SPAN = 1024
TAIL_TM = 512
TAIL_SUB = 256
MXU_DIM = 256
FF_BLOCKS = (4 * MXU_DIM, 4 * MXU_DIM, D_FF - 8 * MXU_DIM)
assert sum(FF_BLOCKS) == D_FF and all(w % MXU_DIM == 0 for w in FF_BLOCKS)

_NT = (((1,), (1,)), ((), ()))


def _dot(a, b):
    return jnp.dot(a, b, preferred_element_type=F32)


def _dot_nt(a, b):
    return lax.dot_general(a, b, _NT, preferred_element_type=F32)


def _const_spec(shape):
    nd = len(shape)
    return pl.BlockSpec(shape, lambda *_: (0,) * nd, pipeline_mode=pl.Buffered(1))


def _arbitrary(n):
    return pltpu.CompilerParams(dimension_semantics=("arbitrary",) * n)


def _rope(z, cos, sin):
    lane = lax.broadcasted_iota(jnp.int32, (z.shape[0], LANES), 1)
    first_half = (lane % HD_B) < (ROT_DIM // 2)
    outs = []
    for c in range(z.shape[1] // LANES):
        zc = z[:, c * LANES:(c + 1) * LANES]
        partner = jnp.where(first_half,
                            pltpu.roll(zc, LANES - ROT_DIM // 2, axis=1),
                            pltpu.roll(zc, ROT_DIM // 2, axis=1))
        outs.append(zc * cos + partner * sin)
    return jnp.concatenate(outs, axis=1)


def _qkv_kernel(x_ref, w_ref, cos_ref, sin_ref,
                qa_ref, ka_ref, va_ref, qb_ref, kb_ref, vb_ref,
                kaf_ref, vaf_ref, kbf_ref, vbf_ref, *, transposed_b):
    xb = x_ref[...].astype(BF16)
    cos = cos_ref[...]
    sin = sin_ref[...]
    tm = xb.shape[0]

    def section(i):
        return _dot(xb, w_ref[:, i * A_WIDTH:(i + 1) * A_WIDTH])

    vb = section(5)
    for h in range(H_B):
        vbf_ref[:, h, :] = vb[:, h * LANES:(h + 1) * LANES]
    if transposed_b:
        ones = (lax.broadcasted_iota(jnp.int32, (V_ROWS - LANES, tm), 0) == 0).astype(BF16)
        for h in range(H_B):
            vb_ref[h, 0, 0:LANES, :] = vb[:, h * LANES:(h + 1) * LANES].T.astype(BF16)
            vb_ref[h, 0, LANES:V_ROWS, :] = ones
    else:
        vb_ref[...] = vb.astype(BF16)
    qb = _rope(section(3), cos, sin) * (HD_B ** -0.5 * (LOG2_E if transposed_b else 1.0))
    if transposed_b:
        for h in range(H_B):
            qb_ref[h] = qb[:, h * LANES:(h + 1) * LANES].T.astype(BF16)
    else:
        qb_ref[...] = qb.astype(BF16)
    kb = _rope(section(4), cos, sin)
    kb_ref[...] = kb.astype(BF16)
    kbf_ref[...] = kb
    qa = section(0)
    qa_ref[...] = (qa * (HD_A ** -0.5 * LOG2_E)).astype(BF16)
    ka = section(1)
    ka_ref[...] = ka.astype(BF16)
    kaf_ref[...] = ka
    va = section(2)
    va_ref[...] = va.astype(BF16)
    vaf_ref[...] = va


def _rope_tables(pos):
    half = ROT_DIM // 2
    inv = ROPE_THETA ** (-jnp.arange(0, ROT_DIM, 2, dtype=F32) / ROT_DIM)
    ang = pos.astype(F32)[:, None] * inv[None, :]
    cos, sin = jnp.cos(ang), jnp.sin(ang)
    lane = np.arange(LANES) % HD_B
    freq = np.arange(half)[:, None]
    place = (lane[None, :] < ROT_DIM) & (lane[None, :] % half == freq)
    sign = np.where(lane < half, -1.0, 1.0)[None, :]
    spread = lambda t, m: jnp.dot(t, jnp.asarray(m, F32), precision=lax.Precision.HIGHEST)
    return (spread(cos, place) + jnp.asarray(lane >= ROT_DIM, F32)[None, :], spread(sin, place * sign))


def _qkv(x, w_bf, pos, tm, transposed_b):
    rows = x.shape[0]
    nt = rows // tm
    cos, sin = _rope_tables(pos)
    row_spec = lambda w: pl.BlockSpec((tm, w), lambda i: (i, 0))
    last_spec = pl.BlockSpec((tm, A_WIDTH), lambda i: (0, 0))
    bf = jax.ShapeDtypeStruct((rows, A_WIDTH), BF16)
    if transposed_b:
        assert tm == B_TK
        qb_spec = pl.BlockSpec((H_B, LANES, tm), lambda i: (0, 0, i))
        qb_shape = jax.ShapeDtypeStruct((H_B, LANES, rows), BF16)
        vb_spec = pl.BlockSpec((H_B, 1, V_ROWS, tm), lambda i: (0, i, 0, 0))
        vb_shape = jax.ShapeDtypeStruct((H_B, nt, V_ROWS, tm), BF16)
    else:
        qb_spec = vb_spec = row_spec(B_WIDTH)
        qb_shape = vb_shape = bf
    return pl.pallas_call(
        functools.partial(_qkv_kernel, transposed_b=transposed_b),
        grid=(nt,),
        in_specs=[row_spec(D_MODEL), _const_spec(w_bf.shape), row_spec(LANES), row_spec(LANES)],
        out_specs=[row_spec(A_WIDTH)] * 3 + [qb_spec, row_spec(B_WIDTH), vb_spec]
                  + [last_spec, last_spec, row_spec(B_WIDTH),
                     pl.BlockSpec((tm, H_B, 2 * HD_B), lambda i: (i, 0, 0))],
        out_shape=[bf] * 3 + [qb_shape, bf, vb_shape] + [jax.ShapeDtypeStruct((tm, A_WIDTH), F32)] * 2
                  + [jax.ShapeDtypeStruct((rows, B_WIDTH), F32),
                     jax.ShapeDtypeStruct((rows, H_B, 2 * HD_B), F32)],
        compiler_params=_arbitrary(1),
        name="qkv_proj",
    )(x, w_bf, cos, sin)


def _bias_row(rel_bias):
    table = rel_bias.astype(F32)
    far = table[:, 2 * REL_CLIP:]
    near = table[:, :1]
    n_far = 2 * A_TQ - REL_CLIP
    n_near = 3 * A_TQ - (n_far + 2 * REL_CLIP + 1)
    h = table.shape[0]
    return jnp.concatenate([jnp.broadcast_to(far, (h, n_far)), table[:, ::-1],
                            jnp.broadcast_to(near, (h, n_near)),
                            jnp.broadcast_to(far, (h, BIAS_SPAN - 3 * A_TQ))], axis=1)


def _build_band_bias(row_ref, bias_ref):
    _, rows, cols = bias_ref.shape
    q = lax.broadcasted_iota(jnp.int32, (rows, cols), 0)
    k = lax.broadcasted_iota(jnp.int32, (rows, cols), 1)
    qc = q // CHUNK
    kc = k // CHUNK - (2 * A_TQ) // CHUNK
    band = (kc >= qc - BAND_PREV) & (kc <= qc)
    for h in range(H_A):
        r = jnp.broadcast_to(row_ref[h:h + 1, :], (rows, BIAS_SPAN))
        toeplitz = pltpu.roll(r, 0, axis=1, stride=1, stride_axis=0)
        bias_ref[h] = jnp.where(band, toeplitz[:, :cols] * LOG2_E, NEG)


def _pair_queries(q2):
    lane = lax.broadcasted_iota(jnp.int32, q2.shape, 1)
    lo = lane < (LANES // 2)
    zero = jnp.zeros_like(q2)
    return jnp.concatenate([jnp.where(lo, q2, zero), jnp.where(lo, zero, q2)], axis=0), lo


def _attn_a_kernel(q_ref, k2_ref, k1_ref, k0_ref, v2_ref, v1_ref, v0_ref, row_ref, o_ref, bias_ref):
    i = pl.program_id(0)

    @pl.when(i == 0)
    def _():
        _build_band_bias(row_ref, bias_ref)

    kidx = lax.broadcasted_iota(jnp.int32, (1, 3 * A_TQ), 1)
    kvalid = (kidx + (i - 2) * A_TQ) >= 0
    ones = jnp.ones((3 * A_TQ, LANES), BF16)
    n_pairs = H_A // 2
    s, lo = [None] * n_pairs, [None] * n_pairs

    def scores(p):
        sl = slice(p * LANES, (p + 1) * LANES)
        qz, lo[p] = _pair_queries(q_ref[:, sl])
        kc = jnp.concatenate([k2_ref[:, sl], k1_ref[:, sl], k0_ref[:, sl]], axis=0)
        b = jnp.concatenate([bias_ref[2 * p], bias_ref[2 * p + 1]], axis=0)
        s[p] = jnp.where(kvalid, _dot_nt(qz, kc) + b, NEG)

    def values(p):
        sl = slice(p * LANES, (p + 1) * LANES)
        e = jnp.exp2(s[p] - jnp.max(s[p], axis=-1, keepdims=True)).astype(BF16)
        vc = jnp.concatenate([v2_ref[:, sl], v1_ref[:, sl], v0_ref[:, sl]], axis=0)
        r = _dot(e, jnp.concatenate([vc, ones], axis=1))
        o = r[:, :LANES] / r[:, LANES:LANES + 1]
        o_ref[:, sl] = jnp.where(lo[p], o[:A_TQ], o[A_TQ:]).astype(BF16)

    for p in range(n_pairs + 1):
        if p < n_pairs:
            scores(p)
        if p >= 1:
            values(p - 1)


def _attn_a_prompt(qa, ka, va, bias_row):
    rows = qa.shape[0]
    cur = pl.BlockSpec((A_TQ, A_WIDTH), lambda i: (i, 0))
    prev1 = pl.BlockSpec((A_TQ, A_WIDTH), lambda i: (jnp.maximum(i - 1, 0), 0))
    prev2 = pl.BlockSpec((A_TQ, A_WIDTH), lambda i: (jnp.maximum(i - 2, 0), 0))
    return pl.pallas_call(
        _attn_a_kernel,
        grid=(rows // A_TQ,),
        in_specs=[cur, prev2, prev1, cur, prev2, prev1, cur, _const_spec(bias_row.shape)],
        out_specs=cur,
        out_shape=jax.ShapeDtypeStruct((rows, A_WIDTH), BF16),
        scratch_shapes=[pltpu.VMEM((H_A, A_TQ, 3 * A_TQ), F32)],
        compiler_params=_arbitrary(1),
        name="attn_a_prompt",
    )(qa, ka, ka, ka, va, va, va, bias_row)


def _attn_a_sample_kernel(q_ref, kn_ref, vn_ref, kc_ref, vc_ref, row_ref, o_ref, bias_ref):
    t = q_ref.shape[0]
    past = kc_ref.shape[1]

    @pl.when(pl.program_id(0) == 0)
    def _():
        _build_band_bias(row_ref, bias_ref)

    kc = kc_ref[0].astype(BF16)
    vc = vc_ref[0].astype(BF16)
    for p in range(H_A // 2):
        sl = slice(p * LANES, (p + 1) * LANES)
        qz, lo = _pair_queries(q_ref[:, sl])
        b_past = jnp.concatenate([bias_ref[2 * p, 0:t, 0:past], bias_ref[2 * p + 1, 0:t, 0:past]], axis=0)
        b_new = jnp.concatenate([bias_ref[2 * p, 0:t, past:past + t],
                                 bias_ref[2 * p + 1, 0:t, past:past + t]], axis=0)
        s_past = _dot_nt(qz, kc[:, sl]) + b_past
        s_new = _dot_nt(qz, kn_ref[:, sl]) + b_new
        m = jnp.maximum(jnp.max(s_past, axis=-1, keepdims=True), jnp.max(s_new, axis=-1, keepdims=True))
        e_past = jnp.exp2(s_past - m)
        e_new = jnp.exp2(s_new - m)
        l = jnp.sum(e_past, axis=-1, keepdims=True) + jnp.sum(e_new, axis=-1, keepdims=True)
        o = (_dot(e_past.astype(BF16), vc[:, sl]) + _dot(e_new.astype(BF16), vn_ref[:, sl])) / l
        o_ref[:, sl] = jnp.where(lo, o[:t], o[t:]).astype(BF16)


def _attn_a_sample(qa, ka, va, cache_k, cache_v, bias_row, t):
    rows = qa.shape[0]
    streams, past, width = cache_k.shape
    assert past == 2 * A_TQ and t <= CHUNK and t % SUBLANES == 0
    row = pl.BlockSpec((t, A_WIDTH), lambda s: (s, 0))
    cache = pl.BlockSpec((1, past, width), lambda s: (s, 0, 0))
    return pl.pallas_call(
        _attn_a_sample_kernel,
        grid=(streams,),
        in_specs=[row, row, row, cache, cache, _const_spec(bias_row.shape)],
        out_specs=row,
        out_shape=jax.ShapeDtypeStruct((rows, A_WIDTH), BF16),
        scratch_shapes=[pltpu.VMEM((H_A, t, past + LANES), F32)],
        compiler_params=_arbitrary(1),
        name="attn_a_sample",
    )(qa, ka, va, cache_k, cache_v, bias_row)


def _lambda(lamp_ref):
    lp = lamp_ref[...]
    a = jnp.sum(lp[0:1] * lp[1:2], axis=-1, keepdims=True)
    b = jnp.sum(lp[2:3] * lp[3:4], axis=-1, keepdims=True)
    return jnp.exp(a) - jnp.exp(b) + LAM_INIT


def _diff_finish(o0, o1, lam, g):
    o = o0 - lam * o1
    o = o * lax.rsqrt(jnp.mean(o * o, axis=-1, keepdims=True) + LN_EPS) * g * (1.0 - LAM_INIT)
    return o.astype(BF16)


def _attn_b_kernel(qt_ref, k_ref, vt_ref, lamp_ref, gcol_ref, o_ref, acc_ref, p_ref, alpha_ref):
    qi = pl.program_id(1)
    n_parts = B_TQ // B_QW
    dim = lax.broadcasted_iota(jnp.int32, (LANES, B_QW), 0)
    zero = jnp.zeros((LANES, B_QW), BF16)
    chains = [(part, (h, mp)) for part in range(n_parts) for h in range(B_HEADS) for mp in range(2)]
    qz = []
    for part, (h, mp) in chains:
        qt = qt_ref[h, :, part * B_QW:(part + 1) * B_QW]
        qz.append(jnp.where(dim < HD_B if mp == 0 else dim >= HD_B, qt, zero))
    acc_ref[...] = jnp.zeros(acc_ref.shape, F32)
    p_ref[1] = jnp.zeros(p_ref.shape[1:], BF16)
    alpha_ref[1] = jnp.ones(alpha_ref.shape[1:], F32)

    def pending_values(c, j_prev, buf):
        h = chains[c][1][0]
        acc_ref[c] = alpha_ref[buf, c] * acc_ref[c] + _dot(vt_ref[h, j_prev], p_ref[buf, c])

    def step(j, ms, buf, parts, diag_parts, pending_parts):
        start = pl.multiple_of(j * B_TK, B_TK)
        j_prev = jnp.maximum(j - 1, 0)
        kpos = start + lax.broadcasted_iota(jnp.int32, (B_TK, B_QW), 0)
        lane = lax.broadcasted_iota(jnp.int32, (B_TK, B_QW), 1)
        out = list(ms)
        for c, (part, (h, _)) in enumerate(chains):
            if part in parts:
                s = _dot(k_ref[pl.ds(start, B_TK), h * LANES:(h + 1) * LANES], qz[c])
                if part in diag_parts:
                    qpos = qi * B_TQ + part * B_QW + lane
                    s = jnp.where(kpos < (qpos // CHUNK + 1) * CHUNK, s, NEG)
                out[c] = jnp.maximum(ms[c], jnp.max(s, axis=0, keepdims=True))
                alpha_ref[buf, c] = jnp.exp2(ms[c] - out[c])
                p_ref[buf, c] = jnp.exp2(s - out[c]).astype(BF16)
            if part in pending_parts:
                pending_values(c, j_prev, 1 - buf)
        return tuple(out)

    every = tuple(range(n_parts))
    low, high = every[:n_parts // 2], every[n_parts // 2:]

    def pair(i, ms):
        ms = step(2 * i, ms, 0, every, (), every)
        return step(2 * i + 1, ms, 1, every, (), every)

    m_init = tuple(jnp.full((1, B_QW), NEG, F32) for _ in chains)
    ms = lax.fori_loop(0, qi, pair, m_init)
    ms = step(2 * qi, ms, 0, every, low, every)
    step(2 * qi + 1, ms, 1, high, high, every)
    for c, (part, _) in enumerate(chains):
        if part in high:
            pending_values(c, 2 * qi + 1, 1)

    lam = _lambda(lamp_ref)
    for c in range(0, len(chains), 2):
        part, (h, _) = chains[c]
        a0 = acc_ref[c]
        a1 = acc_ref[c + 1]
        o = a0[0:LANES] / a0[LANES:LANES + 1] - lam * (a1[0:LANES] / a1[LANES:LANES + 1])
        o = o * lax.rsqrt(jnp.mean(o * o, axis=0, keepdims=True) + LN_EPS) * gcol_ref[...] * (1.0 - LAM_INIT)
        o_ref[part * B_QW:(part + 1) * B_QW, h * LANES:(h + 1) * LANES] = o.T.astype(BF16)


def _attn_b_prompt(qbt, kb, vbt, lamp, gcol):
    rows = kb.shape[0]
    assert rows % B_TQ == 0 and B_TQ == 2 * B_TK and B_TK == 2 * B_QW and B_QW % CHUNK == 0
    assert H_B % B_HEADS == 0
    n_chains = (B_TQ // B_QW) * B_HEADS * 2
    single = pl.Buffered(1)
    return pl.pallas_call(
        _attn_b_kernel,
        grid=(H_B // B_HEADS, rows // B_TQ),
        in_specs=[pl.BlockSpec((B_HEADS, LANES, B_TQ), lambda h, i: (h, 0, i)),
                  pl.BlockSpec((rows, B_HEADS * LANES), lambda h, i: (0, h), pipeline_mode=single),
                  pl.BlockSpec((B_HEADS,) + vbt.shape[1:], lambda h, i: (h, 0, 0, 0), pipeline_mode=single),
                  _const_spec(lamp.shape), _const_spec(gcol.shape)],
        out_specs=pl.BlockSpec((B_TQ, B_HEADS * LANES), lambda h, i: (i, h)),
        out_shape=jax.ShapeDtypeStruct((rows, B_WIDTH), BF16),
        scratch_shapes=[pltpu.VMEM((n_chains, V_ROWS, B_QW), F32),
                        pltpu.VMEM((2, n_chains, B_TK, B_QW), BF16), pltpu.VMEM((2, n_chains, 1, B_QW), F32)],
        compiler_params=_arbitrary(2),
        name="attn_b_prompt",
    )(qbt, kb, vbt, lamp, gcol)


def _attn_b_sample_kernel(q_ref, kn_ref, vn_ref, kc_ref, vc_ref, lamp_ref, g_ref, o_ref):
    t = q_ref.shape[0]
    lam = _lambda(lamp_ref)
    g = g_ref[...]
    for h in range(H_B):
        sl = slice(h * LANES, (h + 1) * LANES)
        qz, _ = _pair_queries(q_ref[:, sl])
        s_past = _dot_nt(qz, kc_ref[0, :, sl].astype(BF16))
        s_new = _dot_nt(qz, kn_ref[:, sl])
        m = jnp.maximum(jnp.max(s_past, axis=-1, keepdims=True), jnp.max(s_new, axis=-1, keepdims=True))
        e_past = jnp.exp(s_past - m)
        e_new = jnp.exp(s_new - m)
        l = jnp.sum(e_past, axis=-1, keepdims=True) + jnp.sum(e_new, axis=-1, keepdims=True)
        o = (_dot(e_past.astype(BF16), vc_ref[0, :, h, :].astype(BF16))
             + _dot(e_new.astype(BF16), vn_ref[:, sl])) / l
        o_ref[:, sl] = _diff_finish(o[:t], o[t:], lam, g)


def _attn_b_sample(qb, kb, vb, cache_k, cache_v, lamp, g, t):
    rows = qb.shape[0]
    streams, past, width = cache_k.shape
    assert past % CHUNK == 0 and t <= CHUNK
    row = pl.BlockSpec((t, B_WIDTH), lambda s: (s, 0))
    cache = pl.BlockSpec((1, past, width), lambda s: (s, 0, 0))
    cache_v_spec = pl.BlockSpec((1,) + cache_v.shape[1:], lambda s: (s, 0, 0, 0))
    return pl.pallas_call(
        _attn_b_sample_kernel,
        grid=(streams,),
        in_specs=[row, row, row, cache, cache_v_spec, _const_spec(lamp.shape), _const_spec(g.shape)],
        out_specs=row,
        out_shape=jax.ShapeDtypeStruct((rows, B_WIDTH), BF16),
        compiler_params=_arbitrary(1),
        name="attn_b_sample",
    )(qb, kb, vb, cache_k, cache_v, lamp, g)


def _mem_kv_kernel(x_ref, wk_ref, wv_ref, kf_ref, vf_ref):
    xb = x_ref[...].astype(BF16)
    kf_ref[...] = _dot(xb, wk_ref[...])
    vf_ref[...] = _dot(xb, wv_ref[...])


def _mem_kv(mem, wk_bf, wv_bf):
    f = jax.ShapeDtypeStruct(mem.shape, F32)
    return pl.pallas_call(
        _mem_kv_kernel,
        out_shape=[f, f],
        name="mem_kv",
    )(mem, wk_bf, wv_bf)


def _layer_norm(x, g, b):
    mu = jnp.mean(x, axis=-1, keepdims=True)
    xc = x - mu
    var = jnp.mean(xc * xc, axis=-1, keepdims=True)
    return xc * lax.rsqrt(var + LN_EPS) * g + b


def _tail_kernel(x_ref, oa_ref, ob_ref, mk_ref, mv_ref, cin_ref,
                 wo_ref, ln1g_ref, ln1b_ref, wmq_ref, wmo_ref, ln2g_ref, ln2b_ref,
                 wup_ref, cw_ref, cb_ref, wdn_ref, ln3g_ref, ln3b_ref,
                 y_ref, carry_ref, *, seg, sub):
    tm = x_ref.shape[0]
    n_seg = tm // seg
    n_sub = tm // sub
    assert n_seg == 1 or n_sub == 1
    subs = [slice(i * sub, (i + 1) * sub) for i in range(n_sub)]

    @pl.when(pl.program_id(0) == 0)
    def _():
        carry_ref[...] = cin_ref[...]

    def mem_attend(q, mk, mv):
        heads = []
        for h in range(H_M):
            sl = slice(h * HD_M, (h + 1) * HD_M)
            s = _dot_nt(q[:, sl], mk[:, sl])
            e = jnp.exp(s - jnp.max(s, axis=-1, keepdims=True))
            l = jnp.sum(e, axis=-1, keepdims=True)
            heads.append((_dot(e.astype(BF16), mv[:, sl]) / l).astype(BF16))
        return jnp.concatenate(heads, axis=1)

    def mem_attend_rows(q):
        if mk_ref.shape[0] == 1:
            return mem_attend(q, mk_ref[0].astype(BF16), mv_ref[0].astype(BF16))
        return jnp.concatenate([mem_attend(q[sg * seg:(sg + 1) * seg],
                                           mk_ref[sg].astype(BF16), mv_ref[sg].astype(BF16))
                                for sg in range(n_seg)], axis=0)

    mix = [_dot(oa_ref[sl, :], wo_ref[0:A_WIDTH, :]) + _dot(ob_ref[sl, :], wo_ref[A_WIDTH:A_WIDTH + B_WIDTH, :])
           for sl in subs]
    x1 = [_layer_norm(DEEPNORM_ALPHA * x_ref[sl, :] + m, ln1g_ref[...], ln1b_ref[...]) for sl, m in zip(subs, mix)]
    qm = [(_dot(v.astype(BF16), wmq_ref[...]) * (HD_M ** -0.5)).astype(BF16) for v in x1]
    om = [mem_attend_rows(q) for q in qm]
    mo = [_dot(v, wmo_ref[...]) for v in om]
    x2 = [_layer_norm(DEEPNORM_ALPHA * a + b, ln2g_ref[...], ln2b_ref[...]) for a, b in zip(x1, mo)]
    x2b = [v.astype(BF16) for v in x2]

    def conv(us, col, width):
        cols = slice(col, col + width)
        w0, w1, w2, b = cw_ref[0:1, cols], cw_ref[1:2, cols], cw_ref[2:3, cols], cb_ref[0:1, cols]

        def taps(u, p2, p1):
            row = lax.broadcasted_iota(jnp.int32, u.shape, 0)
            u1 = jnp.where(row == 0, p1, pltpu.roll(u, 1, axis=0))
            u2 = jnp.where(row == 0, p2, jnp.where(row == 1, p1, pltpu.roll(u, 2, axis=0)))
            return u2 * w0 + u1 * w1 + u * w2 + b

        def state(sg):
            base = SUBLANES * sg
            return carry_ref[base + 6:base + 7, cols], carry_ref[base + 7:base + 8, cols]

        if n_seg == 1:
            p2, p1 = state(0)
            outs = []
            for u in us:
                outs.append(taps(u, p2, p1))
                p2, p1 = u[sub - 2:sub - 1], u[sub - 1:sub]
            carry_ref[0:SUBLANES, cols] = us[-1][sub - SUBLANES:sub]
            return outs
        (u,) = us
        outs = []
        for sg in range(n_seg):
            useg = u[sg * seg:(sg + 1) * seg]
            outs.append(taps(useg, *state(sg)))
            carry_ref[SUBLANES * sg:SUBLANES * (sg + 1), cols] = useg[seg - SUBLANES:seg]
        return [jnp.concatenate(outs, axis=0)]

    f = [jnp.zeros((sub, D_MODEL), F32) for _ in subs]
    c0 = 0
    for width in FF_BLOCKS:
        gate = conv([_dot(v, wup_ref[:, c0:c0 + width]) for v in x2b], c0, width)
        val = conv([_dot(v, wup_ref[:, D_FF + c0:D_FF + c0 + width]) for v in x2b], D_FF + c0, width)
        hid = [(g * (1.0 / (1.0 + jnp.exp(-g))) * v).astype(BF16) for g, v in zip(gate, val)]
        f = [a + _dot(h, wdn_ref[c0:c0 + width, :]) for a, h in zip(f, hid)]
        c0 += width
    for sl, a, b in zip(subs, x2, f):
        y_ref[sl, :] = _layer_norm(DEEPNORM_ALPHA * a + b, ln3g_ref[...], ln3b_ref[...])


def _tail(x, oa, ob, mk, mv, conv_in, params, tm, seg, sub):
    rows = x.shape[0]
    n_seg = tm // seg
    assert tm % sub == 0 and sub % SUBLANES == 0
    assert conv_in.shape == (SUBLANES * n_seg, 2 * D_FF)
    assert n_seg == 1 or rows == tm
    assert mk.shape[0] in (1, n_seg)
    row_spec = lambda w: pl.BlockSpec((tm, w), lambda i: (i, 0))
    carry_spec = pl.BlockSpec(conv_in.shape, lambda i: (0, 0))
    return pl.pallas_call(
        functools.partial(_tail_kernel, seg=seg, sub=sub),
        grid=(rows // tm,),
        in_specs=[row_spec(D_MODEL), row_spec(A_WIDTH), row_spec(B_WIDTH),
                  _const_spec(mk.shape), _const_spec(mv.shape),
                  _const_spec(conv_in.shape)] + [_const_spec(p.shape) for p in params],
        out_specs=[row_spec(D_MODEL), carry_spec],
        out_shape=[jax.ShapeDtypeStruct((rows, D_MODEL), F32),
                   jax.ShapeDtypeStruct(conv_in.shape, F32)],
        compiler_params=_arbitrary(1),
        name="tail",
    )(x, oa, ob, mk, mv, conv_in, *params)


def kernel(x_prompt, x_sample, mem_prompt, cache_a_k, cache_a_v, cache_b_k, cache_b_v, cache_mem_k, cache_mem_v,
           state_conv, w_qkv, rel_bias, lambda_q1, lambda_k1, lambda_q2, lambda_k2, subln_g, w_o, ln1_g, ln1_b,
           w_mq, w_mk, w_mv, w_mo, ln2_g, ln2_b, w_up, conv_w, conv_b, w_down, ln3_g, ln3_b):
    assert w_qkv.shape[0] == DEPTH == 1
    batch, seq, _ = x_prompt.shape
    streams, t_new, _ = x_sample.shape
    past_b = cache_b_k.shape[2]
    assert batch == 1 and seq % QKV_TM == 0 and QKV_TM == BAND_PREV * CHUNK

    bf = lambda w: w[0].astype(BF16)
    row = lambda v: v[0].astype(F32)[None, :]
    w_qkv_bf = bf(w_qkv)
    tail_params = (bf(w_o), row(ln1_g), row(ln1_b), bf(w_mq), bf(w_mo), row(ln2_g), row(ln2_b),
                   bf(w_up), conv_w[0].astype(F32), row(conv_b), bf(w_down), row(ln3_g), row(ln3_b))
    lamp = jnp.stack([lambda_q1[0], lambda_k1[0], lambda_q2[0], lambda_k2[0]]).astype(F32)
    g = row(subln_g)
    gcol = jnp.broadcast_to(subln_g[0].astype(F32)[:, None], (2 * HD_B, B_QW))
    bias = _bias_row(rel_bias[0])

    xp = x_prompt.reshape(seq, D_MODEL)
    qa, ka, va, qbt, kb, vbt, ka_f, va_f, kb_f, vb_f = _qkv(xp, w_qkv_bf, jnp.arange(seq), QKV_TM, True)
    oa = _attn_a_prompt(qa, ka, va, bias)
    ob = _attn_b_prompt(qbt, kb, vbt, lamp, gcol)
    mk_p, mv_p = _mem_kv(mem_prompt.reshape(N_MEM, D_MODEL), bf(w_mk), bf(w_mv))
    conv0 = jnp.zeros((SUBLANES, 2 * D_FF), F32)
    y_p, conv_p = _tail(xp, oa, ob, mk_p[None], mv_p[None], conv0, tail_params, TAIL_TM, TAIL_TM, TAIL_SUB)

    rows_s = streams * t_new
    xs = x_sample.reshape(rows_s, D_MODEL)
    pos_s = jnp.tile(past_b + jnp.arange(t_new), streams)
    qa_s, ka_s, va_s, qb_s, kb_s, vb_s, ka_sf, va_sf, kb_sf, vb_sf = _qkv(xs, w_qkv_bf, pos_s, rows_s, False)
    oa_s = _attn_a_sample(qa_s, ka_s, va_s,
                          cache_a_k[0].reshape(streams, -1, A_WIDTH), cache_a_v[0].reshape(streams, -1, A_WIDTH),
                          bias, t_new)
    ob_s = _attn_b_sample(qb_s, kb_s, vb_s,
                          cache_b_k[0].reshape(streams, past_b, B_WIDTH),
                          cache_b_v[0], lamp, g, t_new)
    conv_in_s = jnp.pad(state_conv[0].astype(F32), ((0, 0), (SUBLANES - (CONV_W - 1), 0), (0, 0)))
    y_s, conv_s = _tail(xs, oa_s, ob_s,
                        cache_mem_k[0].reshape(streams, N_MEM, D_MODEL).astype(BF16),
                        cache_mem_v[0].reshape(streams, N_MEM, D_MODEL).astype(BF16),
                        conv_in_s.reshape(streams * SUBLANES, 2 * D_FF), tail_params, rows_s, t_new, rows_s)

    keep = CONV_W - 1
    return (
        y_p.reshape(batch, seq, D_MODEL),
        y_s.reshape(streams, t_new, D_MODEL),
        ka_f.reshape(1, batch, QKV_TM, H_A, HD_A),
        va_f.reshape(1, batch, QKV_TM, H_A, HD_A),
        kb_f.reshape(1, batch, seq, H_B, 2, HD_B),
        vb_f.reshape(1, batch, seq, H_B, 2 * HD_B),
        mk_p.reshape(1, batch, N_MEM, H_M, HD_M),
        mv_p.reshape(1, batch, N_MEM, H_M, HD_M),
        conv_p[SUBLANES - keep:].reshape(1, batch, keep, 2 * D_FF),
        ka_sf.reshape(1, streams, t_new, H_A, HD_A),
        va_sf.reshape(1, streams, t_new, H_A, HD_A),
        kb_sf.reshape(1, streams, t_new, H_B, 2, HD_B),
        vb_sf.reshape(1, streams, t_new, H_B, 2 * HD_B),
        conv_s.reshape(streams, SUBLANES, 2 * D_FF)[:, SUBLANES - keep:].reshape(1, streams, keep, 2 * D_FF),
    )
```

```python
import functools
import math

import numpy as np
import jax
import jax.numpy as jnp
from jax import lax
from jax.experimental import pallas as pl
from jax.experimental.pallas import tpu as pltpu

F32 = jnp.float32
BF16 = jnp.bfloat16

D_MODEL = 1024
CHUNK = 64
BAND_PREV = 8
REL_CLIP = 128
H_A = 8
HD_A = 64
H_B = 4
HD_B = 64
ROT_DIM = HD_B // 4
ROPE_THETA = 500000.0
N_MEM = 256
H_M = 4
HD_M = D_MODEL // H_M
D_FF = 2816
CONV_W = 3
LN_EPS = 1e-5
DEPTH = 1
DEEPNORM_ALPHA = (2.0 * DEPTH) ** 0.25
A_WIDTH = H_A * HD_A
B_WIDTH = H_B * 2 * HD_B
LAM_INIT = 0.8 - 0.6 * math.exp(-0.3 * 0)

LANES = 128
SUBLANES = 8
NEG = -1e30
LOG2_E = math.log2(math.e)

QKV_TM = 512
A_TQ = 256
A_LAG = 2
B_TQ = 1024
B_QW = 256
B_HEADS = 2
B_TK = 512
ONES_ROWS = 16
V_ROWS = LANES + ONES_ROWS
VA_ROWS = HD_A + ONES_ROWS
BIAS_SPAN = 1024
TAIL_TM = 512
TAIL_SUB = 256
MXU_DIM = 256
FF_BLOCKS = (4 * MXU_DIM, 4 * MXU_DIM, D_FF - 8 * MXU_DIM)
assert sum(FF_BLOCKS) == D_FF and all(w % MXU_DIM == 0 for w in FF_BLOCKS)

_NT = (((1,), (1,)), ((), ()))


def _dot(a, b):
    return jnp.dot(a, b, preferred_element_type=F32)


def _dot_nt(a, b):
    return lax.dot_general(a, b, _NT, preferred_element_type=F32)


def _const_spec(shape):
    nd = len(shape)
    return pl.BlockSpec(shape, lambda *_: (0,) * nd, pipeline_mode=pl.Buffered(1))


def _arbitrary(n):
    return pltpu.CompilerParams(dimension_semantics=("arbitrary",) * n)


def _rope(z, cos, sin):
    lane = lax.broadcasted_iota(jnp.int32, (z.shape[0], LANES), 1)
    first_half = (lane % HD_B) < (ROT_DIM // 2)
    outs = []
    for c in range(z.shape[1] // LANES):
        zc = z[:, c * LANES:(c + 1) * LANES]
        partner = jnp.where(first_half,
                            pltpu.roll(zc, LANES - ROT_DIM // 2, axis=1),
                            pltpu.roll(zc, ROT_DIM // 2, axis=1))
        outs.append(zc * cos + partner * sin)
    return jnp.concatenate(outs, axis=1)


def _qkv_kernel(x_ref, w_ref, rope_ref,
                qa_ref, ka_ref, va_ref, qb_ref, kb_ref, vb_ref,
                kaf_ref, vaf_ref, kbf_ref, vbf_ref, *, transposed):
    xb = x_ref[...].astype(BF16)
    cos, sin = _spread_rope(rope_ref[...])
    tm = xb.shape[0]

    def section(i):
        return _dot(xb, w_ref[:, i * A_WIDTH:(i + 1) * A_WIDTH])

    vb = section(5)
    for h in range(H_B):
        vbf_ref[:, h, :] = vb[:, h * LANES:(h + 1) * LANES]
    def ones_rows(width):
        return (lax.broadcasted_iota(jnp.int32, (ONES_ROWS, width), 0) == 0).astype(BF16)

    if transposed:
        for h in range(H_B):
            vb_ref[h, 0, 0:LANES, :] = vb[:, h * LANES:(h + 1) * LANES].T.astype(BF16)
            vb_ref[h, 0, LANES:V_ROWS, :] = ones_rows(tm)
    else:
        vb_ref[...] = vb.astype(BF16)
    qb = _rope(section(3), cos, sin) * (HD_B ** -0.5 * (LOG2_E if transposed else 1.0))
    if transposed:
        for h in range(H_B):
            qb_ref[h] = qb[:, h * LANES:(h + 1) * LANES].T.astype(BF16)
    else:
        qb_ref[...] = qb.astype(BF16)
    kb = _rope(section(4), cos, sin)
    kb_ref[...] = kb.astype(BF16)
    kbf_ref[...] = kb
    va = section(2)
    vaf_ref[...] = va
    if transposed:
        for p in range(H_A // 2):
            vt = va[:, p * LANES:(p + 1) * LANES].T.astype(BF16)
            for hh in range(2):
                for t in range(tm // A_TQ):
                    cols = slice(t * A_TQ, (t + 1) * A_TQ)
                    va_ref[2 * p + hh, t, 0:HD_A, :] = vt[hh * HD_A:(hh + 1) * HD_A, cols]
                    va_ref[2 * p + hh, t, HD_A:VA_ROWS, :] = ones_rows(A_TQ)
    else:
        va_ref[...] = va.astype(BF16)
    qa = section(0) * (HD_A ** -0.5 * LOG2_E)
    if transposed:
        for p in range(H_A // 2):
            qa_ref[p] = qa[:, p * LANES:(p + 1) * LANES].T.astype(BF16)
    else:
        qa_ref[...] = qa.astype(BF16)
    ka = section(1)
    ka_ref[...] = ka.astype(BF16)
    kaf_ref[...] = ka


def _rope_table(pos):
    inv = ROPE_THETA ** (-jnp.arange(0, ROT_DIM, 2, dtype=F32) / ROT_DIM)
    ang = pos.astype(F32)[:, None] * inv[None, :]
    return jnp.concatenate([jnp.cos(ang), jnp.sin(ang)], axis=1)


def _spread_rope(tab):
    half = ROT_DIM // 2
    lane = lax.broadcasted_iota(jnp.int32, (tab.shape[0], LANES), 1) % HD_B
    cos = jnp.ones((tab.shape[0], LANES), F32)
    sin = jnp.zeros((tab.shape[0], LANES), F32)
    for f in range(half):
        c = tab[:, f:f + 1]
        s = tab[:, half + f:half + f + 1]
        cos = jnp.where((lane == f) | (lane == half + f), c, cos)
        sin = jnp.where(lane == f, -s, jnp.where(lane == half + f, s, sin))
    return cos, sin


def _qkv(x, w_bf, pos, tm, transposed):
    rows = x.shape[0]
    nt = rows // tm
    rope = _rope_table(pos)
    row_spec = lambda w: pl.BlockSpec((tm, w), lambda i: (i, 0))
    last_spec = pl.BlockSpec((tm, A_WIDTH), lambda i: (0, 0))
    bf = jax.ShapeDtypeStruct((rows, A_WIDTH), BF16)
    if transposed:
        assert tm == B_TK and tm % A_TQ == 0
        qa_spec = pl.BlockSpec((H_A // 2, LANES, tm), lambda i: (0, 0, i))
        qa_shape = jax.ShapeDtypeStruct((H_A // 2, LANES, rows), BF16)
        va_spec = pl.BlockSpec((H_A, tm // A_TQ, VA_ROWS, A_TQ), lambda i: (0, i, 0, 0))
        va_shape = jax.ShapeDtypeStruct((H_A, rows // A_TQ, VA_ROWS, A_TQ), BF16)
        qb_spec = pl.BlockSpec((H_B, LANES, tm), lambda i: (0, 0, i))
        qb_shape = jax.ShapeDtypeStruct((H_B, LANES, rows), BF16)
        vb_spec = pl.BlockSpec((H_B, 1, V_ROWS, tm), lambda i: (0, i, 0, 0))
        vb_shape = jax.ShapeDtypeStruct((H_B, nt, V_ROWS, tm), BF16)
    else:
        qa_spec = va_spec = qb_spec = vb_spec = row_spec(B_WIDTH)
        qa_shape = va_shape = qb_shape = vb_shape = bf
    return pl.pallas_call(
        functools.partial(_qkv_kernel, transposed=transposed),
        grid=(nt,),
        in_specs=[row_spec(D_MODEL), _const_spec(w_bf.shape), row_spec(ROT_DIM)],
        out_specs=[qa_spec, row_spec(A_WIDTH), va_spec, qb_spec, row_spec(B_WIDTH), vb_spec]
                  + [last_spec, last_spec, row_spec(B_WIDTH),
                     pl.BlockSpec((tm, H_B, 2 * HD_B), lambda i: (i, 0, 0))],
        out_shape=[qa_shape, bf, va_shape, qb_shape, bf, vb_shape]
                  + [jax.ShapeDtypeStruct((tm, A_WIDTH), F32)] * 2
                  + [jax.ShapeDtypeStruct((rows, B_WIDTH), F32),
                     jax.ShapeDtypeStruct((rows, H_B, 2 * HD_B), F32)],
        compiler_params=_arbitrary(1),
        name="qkv_proj",
    )(x, w_bf, rope)


def _bias_row(rel_bias):
    table = rel_bias.astype(F32)
    far = table[:, 2 * REL_CLIP:]
    near = table[:, :1]
    n_far = 2 * A_TQ - REL_CLIP
    n_near = 3 * A_TQ - (n_far + 2 * REL_CLIP + 1)
    h = table.shape[0]
    return jnp.concatenate([jnp.broadcast_to(far, (h, n_far)), table[:, ::-1],
                            jnp.broadcast_to(near, (h, n_near)),
                            jnp.broadcast_to(far, (h, BIAS_SPAN - 3 * A_TQ))], axis=1)


def _build_band_bias(row_ref, bias_ref):
    _, rows, cols = bias_ref.shape
    q = lax.broadcasted_iota(jnp.int32, (rows, cols), 0)
    k = lax.broadcasted_iota(jnp.int32, (rows, cols), 1)
    qc = q // CHUNK
    kc = k // CHUNK - (2 * A_TQ) // CHUNK
    band = (kc >= qc - BAND_PREV) & (kc <= qc)
    for h in range(H_A):
        r = jnp.broadcast_to(row_ref[h:h + 1, :], (rows, BIAS_SPAN))
        toeplitz = pltpu.roll(r, 0, axis=1, stride=1, stride_axis=0)
        bias_ref[h] = jnp.where(band, toeplitz[:, :cols] * LOG2_E, NEG)


def _pair_queries(q2):
    lane = lax.broadcasted_iota(jnp.int32, q2.shape, 1)
    lo = lane < (LANES // 2)
    zero = jnp.zeros_like(q2)
    return jnp.concatenate([jnp.where(lo, q2, zero), jnp.where(lo, zero, q2)], axis=0), lo


def _build_band_bias_t(row_ref, bias_ref):
    _, n_k, n_q = bias_ref.shape
    k = lax.broadcasted_iota(jnp.int32, (n_k, n_q), 0)
    q = lax.broadcasted_iota(jnp.int32, (n_k, n_q), 1)
    qc = q // CHUNK
    kc = k // CHUNK - (2 * A_TQ) // CHUNK
    band = (kc >= qc - BAND_PREV) & (kc <= qc)
    for h in range(H_A):
        r = jnp.broadcast_to(row_ref[h:h + 1, :], (n_k, BIAS_SPAN))
        toeplitz = pltpu.roll(r, 0, axis=1, stride=1, stride_axis=0)
        bias_ref[h] = jnp.where(band, toeplitz[:, :n_q] * LOG2_E, NEG)


def _attn_a_kernel(qt_ref, k2_ref, k1_ref, k0_ref, v2_ref, v1_ref, v0_ref, row_ref, o_ref, bias_ref):
    i = pl.program_id(0)

    @pl.when(i == 0)
    def _():
        _build_band_bias_t(row_ref, bias_ref)

    def body(check_keys):
        if check_keys:
            kidx = lax.broadcasted_iota(jnp.int32, (3 * A_TQ, A_TQ), 0)
            kvalid = kidx >= (2 - i) * A_TQ
        dim = lax.broadcasted_iota(jnp.int32, (LANES, A_TQ), 0)
        zero = jnp.zeros((LANES, A_TQ), BF16)
        s, o_t = [None] * H_A, [None] * H_A

        def scores(h):
            p = h // 2
            sl = slice(p * LANES, (p + 1) * LANES)
            qz = jnp.where(dim < HD_A if h % 2 == 0 else dim >= HD_A, qt_ref[p], zero)
            kc = jnp.concatenate([k2_ref[:, sl], k1_ref[:, sl], k0_ref[:, sl]], axis=0)
            s[h] = _dot(kc, qz) + bias_ref[h]
            if check_keys:
                s[h] = jnp.where(kvalid, s[h], NEG)

        def values(h):
            e = jnp.exp2(s[h] - jnp.max(s[h], axis=0, keepdims=True)).astype(BF16)
            r = (_dot(v2_ref[h, 0], e[0:A_TQ]) + _dot(v1_ref[h, 0], e[A_TQ:2 * A_TQ])
                 + _dot(v0_ref[h, 0], e[2 * A_TQ:3 * A_TQ]))
            o_t[h] = r[0:HD_A] / r[HD_A:HD_A + 1]
            if h % 2 == 1:
                p = h // 2
                o_ref[:, p * LANES:(p + 1) * LANES] = jnp.concatenate([o_t[h - 1], o_t[h]],
                                                                      axis=0).T.astype(BF16)

        for h in range(H_A + A_LAG):
            if h < H_A:
                scores(h)
            if h >= A_LAG:
                values(h - A_LAG)

    first_steps = BAND_PREV * CHUNK // A_TQ
    pl.when(i < first_steps)(lambda: body(True))
    pl.when(i >= first_steps)(lambda: body(False))


def _attn_a_prompt(qat, ka, vat, bias_row_rev):
    rows = ka.shape[0]
    prev = lambda d: (lambda i: (jnp.maximum(i - d, 0), 0))
    kspec = lambda d: pl.BlockSpec((A_TQ, A_WIDTH), prev(d))
    vspec = lambda d: pl.BlockSpec((H_A, 1, VA_ROWS, A_TQ), lambda i: (0, jnp.maximum(i - d, 0), 0, 0))
    return pl.pallas_call(
        _attn_a_kernel,
        grid=(rows // A_TQ,),
        in_specs=[pl.BlockSpec((H_A // 2, LANES, A_TQ), lambda i: (0, 0, i)),
                  kspec(2), kspec(1), kspec(0), vspec(2), vspec(1), vspec(0), _const_spec(bias_row_rev.shape)],
        out_specs=pl.BlockSpec((A_TQ, A_WIDTH), lambda i: (i, 0)),
        out_shape=jax.ShapeDtypeStruct((rows, A_WIDTH), BF16),
        scratch_shapes=[pltpu.VMEM((H_A, 3 * A_TQ, A_TQ), F32)],
        compiler_params=_arbitrary(1),
        name="attn_a_prompt",
    )(qat, ka, ka, ka, vat, vat, vat, bias_row_rev)


def _attn_a_sample_kernel(q_ref, kn_ref, vn_ref, kc_ref, vc_ref, row_ref, o_ref, bias_ref):
    t = q_ref.shape[0]
    past = kc_ref.shape[1]

    @pl.when(pl.program_id(0) == 0)
    def _():
        _build_band_bias(row_ref, bias_ref)

    kc = kc_ref[0].astype(BF16)
    vc = vc_ref[0].astype(BF16)
    for p in range(H_A // 2):
        sl = slice(p * LANES, (p + 1) * LANES)
        qz, lo = _pair_queries(q_ref[:, sl])
        b_past = jnp.concatenate([bias_ref[2 * p, 0:t, 0:past], bias_ref[2 * p + 1, 0:t, 0:past]], axis=0)
        b_new = jnp.concatenate([bias_ref[2 * p, 0:t, past:past + t],
                                 bias_ref[2 * p + 1, 0:t, past:past + t]], axis=0)
        s_past = _dot_nt(qz, kc[:, sl]) + b_past
        s_new = _dot_nt(qz, kn_ref[:, sl]) + b_new
        m = jnp.maximum(jnp.max(s_past, axis=-1, keepdims=True), jnp.max(s_new, axis=-1, keepdims=True))
        e_past = jnp.exp2(s_past - m)
        e_new = jnp.exp2(s_new - m)
        l = jnp.sum(e_past, axis=-1, keepdims=True) + jnp.sum(e_new, axis=-1, keepdims=True)
        o = (_dot(e_past.astype(BF16), vc[:, sl]) + _dot(e_new.astype(BF16), vn_ref[:, sl])) / l
        o_ref[:, sl] = jnp.where(lo, o[:t], o[t:]).astype(BF16)


def _attn_a_sample(qa, ka, va, cache_k, cache_v, bias_row, t):
    rows = qa.shape[0]
    streams, past, width = cache_k.shape
    assert past == 2 * A_TQ and t <= CHUNK and t % SUBLANES == 0
    row = pl.BlockSpec((t, A_WIDTH), lambda s: (s, 0))
    cache = pl.BlockSpec((1, past, width), lambda s: (s, 0, 0))
    return pl.pallas_call(
        _attn_a_sample_kernel,
        grid=(streams,),
        in_specs=[row, row, row, cache, cache, _const_spec(bias_row.shape)],
        out_specs=row,
        out_shape=jax.ShapeDtypeStruct((rows, A_WIDTH), BF16),
        scratch_shapes=[pltpu.VMEM((H_A, t, past + LANES), F32)],
        compiler_params=_arbitrary(1),
        name="attn_a_sample",
    )(qa, ka, va, cache_k, cache_v, bias_row)


def _lambda(lamp_ref):
    lp = lamp_ref[...]
    a = jnp.sum(lp[0:1] * lp[1:2], axis=-1, keepdims=True)
    b = jnp.sum(lp[2:3] * lp[3:4], axis=-1, keepdims=True)
    return jnp.exp(a) - jnp.exp(b) + LAM_INIT


def _diff_finish(o0, o1, lam, g):
    o = o0 - lam * o1
    o = o * lax.rsqrt(jnp.mean(o * o, axis=-1, keepdims=True) + LN_EPS) * g * (1.0 - LAM_INIT)
    return o.astype(BF16)


def _attn_b_kernel(qt_ref, k_ref, vt_ref, lamp_ref, gcol_ref, o_ref, acc_ref, p_ref, alpha_ref):
    qi = pl.program_id(1)
    n_parts = B_TQ // B_QW
    dim = lax.broadcasted_iota(jnp.int32, (LANES, B_QW), 0)
    zero = jnp.zeros((LANES, B_QW), BF16)
    chains = [(part, (h, mp)) for part in range(n_parts) for h in range(B_HEADS) for mp in range(2)]
    qz = []
    for part, (h, mp) in chains:
        qt = qt_ref[h, :, part * B_QW:(part + 1) * B_QW]
        qz.append(jnp.where(dim < HD_B if mp == 0 else dim >= HD_B, qt, zero))
    acc_ref[...] = jnp.zeros(acc_ref.shape, F32)
    p_ref[1] = jnp.zeros(p_ref.shape[1:], BF16)
    alpha_ref[1] = jnp.ones(alpha_ref.shape[1:], F32)

    def pending_values(c, j_prev, buf):
        h = chains[c][1][0]
        acc_ref[c] = alpha_ref[buf, c] * acc_ref[c] + _dot(vt_ref[h, j_prev], p_ref[buf, c])

    def step(j, ms, buf, parts, diag_parts, pending_parts):
        start = pl.multiple_of(j * B_TK, B_TK)
        j_prev = jnp.maximum(j - 1, 0)
        kpos = start + lax.broadcasted_iota(jnp.int32, (B_TK, B_QW), 0)
        lane = lax.broadcasted_iota(jnp.int32, (B_TK, B_QW), 1)
        out = list(ms)
        for c, (part, (h, _)) in enumerate(chains):
            if part in parts:
                s = _dot(k_ref[pl.ds(start, B_TK), h * LANES:(h + 1) * LANES], qz[c])
                if part in diag_parts:
                    qpos = qi * B_TQ + part * B_QW + lane
                    s = jnp.where(kpos < (qpos // CHUNK + 1) * CHUNK, s, NEG)
                out[c] = jnp.maximum(ms[c], jnp.max(s, axis=0, keepdims=True))
                alpha_ref[buf, c] = jnp.exp2(ms[c] - out[c])
                p_ref[buf, c] = jnp.exp2(s - out[c]).astype(BF16)
            if part in pending_parts:
                pending_values(c, j_prev, 1 - buf)
        return tuple(out)

    every = tuple(range(n_parts))
    low, high = every[:n_parts // 2], every[n_parts // 2:]

    def pair(i, ms):
        ms = step(2 * i, ms, 0, every, (), every)
        return step(2 * i + 1, ms, 1, every, (), every)

    m_init = tuple(jnp.full((1, B_QW), NEG, F32) for _ in chains)
    ms = lax.fori_loop(0, qi, pair, m_init)
    ms = step(2 * qi, ms, 0, every, low, every)
    step(2 * qi + 1, ms, 1, high, high, every)
    for c, (part, _) in enumerate(chains):
        if part in high:
            pending_values(c, 2 * qi + 1, 1)

    lam = _lambda(lamp_ref)
    for c in range(0, len(chains), 2):
        part, (h, _) = chains[c]
        a0 = acc_ref[c]
        a1 = acc_ref[c + 1]
        o = a0[0:LANES] / a0[LANES:LANES + 1] - lam * (a1[0:LANES] / a1[LANES:LANES + 1])
        o = o * lax.rsqrt(jnp.mean(o * o, axis=0, keepdims=True) + LN_EPS) * gcol_ref[...] * (1.0 - LAM_INIT)
        o_ref[part * B_QW:(part + 1) * B_QW, h * LANES:(h + 1) * LANES] = o.T.astype(BF16)


def _attn_b_prompt(qbt, kb, vbt, lamp, gcol):
    rows = kb.shape[0]
    assert rows % B_TQ == 0 and B_TQ == 2 * B_TK and B_TK == 2 * B_QW and B_QW % CHUNK == 0
    assert H_B % B_HEADS == 0
    n_chains = (B_TQ // B_QW) * B_HEADS * 2
    single = pl.Buffered(1)
    return pl.pallas_call(
        _attn_b_kernel,
        grid=(H_B // B_HEADS, rows // B_TQ),
        in_specs=[pl.BlockSpec((B_HEADS, LANES, B_TQ), lambda h, i: (h, 0, i)),
                  pl.BlockSpec((rows, B_HEADS * LANES), lambda h, i: (0, h), pipeline_mode=single),
                  pl.BlockSpec((B_HEADS,) + vbt.shape[1:], lambda h, i: (h, 0, 0, 0), pipeline_mode=single),
                  _const_spec(lamp.shape), _const_spec(gcol.shape)],
        out_specs=pl.BlockSpec((B_TQ, B_HEADS * LANES), lambda h, i: (i, h)),
        out_shape=jax.ShapeDtypeStruct((rows, B_WIDTH), BF16),
        scratch_shapes=[pltpu.VMEM((n_chains, V_ROWS, B_QW), F32),
                        pltpu.VMEM((2, n_chains, B_TK, B_QW), BF16), pltpu.VMEM((2, n_chains, 1, B_QW), F32)],
        compiler_params=_arbitrary(2),
        name="attn_b_prompt",
    )(qbt, kb, vbt, lamp, gcol)


def _attn_b_sample_kernel(q_ref, kn_ref, vn_ref, kc_ref, vc_ref, lamp_ref, g_ref, o_ref):
    t = q_ref.shape[0]
    lam = _lambda(lamp_ref)
    g = g_ref[...]
    for h in range(H_B):
        sl = slice(h * LANES, (h + 1) * LANES)
        qz, _ = _pair_queries(q_ref[:, sl])
        s_past = _dot_nt(qz, kc_ref[0, :, sl].astype(BF16))
        s_new = _dot_nt(qz, kn_ref[:, sl])
        m = jnp.maximum(jnp.max(s_past, axis=-1, keepdims=True), jnp.max(s_new, axis=-1, keepdims=True))
        e_past = jnp.exp(s_past - m)
        e_new = jnp.exp(s_new - m)
        l = jnp.sum(e_past, axis=-1, keepdims=True) + jnp.sum(e_new, axis=-1, keepdims=True)
        o = (_dot(e_past.astype(BF16), vc_ref[0, :, h, :].astype(BF16))
             + _dot(e_new.astype(BF16), vn_ref[:, sl])) / l
        o_ref[:, sl] = _diff_finish(o[:t], o[t:], lam, g)


def _attn_b_sample(qb, kb, vb, cache_k, cache_v, lamp, g, t):
    rows = qb.shape[0]
    streams, past, width = cache_k.shape
    assert past % CHUNK == 0 and t <= CHUNK
    row = pl.BlockSpec((t, B_WIDTH), lambda s: (s, 0))
    cache = pl.BlockSpec((1, past, width), lambda s: (s, 0, 0))
    cache_v_spec = pl.BlockSpec((1,) + cache_v.shape[1:], lambda s: (s, 0, 0, 0))
    return pl.pallas_call(
        _attn_b_sample_kernel,
        grid=(streams,),
        in_specs=[row, row, row, cache, cache_v_spec, _const_spec(lamp.shape), _const_spec(g.shape)],
        out_specs=row,
        out_shape=jax.ShapeDtypeStruct((rows, B_WIDTH), BF16),
        compiler_params=_arbitrary(1),
        name="attn_b_sample",
    )(qb, kb, vb, cache_k, cache_v, lamp, g)


def _mem_kv_kernel(x_ref, wk_ref, wv_ref, kf_ref, vf_ref):
    xb = x_ref[...].astype(BF16)
    kf_ref[...] = _dot(xb, wk_ref[...])
    vf_ref[...] = _dot(xb, wv_ref[...])


def _mem_kv(mem, wk_bf, wv_bf):
    f = jax.ShapeDtypeStruct(mem.shape, F32)
    return pl.pallas_call(
        _mem_kv_kernel,
        out_shape=[f, f],
        name="mem_kv",
    )(mem, wk_bf, wv_bf)


def _layer_norm(x, g, b):
    mu = jnp.mean(x, axis=-1, keepdims=True)
    xc = x - mu
    var = jnp.mean(xc * xc, axis=-1, keepdims=True)
    return xc * lax.rsqrt(var + LN_EPS) * g + b


def _tail_kernel(x_ref, oa_ref, ob_ref, mk_ref, mv_ref, cin_ref,
                 wo_ref, ln1g_ref, ln1b_ref, wmq_ref, wmo_ref, ln2g_ref, ln2b_ref,
                 wup_ref, cw_ref, cb_ref, wdn_ref, ln3g_ref, ln3b_ref,
                 y_ref, carry_ref, *, seg, sub):
    tm = x_ref.shape[0]
    n_seg = tm // seg
    n_sub = tm // sub
    assert n_seg == 1 or n_sub == 1
    subs = [slice(i * sub, (i + 1) * sub) for i in range(n_sub)]

    @pl.when(pl.program_id(0) == 0)
    def _():
        carry_ref[...] = cin_ref[...]

    def mem_attend(q, mk, mv):
        heads = []
        for h in range(H_M):
            sl = slice(h * HD_M, (h + 1) * HD_M)
            s = _dot_nt(q[:, sl], mk[:, sl])
            e = jnp.exp(s - jnp.max(s, axis=-1, keepdims=True))
            l = jnp.sum(e, axis=-1, keepdims=True)
            heads.append((_dot(e.astype(BF16), mv[:, sl]) / l).astype(BF16))
        return jnp.concatenate(heads, axis=1)

    def mem_attend_rows(q):
        if mk_ref.shape[0] == 1:
            return mem_attend(q, mk_ref[0].astype(BF16), mv_ref[0].astype(BF16))
        return jnp.concatenate([mem_attend(q[sg * seg:(sg + 1) * seg],
                                           mk_ref[sg].astype(BF16), mv_ref[sg].astype(BF16))
                                for sg in range(n_seg)], axis=0)

    mix = [_dot(oa_ref[sl, :], wo_ref[0:A_WIDTH, :]) + _dot(ob_ref[sl, :], wo_ref[A_WIDTH:A_WIDTH + B_WIDTH, :])
           for sl in subs]
    x1 = [_layer_norm(DEEPNORM_ALPHA * x_ref[sl, :] + m, ln1g_ref[...], ln1b_ref[...]) for sl, m in zip(subs, mix)]
    qm = [(_dot(v.astype(BF16), wmq_ref[...]) * (HD_M ** -0.5)).astype(BF16) for v in x1]
    om = [mem_attend_rows(q) for q in qm]
    mo = [_dot(v, wmo_ref[...]) for v in om]
    x2 = [_layer_norm(DEEPNORM_ALPHA * a + b, ln2g_ref[...], ln2b_ref[...]) for a, b in zip(x1, mo)]
    x2b = [v.astype(BF16) for v in x2]

    def conv(us, col, width):
        cols = slice(col, col + width)
        w0, w1, w2, b = cw_ref[0:1, cols], cw_ref[1:2, cols], cw_ref[2:3, cols], cb_ref[0:1, cols]

        def taps(u, p2, p1):
            row = lax.broadcasted_iota(jnp.int32, u.shape, 0)
            u1 = jnp.where(row == 0, p1, pltpu.roll(u, 1, axis=0))
            u2 = jnp.where(row == 0, p2, jnp.where(row == 1, p1, pltpu.roll(u, 2, axis=0)))
            return u2 * w0 + u1 * w1 + u * w2 + b

        def state(sg):
            base = SUBLANES * sg
            return carry_ref[base + 6:base + 7, cols], carry_ref[base + 7:base + 8, cols]

        if n_seg == 1:
            p2, p1 = state(0)
            outs = []
            for u in us:
                outs.append(taps(u, p2, p1))
                p2, p1 = u[sub - 2:sub - 1], u[sub - 1:sub]
            carry_ref[0:SUBLANES, cols] = us[-1][sub - SUBLANES:sub]
            return outs
        (u,) = us
        outs = []
        for sg in range(n_seg):
            useg = u[sg * seg:(sg + 1) * seg]
            outs.append(taps(useg, *state(sg)))
            carry_ref[SUBLANES * sg:SUBLANES * (sg + 1), cols] = useg[seg - SUBLANES:seg]
        return [jnp.concatenate(outs, axis=0)]

    f = [jnp.zeros((sub, D_MODEL), F32) for _ in subs]
    c0 = 0
    for width in FF_BLOCKS:
        gate = conv([_dot(v, wup_ref[:, c0:c0 + width]) for v in x2b], c0, width)
        val = conv([_dot(v, wup_ref[:, D_FF + c0:D_FF + c0 + width]) for v in x2b], D_FF + c0, width)
        hid = [(g * (1.0 / (1.0 + jnp.exp(-g))) * v).astype(BF16) for g, v in zip(gate, val)]
        f = [a + _dot(h, wdn_ref[c0:c0 + width, :]) for a, h in zip(f, hid)]
        c0 += width
    for sl, a, b in zip(subs, x2, f):
        y_ref[sl, :] = _layer_norm(DEEPNORM_ALPHA * a + b, ln3g_ref[...], ln3b_ref[...])


def _tail(x, oa, ob, mk, mv, conv_in, params, tm, seg, sub):
    rows = x.shape[0]
    n_seg = tm // seg
    assert tm % sub == 0 and sub % SUBLANES == 0
    assert conv_in.shape == (SUBLANES * n_seg, 2 * D_FF)
    assert n_seg == 1 or rows == tm
    assert mk.shape[0] in (1, n_seg)
    row_spec = lambda w: pl.BlockSpec((tm, w), lambda i: (i, 0))
    carry_spec = pl.BlockSpec(conv_in.shape, lambda i: (0, 0))
    return pl.pallas_call(
        functools.partial(_tail_kernel, seg=seg, sub=sub),
        grid=(rows // tm,),
        in_specs=[row_spec(D_MODEL), row_spec(A_WIDTH), row_spec(B_WIDTH),
                  _const_spec(mk.shape), _const_spec(mv.shape),
                  _const_spec(conv_in.shape)] + [_const_spec(p.shape) for p in params],
        out_specs=[row_spec(D_MODEL), carry_spec],
        out_shape=[jax.ShapeDtypeStruct((rows, D_MODEL), F32),
                   jax.ShapeDtypeStruct(conv_in.shape, F32)],
        compiler_params=_arbitrary(1),
        name="tail",
    )(x, oa, ob, mk, mv, conv_in, *params)


def kernel(x_prompt, x_sample, mem_prompt, cache_a_k, cache_a_v, cache_b_k, cache_b_v, cache_mem_k, cache_mem_v,
           state_conv, w_qkv, rel_bias, lambda_q1, lambda_k1, lambda_q2, lambda_k2, subln_g, w_o, ln1_g, ln1_b,
           w_mq, w_mk, w_mv, w_mo, ln2_g, ln2_b, w_up, conv_w, conv_b, w_down, ln3_g, ln3_b):
    assert w_qkv.shape[0] == DEPTH == 1
    batch, seq, _ = x_prompt.shape
    streams, t_new, _ = x_sample.shape
    past_b = cache_b_k.shape[2]
    assert batch == 1 and seq % QKV_TM == 0 and QKV_TM == BAND_PREV * CHUNK

    bf = lambda w: w[0].astype(BF16)
    row = lambda v: v[0].astype(F32)[None, :]
    w_qkv_bf = bf(w_qkv)
    tail_params = (bf(w_o), row(ln1_g), row(ln1_b), bf(w_mq), bf(w_mo), row(ln2_g), row(ln2_b),
                   bf(w_up), conv_w[0].astype(F32), row(conv_b), bf(w_down), row(ln3_g), row(ln3_b))
    lamp = jnp.stack([lambda_q1[0], lambda_k1[0], lambda_q2[0], lambda_k2[0]]).astype(F32)
    g = row(subln_g)
    gcol = jnp.broadcast_to(subln_g[0].astype(F32)[:, None], (2 * HD_B, B_QW))
    bias = _bias_row(rel_bias[0])

    xp = x_prompt.reshape(seq, D_MODEL)
    qat, ka, vat, qbt, kb, vbt, ka_f, va_f, kb_f, vb_f = _qkv(xp, w_qkv_bf, jnp.arange(seq), QKV_TM, True)
    bias_rev = jnp.roll(bias[:, ::-1], 1, axis=1)
    oa = _attn_a_prompt(qat, ka, vat, bias_rev)
    ob = _attn_b_prompt(qbt, kb, vbt, lamp, gcol)
    mk_p, mv_p = _mem_kv(mem_prompt.reshape(N_MEM, D_MODEL), bf(w_mk), bf(w_mv))
    conv0 = jnp.zeros((SUBLANES, 2 * D_FF), F32)
    y_p, conv_p = _tail(xp, oa, ob, mk_p[None], mv_p[None], conv0, tail_params, TAIL_TM, TAIL_TM, TAIL_SUB)

    rows_s = streams * t_new
    xs = x_sample.reshape(rows_s, D_MODEL)
    pos_s = jnp.tile(past_b + jnp.arange(t_new), streams)
    qa_s, ka_s, va_s, qb_s, kb_s, vb_s, ka_sf, va_sf, kb_sf, vb_sf = _qkv(xs, w_qkv_bf, pos_s, rows_s, False)
    oa_s = _attn_a_sample(qa_s, ka_s, va_s,
                          cache_a_k[0].reshape(streams, -1, A_WIDTH), cache_a_v[0].reshape(streams, -1, A_WIDTH),
                          bias, t_new)
    ob_s = _attn_b_sample(qb_s, kb_s, vb_s,
                          cache_b_k[0].reshape(streams, past_b, B_WIDTH),
                          cache_b_v[0], lamp, g, t_new)
    conv_in_s = jnp.pad(state_conv[0].astype(F32), ((0, 0), (SUBLANES - (CONV_W - 1), 0), (0, 0)))
    y_s, conv_s = _tail(xs, oa_s, ob_s,
                        cache_mem_k[0].reshape(streams, N_MEM, D_MODEL).astype(BF16),
                        cache_mem_v[0].reshape(streams, N_MEM, D_MODEL).astype(BF16),
                        conv_in_s.reshape(streams * SUBLANES, 2 * D_FF), tail_params, rows_s, t_new, rows_s)

    keep = CONV_W - 1
    return (
        y_p.reshape(batch, seq, D_MODEL),
        y_s.reshape(streams, t_new, D_MODEL),
        ka_f.reshape(1, batch, QKV_TM, H_A, HD_A),
        va_f.reshape(1, batch, QKV_TM, H_A, HD_A),
        kb_f.reshape(1, batch, seq, H_B, 2, HD_B),
        vb_f.reshape(1, batch, seq, H_B, 2 * HD_B),
        mk_p.reshape(1, batch, N_MEM, H_M, HD_M),
        mv_p.reshape(1, batch, N_MEM, H_M, HD_M),
        conv_p[SUBLANES - keep:].reshape(1, batch, keep, 2 * D_FF),
        ka_sf.reshape(1, streams, t_new, H_A, HD_A),
        va_sf.reshape(1, streams, t_new, H_A, HD_A),
        kb_sf.reshape(1, streams, t_new, H_B, 2, HD_B),
        vb_sf.reshape(1, streams, t_new, H_B, 2 * HD_B),
        conv_s.reshape(streams, SUBLANES, 2 * D_FF)[:, SUBLANES - keep:].reshape(1, streams, keep, 2 * D_FF),
    )
```

```python
import functools
import math

import numpy as np
import jax
import jax.numpy as jnp
from jax import lax
from jax.experimental import pallas as pl
from jax.experimental.pallas import tpu as pltpu

F32 = jnp.float32
BF16 = jnp.bfloat16

D_MODEL = 1024
CHUNK = 64
BAND_PREV = 8
REL_CLIP = 128
H_A = 8
HD_A = 64
H_B = 4
HD_B = 64
ROT_DIM = HD_B // 4
ROPE_THETA = 500000.0
N_MEM = 256
H_M = 4
HD_M = D_MODEL // H_M
D_FF = 2816
CONV_W = 3
LN_EPS = 1e-5
DEPTH = 1
DEEPNORM_ALPHA = (2.0 * DEPTH) ** 0.25
A_WIDTH = H_A * HD_A
B_WIDTH = H_B * 2 * HD_B
LAM_INIT = 0.8 - 0.6 * math.exp(-0.3 * 0)

LANES = 128
SUBLANES = 8
NEG = -1e30
LOG2_E = math.log2(math.e)

QKV_TM = 512
A_TQ = 256
A_LAG = 2
B_TQ = 1024
B_QW = 256
B_HEADS = 2
B_TK = 512
ONES_ROWS = 16
V_ROWS = LANES + ONES_ROWS
VA_ROWS = HD_A + ONES_ROWS
BIAS_SPAN = 1024
TAIL_TM = 512
TAIL_SUB = 256
MXU_DIM = 256
FF_BLOCKS = (4 * MXU_DIM, 4 * MXU_DIM, D_FF - 8 * MXU_DIM)
assert sum(FF_BLOCKS) == D_FF and all(w % MXU_DIM == 0 for w in FF_BLOCKS)

_NT = (((1,), (1,)), ((), ()))


def _dot(a, b):
    return jnp.dot(a, b, preferred_element_type=F32)


def _dot_nt(a, b):
    return lax.dot_general(a, b, _NT, preferred_element_type=F32)


def _const_spec(shape):
    nd = len(shape)
    return pl.BlockSpec(shape, lambda *_: (0,) * nd, pipeline_mode=pl.Buffered(1))


def _arbitrary(n):
    return pltpu.CompilerParams(dimension_semantics=("arbitrary",) * n)


def _rope(z, cos, sin):
    lane = lax.broadcasted_iota(jnp.int32, (z.shape[0], LANES), 1)
    first_half = (lane % HD_B) < (ROT_DIM // 2)
    outs = []
    for c in range(z.shape[1] // LANES):
        zc = z[:, c * LANES:(c + 1) * LANES]
        partner = jnp.where(first_half,
                            pltpu.roll(zc, LANES - ROT_DIM // 2, axis=1),
                            pltpu.roll(zc, ROT_DIM // 2, axis=1))
        outs.append(zc * cos + partner * sin)
    return jnp.concatenate(outs, axis=1)


def _qkv_kernel(x_ref, w_ref, rope_ref,
                qa_ref, ka_ref, va_ref, qb_ref, kb_ref, vb_ref,
                kaf_ref, vaf_ref, kbf_ref, vbf_ref, *, transposed):
    xb = x_ref[...].astype(BF16)
    cos, sin = _spread_rope(rope_ref[...])
    tm = xb.shape[0]

    def section(i):
        return _dot(xb, w_ref[:, i * A_WIDTH:(i + 1) * A_WIDTH])

    vb = section(5)
    for h in range(H_B):
        vbf_ref[:, h, :] = vb[:, h * LANES:(h + 1) * LANES]
    def ones_rows(width):
        return (lax.broadcasted_iota(jnp.int32, (ONES_ROWS, width), 0) == 0).astype(BF16)

    if transposed:
        for h in range(H_B):
            vb_ref[h, 0, 0:LANES, :] = vb[:, h * LANES:(h + 1) * LANES].T.astype(BF16)
            vb_ref[h, 0, LANES:V_ROWS, :] = ones_rows(tm)
    else:
        vb_ref[...] = vb.astype(BF16)
    qb = _rope(section(3), cos, sin) * (HD_B ** -0.5 * (LOG2_E if transposed else 1.0))
    if transposed:
        for h in range(H_B):
            qb_ref[h] = qb[:, h * LANES:(h + 1) * LANES].T.astype(BF16)
    else:
        qb_ref[...] = qb.astype(BF16)
    kb = _rope(section(4), cos, sin)
    kb_ref[...] = kb.astype(BF16)
    kbf_ref[...] = kb
    va = section(2)
    vaf_ref[...] = va
    if transposed:
        for p in range(H_A // 2):
            vt = va[:, p * LANES:(p + 1) * LANES].T.astype(BF16)
            for hh in range(2):
                for t in range(tm // A_TQ):
                    cols = slice(t * A_TQ, (t + 1) * A_TQ)
                    va_ref[2 * p + hh, t, 0:HD_A, :] = vt[hh * HD_A:(hh + 1) * HD_A, cols]
                    va_ref[2 * p + hh, t, HD_A:VA_ROWS, :] = ones_rows(A_TQ)
    else:
        va_ref[...] = va.astype(BF16)
    qa = section(0) * (HD_A ** -0.5 * LOG2_E)
    if transposed:
        for p in range(H_A // 2):
            qa_ref[p] = qa[:, p * LANES:(p + 1) * LANES].T.astype(BF16)
    else:
        qa_ref[...] = qa.astype(BF16)
    ka = section(1)
    ka_ref[...] = ka.astype(BF16)
    kaf_ref[...] = ka


def _rope_table(pos):
    inv = ROPE_THETA ** (-jnp.arange(0, ROT_DIM, 2, dtype=F32) / ROT_DIM)
    ang = pos.astype(F32)[:, None] * inv[None, :]
    return jnp.concatenate([jnp.cos(ang), jnp.sin(ang), jnp.zeros((pos.shape[0], LANES - ROT_DIM), F32)], axis=1)


def _spread_rope(tab):
    half = ROT_DIM // 2
    lane = lax.broadcasted_iota(jnp.int32, tab.shape, 1)
    c = jnp.where(lane < half, tab, 0.0)
    s = jnp.where((lane >= half) & (lane < ROT_DIM), tab, 0.0)
    c = c + pltpu.roll(c, half, axis=1)
    s = s - pltpu.roll(s, LANES - half, axis=1)
    c = c + pltpu.roll(c, HD_B, axis=1)
    s = s + pltpu.roll(s, HD_B, axis=1)
    return jnp.where(lane % HD_B < ROT_DIM, c, 1.0), s


def _qkv(x, w_bf, pos, tm, transposed):
    rows = x.shape[0]
    nt = rows // tm
    rope = _rope_table(pos)
    row_spec = lambda w: pl.BlockSpec((tm, w), lambda i: (i, 0))
    last_spec = pl.BlockSpec((tm, A_WIDTH), lambda i: (0, 0))
    bf = jax.ShapeDtypeStruct((rows, A_WIDTH), BF16)
    if transposed:
        assert tm == B_TK and tm % A_TQ == 0
        qa_spec = pl.BlockSpec((H_A // 2, LANES, tm), lambda i: (0, 0, i))
        qa_shape = jax.ShapeDtypeStruct((H_A // 2, LANES, rows), BF16)
        va_spec = pl.BlockSpec((H_A, tm // A_TQ, VA_ROWS, A_TQ), lambda i: (0, i, 0, 0))
        va_shape = jax.ShapeDtypeStruct((H_A, rows // A_TQ, VA_ROWS, A_TQ), BF16)
        qb_spec = pl.BlockSpec((H_B, LANES, tm), lambda i: (0, 0, i))
        qb_shape = jax.ShapeDtypeStruct((H_B, LANES, rows), BF16)
        vb_spec = pl.BlockSpec((H_B, 1, V_ROWS, tm), lambda i: (0, i, 0, 0))
        vb_shape = jax.ShapeDtypeStruct((H_B, nt, V_ROWS, tm), BF16)
    else:
        qa_spec = va_spec = qb_spec = vb_spec = row_spec(B_WIDTH)
        qa_shape = va_shape = qb_shape = vb_shape = bf
    return pl.pallas_call(
        functools.partial(_qkv_kernel, transposed=transposed),
        grid=(nt,),
        in_specs=[row_spec(D_MODEL), _const_spec(w_bf.shape), row_spec(LANES)],
        out_specs=[qa_spec, row_spec(A_WIDTH), va_spec, qb_spec, row_spec(B_WIDTH), vb_spec]
                  + [last_spec, last_spec, row_spec(B_WIDTH),
                     pl.BlockSpec((tm, H_B, 2 * HD_B), lambda i: (i, 0, 0))],
        out_shape=[qa_shape, bf, va_shape, qb_shape, bf, vb_shape]
                  + [jax.ShapeDtypeStruct((tm, A_WIDTH), F32)] * 2
                  + [jax.ShapeDtypeStruct((rows, B_WIDTH), F32),
                     jax.ShapeDtypeStruct((rows, H_B, 2 * HD_B), F32)],
        compiler_params=_arbitrary(1),
        name="qkv_proj",
    )(x, w_bf, rope)


def _bias_row(rel_bias):
    table = rel_bias.astype(F32)
    far = table[:, 2 * REL_CLIP:]
    near = table[:, :1]
    n_far = 2 * A_TQ - REL_CLIP
    n_near = 3 * A_TQ - (n_far + 2 * REL_CLIP + 1)
    h = table.shape[0]
    return jnp.concatenate([jnp.broadcast_to(far, (h, n_far)), table[:, ::-1],
                            jnp.broadcast_to(near, (h, n_near)),
                            jnp.broadcast_to(far, (h, BIAS_SPAN - 3 * A_TQ))], axis=1)


def _build_band_bias(row_ref, bias_ref):
    _, rows, cols = bias_ref.shape
    q = lax.broadcasted_iota(jnp.int32, (rows, cols), 0)
    k = lax.broadcasted_iota(jnp.int32, (rows, cols), 1)
    qc = q // CHUNK
    kc = k // CHUNK - (2 * A_TQ) // CHUNK
    band = (kc >= qc - BAND_PREV) & (kc <= qc)
    for h in range(H_A):
        r = jnp.broadcast_to(row_ref[h:h + 1, :], (rows, BIAS_SPAN))
        toeplitz = pltpu.roll(r, 0, axis=1, stride=1, stride_axis=0)
        bias_ref[h] = jnp.where(band, toeplitz[:, :cols] * LOG2_E, NEG)


def _pair_queries(q2):
    lane = lax.broadcasted_iota(jnp.int32, q2.shape, 1)
    lo = lane < (LANES // 2)
    zero = jnp.zeros_like(q2)
    return jnp.concatenate([jnp.where(lo, q2, zero), jnp.where(lo, zero, q2)], axis=0), lo


def _build_band_bias_t(row_ref, bias_ref):
    _, n_k, n_q = bias_ref.shape
    k = lax.broadcasted_iota(jnp.int32, (n_k, n_q), 0)
    q = lax.broadcasted_iota(jnp.int32, (n_k, n_q), 1)
    qc = q // CHUNK
    kc = k // CHUNK - (2 * A_TQ) // CHUNK
    band = (kc >= qc - BAND_PREV) & (kc <= qc)
    for h in range(H_A):
        r = jnp.broadcast_to(row_ref[h:h + 1, :], (n_k, BIAS_SPAN))
        toeplitz = pltpu.roll(r, 0, axis=1, stride=1, stride_axis=0)
        bias_ref[h] = jnp.where(band, toeplitz[:, :n_q] * LOG2_E, NEG)


def _attn_a_kernel(qt_ref, k2_ref, k1_ref, k0_ref, v2_ref, v1_ref, v0_ref, row_ref, o_ref, bias_ref):
    i = pl.program_id(0)

    @pl.when(i == 0)
    def _():
        _build_band_bias_t(row_ref, bias_ref)

    def body(check_keys):
        if check_keys:
            kidx = lax.broadcasted_iota(jnp.int32, (3 * A_TQ, A_TQ), 0)
            kvalid = kidx >= (2 - i) * A_TQ
        dim = lax.broadcasted_iota(jnp.int32, (LANES, A_TQ), 0)
        zero = jnp.zeros((LANES, A_TQ), BF16)
        s, o_t = [None] * H_A, [None] * H_A

        def scores(h):
            p = h // 2
            sl = slice(p * LANES, (p + 1) * LANES)
            qz = jnp.where(dim < HD_A if h % 2 == 0 else dim >= HD_A, qt_ref[p], zero)
            kc = jnp.concatenate([k2_ref[:, sl], k1_ref[:, sl], k0_ref[:, sl]], axis=0)
            s[h] = _dot(kc, qz) + bias_ref[h]
            if check_keys:
                s[h] = jnp.where(kvalid, s[h], NEG)

        def values(h):
            e = jnp.exp2(s[h] - jnp.max(s[h], axis=0, keepdims=True)).astype(BF16)
            r = (_dot(v2_ref[h, 0], e[0:A_TQ]) + _dot(v1_ref[h, 0], e[A_TQ:2 * A_TQ])
                 + _dot(v0_ref[h, 0], e[2 * A_TQ:3 * A_TQ]))
            o_t[h] = r[0:HD_A] / r[HD_A:HD_A + 1]
            if h % 2 == 1:
                p = h // 2
                o_ref[:, p * LANES:(p + 1) * LANES] = jnp.concatenate([o_t[h - 1], o_t[h]],
                                                                      axis=0).T.astype(BF16)

        for h in range(H_A + A_LAG):
            if h < H_A:
                scores(h)
            if h >= A_LAG:
                values(h - A_LAG)

    first_steps = BAND_PREV * CHUNK // A_TQ
    pl.when(i < first_steps)(lambda: body(True))
    pl.when(i >= first_steps)(lambda: body(False))


def _attn_a_prompt(qat, ka, vat, bias_row_rev):
    rows = ka.shape[0]
    prev = lambda d: (lambda i: (jnp.maximum(i - d, 0), 0))
    kspec = lambda d: pl.BlockSpec((A_TQ, A_WIDTH), prev(d))
    vspec = lambda d: pl.BlockSpec((H_A, 1, VA_ROWS, A_TQ), lambda i: (0, jnp.maximum(i - d, 0), 0, 0))
    return pl.pallas_call(
        _attn_a_kernel,
        grid=(rows // A_TQ,),
        in_specs=[pl.BlockSpec((H_A // 2, LANES, A_TQ), lambda i: (0, 0, i)),
                  kspec(2), kspec(1), kspec(0), vspec(2), vspec(1), vspec(0), _const_spec(bias_row_rev.shape)],
        out_specs=pl.BlockSpec((A_TQ, A_WIDTH), lambda i: (i, 0)),
        out_shape=jax.ShapeDtypeStruct((rows, A_WIDTH), BF16),
        scratch_shapes=[pltpu.VMEM((H_A, 3 * A_TQ, A_TQ), F32)],
        compiler_params=_arbitrary(1),
        name="attn_a_prompt",
    )(qat, ka, ka, ka, vat, vat, vat, bias_row_rev)


def _attn_a_sample_kernel(q_ref, kn_ref, vn_ref, kc_ref, vc_ref, row_ref, o_ref, bias_ref):
    t = q_ref.shape[0]
    past = kc_ref.shape[1]

    @pl.when(pl.program_id(0) == 0)
    def _():
        _build_band_bias(row_ref, bias_ref)

    kc = kc_ref[0].astype(BF16)
    vc = vc_ref[0].astype(BF16)
    for p in range(H_A // 2):
        sl = slice(p * LANES, (p + 1) * LANES)
        qz, lo = _pair_queries(q_ref[:, sl])
        b_past = jnp.concatenate([bias_ref[2 * p, 0:t, 0:past], bias_ref[2 * p + 1, 0:t, 0:past]], axis=0)
        b_new = jnp.concatenate([bias_ref[2 * p, 0:t, past:past + t],
                                 bias_ref[2 * p + 1, 0:t, past:past + t]], axis=0)
        s_past = _dot_nt(qz, kc[:, sl]) + b_past
        s_new = _dot_nt(qz, kn_ref[:, sl]) + b_new
        m = jnp.maximum(jnp.max(s_past, axis=-1, keepdims=True), jnp.max(s_new, axis=-1, keepdims=True))
        e_past = jnp.exp2(s_past - m)
        e_new = jnp.exp2(s_new - m)
        l = jnp.sum(e_past, axis=-1, keepdims=True) + jnp.sum(e_new, axis=-1, keepdims=True)
        o = (_dot(e_past.astype(BF16), vc[:, sl]) + _dot(e_new.astype(BF16), vn_ref[:, sl])) / l
        o_ref[:, sl] = jnp.where(lo, o[:t], o[t:]).astype(BF16)


def _attn_a_sample(qa, ka, va, cache_k, cache_v, bias_row, t):
    rows = qa.shape[0]
    streams, past, width = cache_k.shape
    assert past == 2 * A_TQ and t <= CHUNK and t % SUBLANES == 0
    row = pl.BlockSpec((t, A_WIDTH), lambda s: (s, 0))
    cache = pl.BlockSpec((1, past, width), lambda s: (s, 0, 0))
    return pl.pallas_call(
        _attn_a_sample_kernel,
        grid=(streams,),
        in_specs=[row, row, row, cache, cache, _const_spec(bias_row.shape)],
        out_specs=row,
        out_shape=jax.ShapeDtypeStruct((rows, A_WIDTH), BF16),
        scratch_shapes=[pltpu.VMEM((H_A, t, past + LANES), F32)],
        compiler_params=_arbitrary(1),
        name="attn_a_sample",
    )(qa, ka, va, cache_k, cache_v, bias_row)


def _lambda(lamp_ref):
    lp = lamp_ref[...]
    a = jnp.sum(lp[0:1] * lp[1:2], axis=-1, keepdims=True)
    b = jnp.sum(lp[2:3] * lp[3:4], axis=-1, keepdims=True)
    return jnp.exp(a) - jnp.exp(b) + LAM_INIT


def _diff_finish(o0, o1, lam, g):
    o = o0 - lam * o1
    o = o * lax.rsqrt(jnp.mean(o * o, axis=-1, keepdims=True) + LN_EPS) * g * (1.0 - LAM_INIT)
    return o.astype(BF16)


def _attn_b_kernel(qt_ref, k_ref, vt_ref, lamp_ref, gcol_ref, o_ref, acc_ref, p_ref, alpha_ref):
    qi = pl.program_id(1)
    n_parts = B_TQ // B_QW
    dim = lax.broadcasted_iota(jnp.int32, (LANES, B_QW), 0)
    zero = jnp.zeros((LANES, B_QW), BF16)
    chains = [(part, (h, mp)) for part in range(n_parts) for h in range(B_HEADS) for mp in range(2)]
    qz = []
    for part, (h, mp) in chains:
        qt = qt_ref[h, :, part * B_QW:(part + 1) * B_QW]
        qz.append(jnp.where(dim < HD_B if mp == 0 else dim >= HD_B, qt, zero))
    acc_ref[...] = jnp.zeros(acc_ref.shape, F32)
    p_ref[1] = jnp.zeros(p_ref.shape[1:], BF16)
    alpha_ref[1] = jnp.ones(alpha_ref.shape[1:], F32)

    def pending_values(c, j_prev, buf):
        h = chains[c][1][0]
        acc_ref[c] = alpha_ref[buf, c] * acc_ref[c] + _dot(vt_ref[h, j_prev], p_ref[buf, c])

    def step(j, ms, buf, parts, diag_parts, pending_parts):
        start = pl.multiple_of(j * B_TK, B_TK)
        j_prev = jnp.maximum(j - 1, 0)
        kpos = start + lax.broadcasted_iota(jnp.int32, (B_TK, B_QW), 0)
        lane = lax.broadcasted_iota(jnp.int32, (B_TK, B_QW), 1)
        out = list(ms)
        for c, (part, (h, _)) in enumerate(chains):
            if part in parts:
                s = _dot(k_ref[pl.ds(start, B_TK), h * LANES:(h + 1) * LANES], qz[c])
                if part in diag_parts:
                    qpos = qi * B_TQ + part * B_QW + lane
                    s = jnp.where(kpos < (qpos // CHUNK + 1) * CHUNK, s, NEG)
                out[c] = jnp.maximum(ms[c], jnp.max(s, axis=0, keepdims=True))
                alpha_ref[buf, c] = jnp.exp2(ms[c] - out[c])
                p_ref[buf, c] = jnp.exp2(s - out[c]).astype(BF16)
            if part in pending_parts:
                pending_values(c, j_prev, 1 - buf)
        return tuple(out)

    every = tuple(range(n_parts))
    low, high = every[:n_parts // 2], every[n_parts // 2:]

    def pair(i, ms):
        ms = step(2 * i, ms, 0, every, (), every)
        return step(2 * i + 1, ms, 1, every, (), every)

    m_init = tuple(jnp.full((1, B_QW), NEG, F32) for _ in chains)
    ms = lax.fori_loop(0, qi, pair, m_init)
    ms = step(2 * qi, ms, 0, every, low, every)
    step(2 * qi + 1, ms, 1, high, high, every)
    for c, (part, _) in enumerate(chains):
        if part in high:
            pending_values(c, 2 * qi + 1, 1)

    lam = _lambda(lamp_ref)
    for c in range(0, len(chains), 2):
        part, (h, _) = chains[c]
        a0 = acc_ref[c]
        a1 = acc_ref[c + 1]
        o = a0[0:LANES] / a0[LANES:LANES + 1] - lam * (a1[0:LANES] / a1[LANES:LANES + 1])
        o = o * lax.rsqrt(jnp.mean(o * o, axis=0, keepdims=True) + LN_EPS) * gcol_ref[...] * (1.0 - LAM_INIT)
        o_ref[part * B_QW:(part + 1) * B_QW, h * LANES:(h + 1) * LANES] = o.T.astype(BF16)


def _attn_b_prompt(qbt, kb, vbt, lamp, gcol):
    rows = kb.shape[0]
    assert rows % B_TQ == 0 and B_TQ == 2 * B_TK and B_TK == 2 * B_QW and B_QW % CHUNK == 0
    assert H_B % B_HEADS == 0
    n_chains = (B_TQ // B_QW) * B_HEADS * 2
    single = pl.Buffered(1)
    return pl.pallas_call(
        _attn_b_kernel,
        grid=(H_B // B_HEADS, rows // B_TQ),
        in_specs=[pl.BlockSpec((B_HEADS, LANES, B_TQ), lambda h, i: (h, 0, i)),
                  pl.BlockSpec((rows, B_HEADS * LANES), lambda h, i: (0, h), pipeline_mode=single),
                  pl.BlockSpec((B_HEADS,) + vbt.shape[1:], lambda h, i: (h, 0, 0, 0), pipeline_mode=single),
                  _const_spec(lamp.shape), _const_spec(gcol.shape)],
        out_specs=pl.BlockSpec((B_TQ, B_HEADS * LANES), lambda h, i: (i, h)),
        out_shape=jax.ShapeDtypeStruct((rows, B_WIDTH), BF16),
        scratch_shapes=[pltpu.VMEM((n_chains, V_ROWS, B_QW), F32),
                        pltpu.VMEM((2, n_chains, B_TK, B_QW), BF16), pltpu.VMEM((2, n_chains, 1, B_QW), F32)],
        compiler_params=_arbitrary(2),
        name="attn_b_prompt",
    )(qbt, kb, vbt, lamp, gcol)


def _attn_b_sample_kernel(q_ref, kn_ref, vn_ref, kc_ref, vc_ref, lamp_ref, g_ref, o_ref):
    t = q_ref.shape[0]
    lam = _lambda(lamp_ref)
    g = g_ref[...]
    for h in range(H_B):
        sl = slice(h * LANES, (h + 1) * LANES)
        qz, _ = _pair_queries(q_ref[:, sl])
        s_past = _dot_nt(qz, kc_ref[0, :, sl].astype(BF16))
        s_new = _dot_nt(qz, kn_ref[:, sl])
        m = jnp.maximum(jnp.max(s_past, axis=-1, keepdims=True), jnp.max(s_new, axis=-1, keepdims=True))
        e_past = jnp.exp(s_past - m)
        e_new = jnp.exp(s_new - m)
        l = jnp.sum(e_past, axis=-1, keepdims=True) + jnp.sum(e_new, axis=-1, keepdims=True)
        o = (_dot(e_past.astype(BF16), vc_ref[0, :, h, :].astype(BF16))
             + _dot(e_new.astype(BF16), vn_ref[:, sl])) / l
        o_ref[:, sl] = _diff_finish(o[:t], o[t:], lam, g)


def _attn_b_sample(qb, kb, vb, cache_k, cache_v, lamp, g, t):
    rows = qb.shape[0]
    streams, past, width = cache_k.shape
    assert past % CHUNK == 0 and t <= CHUNK
    row = pl.BlockSpec((t, B_WIDTH), lambda s: (s, 0))
    cache = pl.BlockSpec((1, past, width), lambda s: (s, 0, 0))
    cache_v_spec = pl.BlockSpec((1,) + cache_v.shape[1:], lambda s: (s, 0, 0, 0))
    return pl.pallas_call(
        _attn_b_sample_kernel,
        grid=(streams,),
        in_specs=[row, row, row, cache, cache_v_spec, _const_spec(lamp.shape), _const_spec(g.shape)],
        out_specs=row,
        out_shape=jax.ShapeDtypeStruct((rows, B_WIDTH), BF16),
        compiler_params=_arbitrary(1),
        name="attn_b_sample",
    )(qb, kb, vb, cache_k, cache_v, lamp, g)


def _mem_kv_kernel(x_ref, wk_ref, wv_ref, kf_ref, vf_ref):
    xb = x_ref[...].astype(BF16)
    k = _dot(xb, wk_ref[...])
    v = _dot(xb, wv_ref[...])
    for h in range(H_M):
        kf_ref[:, h, :] = k[:, h * HD_M:(h + 1) * HD_M]
        vf_ref[:, h, :] = v[:, h * HD_M:(h + 1) * HD_M]


def _mem_kv(mem, wk_bf, wv_bf):
    f = jax.ShapeDtypeStruct((mem.shape[0], H_M, HD_M), F32)
    return pl.pallas_call(
        _mem_kv_kernel,
        out_shape=[f, f],
        name="mem_kv",
    )(mem, wk_bf, wv_bf)


def _layer_norm(x, g, b):
    mu = jnp.mean(x, axis=-1, keepdims=True)
    xc = x - mu
    var = jnp.mean(xc * xc, axis=-1, keepdims=True)
    return xc * lax.rsqrt(var + LN_EPS) * g + b


def _tail_kernel(x_ref, oa_ref, ob_ref, mk_ref, mv_ref, cin_ref,
                 wo_ref, ln1g_ref, ln1b_ref, wmq_ref, wmo_ref, ln2g_ref, ln2b_ref,
                 wup_ref, cw_ref, cb_ref, wdn_ref, ln3g_ref, ln3b_ref,
                 y_ref, carry_ref, *, seg, sub):
    tm = x_ref.shape[0]
    n_seg = tm // seg
    n_sub = tm // sub
    assert n_seg == 1 or n_sub == 1
    subs = [slice(i * sub, (i + 1) * sub) for i in range(n_sub)]

    @pl.when(pl.program_id(0) == 0)
    def _():
        carry_ref[...] = cin_ref[...]

    def mem_attend(q, m):
        heads = []
        for h in range(H_M):
            sl = slice(h * HD_M, (h + 1) * HD_M)
            s = _dot_nt(q[:, sl], mk_ref[m, :, h, :].astype(BF16))
            e = jnp.exp(s - jnp.max(s, axis=-1, keepdims=True))
            l = jnp.sum(e, axis=-1, keepdims=True)
            heads.append((_dot(e.astype(BF16), mv_ref[m, :, h, :].astype(BF16)) / l).astype(BF16))
        return jnp.concatenate(heads, axis=1)

    def mem_attend_rows(q):
        if mk_ref.shape[0] == 1:
            return mem_attend(q, 0)
        return jnp.concatenate([mem_attend(q[sg * seg:(sg + 1) * seg], sg)
                                for sg in range(n_seg)], axis=0)

    mix = [_dot(oa_ref[sl, :], wo_ref[0:A_WIDTH, :]) + _dot(ob_ref[sl, :], wo_ref[A_WIDTH:A_WIDTH + B_WIDTH, :])
           for sl in subs]
    x1 = [_layer_norm(DEEPNORM_ALPHA * x_ref[sl, :] + m, ln1g_ref[...], ln1b_ref[...]) for sl, m in zip(subs, mix)]
    qm = [(_dot(v.astype(BF16), wmq_ref[...]) * (HD_M ** -0.5)).astype(BF16) for v in x1]
    om = [mem_attend_rows(q) for q in qm]
    mo = [_dot(v, wmo_ref[...]) for v in om]
    x2 = [_layer_norm(DEEPNORM_ALPHA * a + b, ln2g_ref[...], ln2b_ref[...]) for a, b in zip(x1, mo)]
    x2b = [v.astype(BF16) for v in x2]

    def conv(us, col, width):
        cols = slice(col, col + width)
        w0, w1, w2, b = cw_ref[0:1, cols], cw_ref[1:2, cols], cw_ref[2:3, cols], cb_ref[0:1, cols]

        def taps(u, p2, p1):
            row = lax.broadcasted_iota(jnp.int32, u.shape, 0)
            u1 = jnp.where(row == 0, p1, pltpu.roll(u, 1, axis=0))
            u2 = jnp.where(row == 0, p2, jnp.where(row == 1, p1, pltpu.roll(u, 2, axis=0)))
            return u2 * w0 + u1 * w1 + u * w2 + b

        def state(sg):
            base = SUBLANES * sg
            return carry_ref[base + 6:base + 7, cols], carry_ref[base + 7:base + 8, cols]

        if n_seg == 1:
            p2, p1 = state(0)
            outs = []
            for u in us:
                outs.append(taps(u, p2, p1))
                p2, p1 = u[sub - 2:sub - 1], u[sub - 1:sub]
            carry_ref[0:SUBLANES, cols] = us[-1][sub - SUBLANES:sub]
            return outs
        (u,) = us
        outs = []
        for sg in range(n_seg):
            useg = u[sg * seg:(sg + 1) * seg]
            outs.append(taps(useg, *state(sg)))
            carry_ref[SUBLANES * sg:SUBLANES * (sg + 1), cols] = useg[seg - SUBLANES:seg]
        return [jnp.concatenate(outs, axis=0)]

    f = [jnp.zeros((sub, D_MODEL), F32) for _ in subs]
    c0 = 0
    for width in FF_BLOCKS:
        gate = conv([_dot(v, wup_ref[:, c0:c0 + width]) for v in x2b], c0, width)
        val = conv([_dot(v, wup_ref[:, D_FF + c0:D_FF + c0 + width]) for v in x2b], D_FF + c0, width)
        hid = [(g * (1.0 / (1.0 + jnp.exp(-g))) * v).astype(BF16) for g, v in zip(gate, val)]
        f = [a + _dot(h, wdn_ref[c0:c0 + width, :]) for a, h in zip(f, hid)]
        c0 += width
    for sl, a, b in zip(subs, x2, f):
        y_ref[sl, :] = _layer_norm(DEEPNORM_ALPHA * a + b, ln3g_ref[...], ln3b_ref[...])


def _tail(x, oa, ob, mk, mv, conv_in, params, tm, seg, sub):
    rows = x.shape[0]
    n_seg = tm // seg
    assert tm % sub == 0 and sub % SUBLANES == 0
    assert conv_in.shape == (SUBLANES * n_seg, 2 * D_FF)
    assert n_seg == 1 or rows == tm
    assert mk.shape[0] in (1, n_seg)
    row_spec = lambda w: pl.BlockSpec((tm, w), lambda i: (i, 0))
    carry_spec = pl.BlockSpec(conv_in.shape, lambda i: (0, 0))
    return pl.pallas_call(
        functools.partial(_tail_kernel, seg=seg, sub=sub),
        grid=(rows // tm,),
        in_specs=[row_spec(D_MODEL), row_spec(A_WIDTH), row_spec(B_WIDTH),
                  _const_spec(mk.shape), _const_spec(mv.shape),
                  _const_spec(conv_in.shape)] + [_const_spec(p.shape) for p in params],
        out_specs=[row_spec(D_MODEL), carry_spec],
        out_shape=[jax.ShapeDtypeStruct((rows, D_MODEL), F32),
                   jax.ShapeDtypeStruct(conv_in.shape, F32)],
        compiler_params=_arbitrary(1),
        name="tail",
    )(x, oa, ob, mk, mv, conv_in, *params)


def kernel(x_prompt, x_sample, mem_prompt, cache_a_k, cache_a_v, cache_b_k, cache_b_v, cache_mem_k, cache_mem_v,
           state_conv, w_qkv, rel_bias, lambda_q1, lambda_k1, lambda_q2, lambda_k2, subln_g, w_o, ln1_g, ln1_b,
           w_mq, w_mk, w_mv, w_mo, ln2_g, ln2_b, w_up, conv_w, conv_b, w_down, ln3_g, ln3_b):
    assert w_qkv.shape[0] == DEPTH == 1
    batch, seq, _ = x_prompt.shape
    streams, t_new, _ = x_sample.shape
    past_b = cache_b_k.shape[2]
    assert batch == 1 and seq % QKV_TM == 0 and QKV_TM == BAND_PREV * CHUNK

    bf = lambda w: w[0].astype(BF16)
    row = lambda v: v[0].astype(F32)[None, :]
    w_qkv_bf = bf(w_qkv)
    tail_params = (bf(w_o), row(ln1_g), row(ln1_b), bf(w_mq), bf(w_mo), row(ln2_g), row(ln2_b),
                   bf(w_up), conv_w[0].astype(F32), row(conv_b), bf(w_down), row(ln3_g), row(ln3_b))
    lamp = jnp.stack([lambda_q1[0], lambda_k1[0], lambda_q2[0], lambda_k2[0]]).astype(F32)
    g = row(subln_g)
    gcol = jnp.broadcast_to(subln_g[0].astype(F32)[:, None], (2 * HD_B, B_QW))
    bias = _bias_row(rel_bias[0])

    xp = x_prompt.reshape(seq, D_MODEL)
    qat, ka, vat, qbt, kb, vbt, ka_f, va_f, kb_f, vb_f = _qkv(xp, w_qkv_bf, jnp.arange(seq), QKV_TM, True)
    bias_rev = jnp.roll(bias[:, ::-1], 1, axis=1)
    oa = _attn_a_prompt(qat, ka, vat, bias_rev)
    ob = _attn_b_prompt(qbt, kb, vbt, lamp, gcol)
    mk_p, mv_p = _mem_kv(mem_prompt.reshape(N_MEM, D_MODEL), bf(w_mk), bf(w_mv))
    conv0 = jnp.zeros((SUBLANES, 2 * D_FF), F32)
    y_p, conv_p = _tail(xp, oa, ob, mk_p[None], mv_p[None], conv0, tail_params, TAIL_TM, TAIL_TM, TAIL_SUB)

    rows_s = streams * t_new
    xs = x_sample.reshape(rows_s, D_MODEL)
    pos_s = jnp.tile(past_b + jnp.arange(t_new), streams)
    qa_s, ka_s, va_s, qb_s, kb_s, vb_s, ka_sf, va_sf, kb_sf, vb_sf = _qkv(xs, w_qkv_bf, pos_s, rows_s, False)
    oa_s = _attn_a_sample(qa_s, ka_s, va_s,
                          cache_a_k[0].reshape(streams, -1, A_WIDTH), cache_a_v[0].reshape(streams, -1, A_WIDTH),
                          bias, t_new)
    ob_s = _attn_b_sample(qb_s, kb_s, vb_s,
                          cache_b_k[0].reshape(streams, past_b, B_WIDTH),
                          cache_b_v[0], lamp, g, t_new)
    conv_in_s = jnp.pad(state_conv[0].astype(F32), ((0, 0), (SUBLANES - (CONV_W - 1), 0), (0, 0)))
    y_s, conv_s = _tail(xs, oa_s, ob_s,
                        cache_mem_k[0], cache_mem_v[0],
                        conv_in_s.reshape(streams * SUBLANES, 2 * D_FF), tail_params, rows_s, t_new, rows_s)

    keep = CONV_W - 1
    return (
        y_p.reshape(batch, seq, D_MODEL),
        y_s.reshape(streams, t_new, D_MODEL),
        ka_f.reshape(1, batch, QKV_TM, H_A, HD_A),
        va_f.reshape(1, batch, QKV_TM, H_A, HD_A),
        kb_f.reshape(1, batch, seq, H_B, 2, HD_B),
        vb_f.reshape(1, batch, seq, H_B, 2 * HD_B),
        mk_p.reshape(1, batch, N_MEM, H_M, HD_M),
        mv_p.reshape(1, batch, N_MEM, H_M, HD_M),
        conv_p[SUBLANES - keep:].reshape(1, batch, keep, 2 * D_FF),
        ka_sf.reshape(1, streams, t_new, H_A, HD_A),
        va_sf.reshape(1, streams, t_new, H_A, HD_A),
        kb_sf.reshape(1, streams, t_new, H_B, 2, HD_B),
        vb_sf.reshape(1, streams, t_new, H_B, 2 * HD_B),
        conv_s.reshape(streams, SUBLANES, 2 * D_FF)[:, SUBLANES - keep:].reshape(1, streams, keep, 2 * D_FF),
    )
```

```python
import functools
import math

import numpy as np
import jax
import jax.numpy as jnp
from jax import lax
from jax.experimental import pallas as pl
from jax.experimental.pallas import tpu as pltpu

F32 = jnp.float32
BF16 = jnp.bfloat16

D_MODEL = 1024
CHUNK = 64
BAND_PREV = 8
REL_CLIP = 128
H_A = 8
HD_A = 64
H_B = 4
HD_B = 64
ROT_DIM = HD_B // 4
ROPE_THETA = 500000.0
N_MEM = 256
H_M = 4
HD_M = D_MODEL // H_M
D_FF = 2816
CONV_W = 3
LN_EPS = 1e-5
DEPTH = 1
DEEPNORM_ALPHA = (2.0 * DEPTH) ** 0.25
A_WIDTH = H_A * HD_A
B_WIDTH = H_B * 2 * HD_B
LAM_INIT = 0.8 - 0.6 * math.exp(-0.3 * 0)

LANES = 128
SUBLANES = 8
NEG = -1e30
LOG2_E = math.log2(math.e)

QKV_TM = 512
A_TQ = 256
A_LAG = 2
B_TQ = 1024
B_QW = 256
B_HEADS = 2
B_TK = 512
ONES_ROWS = 16
V_ROWS = LANES + ONES_ROWS
VA_ROWS = HD_A + ONES_ROWS
BIAS_SPAN = 1024
TAIL_TM = 512
TAIL_SUB = 256
MXU_DIM = 256
FF_BLOCKS = (4 * MXU_DIM, 4 * MXU_DIM, D_FF - 8 * MXU_DIM)
assert sum(FF_BLOCKS) == D_FF and all(w % MXU_DIM == 0 for w in FF_BLOCKS)

_NT = (((1,), (1,)), ((), ()))


def _dot(a, b):
    return jnp.dot(a, b, preferred_element_type=F32)


def _dot_nt(a, b):
    return lax.dot_general(a, b, _NT, preferred_element_type=F32)


def _const_spec(shape):
    nd = len(shape)
    return pl.BlockSpec(shape, lambda *_: (0,) * nd, pipeline_mode=pl.Buffered(1))


def _arbitrary(n):
    return pltpu.CompilerParams(dimension_semantics=("arbitrary",) * n)


def _rope(z, cos, sin):
    lane = lax.broadcasted_iota(jnp.int32, (z.shape[0], LANES), 1)
    first_half = (lane % HD_B) < (ROT_DIM // 2)
    outs = []
    for c in range(z.shape[1] // LANES):
        zc = z[:, c * LANES:(c + 1) * LANES]
        partner = jnp.where(first_half,
                            pltpu.roll(zc, LANES - ROT_DIM // 2, axis=1),
                            pltpu.roll(zc, ROT_DIM // 2, axis=1))
        outs.append(zc * cos + partner * sin)
    return jnp.concatenate(outs, axis=1)


def _qkv_kernel(x_ref, w_ref, rope_ref,
                qa_ref, ka_ref, va_ref, qb_ref, kb_ref, vb_ref,
                kaf_ref, vaf_ref, kbf_ref, vbf_ref, *, transposed):
    xb = x_ref[...].astype(BF16)
    cos, sin = _spread_rope(rope_ref[...])
    tm = xb.shape[0]

    def section(i):
        return _dot(xb, w_ref[:, i * A_WIDTH:(i + 1) * A_WIDTH])

    vb = section(5)
    for h in range(H_B):
        vbf_ref[:, h, :] = vb[:, h * LANES:(h + 1) * LANES]
    def ones_rows(width):
        return (lax.broadcasted_iota(jnp.int32, (ONES_ROWS, width), 0) == 0).astype(BF16)

    if transposed:
        for h in range(H_B):
            vb_ref[h, 0, 0:LANES, :] = vb[:, h * LANES:(h + 1) * LANES].T.astype(BF16)
            vb_ref[h, 0, LANES:V_ROWS, :] = ones_rows(tm)
    else:
        vb_ref[...] = vb.astype(BF16)
    qb = _rope(section(3), cos, sin) * (HD_B ** -0.5 * (LOG2_E if transposed else 1.0))
    if transposed:
        for h in range(H_B):
            qb_ref[h] = qb[:, h * LANES:(h + 1) * LANES].T.astype(BF16)
    else:
        qb_ref[...] = qb.astype(BF16)
    kb = _rope(section(4), cos, sin)
    kb_ref[...] = kb.astype(BF16)
    kbf_ref[...] = kb
    va = section(2)
    vaf_ref[...] = va
    if transposed:
        for p in range(H_A // 2):
            vt = va[:, p * LANES:(p + 1) * LANES].T.astype(BF16)
            for hh in range(2):
                for t in range(tm // A_TQ):
                    cols = slice(t * A_TQ, (t + 1) * A_TQ)
                    va_ref[2 * p + hh, t, 0:HD_A, :] = vt[hh * HD_A:(hh + 1) * HD_A, cols]
                    va_ref[2 * p + hh, t, HD_A:VA_ROWS, :] = ones_rows(A_TQ)
    else:
        va_ref[...] = va.astype(BF16)
    qa = section(0) * (HD_A ** -0.5 * LOG2_E)
    if transposed:
        for p in range(H_A // 2):
            qa_ref[p] = qa[:, p * LANES:(p + 1) * LANES].T.astype(BF16)
    else:
        qa_ref[...] = qa.astype(BF16)
    ka = section(1)
    ka_ref[...] = ka.astype(BF16)
    kaf_ref[...] = ka


def _rope_table(pos):
    inv = ROPE_THETA ** (-jnp.arange(0, ROT_DIM, 2, dtype=F32) / ROT_DIM)
    ang = pos.astype(F32)[:, None] * inv[None, :]
    cs = lax.optimization_barrier(jnp.concatenate([jnp.cos(ang), jnp.sin(ang)], axis=1))
    return jnp.pad(cs, ((0, 0), (0, LANES - ROT_DIM)))


def _spread_rope(tab):
    half = ROT_DIM // 2
    lane = lax.broadcasted_iota(jnp.int32, tab.shape, 1)
    c = jnp.where(lane < half, tab, 0.0)
    s = jnp.where((lane >= half) & (lane < ROT_DIM), tab, 0.0)
    c = c + pltpu.roll(c, half, axis=1)
    s = s - pltpu.roll(s, LANES - half, axis=1)
    c = c + pltpu.roll(c, HD_B, axis=1)
    s = s + pltpu.roll(s, HD_B, axis=1)
    return jnp.where(lane % HD_B < ROT_DIM, c, 1.0), s


def _qkv(x, w_bf, pos, tm, transposed):
    rows = x.shape[0]
    nt = rows // tm
    rope = _rope_table(pos)
    row_spec = lambda w: pl.BlockSpec((tm, w), lambda i: (i, 0))
    last_spec = pl.BlockSpec((tm, A_WIDTH), lambda i: (0, 0))
    bf = jax.ShapeDtypeStruct((rows, A_WIDTH), BF16)
    if transposed:
        assert tm == B_TK and tm % A_TQ == 0
        qa_spec = pl.BlockSpec((H_A // 2, LANES, tm), lambda i: (0, 0, i))
        qa_shape = jax.ShapeDtypeStruct((H_A // 2, LANES, rows), BF16)
        va_spec = pl.BlockSpec((H_A, tm // A_TQ, VA_ROWS, A_TQ), lambda i: (0, i, 0, 0))
        va_shape = jax.ShapeDtypeStruct((H_A, rows // A_TQ, VA_ROWS, A_TQ), BF16)
        qb_spec = pl.BlockSpec((H_B, LANES, tm), lambda i: (0, 0, i))
        qb_shape = jax.ShapeDtypeStruct((H_B, LANES, rows), BF16)
        vb_spec = pl.BlockSpec((H_B, 1, V_ROWS, tm), lambda i: (0, i, 0, 0))
        vb_shape = jax.ShapeDtypeStruct((H_B, nt, V_ROWS, tm), BF16)
    else:
        qa_spec = va_spec = qb_spec = vb_spec = row_spec(B_WIDTH)
        qa_shape = va_shape = qb_shape = vb_shape = bf
    return pl.pallas_call(
        functools.partial(_qkv_kernel, transposed=transposed),
        grid=(nt,),
        in_specs=[row_spec(D_MODEL), _const_spec(w_bf.shape), row_spec(LANES)],
        out_specs=[qa_spec, row_spec(A_WIDTH), va_spec, qb_spec, row_spec(B_WIDTH), vb_spec]
                  + [last_spec, last_spec, row_spec(B_WIDTH),
                     pl.BlockSpec((tm, H_B, 2 * HD_B), lambda i: (i, 0, 0))],
        out_shape=[qa_shape, bf, va_shape, qb_shape, bf, vb_shape]
                  + [jax.ShapeDtypeStruct((tm, A_WIDTH), F32)] * 2
                  + [jax.ShapeDtypeStruct((rows, B_WIDTH), F32),
                     jax.ShapeDtypeStruct((rows, H_B, 2 * HD_B), F32)],
        compiler_params=_arbitrary(1),
        name="qkv_proj",
    )(x, w_bf, rope)


def _bias_row(rel_bias):
    table = rel_bias.astype(F32)
    far = table[:, 2 * REL_CLIP:]
    near = table[:, :1]
    n_far = 2 * A_TQ - REL_CLIP
    n_near = 3 * A_TQ - (n_far + 2 * REL_CLIP + 1)
    h = table.shape[0]
    return jnp.concatenate([jnp.broadcast_to(far, (h, n_far)), table[:, ::-1],
                            jnp.broadcast_to(near, (h, n_near)),
                            jnp.broadcast_to(far, (h, BIAS_SPAN - 3 * A_TQ))], axis=1)


def _build_band_bias(row_ref, bias_ref):
    _, rows, cols = bias_ref.shape
    q = lax.broadcasted_iota(jnp.int32, (rows, cols), 0)
    k = lax.broadcasted_iota(jnp.int32, (rows, cols), 1)
    qc = q // CHUNK
    kc = k // CHUNK - (2 * A_TQ) // CHUNK
    band = (kc >= qc - BAND_PREV) & (kc <= qc)
    for h in range(H_A):
        r = jnp.broadcast_to(row_ref[h:h + 1, :], (rows, BIAS_SPAN))
        toeplitz = pltpu.roll(r, 0, axis=1, stride=1, stride_axis=0)
        bias_ref[h] = jnp.where(band, toeplitz[:, :cols] * LOG2_E, NEG)


def _pair_queries(q2):
    lane = lax.broadcasted_iota(jnp.int32, q2.shape, 1)
    lo = lane < (LANES // 2)
    zero = jnp.zeros_like(q2)
    return jnp.concatenate([jnp.where(lo, q2, zero), jnp.where(lo, zero, q2)], axis=0), lo


def _build_band_bias_t(row_ref, bias_ref):
    _, n_k, n_q = bias_ref.shape
    k = lax.broadcasted_iota(jnp.int32, (n_k, n_q), 0)
    q = lax.broadcasted_iota(jnp.int32, (n_k, n_q), 1)
    qc = q // CHUNK
    kc = k // CHUNK - (2 * A_TQ) // CHUNK
    band = (kc >= qc - BAND_PREV) & (kc <= qc)
    for h in range(H_A):
        r = jnp.broadcast_to(row_ref[h:h + 1, :], (n_k, BIAS_SPAN))
        toeplitz = pltpu.roll(r, 0, axis=1, stride=1, stride_axis=0)
        bias_ref[h] = jnp.where(band, toeplitz[:, :n_q] * LOG2_E, NEG)


def _attn_a_kernel(qt_ref, k2_ref, k1_ref, k0_ref, v2_ref, v1_ref, v0_ref, row_ref, o_ref, bias_ref):
    i = pl.program_id(0)

    @pl.when(i == 0)
    def _():
        _build_band_bias_t(row_ref, bias_ref)

    def body(check_keys):
        if check_keys:
            kidx = lax.broadcasted_iota(jnp.int32, (3 * A_TQ, A_TQ), 0)
            kvalid = kidx >= (2 - i) * A_TQ
        dim = lax.broadcasted_iota(jnp.int32, (LANES, A_TQ), 0)
        zero = jnp.zeros((LANES, A_TQ), BF16)
        s, o_t = [None] * H_A, [None] * H_A

        def scores(h):
            p = h // 2
            sl = slice(p * LANES, (p + 1) * LANES)
            qz = jnp.where(dim < HD_A if h % 2 == 0 else dim >= HD_A, qt_ref[p], zero)
            kc = jnp.concatenate([k2_ref[:, sl], k1_ref[:, sl], k0_ref[:, sl]], axis=0)
            s[h] = _dot(kc, qz) + bias_ref[h]
            if check_keys:
                s[h] = jnp.where(kvalid, s[h], NEG)

        def values(h):
            e = jnp.exp2(s[h] - jnp.max(s[h], axis=0, keepdims=True)).astype(BF16)
            r = (_dot(v2_ref[h, 0], e[0:A_TQ]) + _dot(v1_ref[h, 0], e[A_TQ:2 * A_TQ])
                 + _dot(v0_ref[h, 0], e[2 * A_TQ:3 * A_TQ]))
            o_t[h] = r[0:HD_A] / r[HD_A:HD_A + 1]
            if h % 2 == 1:
                p = h // 2
                o_ref[:, p * LANES:(p + 1) * LANES] = jnp.concatenate([o_t[h - 1], o_t[h]],
                                                                      axis=0).T.astype(BF16)

        for h in range(H_A + A_LAG):
            if h < H_A:
                scores(h)
            if h >= A_LAG:
                values(h - A_LAG)

    first_steps = BAND_PREV * CHUNK // A_TQ
    pl.when(i < first_steps)(lambda: body(True))
    pl.when(i >= first_steps)(lambda: body(False))


def _attn_a_prompt(qat, ka, vat, bias_row_rev):
    rows = ka.shape[0]
    prev = lambda d: (lambda i: (jnp.maximum(i - d, 0), 0))
    kspec = lambda d: pl.BlockSpec((A_TQ, A_WIDTH), prev(d))
    vspec = lambda d: pl.BlockSpec((H_A, 1, VA_ROWS, A_TQ), lambda i: (0, jnp.maximum(i - d, 0), 0, 0))
    return pl.pallas_call(
        _attn_a_kernel,
        grid=(rows // A_TQ,),
        in_specs=[pl.BlockSpec((H_A // 2, LANES, A_TQ), lambda i: (0, 0, i)),
                  kspec(2), kspec(1), kspec(0), vspec(2), vspec(1), vspec(0), _const_spec(bias_row_rev.shape)],
        out_specs=pl.BlockSpec((A_TQ, A_WIDTH), lambda i: (i, 0)),
        out_shape=jax.ShapeDtypeStruct((rows, A_WIDTH), BF16),
        scratch_shapes=[pltpu.VMEM((H_A, 3 * A_TQ, A_TQ), F32)],
        compiler_params=_arbitrary(1),
        name="attn_a_prompt",
    )(qat, ka, ka, ka, vat, vat, vat, bias_row_rev)


def _attn_a_sample_kernel(q_ref, kn_ref, vn_ref, kc_ref, vc_ref, row_ref, o_ref, bias_ref):
    t = q_ref.shape[0]
    past = kc_ref.shape[1]

    @pl.when(pl.program_id(0) == 0)
    def _():
        _build_band_bias(row_ref, bias_ref)

    kc = kc_ref[0].astype(BF16)
    vc = vc_ref[0].astype(BF16)
    for p in range(H_A // 2):
        sl = slice(p * LANES, (p + 1) * LANES)
        qz, lo = _pair_queries(q_ref[:, sl])
        b_past = jnp.concatenate([bias_ref[2 * p, 0:t, 0:past], bias_ref[2 * p + 1, 0:t, 0:past]], axis=0)
        b_new = jnp.concatenate([bias_ref[2 * p, 0:t, past:past + t],
                                 bias_ref[2 * p + 1, 0:t, past:past + t]], axis=0)
        s_past = _dot_nt(qz, kc[:, sl]) + b_past
        s_new = _dot_nt(qz, kn_ref[:, sl]) + b_new
        m = jnp.maximum(jnp.max(s_past, axis=-1, keepdims=True), jnp.max(s_new, axis=-1, keepdims=True))
        e_past = jnp.exp2(s_past - m)
        e_new = jnp.exp2(s_new - m)
        l = jnp.sum(e_past, axis=-1, keepdims=True) + jnp.sum(e_new, axis=-1, keepdims=True)
        o = (_dot(e_past.astype(BF16), vc[:, sl]) + _dot(e_new.astype(BF16), vn_ref[:, sl])) / l
        o_ref[:, sl] = jnp.where(lo, o[:t], o[t:]).astype(BF16)


def _attn_a_sample(qa, ka, va, cache_k, cache_v, bias_row, t):
    rows = qa.shape[0]
    streams, past, width = cache_k.shape
    assert past == 2 * A_TQ and t <= CHUNK and t % SUBLANES == 0
    row = pl.BlockSpec((t, A_WIDTH), lambda s: (s, 0))
    cache = pl.BlockSpec((1, past, width), lambda s: (s, 0, 0))
    return pl.pallas_call(
        _attn_a_sample_kernel,
        grid=(streams,),
        in_specs=[row, row, row, cache, cache, _const_spec(bias_row.shape)],
        out_specs=row,
        out_shape=jax.ShapeDtypeStruct((rows, A_WIDTH), BF16),
        scratch_shapes=[pltpu.VMEM((H_A, t, past + LANES), F32)],
        compiler_params=_arbitrary(1),
        name="attn_a_sample",
    )(qa, ka, va, cache_k, cache_v, bias_row)


def _lambda(lamp_ref):
    lp = lamp_ref[...]
    a = jnp.sum(lp[0:1] * lp[1:2], axis=-1, keepdims=True)
    b = jnp.sum(lp[2:3] * lp[3:4], axis=-1, keepdims=True)
    return jnp.exp(a) - jnp.exp(b) + LAM_INIT


def _diff_finish(o0, o1, lam, g):
    o = o0 - lam * o1
    o = o * lax.rsqrt(jnp.mean(o * o, axis=-1, keepdims=True) + LN_EPS) * g * (1.0 - LAM_INIT)
    return o.astype(BF16)


def _attn_b_kernel(qt_ref, k_ref, vt_ref, lamp_ref, gcol_ref, o_ref, acc_ref, p_ref, alpha_ref):
    qi = pl.program_id(1)
    n_parts = B_TQ // B_QW
    dim = lax.broadcasted_iota(jnp.int32, (LANES, B_QW), 0)
    zero = jnp.zeros((LANES, B_QW), BF16)
    chains = [(part, (h, mp)) for part in range(n_parts) for h in range(B_HEADS) for mp in range(2)]
    qz = []
    for part, (h, mp) in chains:
        qt = qt_ref[h, :, part * B_QW:(part + 1) * B_QW]
        qz.append(jnp.where(dim < HD_B if mp == 0 else dim >= HD_B, qt, zero))
    acc_ref[...] = jnp.zeros(acc_ref.shape, F32)
    p_ref[1] = jnp.zeros(p_ref.shape[1:], BF16)
    alpha_ref[1] = jnp.ones(alpha_ref.shape[1:], F32)

    def pending_values(c, j_prev, buf):
        h = chains[c][1][0]
        acc_ref[c] = alpha_ref[buf, c] * acc_ref[c] + _dot(vt_ref[h, j_prev], p_ref[buf, c])

    def step(j, ms, buf, parts, diag_parts, pending_parts):
        start = pl.multiple_of(j * B_TK, B_TK)
        j_prev = jnp.maximum(j - 1, 0)
        kpos = start + lax.broadcasted_iota(jnp.int32, (B_TK, B_QW), 0)
        lane = lax.broadcasted_iota(jnp.int32, (B_TK, B_QW), 1)
        out = list(ms)
        for c, (part, (h, _)) in enumerate(chains):
            if part in parts:
                s = _dot(k_ref[pl.ds(start, B_TK), h * LANES:(h + 1) * LANES], qz[c])
                if part in diag_parts:
                    qpos = qi * B_TQ + part * B_QW + lane
                    s = jnp.where(kpos < (qpos // CHUNK + 1) * CHUNK, s, NEG)
                out[c] = jnp.maximum(ms[c], jnp.max(s, axis=0, keepdims=True))
                alpha_ref[buf, c] = jnp.exp2(ms[c] - out[c])
                p_ref[buf, c] = jnp.exp2(s - out[c]).astype(BF16)
            if part in pending_parts:
                pending_values(c, j_prev, 1 - buf)
        return tuple(out)

    every = tuple(range(n_parts))
    low, high = every[:n_parts // 2], every[n_parts // 2:]

    def pair(i, ms):
        ms = step(2 * i, ms, 0, every, (), every)
        return step(2 * i + 1, ms, 1, every, (), every)

    m_init = tuple(jnp.full((1, B_QW), NEG, F32) for _ in chains)
    ms = lax.fori_loop(0, qi, pair, m_init)
    ms = step(2 * qi, ms, 0, every, low, every)
    step(2 * qi + 1, ms, 1, high, high, every)
    for c, (part, _) in enumerate(chains):
        if part in high:
            pending_values(c, 2 * qi + 1, 1)

    lam = _lambda(lamp_ref)
    for c in range(0, len(chains), 2):
        part, (h, _) = chains[c]
        a0 = acc_ref[c]
        a1 = acc_ref[c + 1]
        o = a0[0:LANES] / a0[LANES:LANES + 1] - lam * (a1[0:LANES] / a1[LANES:LANES + 1])
        o = o * lax.rsqrt(jnp.mean(o * o, axis=0, keepdims=True) + LN_EPS) * gcol_ref[...] * (1.0 - LAM_INIT)
        o_ref[part * B_QW:(part + 1) * B_QW, h * LANES:(h + 1) * LANES] = o.T.astype(BF16)


def _attn_b_prompt(qbt, kb, vbt, lamp, gcol):
    rows = kb.shape[0]
    assert rows % B_TQ == 0 and B_TQ == 2 * B_TK and B_TK == 2 * B_QW and B_QW % CHUNK == 0
    assert H_B % B_HEADS == 0
    n_chains = (B_TQ // B_QW) * B_HEADS * 2
    single = pl.Buffered(1)
    return pl.pallas_call(
        _attn_b_kernel,
        grid=(H_B // B_HEADS, rows // B_TQ),
        in_specs=[pl.BlockSpec((B_HEADS, LANES, B_TQ), lambda h, i: (h, 0, i)),
                  pl.BlockSpec((rows, B_HEADS * LANES), lambda h, i: (0, h), pipeline_mode=single),
                  pl.BlockSpec((B_HEADS,) + vbt.shape[1:], lambda h, i: (h, 0, 0, 0), pipeline_mode=single),
                  _const_spec(lamp.shape), _const_spec(gcol.shape)],
        out_specs=pl.BlockSpec((B_TQ, B_HEADS * LANES), lambda h, i: (i, h)),
        out_shape=jax.ShapeDtypeStruct((rows, B_WIDTH), BF16),
        scratch_shapes=[pltpu.VMEM((n_chains, V_ROWS, B_QW), F32),
                        pltpu.VMEM((2, n_chains, B_TK, B_QW), BF16), pltpu.VMEM((2, n_chains, 1, B_QW), F32)],
        compiler_params=_arbitrary(2),
        name="attn_b_prompt",
    )(qbt, kb, vbt, lamp, gcol)


def _attn_b_sample_kernel(q_ref, kn_ref, vn_ref, kc_ref, vc_ref, lamp_ref, g_ref, o_ref):
    t = q_ref.shape[0]
    lam = _lambda(lamp_ref)
    g = g_ref[...]
    for h in range(H_B):
        sl = slice(h * LANES, (h + 1) * LANES)
        qz, _ = _pair_queries(q_ref[:, sl])
        s_past = _dot_nt(qz, kc_ref[0, :, sl].astype(BF16))
        s_new = _dot_nt(qz, kn_ref[:, sl])
        m = jnp.maximum(jnp.max(s_past, axis=-1, keepdims=True), jnp.max(s_new, axis=-1, keepdims=True))
        e_past = jnp.exp(s_past - m)
        e_new = jnp.exp(s_new - m)
        l = jnp.sum(e_past, axis=-1, keepdims=True) + jnp.sum(e_new, axis=-1, keepdims=True)
        o = (_dot(e_past.astype(BF16), vc_ref[0, :, h, :].astype(BF16))
             + _dot(e_new.astype(BF16), vn_ref[:, sl])) / l
        o_ref[:, sl] = _diff_finish(o[:t], o[t:], lam, g)


def _attn_b_sample(qb, kb, vb, cache_k, cache_v, lamp, g, t):
    rows = qb.shape[0]
    streams, past, width = cache_k.shape
    assert past % CHUNK == 0 and t <= CHUNK
    row = pl.BlockSpec((t, B_WIDTH), lambda s: (s, 0))
    cache = pl.BlockSpec((1, past, width), lambda s: (s, 0, 0))
    cache_v_spec = pl.BlockSpec((1,) + cache_v.shape[1:], lambda s: (s, 0, 0, 0))
    return pl.pallas_call(
        _attn_b_sample_kernel,
        grid=(streams,),
        in_specs=[row, row, row, cache, cache_v_spec, _const_spec(lamp.shape), _const_spec(g.shape)],
        out_specs=row,
        out_shape=jax.ShapeDtypeStruct((rows, B_WIDTH), BF16),
        compiler_params=_arbitrary(1),
        name="attn_b_sample",
    )(qb, kb, vb, cache_k, cache_v, lamp, g)


def _mem_kv_kernel(x_ref, wk_ref, wv_ref, kf_ref, vf_ref):
    xb = x_ref[...].astype(BF16)
    kf_ref[...] = _dot(xb, wk_ref[...])
    vf_ref[...] = _dot(xb, wv_ref[...])


def _mem_kv(mem, wk_bf, wv_bf):
    f = jax.ShapeDtypeStruct(mem.shape, F32)
    return pl.pallas_call(
        _mem_kv_kernel,
        out_shape=[f, f],
        name="mem_kv",
    )(mem, wk_bf, wv_bf)


def _layer_norm(x, g, b):
    mu = jnp.mean(x, axis=-1, keepdims=True)
    xc = x - mu
    var = jnp.mean(xc * xc, axis=-1, keepdims=True)
    return xc * lax.rsqrt(var + LN_EPS) * g + b


def _tail_kernel(x_ref, oa_ref, ob_ref, mk_ref, mv_ref, cin_ref,
                 wo_ref, ln1g_ref, ln1b_ref, wmq_ref, wmo_ref, ln2g_ref, ln2b_ref,
                 wup_ref, cw_ref, cb_ref, wdn_ref, ln3g_ref, ln3b_ref,
                 y_ref, carry_ref, *, seg, sub):
    tm = x_ref.shape[0]
    n_seg = tm // seg
    n_sub = tm // sub
    assert n_seg == 1 or n_sub == 1
    subs = [slice(i * sub, (i + 1) * sub) for i in range(n_sub)]

    @pl.when(pl.program_id(0) == 0)
    def _():
        carry_ref[...] = cin_ref[...]

    def mem_attend(q, mk, mv):
        heads = []
        for h in range(H_M):
            sl = slice(h * HD_M, (h + 1) * HD_M)
            s = _dot_nt(q[:, sl], mk[:, sl])
            e = jnp.exp(s - jnp.max(s, axis=-1, keepdims=True))
            l = jnp.sum(e, axis=-1, keepdims=True)
            heads.append((_dot(e.astype(BF16), mv[:, sl]) / l).astype(BF16))
        return jnp.concatenate(heads, axis=1)

    def mem_attend_rows(q):
        if mk_ref.shape[0] == 1:
            return mem_attend(q, mk_ref[0].astype(BF16), mv_ref[0].astype(BF16))
        return jnp.concatenate([mem_attend(q[sg * seg:(sg + 1) * seg],
                                           mk_ref[sg].astype(BF16), mv_ref[sg].astype(BF16))
                                for sg in range(n_seg)], axis=0)

    mix = [_dot(oa_ref[sl, :], wo_ref[0:A_WIDTH, :]) + _dot(ob_ref[sl, :], wo_ref[A_WIDTH:A_WIDTH + B_WIDTH, :])
           for sl in subs]
    x1 = [_layer_norm(DEEPNORM_ALPHA * x_ref[sl, :] + m, ln1g_ref[...], ln1b_ref[...]) for sl, m in zip(subs, mix)]
    qm = [(_dot(v.astype(BF16), wmq_ref[...]) * (HD_M ** -0.5)).astype(BF16) for v in x1]
    om = [mem_attend_rows(q) for q in qm]
    mo = [_dot(v, wmo_ref[...]) for v in om]
    x2 = [_layer_norm(DEEPNORM_ALPHA * a + b, ln2g_ref[...], ln2b_ref[...]) for a, b in zip(x1, mo)]
    x2b = [v.astype(BF16) for v in x2]

    def conv(us, col, width):
        cols = slice(col, col + width)
        w0, w1, w2, b = cw_ref[0:1, cols], cw_ref[1:2, cols], cw_ref[2:3, cols], cb_ref[0:1, cols]

        def taps(u, p2, p1):
            row = lax.broadcasted_iota(jnp.int32, u.shape, 0)
            u1 = jnp.where(row == 0, p1, pltpu.roll(u, 1, axis=0))
            u2 = jnp.where(row == 0, p2, jnp.where(row == 1, p1, pltpu.roll(u, 2, axis=0)))
            return u2 * w0 + u1 * w1 + u * w2 + b

        def state(sg):
            base = SUBLANES * sg
            return carry_ref[base + 6:base + 7, cols], carry_ref[base + 7:base + 8, cols]

        if n_seg == 1:
            p2, p1 = state(0)
            outs = []
            for u in us:
                outs.append(taps(u, p2, p1))
                p2, p1 = u[sub - 2:sub - 1], u[sub - 1:sub]
            carry_ref[0:SUBLANES, cols] = us[-1][sub - SUBLANES:sub]
            return outs
        (u,) = us
        outs = []
        for sg in range(n_seg):
            useg = u[sg * seg:(sg + 1) * seg]
            outs.append(taps(useg, *state(sg)))
            carry_ref[SUBLANES * sg:SUBLANES * (sg + 1), cols] = useg[seg - SUBLANES:seg]
        return [jnp.concatenate(outs, axis=0)]

    f = [jnp.zeros((sub, D_MODEL), F32) for _ in subs]
    c0 = 0
    for width in FF_BLOCKS:
        gate = conv([_dot(v, wup_ref[:, c0:c0 + width]) for v in x2b], c0, width)
        val = conv([_dot(v, wup_ref[:, D_FF + c0:D_FF + c0 + width]) for v in x2b], D_FF + c0, width)
        hid = [(g * (1.0 / (1.0 + jnp.exp(-g))) * v).astype(BF16) for g, v in zip(gate, val)]
        f = [a + _dot(h, wdn_ref[c0:c0 + width, :]) for a, h in zip(f, hid)]
        c0 += width
    for sl, a, b in zip(subs, x2, f):
        y_ref[sl, :] = _layer_norm(DEEPNORM_ALPHA * a + b, ln3g_ref[...], ln3b_ref[...])


def _tail(x, oa, ob, mk, mv, conv_in, params, tm, seg, sub):
    rows = x.shape[0]
    n_seg = tm // seg
    assert tm % sub == 0 and sub % SUBLANES == 0
    assert conv_in.shape == (SUBLANES * n_seg, 2 * D_FF)
    assert n_seg == 1 or rows == tm
    assert mk.shape[0] in (1, n_seg)
    row_spec = lambda w: pl.BlockSpec((tm, w), lambda i: (i, 0))
    carry_spec = pl.BlockSpec(conv_in.shape, lambda i: (0, 0))
    return pl.pallas_call(
        functools.partial(_tail_kernel, seg=seg, sub=sub),
        grid=(rows // tm,),
        in_specs=[row_spec(D_MODEL), row_spec(A_WIDTH), row_spec(B_WIDTH),
                  _const_spec(mk.shape), _const_spec(mv.shape),
                  _const_spec(conv_in.shape)] + [_const_spec(p.shape) for p in params],
        out_specs=[row_spec(D_MODEL), carry_spec],
        out_shape=[jax.ShapeDtypeStruct((rows, D_MODEL), F32),
                   jax.ShapeDtypeStruct(conv_in.shape, F32)],
        compiler_params=_arbitrary(1),
        name="tail",
    )(x, oa, ob, mk, mv, conv_in, *params)


def kernel(x_prompt, x_sample, mem_prompt, cache_a_k, cache_a_v, cache_b_k, cache_b_v, cache_mem_k, cache_mem_v,
           state_conv, w_qkv, rel_bias, lambda_q1, lambda_k1, lambda_q2, lambda_k2, subln_g, w_o, ln1_g, ln1_b,
           w_mq, w_mk, w_mv, w_mo, ln2_g, ln2_b, w_up, conv_w, conv_b, w_down, ln3_g, ln3_b):
    assert w_qkv.shape[0] == DEPTH == 1
    batch, seq, _ = x_prompt.shape
    streams, t_new, _ = x_sample.shape
    past_b = cache_b_k.shape[2]
    assert batch == 1 and seq % QKV_TM == 0 and QKV_TM == BAND_PREV * CHUNK

    bf = lambda w: w[0].astype(BF16)
    row = lambda v: v[0].astype(F32)[None, :]
    w_qkv_bf = bf(w_qkv)
    tail_params = (bf(w_o), row(ln1_g), row(ln1_b), bf(w_mq), bf(w_mo), row(ln2_g), row(ln2_b),
                   bf(w_up), conv_w[0].astype(F32), row(conv_b), bf(w_down), row(ln3_g), row(ln3_b))
    lamp = jnp.stack([lambda_q1[0], lambda_k1[0], lambda_q2[0], lambda_k2[0]]).astype(F32)
    g = row(subln_g)
    gcol = jnp.broadcast_to(subln_g[0].astype(F32)[:, None], (2 * HD_B, B_QW))
    bias = _bias_row(rel_bias[0])

    xp = x_prompt.reshape(seq, D_MODEL)
    qat, ka, vat, qbt, kb, vbt, ka_f, va_f, kb_f, vb_f = _qkv(xp, w_qkv_bf, jnp.arange(seq), QKV_TM, True)
    bias_rev = jnp.roll(bias[:, ::-1], 1, axis=1)
    oa = _attn_a_prompt(qat, ka, vat, bias_rev)
    ob = _attn_b_prompt(qbt, kb, vbt, lamp, gcol)
    mk_p, mv_p = _mem_kv(mem_prompt.reshape(N_MEM, D_MODEL), bf(w_mk), bf(w_mv))
    conv0 = jnp.zeros((SUBLANES, 2 * D_FF), F32)
    y_p, conv_p = _tail(xp, oa, ob, mk_p[None], mv_p[None], conv0, tail_params, TAIL_TM, TAIL_TM, TAIL_SUB)

    rows_s = streams * t_new
    xs = x_sample.reshape(rows_s, D_MODEL)
    pos_s = jnp.tile(past_b + jnp.arange(t_new), streams)
    qa_s, ka_s, va_s, qb_s, kb_s, vb_s, ka_sf, va_sf, kb_sf, vb_sf = _qkv(xs, w_qkv_bf, pos_s, rows_s, False)
    oa_s = _attn_a_sample(qa_s, ka_s, va_s,
                          cache_a_k[0].reshape(streams, -1, A_WIDTH), cache_a_v[0].reshape(streams, -1, A_WIDTH),
                          bias, t_new)
    ob_s = _attn_b_sample(qb_s, kb_s, vb_s,
                          cache_b_k[0].reshape(streams, past_b, B_WIDTH),
                          cache_b_v[0], lamp, g, t_new)
    conv_in_s = jnp.pad(state_conv[0].astype(F32), ((0, 0), (SUBLANES - (CONV_W - 1), 0), (0, 0)))
    y_s, conv_s = _tail(xs, oa_s, ob_s,
                        cache_mem_k[0].reshape(streams, N_MEM, D_MODEL).astype(BF16),
                        cache_mem_v[0].reshape(streams, N_MEM, D_MODEL).astype(BF16),
                        conv_in_s.reshape(streams * SUBLANES, 2 * D_FF), tail_params, rows_s, t_new, rows_s)

    keep = CONV_W - 1
    return (
        y_p.reshape(batch, seq, D_MODEL),
        y_s.reshape(streams, t_new, D_MODEL),
        ka_f.reshape(1, batch, QKV_TM, H_A, HD_A),
        va_f.reshape(1, batch, QKV_TM, H_A, HD_A),
        kb_f.reshape(1, batch, seq, H_B, 2, HD_B),
        vb_f.reshape(1, batch, seq, H_B, 2 * HD_B),
        mk_p.reshape(1, batch, N_MEM, H_M, HD_M),
        mv_p.reshape(1, batch, N_MEM, H_M, HD_M),
        conv_p[SUBLANES - keep:].reshape(1, batch, keep, 2 * D_FF),
        ka_sf.reshape(1, streams, t_new, H_A, HD_A),
        va_sf.reshape(1, streams, t_new, H_A, HD_A),
        kb_sf.reshape(1, streams, t_new, H_B, 2, HD_B),
        vb_sf.reshape(1, streams, t_new, H_B, 2 * HD_B),
        conv_s.reshape(streams, SUBLANES, 2 * D_FF)[:, SUBLANES - keep:].reshape(1, streams, keep, 2 * D_FF),
    )
```

```python
import functools
import math

import jax
import jax.numpy as jnp
from jax import lax
from jax.experimental import pallas as pl
from jax.experimental.pallas import tpu as pltpu

F32 = jnp.float32
BF16 = jnp.bfloat16

D_MODEL = 1024
CHUNK = 64
BAND_PREV = 8
REL_CLIP = 128
H_A = 8
HD_A = 64
H_B = 4
HD_B = 64
ROT_DIM = HD_B // 4
ROPE_THETA = 500000.0
N_MEM = 256
H_M = 4
HD_M = D_MODEL // H_M
D_FF = 2816
CONV_W = 3
LN_EPS = 1e-5
DEPTH = 1
DEEPNORM_ALPHA = (2.0 * DEPTH) ** 0.25
A_WIDTH = H_A * HD_A
B_WIDTH = H_B * 2 * HD_B
LAM_INIT = 0.8 - 0.6 * math.exp(-0.3 * 0)

LANES = 128
SUBLANES = 8
BF16_SUBLANES = 16
NEG = -1e30
LOG2_E = math.log2(math.e)

QKV_TM = 512
A_TQ = 256
A_LAG = 2
B_TQ = 1024
B_QW = 256
B_HEADS = 2
B_TK = 512
ONES_ROWS = 16
V_ROWS = LANES + ONES_ROWS
VA_ROWS = HD_A + ONES_ROWS
BIAS_SPAN = 1024
TAIL_TM = 512
TAIL_SUB = 256
MXU_DIM = 256
FF_BLOCKS = (4 * MXU_DIM, 4 * MXU_DIM, D_FF - 8 * MXU_DIM)
assert sum(FF_BLOCKS) == D_FF and all(w % MXU_DIM == 0 for w in FF_BLOCKS)

_NT = (((1,), (1,)), ((), ()))


def _dot(a, b):
    return jnp.dot(a, b, preferred_element_type=F32)


def _dot_nt(a, b):
    return lax.dot_general(a, b, _NT, preferred_element_type=F32)


def _const_spec(shape):
    nd = len(shape)
    return pl.BlockSpec(shape, lambda *_: (0,) * nd, pipeline_mode=pl.Buffered(1))


def _arbitrary(n):
    return pltpu.CompilerParams(dimension_semantics=("arbitrary",) * n)


def _rope(z, cos, sin):
    lane = lax.broadcasted_iota(jnp.int32, (z.shape[0], LANES), 1)
    first_half = (lane % HD_B) < (ROT_DIM // 2)
    outs = []
    for c in range(z.shape[1] // LANES):
        zc = z[:, c * LANES:(c + 1) * LANES]
        partner = jnp.where(first_half,
                            pltpu.roll(zc, LANES - ROT_DIM // 2, axis=1),
                            pltpu.roll(zc, ROT_DIM // 2, axis=1))
        outs.append(zc * cos + partner * sin)
    return jnp.concatenate(outs, axis=1)


def _qkv_kernel(x_ref, w_ref, rope_ref,
                qa_ref, ka_ref, va_ref, qb_ref, kb_ref, vb_ref,
                kaf_ref, vaf_ref, kbf_ref, vbf_ref, *, transposed):
    xb = x_ref[...].astype(BF16)
    cos, sin = _spread_rope(rope_ref[...])
    tm = xb.shape[0]

    def section(i):
        return _dot(xb, w_ref[:, i * A_WIDTH:(i + 1) * A_WIDTH])

    vb = section(5)
    for h in range(H_B):
        vbf_ref[:, h, :] = vb[:, h * LANES:(h + 1) * LANES]
    def ones_rows(width):
        return (lax.broadcasted_iota(jnp.int32, (ONES_ROWS, width), 0) == 0).astype(BF16)

    if transposed:
        for h in range(H_B):
            vb_ref[h, 0, 0:LANES, :] = vb[:, h * LANES:(h + 1) * LANES].T.astype(BF16)
            vb_ref[h, 0, LANES:V_ROWS, :] = ones_rows(tm)
    else:
        vb_ref[...] = vb.astype(BF16)
    qb = _rope(section(3), cos, sin) * (HD_B ** -0.5 * (LOG2_E if transposed else 1.0))
    if transposed:
        for h in range(H_B):
            qb_ref[h] = qb[:, h * LANES:(h + 1) * LANES].T.astype(BF16)
    else:
        qb_ref[...] = qb.astype(BF16)
    kb = _rope(section(4), cos, sin)
    kb_ref[...] = kb.astype(BF16)
    kbf_ref[...] = kb
    va = section(2)
    vaf_ref[...] = va
    if transposed:
        for p in range(H_A // 2):
            vt = va[:, p * LANES:(p + 1) * LANES].T.astype(BF16)
            for hh in range(2):
                for t in range(tm // A_TQ):
                    cols = slice(t * A_TQ, (t + 1) * A_TQ)
                    va_ref[2 * p + hh, t, 0:HD_A, :] = vt[hh * HD_A:(hh + 1) * HD_A, cols]
                    va_ref[2 * p + hh, t, HD_A:VA_ROWS, :] = ones_rows(A_TQ)
    else:
        va_ref[...] = va.astype(BF16)
    qa = section(0) * (HD_A ** -0.5 * LOG2_E)
    if transposed:
        for p in range(H_A // 2):
            qa_ref[p] = qa[:, p * LANES:(p + 1) * LANES].T.astype(BF16)
    else:
        qa_ref[...] = qa.astype(BF16)
    ka = section(1)
    ka_ref[...] = ka.astype(BF16)
    kaf_ref[...] = ka


def _rope_table(pos):
    inv = ROPE_THETA ** (-jnp.arange(0, ROT_DIM, 2, dtype=F32) / ROT_DIM)
    ang = pos.astype(F32)[:, None] * inv[None, :]
    cs = lax.optimization_barrier(jnp.concatenate([jnp.cos(ang), jnp.sin(ang)], axis=1))
    return jnp.pad(cs, ((0, 0), (0, LANES - ROT_DIM)))


def _spread_rope(tab):
    half = ROT_DIM // 2
    lane = lax.broadcasted_iota(jnp.int32, tab.shape, 1)
    c = jnp.where(lane < half, tab, 0.0)
    s = jnp.where((lane >= half) & (lane < ROT_DIM), tab, 0.0)
    c = c + pltpu.roll(c, half, axis=1)
    s = s - pltpu.roll(s, LANES - half, axis=1)
    c = c + pltpu.roll(c, HD_B, axis=1)
    s = s + pltpu.roll(s, HD_B, axis=1)
    return jnp.where(lane % HD_B < ROT_DIM, c, 1.0), s


def _qkv(x, w_bf, pos, tm, transposed):
    rows = x.shape[0]
    nt = rows // tm
    rope = _rope_table(pos)
    row_spec = lambda w: pl.BlockSpec((tm, w), lambda i: (i, 0))
    last_spec = pl.BlockSpec((tm, A_WIDTH), lambda i: (0, 0))
    bf = jax.ShapeDtypeStruct((rows, A_WIDTH), BF16)
    if transposed:
        assert tm == B_TK and tm % A_TQ == 0
        qa_spec = pl.BlockSpec((H_A // 2, LANES, tm), lambda i: (0, 0, i))
        qa_shape = jax.ShapeDtypeStruct((H_A // 2, LANES, rows), BF16)
        va_spec = pl.BlockSpec((H_A, tm // A_TQ, VA_ROWS, A_TQ), lambda i: (0, i, 0, 0))
        va_shape = jax.ShapeDtypeStruct((H_A, rows // A_TQ, VA_ROWS, A_TQ), BF16)
        qb_spec = pl.BlockSpec((H_B, LANES, tm), lambda i: (0, 0, i))
        qb_shape = jax.ShapeDtypeStruct((H_B, LANES, rows), BF16)
        vb_spec = pl.BlockSpec((H_B, 1, V_ROWS, tm), lambda i: (0, i, 0, 0))
        vb_shape = jax.ShapeDtypeStruct((H_B, nt, V_ROWS, tm), BF16)
    else:
        qa_spec = va_spec = qb_spec = vb_spec = row_spec(B_WIDTH)
        qa_shape = va_shape = qb_shape = vb_shape = bf
    return pl.pallas_call(
        functools.partial(_qkv_kernel, transposed=transposed),
        grid=(nt,),
        in_specs=[row_spec(D_MODEL), _const_spec(w_bf.shape), row_spec(LANES)],
        out_specs=[qa_spec, row_spec(A_WIDTH), va_spec, qb_spec, row_spec(B_WIDTH), vb_spec]
                  + [last_spec, last_spec, row_spec(B_WIDTH),
                     pl.BlockSpec((tm, H_B, 2 * HD_B), lambda i: (i, 0, 0))],
        out_shape=[qa_shape, bf, va_shape, qb_shape, bf, vb_shape]
                  + [jax.ShapeDtypeStruct((tm, A_WIDTH), F32)] * 2
                  + [jax.ShapeDtypeStruct((rows, B_WIDTH), F32),
                     jax.ShapeDtypeStruct((rows, H_B, 2 * HD_B), F32)],
        compiler_params=_arbitrary(1),
        name="qkv_proj",
    )(x, w_bf, rope)


def _bias_row(rel_bias):
    table = rel_bias.astype(F32)
    far = table[:, 2 * REL_CLIP:]
    near = table[:, :1]
    n_far = 2 * A_TQ - REL_CLIP
    n_near = 3 * A_TQ - (n_far + 2 * REL_CLIP + 1)
    h = table.shape[0]
    return jnp.concatenate([jnp.broadcast_to(far, (h, n_far)), table[:, ::-1],
                            jnp.broadcast_to(near, (h, n_near)),
                            jnp.broadcast_to(far, (h, BIAS_SPAN - 3 * A_TQ))], axis=1)


def _build_band_bias(row_ref, bias_ref):
    _, rows, cols = bias_ref.shape
    q = lax.broadcasted_iota(jnp.int32, (rows, cols), 0)
    k = lax.broadcasted_iota(jnp.int32, (rows, cols), 1)
    qc = q // CHUNK
    kc = k // CHUNK - (2 * A_TQ) // CHUNK
    band = (kc >= qc - BAND_PREV) & (kc <= qc)
    for h in range(H_A):
        r = jnp.broadcast_to(row_ref[h:h + 1, :], (rows, BIAS_SPAN))
        toeplitz = pltpu.roll(r, 0, axis=1, stride=1, stride_axis=0)
        bias_ref[h] = jnp.where(band, toeplitz[:, :cols] * LOG2_E, NEG)


def _pair_queries(q2):
    lane = lax.broadcasted_iota(jnp.int32, q2.shape, 1)
    lo = lane < (LANES // 2)
    zero = jnp.zeros_like(q2)
    return jnp.concatenate([jnp.where(lo, q2, zero), jnp.where(lo, zero, q2)], axis=0), lo


def _build_band_bias_t(row_ref, bias_ref):
    _, n_k, n_q = bias_ref.shape
    k = lax.broadcasted_iota(jnp.int32, (n_k, n_q), 0)
    q = lax.broadcasted_iota(jnp.int32, (n_k, n_q), 1)
    qc = q // CHUNK
    kc = k // CHUNK - (2 * A_TQ) // CHUNK
    band = (kc >= qc - BAND_PREV) & (kc <= qc)
    for h in range(H_A):
        r = jnp.broadcast_to(row_ref[h:h + 1, :], (n_k, BIAS_SPAN))
        toeplitz = pltpu.roll(r, 0, axis=1, stride=1, stride_axis=0)
        bias_ref[h] = jnp.where(band, toeplitz[:, :n_q] * LOG2_E, NEG)


def _attn_a_kernel(qt_ref, k2_ref, k1_ref, k0_ref, v2_ref, v1_ref, v0_ref, row_ref, o_ref, bias_ref):
    i = pl.program_id(0)

    @pl.when(i == 0)
    def _():
        _build_band_bias_t(row_ref, bias_ref)

    def body(check_keys):
        if check_keys:
            kidx = lax.broadcasted_iota(jnp.int32, (3 * A_TQ, A_TQ), 0)
            kvalid = kidx >= (2 - i) * A_TQ
        dim = lax.broadcasted_iota(jnp.int32, (LANES, A_TQ), 0)
        zero = jnp.zeros((LANES, A_TQ), BF16)
        s, o_t = [None] * H_A, [None] * H_A

        def scores(h):
            p = h // 2
            sl = slice(p * LANES, (p + 1) * LANES)
            qz = jnp.where(dim < HD_A if h % 2 == 0 else dim >= HD_A, qt_ref[p], zero)
            kc = jnp.concatenate([k2_ref[:, sl], k1_ref[:, sl], k0_ref[:, sl]], axis=0)
            s[h] = _dot(kc, qz) + bias_ref[h]
            if check_keys:
                s[h] = jnp.where(kvalid, s[h], NEG)

        def values(h):
            e = jnp.exp2(s[h] - jnp.max(s[h], axis=0, keepdims=True)).astype(BF16)
            r = (_dot(v2_ref[h, 0], e[0:A_TQ]) + _dot(v1_ref[h, 0], e[A_TQ:2 * A_TQ])
                 + _dot(v0_ref[h, 0], e[2 * A_TQ:3 * A_TQ]))
            o_t[h] = r[0:HD_A] / r[HD_A:HD_A + 1]
            if h % 2 == 1:
                p = h // 2
                o_ref[:, p * LANES:(p + 1) * LANES] = jnp.concatenate([o_t[h - 1], o_t[h]],
                                                                      axis=0).T.astype(BF16)

        for h in range(H_A + A_LAG):
            if h < H_A:
                scores(h)
            if h >= A_LAG:
                values(h - A_LAG)

    first_steps = BAND_PREV * CHUNK // A_TQ
    pl.when(i < first_steps)(lambda: body(True))
    pl.when(i >= first_steps)(lambda: body(False))


def _attn_a_prompt(qat, ka, vat, bias_row_rev):
    rows = ka.shape[0]
    prev = lambda d: (lambda i: (jnp.maximum(i - d, 0), 0))
    kspec = lambda d: pl.BlockSpec((A_TQ, A_WIDTH), prev(d))
    vspec = lambda d: pl.BlockSpec((H_A, 1, VA_ROWS, A_TQ), lambda i: (0, jnp.maximum(i - d, 0), 0, 0))
    return pl.pallas_call(
        _attn_a_kernel,
        grid=(rows // A_TQ,),
        in_specs=[pl.BlockSpec((H_A // 2, LANES, A_TQ), lambda i: (0, 0, i)),
                  kspec(2), kspec(1), kspec(0), vspec(2), vspec(1), vspec(0), _const_spec(bias_row_rev.shape)],
        out_specs=pl.BlockSpec((A_TQ, A_WIDTH), lambda i: (i, 0)),
        out_shape=jax.ShapeDtypeStruct((rows, A_WIDTH), BF16),
        scratch_shapes=[pltpu.VMEM((H_A, 3 * A_TQ, A_TQ), F32)],
        compiler_params=_arbitrary(1),
        name="attn_a_prompt",
    )(qat, ka, ka, ka, vat, vat, vat, bias_row_rev)


def _attn_a_sample_kernel(q_ref, kn_ref, vn_ref, kc_ref, vc_ref, row_ref, o_ref, bias_ref):
    t = q_ref.shape[0]
    past = kc_ref.shape[1]

    @pl.when(pl.program_id(0) == 0)
    def _():
        _build_band_bias(row_ref, bias_ref)

    kc = kc_ref[0].astype(BF16)
    vc = vc_ref[0].astype(BF16)
    for p in range(H_A // 2):
        sl = slice(p * LANES, (p + 1) * LANES)
        qz, lo = _pair_queries(q_ref[:, sl])
        b_past = jnp.concatenate([bias_ref[2 * p, 0:t, 0:past], bias_ref[2 * p + 1, 0:t, 0:past]], axis=0)
        b_new = jnp.concatenate([bias_ref[2 * p, 0:t, past:past + t],
                                 bias_ref[2 * p + 1, 0:t, past:past + t]], axis=0)
        s_past = _dot_nt(qz, kc[:, sl]) + b_past
        s_new = _dot_nt(qz, kn_ref[:, sl]) + b_new
        m = jnp.maximum(jnp.max(s_past, axis=-1, keepdims=True), jnp.max(s_new, axis=-1, keepdims=True))
        e_past = jnp.exp2(s_past - m)
        e_new = jnp.exp2(s_new - m)
        l = jnp.sum(e_past, axis=-1, keepdims=True) + jnp.sum(e_new, axis=-1, keepdims=True)
        o = (_dot(e_past.astype(BF16), vc[:, sl]) + _dot(e_new.astype(BF16), vn_ref[:, sl])) / l
        o_ref[:, sl] = jnp.where(lo, o[:t], o[t:]).astype(BF16)


def _attn_a_sample(qa, ka, va, cache_k, cache_v, bias_row, t):
    rows = qa.shape[0]
    streams, past, width = cache_k.shape
    assert past == 2 * A_TQ and t <= CHUNK and t % SUBLANES == 0
    row = pl.BlockSpec((t, A_WIDTH), lambda s: (s, 0))
    cache = pl.BlockSpec((1, past, width), lambda s: (s, 0, 0))
    return pl.pallas_call(
        _attn_a_sample_kernel,
        grid=(streams,),
        in_specs=[row, row, row, cache, cache, _const_spec(bias_row.shape)],
        out_specs=row,
        out_shape=jax.ShapeDtypeStruct((rows, A_WIDTH), BF16),
        scratch_shapes=[pltpu.VMEM((H_A, t, past + LANES), F32)],
        compiler_params=_arbitrary(1),
        name="attn_a_sample",
    )(qa, ka, va, cache_k, cache_v, bias_row)


def _lambda(lamp_ref):
    lp = lamp_ref[...]
    a = jnp.sum(lp[0:1] * lp[1:2], axis=-1, keepdims=True)
    b = jnp.sum(lp[2:3] * lp[3:4], axis=-1, keepdims=True)
    return jnp.exp(a) - jnp.exp(b) + LAM_INIT


def _diff_finish(o0, o1, lam, g):
    o = o0 - lam * o1
    o = o * lax.rsqrt(jnp.mean(o * o, axis=-1, keepdims=True) + LN_EPS) * g * (1.0 - LAM_INIT)
    return o.astype(BF16)


def _attn_b_kernel(*refs, cast_steps):
    n_cast = len(cast_steps)
    qt_ref, k_ref, vt_ref, lamp_ref, gcol_ref = refs[:5]
    w_refs = refs[5:5 + n_cast]
    o_ref = refs[5 + n_cast]
    wbf_refs = refs[6 + n_cast:6 + 2 * n_cast]
    acc_ref, p_ref, alpha_ref = refs[6 + 2 * n_cast:]

    step_id = pl.program_id(0) * pl.num_programs(1) + pl.program_id(1)
    for w_ref, wbf_ref, n_steps in zip(w_refs, wbf_refs, cast_steps):
        def cast(w_ref=w_ref, wbf_ref=wbf_ref):
            wbf_ref[...] = w_ref[...].astype(BF16)
        pl.when(step_id < n_steps)(cast)

    qi = pl.program_id(1)
    n_parts = B_TQ // B_QW
    dim = lax.broadcasted_iota(jnp.int32, (LANES, B_QW), 0)
    zero = jnp.zeros((LANES, B_QW), BF16)
    chains = [(part, (h, mp)) for part in range(n_parts) for h in range(B_HEADS) for mp in range(2)]
    qz = []
    for part, (h, mp) in chains:
        qt = qt_ref[h, :, part * B_QW:(part + 1) * B_QW]
        qz.append(jnp.where(dim < HD_B if mp == 0 else dim >= HD_B, qt, zero))
    acc_ref[...] = jnp.zeros(acc_ref.shape, F32)
    p_ref[1] = jnp.zeros(p_ref.shape[1:], BF16)
    alpha_ref[1] = jnp.ones(alpha_ref.shape[1:], F32)

    def pending_values(c, j_prev, buf):
        h = chains[c][1][0]
        acc_ref[c] = alpha_ref[buf, c] * acc_ref[c] + _dot(vt_ref[h, j_prev], p_ref[buf, c])

    def step(j, ms, buf, parts, diag_parts, pending_parts):
        start = pl.multiple_of(j * B_TK, B_TK)
        j_prev = jnp.maximum(j - 1, 0)
        kpos = start + lax.broadcasted_iota(jnp.int32, (B_TK, B_QW), 0)
        lane = lax.broadcasted_iota(jnp.int32, (B_TK, B_QW), 1)
        out = list(ms)
        for c, (part, (h, _)) in enumerate(chains):
            if part in parts:
                s = _dot(k_ref[pl.ds(start, B_TK), h * LANES:(h + 1) * LANES], qz[c])
                if part in diag_parts:
                    qpos = qi * B_TQ + part * B_QW + lane
                    s = jnp.where(kpos < (qpos // CHUNK + 1) * CHUNK, s, NEG)
                out[c] = jnp.maximum(ms[c], jnp.max(s, axis=0, keepdims=True))
                alpha_ref[buf, c] = jnp.exp2(ms[c] - out[c])
                p_ref[buf, c] = jnp.exp2(s - out[c]).astype(BF16)
            if part in pending_parts:
                pending_values(c, j_prev, 1 - buf)
        return tuple(out)

    every = tuple(range(n_parts))
    low, high = every[:n_parts // 2], every[n_parts // 2:]

    def pair(i, ms):
        ms = step(2 * i, ms, 0, every, (), every)
        return step(2 * i + 1, ms, 1, every, (), every)

    m_init = tuple(jnp.full((1, B_QW), NEG, F32) for _ in chains)
    ms = lax.fori_loop(0, qi, pair, m_init)
    ms = step(2 * qi, ms, 0, every, low, every)
    step(2 * qi + 1, ms, 1, high, high, every)
    for c, (part, _) in enumerate(chains):
        if part in high:
            pending_values(c, 2 * qi + 1, 1)

    lam = _lambda(lamp_ref)
    for c in range(0, len(chains), 2):
        part, (h, _) = chains[c]
        a0 = acc_ref[c]
        a1 = acc_ref[c + 1]
        o = a0[0:LANES] / a0[LANES:LANES + 1] - lam * (a1[0:LANES] / a1[LANES:LANES + 1])
        o = o * lax.rsqrt(jnp.mean(o * o, axis=0, keepdims=True) + LN_EPS) * gcol_ref[...] * (1.0 - LAM_INIT)
        o_ref[part * B_QW:(part + 1) * B_QW, h * LANES:(h + 1) * LANES] = o.T.astype(BF16)


def _attn_b_prompt(qbt, kb, vbt, lamp, gcol, weights):
    rows = kb.shape[0]
    assert rows % B_TQ == 0 and B_TQ == 2 * B_TK and B_TK == 2 * B_QW and B_QW % CHUNK == 0
    assert H_B % B_HEADS == 0
    n_chains = (B_TQ // B_QW) * B_HEADS * 2
    n_q = rows // B_TQ
    total_steps = (H_B // B_HEADS) * n_q
    single = pl.Buffered(1)

    def cast_plan(w):
        n_steps = total_steps
        while w.shape[0] % (n_steps * BF16_SUBLANES):
            n_steps //= 2
        block = (w.shape[0] // n_steps, w.shape[1])
        spec = pl.BlockSpec(block, lambda h, i, n=n_steps: (jnp.minimum(h * n_q + i, n - 1), 0))
        return n_steps, spec

    plans = [cast_plan(w) for w in weights]
    outs = pl.pallas_call(
        functools.partial(_attn_b_kernel, cast_steps=tuple(n for n, _ in plans)),
        grid=(H_B // B_HEADS, n_q),
        in_specs=[pl.BlockSpec((B_HEADS, LANES, B_TQ), lambda h, i: (h, 0, i)),
                  pl.BlockSpec((rows, B_HEADS * LANES), lambda h, i: (0, h), pipeline_mode=single),
                  pl.BlockSpec((B_HEADS,) + vbt.shape[1:], lambda h, i: (h, 0, 0, 0), pipeline_mode=single),
                  _const_spec(lamp.shape), _const_spec(gcol.shape)] + [spec for _, spec in plans],
        out_specs=[pl.BlockSpec((B_TQ, B_HEADS * LANES), lambda h, i: (i, h))] + [spec for _, spec in plans],
        out_shape=[jax.ShapeDtypeStruct((rows, B_WIDTH), BF16)]
                  + [jax.ShapeDtypeStruct(w.shape, BF16) for w in weights],
        scratch_shapes=[pltpu.VMEM((n_chains, V_ROWS, B_QW), F32),
                        pltpu.VMEM((2, n_chains, B_TK, B_QW), BF16), pltpu.VMEM((2, n_chains, 1, B_QW), F32)],
        compiler_params=_arbitrary(2),
        name="attn_b_prompt",
    )(qbt, kb, vbt, lamp, gcol, *weights)
    return outs[0], outs[1:]


def _attn_b_sample_kernel(q_ref, kn_ref, vn_ref, kc_ref, vc_ref, lamp_ref, g_ref, o_ref):
    t = q_ref.shape[0]
    lam = _lambda(lamp_ref)
    g = g_ref[...]
    for h in range(H_B):
        sl = slice(h * LANES, (h + 1) * LANES)
        qz, _ = _pair_queries(q_ref[:, sl])
        s_past = _dot_nt(qz, kc_ref[0, :, sl].astype(BF16))
        s_new = _dot_nt(qz, kn_ref[:, sl])
        m = jnp.maximum(jnp.max(s_past, axis=-1, keepdims=True), jnp.max(s_new, axis=-1, keepdims=True))
        e_past = jnp.exp(s_past - m)
        e_new = jnp.exp(s_new - m)
        l = jnp.sum(e_past, axis=-1, keepdims=True) + jnp.sum(e_new, axis=-1, keepdims=True)
        o = (_dot(e_past.astype(BF16), vc_ref[0, :, h, :].astype(BF16))
             + _dot(e_new.astype(BF16), vn_ref[:, sl])) / l
        o_ref[:, sl] = _diff_finish(o[:t], o[t:], lam, g)


def _attn_b_sample(qb, kb, vb, cache_k, cache_v, lamp, g, t):
    rows = qb.shape[0]
    streams, past, width = cache_k.shape
    assert past % CHUNK == 0 and t <= CHUNK
    row = pl.BlockSpec((t, B_WIDTH), lambda s: (s, 0))
    cache = pl.BlockSpec((1, past, width), lambda s: (s, 0, 0))
    cache_v_spec = pl.BlockSpec((1,) + cache_v.shape[1:], lambda s: (s, 0, 0, 0))
    return pl.pallas_call(
        _attn_b_sample_kernel,
        grid=(streams,),
        in_specs=[row, row, row, cache, cache_v_spec, _const_spec(lamp.shape), _const_spec(g.shape)],
        out_specs=row,
        out_shape=jax.ShapeDtypeStruct((rows, B_WIDTH), BF16),
        compiler_params=_arbitrary(1),
        name="attn_b_sample",
    )(qb, kb, vb, cache_k, cache_v, lamp, g)


def _mem_kv_kernel(x_ref, wk_ref, wv_ref, kf_ref, vf_ref):
    xb = x_ref[...].astype(BF16)
    kf_ref[...] = _dot(xb, wk_ref[...])
    vf_ref[...] = _dot(xb, wv_ref[...])


def _mem_kv(mem, wk_bf, wv_bf):
    f = jax.ShapeDtypeStruct(mem.shape, F32)
    return pl.pallas_call(
        _mem_kv_kernel,
        out_shape=[f, f],
        name="mem_kv",
    )(mem, wk_bf, wv_bf)


def _layer_norm(x, g, b):
    mu = jnp.mean(x, axis=-1, keepdims=True)
    xc = x - mu
    var = jnp.mean(xc * xc, axis=-1, keepdims=True)
    return xc * lax.rsqrt(var + LN_EPS) * g + b


def _tail_kernel(x_ref, oa_ref, ob_ref, mk_ref, mv_ref, cin_ref,
                 wo_ref, ln1g_ref, ln1b_ref, wmq_ref, wmo_ref, ln2g_ref, ln2b_ref,
                 wup_ref, cw_ref, cb_ref, wdn_ref, ln3g_ref, ln3b_ref,
                 y_ref, carry_ref, *, seg, sub):
    tm = x_ref.shape[0]
    n_seg = tm // seg
    n_sub = tm // sub
    assert n_seg == 1 or n_sub == 1
    subs = [slice(i * sub, (i + 1) * sub) for i in range(n_sub)]

    @pl.when(pl.program_id(0) == 0)
    def _():
        carry_ref[...] = cin_ref[...]

    def mem_attend(q, mk, mv):
        heads = []
        for h in range(H_M):
            sl = slice(h * HD_M, (h + 1) * HD_M)
            s = _dot_nt(q[:, sl], mk[:, sl])
            e = jnp.exp(s - jnp.max(s, axis=-1, keepdims=True))
            l = jnp.sum(e, axis=-1, keepdims=True)
            heads.append((_dot(e.astype(BF16), mv[:, sl]) / l).astype(BF16))
        return jnp.concatenate(heads, axis=1)

    def mem_attend_rows(q):
        if mk_ref.shape[0] == 1:
            return mem_attend(q, mk_ref[0].astype(BF16), mv_ref[0].astype(BF16))
        return jnp.concatenate([mem_attend(q[sg * seg:(sg + 1) * seg],
                                           mk_ref[sg].astype(BF16), mv_ref[sg].astype(BF16))
                                for sg in range(n_seg)], axis=0)

    mix = [_dot(oa_ref[sl, :], wo_ref[0:A_WIDTH, :]) + _dot(ob_ref[sl, :], wo_ref[A_WIDTH:A_WIDTH + B_WIDTH, :])
           for sl in subs]
    x1 = [_layer_norm(DEEPNORM_ALPHA * x_ref[sl, :] + m, ln1g_ref[...], ln1b_ref[...]) for sl, m in zip(subs, mix)]
    qm = [(_dot(v.astype(BF16), wmq_ref[...]) * (HD_M ** -0.5)).astype(BF16) for v in x1]
    om = [mem_attend_rows(q) for q in qm]
    mo = [_dot(v, wmo_ref[...]) for v in om]
    x2 = [_layer_norm(DEEPNORM_ALPHA * a + b, ln2g_ref[...], ln2b_ref[...]) for a, b in zip(x1, mo)]
    x2b = [v.astype(BF16) for v in x2]

    def conv(us, col, width):
        cols = slice(col, col + width)
        w0, w1, w2, b = cw_ref[0:1, cols], cw_ref[1:2, cols], cw_ref[2:3, cols], cb_ref[0:1, cols]

        def taps(u, p2, p1):
            row = lax.broadcasted_iota(jnp.int32, u.shape, 0)
            u1 = jnp.where(row == 0, p1, pltpu.roll(u, 1, axis=0))
            u2 = jnp.where(row == 0, p2, jnp.where(row == 1, p1, pltpu.roll(u, 2, axis=0)))
            return u2 * w0 + u1 * w1 + u * w2 + b

        def state(sg):
            base = SUBLANES * sg
            return carry_ref[base + 6:base + 7, cols], carry_ref[base + 7:base + 8, cols]

        if n_seg == 1:
            p2, p1 = state(0)
            outs = []
            for u in us:
                outs.append(taps(u, p2, p1))
                p2, p1 = u[sub - 2:sub - 1], u[sub - 1:sub]
            carry_ref[0:SUBLANES, cols] = us[-1][sub - SUBLANES:sub]
            return outs
        (u,) = us
        outs = []
        for sg in range(n_seg):
            useg = u[sg * seg:(sg + 1) * seg]
            outs.append(taps(useg, *state(sg)))
            carry_ref[SUBLANES * sg:SUBLANES * (sg + 1), cols] = useg[seg - SUBLANES:seg]
        return [jnp.concatenate(outs, axis=0)]

    f = [jnp.zeros((sub, D_MODEL), F32) for _ in subs]
    c0 = 0
    for width in FF_BLOCKS:
        gate = conv([_dot(v, wup_ref[:, c0:c0 + width]) for v in x2b], c0, width)
        val = conv([_dot(v, wup_ref[:, D_FF + c0:D_FF + c0 + width]) for v in x2b], D_FF + c0, width)
        hid = [(g * (1.0 / (1.0 + jnp.exp(-g))) * v).astype(BF16) for g, v in zip(gate, val)]
        f = [a + _dot(h, wdn_ref[c0:c0 + width, :]) for a, h in zip(f, hid)]
        c0 += width
    for sl, a, b in zip(subs, x2, f):
        y_ref[sl, :] = _layer_norm(DEEPNORM_ALPHA * a + b, ln3g_ref[...], ln3b_ref[...])


def _tail(x, oa, ob, mk, mv, conv_in, params, tm, seg, sub):
    rows = x.shape[0]
    n_seg = tm // seg
    assert tm % sub == 0 and sub % SUBLANES == 0
    assert conv_in.shape == (SUBLANES * n_seg, 2 * D_FF)
    assert n_seg == 1 or rows == tm
    assert mk.shape[0] in (1, n_seg)
    row_spec = lambda w: pl.BlockSpec((tm, w), lambda i: (i, 0))
    carry_spec = pl.BlockSpec(conv_in.shape, lambda i: (0, 0))
    return pl.pallas_call(
        functools.partial(_tail_kernel, seg=seg, sub=sub),
        grid=(rows // tm,),
        in_specs=[row_spec(D_MODEL), row_spec(A_WIDTH), row_spec(B_WIDTH),
                  _const_spec(mk.shape), _const_spec(mv.shape),
                  _const_spec(conv_in.shape)] + [_const_spec(p.shape) for p in params],
        out_specs=[row_spec(D_MODEL), carry_spec],
        out_shape=[jax.ShapeDtypeStruct((rows, D_MODEL), F32),
                   jax.ShapeDtypeStruct(conv_in.shape, F32)],
        compiler_params=_arbitrary(1),
        name="tail",
    )(x, oa, ob, mk, mv, conv_in, *params)


def kernel(x_prompt, x_sample, mem_prompt, cache_a_k, cache_a_v, cache_b_k, cache_b_v, cache_mem_k, cache_mem_v,
           state_conv, w_qkv, rel_bias, lambda_q1, lambda_k1, lambda_q2, lambda_k2, subln_g, w_o, ln1_g, ln1_b,
           w_mq, w_mk, w_mv, w_mo, ln2_g, ln2_b, w_up, conv_w, conv_b, w_down, ln3_g, ln3_b):
    assert w_qkv.shape[0] == DEPTH == 1
    batch, seq, _ = x_prompt.shape
    streams, t_new, _ = x_sample.shape
    past_b = cache_b_k.shape[2]
    assert batch == 1 and seq % QKV_TM == 0 and QKV_TM == BAND_PREV * CHUNK

    row = lambda v: v[0].astype(F32)[None, :]
    w_qkv_bf = w_qkv[0].astype(BF16)
    lamp = jnp.stack([lambda_q1[0], lambda_k1[0], lambda_q2[0], lambda_k2[0]]).astype(F32)
    g = row(subln_g)
    gcol = jnp.broadcast_to(subln_g[0].astype(F32)[:, None], (2 * HD_B, B_QW))
    bias = _bias_row(rel_bias[0])

    xp = x_prompt.reshape(seq, D_MODEL)
    qat, ka, vat, qbt, kb, vbt, ka_f, va_f, kb_f, vb_f = _qkv(xp, w_qkv_bf, jnp.arange(seq), QKV_TM, True)
    bias_rev = jnp.roll(bias[:, ::-1], 1, axis=1)
    oa = _attn_a_prompt(qat, ka, vat, bias_rev)
    later = [w[0].astype(F32) for w in (w_o, w_mq, w_mo, w_up, w_down, w_mk, w_mv)]
    ob, (w_o_bf, w_mq_bf, w_mo_bf, w_up_bf, w_down_bf, w_mk_bf, w_mv_bf) = _attn_b_prompt(
        qbt, kb, vbt, lamp, gcol, later)
    tail_params = (w_o_bf, row(ln1_g), row(ln1_b), w_mq_bf, w_mo_bf, row(ln2_g), row(ln2_b),
                   w_up_bf, conv_w[0].astype(F32), row(conv_b), w_down_bf, row(ln3_g), row(ln3_b))
    mk_p, mv_p = _mem_kv(mem_prompt.reshape(N_MEM, D_MODEL), w_mk_bf, w_mv_bf)
    conv0 = jnp.zeros((SUBLANES, 2 * D_FF), F32)
    y_p, conv_p = _tail(xp, oa, ob, mk_p[None], mv_p[None], conv0, tail_params, TAIL_TM, TAIL_TM, TAIL_SUB)

    rows_s = streams * t_new
    xs = x_sample.reshape(rows_s, D_MODEL)
    pos_s = jnp.tile(past_b + jnp.arange(t_new), streams)
    qa_s, ka_s, va_s, qb_s, kb_s, vb_s, ka_sf, va_sf, kb_sf, vb_sf = _qkv(xs, w_qkv_bf, pos_s, rows_s, False)
    oa_s = _attn_a_sample(qa_s, ka_s, va_s,
                          cache_a_k[0].reshape(streams, -1, A_WIDTH), cache_a_v[0].reshape(streams, -1, A_WIDTH),
                          bias, t_new)
    ob_s = _attn_b_sample(qb_s, kb_s, vb_s,
                          cache_b_k[0].reshape(streams, past_b, B_WIDTH),
                          cache_b_v[0], lamp, g, t_new)
    conv_in_s = jnp.pad(state_conv[0].astype(F32), ((0, 0), (SUBLANES - (CONV_W - 1), 0), (0, 0)))
    y_s, conv_s = _tail(xs, oa_s, ob_s,
                        cache_mem_k[0].reshape(streams, N_MEM, D_MODEL).astype(BF16),
                        cache_mem_v[0].reshape(streams, N_MEM, D_MODEL).astype(BF16),
                        conv_in_s.reshape(streams * SUBLANES, 2 * D_FF), tail_params, rows_s, t_new, rows_s)

    keep = CONV_W - 1
    return (
        y_p.reshape(batch, seq, D_MODEL),
        y_s.reshape(streams, t_new, D_MODEL),
        ka_f.reshape(1, batch, QKV_TM, H_A, HD_A),
        va_f.reshape(1, batch, QKV_TM, H_A, HD_A),
        kb_f.reshape(1, batch, seq, H_B, 2, HD_B),
        vb_f.reshape(1, batch, seq, H_B, 2 * HD_B),
        mk_p.reshape(1, batch, N_MEM, H_M, HD_M),
        mv_p.reshape(1, batch, N_MEM, H_M, HD_M),
        conv_p[SUBLANES - keep:].reshape(1, batch, keep, 2 * D_FF),
        ka_sf.reshape(1, streams, t_new, H_A, HD_A),
        va_sf.reshape(1, streams, t_new, H_A, HD_A),
        kb_sf.reshape(1, streams, t_new, H_B, 2, HD_B),
        vb_sf.reshape(1, streams, t_new, H_B, 2 * HD_B),
        conv_s.reshape(streams, SUBLANES, 2 * D_FF)[:, SUBLANES - keep:].reshape(1, streams, keep, 2 * D_FF),
    )
```

```python
import functools
import math

import jax
import jax.numpy as jnp
from jax import lax
from jax.experimental import pallas as pl
from jax.experimental.pallas import tpu as pltpu

F32 = jnp.float32
BF16 = jnp.bfloat16

D_MODEL = 1024
CHUNK = 64
BAND_PREV = 8
REL_CLIP = 128
H_A = 8
HD_A = 64
H_B = 4
HD_B = 64
ROT_DIM = HD_B // 4
ROPE_THETA = 500000.0
N_MEM = 256
H_M = 4
HD_M = D_MODEL // H_M
D_FF = 2816
CONV_W = 3
LN_EPS = 1e-5
DEPTH = 1
DEEPNORM_ALPHA = (2.0 * DEPTH) ** 0.25
A_WIDTH = H_A * HD_A
B_WIDTH = H_B * 2 * HD_B
LAM_INIT = 0.8 - 0.6 * math.exp(-0.3 * 0)

LANES = 128
SUBLANES = 8
BF16_SUBLANES = 16
NEG = -1e30
LOG2_E = math.log2(math.e)

QKV_TM = 512
A_TQ = 256
A_LAG = 2
B_TQ = 1024
B_QW = 256
B_HEADS = 2
B_TK = 512
ONES_ROWS = 16
V_ROWS = LANES + ONES_ROWS
VA_ROWS = HD_A + ONES_ROWS
BIAS_SPAN = 1024
TAIL_TM = 512
TAIL_SUB = 256
MXU_DIM = 256
FF_BLOCKS = (4 * MXU_DIM, 4 * MXU_DIM, D_FF - 8 * MXU_DIM)
assert sum(FF_BLOCKS) == D_FF and all(w % MXU_DIM == 0 for w in FF_BLOCKS)

_NT = (((1,), (1,)), ((), ()))


def _dot(a, b):
    return jnp.dot(a, b, preferred_element_type=F32)


def _dot_nt(a, b):
    return lax.dot_general(a, b, _NT, preferred_element_type=F32)


def _const_spec(shape):
    nd = len(shape)
    return pl.BlockSpec(shape, lambda *_: (0,) * nd, pipeline_mode=pl.Buffered(1))


def _arbitrary(n):
    return pltpu.CompilerParams(dimension_semantics=("arbitrary",) * n)


def _rope(z, cos, sin):
    lane = lax.broadcasted_iota(jnp.int32, (z.shape[0], LANES), 1)
    first_half = (lane % HD_B) < (ROT_DIM // 2)
    outs = []
    for c in range(z.shape[1] // LANES):
        zc = z[:, c * LANES:(c + 1) * LANES]
        partner = jnp.where(first_half,
                            pltpu.roll(zc, LANES - ROT_DIM // 2, axis=1),
                            pltpu.roll(zc, ROT_DIM // 2, axis=1))
        outs.append(zc * cos + partner * sin)
    return jnp.concatenate(outs, axis=1)


def _qkv_kernel(x_ref, w_ref, rope_ref,
                qa_ref, ka_ref, va_ref, qb_ref, kb_ref, vb_ref,
                kaf_ref, vaf_ref, kbf_ref, vbf_ref, *, transposed):
    xb = x_ref[...].astype(BF16)
    cos, sin = _spread_rope(rope_ref[...])
    tm = xb.shape[0]

    def section(i):
        return _dot(xb, w_ref[:, i * A_WIDTH:(i + 1) * A_WIDTH])

    vb = section(5)
    for h in range(H_B):
        vbf_ref[:, h, :] = vb[:, h * LANES:(h + 1) * LANES]
    def ones_rows(width):
        return (lax.broadcasted_iota(jnp.int32, (ONES_ROWS, width), 0) == 0).astype(BF16)

    if transposed:
        for h in range(H_B):
            vb_ref[h, 0, 0:LANES, :] = vb[:, h * LANES:(h + 1) * LANES].T.astype(BF16)
            vb_ref[h, 0, LANES:V_ROWS, :] = ones_rows(tm)
    else:
        vb_ref[...] = vb.astype(BF16)
    qb = _rope(section(3), cos, sin) * (HD_B ** -0.5 * (LOG2_E if transposed else 1.0))
    if transposed:
        for h in range(H_B):
            qb_ref[h] = qb[:, h * LANES:(h + 1) * LANES].T.astype(BF16)
    else:
        qb_ref[...] = qb.astype(BF16)
    kb = _rope(section(4), cos, sin)
    kb_ref[...] = kb.astype(BF16)
    kbf_ref[...] = kb
    va = section(2)
    vaf_ref[...] = va
    if transposed:
        for p in range(H_A // 2):
            vt = va[:, p * LANES:(p + 1) * LANES].T.astype(BF16)
            for hh in range(2):
                for t in range(tm // A_TQ):
                    cols = slice(t * A_TQ, (t + 1) * A_TQ)
                    va_ref[2 * p + hh, t, 0:HD_A, :] = vt[hh * HD_A:(hh + 1) * HD_A, cols]
                    va_ref[2 * p + hh, t, HD_A:VA_ROWS, :] = ones_rows(A_TQ)
    else:
        va_ref[...] = va.astype(BF16)
    qa = section(0) * (HD_A ** -0.5 * LOG2_E)
    if transposed:
        for p in range(H_A // 2):
            qa_ref[p] = qa[:, p * LANES:(p + 1) * LANES].T.astype(BF16)
    else:
        qa_ref[...] = qa.astype(BF16)
    ka = section(1)
    ka_ref[...] = ka.astype(BF16)
    kaf_ref[...] = ka


def _rope_table(pos):
    inv = ROPE_THETA ** (-jnp.arange(0, ROT_DIM, 2, dtype=F32) / ROT_DIM)
    ang = pos.astype(F32)[:, None] * inv[None, :]
    cs = lax.optimization_barrier(jnp.concatenate([jnp.cos(ang), jnp.sin(ang)], axis=1))
    return jnp.pad(cs, ((0, 0), (0, LANES - ROT_DIM)))


def _spread_rope(tab):
    half = ROT_DIM // 2
    lane = lax.broadcasted_iota(jnp.int32, tab.shape, 1)
    c = jnp.where(lane < half, tab, 0.0)
    s = jnp.where((lane >= half) & (lane < ROT_DIM), tab, 0.0)
    c = c + pltpu.roll(c, half, axis=1)
    s = s - pltpu.roll(s, LANES - half, axis=1)
    c = c + pltpu.roll(c, HD_B, axis=1)
    s = s + pltpu.roll(s, HD_B, axis=1)
    return jnp.where(lane % HD_B < ROT_DIM, c, 1.0), s


def _qkv(x, w_bf, pos, tm, transposed):
    rows = x.shape[0]
    nt = rows // tm
    rope = _rope_table(pos)
    row_spec = lambda w: pl.BlockSpec((tm, w), lambda i: (i, 0))
    last_spec = pl.BlockSpec((tm, A_WIDTH), lambda i: (0, 0))
    bf = jax.ShapeDtypeStruct((rows, A_WIDTH), BF16)
    if transposed:
        assert tm == B_TK and tm % A_TQ == 0
        qa_spec = pl.BlockSpec((H_A // 2, LANES, tm), lambda i: (0, 0, i))
        qa_shape = jax.ShapeDtypeStruct((H_A // 2, LANES, rows), BF16)
        va_spec = pl.BlockSpec((H_A, tm // A_TQ, VA_ROWS, A_TQ), lambda i: (0, i, 0, 0))
        va_shape = jax.ShapeDtypeStruct((H_A, rows // A_TQ, VA_ROWS, A_TQ), BF16)
        qb_spec = pl.BlockSpec((H_B, LANES, tm), lambda i: (0, 0, i))
        qb_shape = jax.ShapeDtypeStruct((H_B, LANES, rows), BF16)
        vb_spec = pl.BlockSpec((H_B, 1, V_ROWS, tm), lambda i: (0, i, 0, 0))
        vb_shape = jax.ShapeDtypeStruct((H_B, nt, V_ROWS, tm), BF16)
    else:
        qa_spec = va_spec = qb_spec = vb_spec = row_spec(B_WIDTH)
        qa_shape = va_shape = qb_shape = vb_shape = bf
    return pl.pallas_call(
        functools.partial(_qkv_kernel, transposed=transposed),
        grid=(nt,),
        in_specs=[row_spec(D_MODEL), _const_spec(w_bf.shape), row_spec(LANES)],
        out_specs=[qa_spec, row_spec(A_WIDTH), va_spec, qb_spec, row_spec(B_WIDTH), vb_spec]
                  + [last_spec, last_spec, row_spec(B_WIDTH),
                     pl.BlockSpec((tm, H_B, 2 * HD_B), lambda i: (i, 0, 0))],
        out_shape=[qa_shape, bf, va_shape, qb_shape, bf, vb_shape]
                  + [jax.ShapeDtypeStruct((tm, A_WIDTH), F32)] * 2
                  + [jax.ShapeDtypeStruct((rows, B_WIDTH), F32),
                     jax.ShapeDtypeStruct((rows, H_B, 2 * HD_B), F32)],
        compiler_params=_arbitrary(1),
        name="qkv_proj",
    )(x, w_bf, rope)


def _bias_row(rel_bias):
    table = rel_bias.astype(F32)
    far = table[:, 2 * REL_CLIP:]
    near = table[:, :1]
    n_far = 2 * A_TQ - REL_CLIP
    n_near = 3 * A_TQ - (n_far + 2 * REL_CLIP + 1)
    h = table.shape[0]
    return jnp.concatenate([jnp.broadcast_to(far, (h, n_far)), table[:, ::-1],
                            jnp.broadcast_to(near, (h, n_near)),
                            jnp.broadcast_to(far, (h, BIAS_SPAN - 3 * A_TQ))], axis=1)


def _build_band_bias(row_ref, bias_ref):
    _, rows, cols = bias_ref.shape
    q = lax.broadcasted_iota(jnp.int32, (rows, cols), 0)
    k = lax.broadcasted_iota(jnp.int32, (rows, cols), 1)
    qc = q // CHUNK
    kc = k // CHUNK - (2 * A_TQ) // CHUNK
    band = (kc >= qc - BAND_PREV) & (kc <= qc)
    for h in range(H_A):
        r = jnp.broadcast_to(row_ref[h:h + 1, :], (rows, BIAS_SPAN))
        toeplitz = pltpu.roll(r, 0, axis=1, stride=1, stride_axis=0)
        bias_ref[h] = jnp.where(band, toeplitz[:, :cols] * LOG2_E, NEG)


def _pair_queries(q2):
    lane = lax.broadcasted_iota(jnp.int32, q2.shape, 1)
    lo = lane < (LANES // 2)
    zero = jnp.zeros_like(q2)
    return jnp.concatenate([jnp.where(lo, q2, zero), jnp.where(lo, zero, q2)], axis=0), lo


def _build_band_bias_t(row_ref, bias_ref):
    _, n_k, n_q = bias_ref.shape
    k = lax.broadcasted_iota(jnp.int32, (n_k, n_q), 0)
    q = lax.broadcasted_iota(jnp.int32, (n_k, n_q), 1)
    qc = q // CHUNK
    kc = k // CHUNK - (2 * A_TQ) // CHUNK
    band = (kc >= qc - BAND_PREV) & (kc <= qc)
    for h in range(H_A):
        r = jnp.broadcast_to(row_ref[h:h + 1, :], (n_k, BIAS_SPAN))
        toeplitz = pltpu.roll(r, 0, axis=1, stride=1, stride_axis=0)
        bias_ref[h] = jnp.where(band, toeplitz[:, :n_q] * LOG2_E, NEG)


def _attn_a_kernel(qt_ref, k2_ref, k1_ref, k0_ref, v2_ref, v1_ref, v0_ref, row_ref, o_ref, bias_ref):
    i = pl.program_id(0)

    @pl.when(i == 0)
    def _():
        _build_band_bias_t(row_ref, bias_ref)

    def body(check_keys):
        if check_keys:
            kidx = lax.broadcasted_iota(jnp.int32, (3 * A_TQ, A_TQ), 0)
            kvalid = kidx >= (2 - i) * A_TQ
        dim = lax.broadcasted_iota(jnp.int32, (LANES, A_TQ), 0)
        zero = jnp.zeros((LANES, A_TQ), BF16)
        s, o_t = [None] * H_A, [None] * H_A

        def scores(h):
            p = h // 2
            sl = slice(p * LANES, (p + 1) * LANES)
            qz = jnp.where(dim < HD_A if h % 2 == 0 else dim >= HD_A, qt_ref[p], zero)
            kc = jnp.concatenate([k2_ref[:, sl], k1_ref[:, sl], k0_ref[:, sl]], axis=0)
            s[h] = _dot(kc, qz) + bias_ref[h]
            if check_keys:
                s[h] = jnp.where(kvalid, s[h], NEG)

        def values(h):
            e = jnp.exp2(s[h] - jnp.max(s[h], axis=0, keepdims=True)).astype(BF16)
            r = (_dot(v2_ref[h, 0], e[0:A_TQ]) + _dot(v1_ref[h, 0], e[A_TQ:2 * A_TQ])
                 + _dot(v0_ref[h, 0], e[2 * A_TQ:3 * A_TQ]))
            o_t[h] = r[0:HD_A] / r[HD_A:HD_A + 1]
            if h % 2 == 1:
                p = h // 2
                o_ref[:, p * LANES:(p + 1) * LANES] = jnp.concatenate([o_t[h - 1], o_t[h]],
                                                                      axis=0).T.astype(BF16)

        for h in range(H_A + A_LAG):
            if h < H_A:
                scores(h)
            if h >= A_LAG:
                values(h - A_LAG)

    first_steps = BAND_PREV * CHUNK // A_TQ
    pl.when(i < first_steps)(lambda: body(True))
    pl.when(i >= first_steps)(lambda: body(False))


def _attn_a_prompt(qat, ka, vat, bias_row_rev):
    rows = ka.shape[0]
    prev = lambda d: (lambda i: (jnp.maximum(i - d, 0), 0))
    kspec = lambda d: pl.BlockSpec((A_TQ, A_WIDTH), prev(d))
    vspec = lambda d: pl.BlockSpec((H_A, 1, VA_ROWS, A_TQ), lambda i: (0, jnp.maximum(i - d, 0), 0, 0))
    return pl.pallas_call(
        _attn_a_kernel,
        grid=(rows // A_TQ,),
        in_specs=[pl.BlockSpec((H_A // 2, LANES, A_TQ), lambda i: (0, 0, i)),
                  kspec(2), kspec(1), kspec(0), vspec(2), vspec(1), vspec(0), _const_spec(bias_row_rev.shape)],
        out_specs=pl.BlockSpec((A_TQ, A_WIDTH), lambda i: (i, 0)),
        out_shape=jax.ShapeDtypeStruct((rows, A_WIDTH), BF16),
        scratch_shapes=[pltpu.VMEM((H_A, 3 * A_TQ, A_TQ), F32)],
        compiler_params=_arbitrary(1),
        name="attn_a_prompt",
    )(qat, ka, ka, ka, vat, vat, vat, bias_row_rev)


def _attn_a_sample_kernel(q_ref, kn_ref, vn_ref, kc_ref, vc_ref, row_ref, o_ref, bias_ref):
    t = q_ref.shape[0]
    past = kc_ref.shape[1]

    @pl.when(pl.program_id(0) == 0)
    def _():
        _build_band_bias(row_ref, bias_ref)

    kc = kc_ref[0].astype(BF16)
    vc = vc_ref[0].astype(BF16)
    for p in range(H_A // 2):
        sl = slice(p * LANES, (p + 1) * LANES)
        qz, lo = _pair_queries(q_ref[:, sl])
        b_past = jnp.concatenate([bias_ref[2 * p, 0:t, 0:past], bias_ref[2 * p + 1, 0:t, 0:past]], axis=0)
        b_new = jnp.concatenate([bias_ref[2 * p, 0:t, past:past + t],
                                 bias_ref[2 * p + 1, 0:t, past:past + t]], axis=0)
        s_past = _dot_nt(qz, kc[:, sl]) + b_past
        s_new = _dot_nt(qz, kn_ref[:, sl]) + b_new
        m = jnp.maximum(jnp.max(s_past, axis=-1, keepdims=True), jnp.max(s_new, axis=-1, keepdims=True))
        e_past = jnp.exp2(s_past - m)
        e_new = jnp.exp2(s_new - m)
        l = jnp.sum(e_past, axis=-1, keepdims=True) + jnp.sum(e_new, axis=-1, keepdims=True)
        o = (_dot(e_past.astype(BF16), vc[:, sl]) + _dot(e_new.astype(BF16), vn_ref[:, sl])) / l
        o_ref[:, sl] = jnp.where(lo, o[:t], o[t:]).astype(BF16)


def _attn_a_sample(qa, ka, va, cache_k, cache_v, bias_row, t):
    rows = qa.shape[0]
    streams, past, width = cache_k.shape
    assert past == 2 * A_TQ and t <= CHUNK and t % SUBLANES == 0
    row = pl.BlockSpec((t, A_WIDTH), lambda s: (s, 0))
    cache = pl.BlockSpec((1, past, width), lambda s: (s, 0, 0))
    return pl.pallas_call(
        _attn_a_sample_kernel,
        grid=(streams,),
        in_specs=[row, row, row, cache, cache, _const_spec(bias_row.shape)],
        out_specs=row,
        out_shape=jax.ShapeDtypeStruct((rows, A_WIDTH), BF16),
        scratch_shapes=[pltpu.VMEM((H_A, t, past + LANES), F32)],
        compiler_params=_arbitrary(1),
        name="attn_a_sample",
    )(qa, ka, va, cache_k, cache_v, bias_row)


def _lambda(lamp_ref):
    lp = lamp_ref[...]
    a = jnp.sum(lp[0:1] * lp[1:2], axis=-1, keepdims=True)
    b = jnp.sum(lp[2:3] * lp[3:4], axis=-1, keepdims=True)
    return jnp.exp(a) - jnp.exp(b) + LAM_INIT


def _diff_finish(o0, o1, lam, g):
    o = o0 - lam * o1
    o = o * lax.rsqrt(jnp.mean(o * o, axis=-1, keepdims=True) + LN_EPS) * g * (1.0 - LAM_INIT)
    return o.astype(BF16)


def _attn_b_kernel(*refs, cast_steps, total_steps):
    n_cast = len(cast_steps)
    qt_ref, k_ref, vt_ref, lamp_ref, gcol_ref = refs[:5]
    w_refs = refs[5:5 + n_cast]
    o_ref = refs[5 + n_cast]
    wbf_refs = refs[6 + n_cast:6 + 2 * n_cast]
    acc_ref, p_ref, alpha_ref = refs[6 + 2 * n_cast:]

    qi = pl.program_id(1)
    n_parts = B_TQ // B_QW
    dim = lax.broadcasted_iota(jnp.int32, (LANES, B_QW), 0)
    zero = jnp.zeros((LANES, B_QW), BF16)
    chains = [(part, (h, mp)) for part in range(n_parts) for h in range(B_HEADS) for mp in range(2)]
    qz = []
    for part, (h, mp) in chains:
        qt = qt_ref[h, :, part * B_QW:(part + 1) * B_QW]
        qz.append(jnp.where(dim < HD_B if mp == 0 else dim >= HD_B, qt, zero))
    acc_ref[...] = jnp.zeros(acc_ref.shape, F32)
    p_ref[1] = jnp.zeros(p_ref.shape[1:], BF16)
    alpha_ref[1] = jnp.ones(alpha_ref.shape[1:], F32)

    def pending_values(c, j_prev, buf):
        h = chains[c][1][0]
        acc_ref[c] = alpha_ref[buf, c] * acc_ref[c] + _dot(vt_ref[h, j_prev], p_ref[buf, c])

    def step(j, ms, buf, parts, diag_parts, pending_parts):
        start = pl.multiple_of(j * B_TK, B_TK)
        j_prev = jnp.maximum(j - 1, 0)
        kpos = start + lax.broadcasted_iota(jnp.int32, (B_TK, B_QW), 0)
        lane = lax.broadcasted_iota(jnp.int32, (B_TK, B_QW), 1)
        out = list(ms)
        for c, (part, (h, _)) in enumerate(chains):
            if part in parts:
                s = _dot(k_ref[pl.ds(start, B_TK), h * LANES:(h + 1) * LANES], qz[c])
                if part in diag_parts:
                    qpos = qi * B_TQ + part * B_QW + lane
                    s = jnp.where(kpos < (qpos // CHUNK + 1) * CHUNK, s, NEG)
                out[c] = jnp.maximum(ms[c], jnp.max(s, axis=0, keepdims=True))
                alpha_ref[buf, c] = jnp.exp2(ms[c] - out[c])
                p_ref[buf, c] = jnp.exp2(s - out[c]).astype(BF16)
            if part in pending_parts:
                pending_values(c, j_prev, 1 - buf)
        return tuple(out)

    every = tuple(range(n_parts))
    low, high = every[:n_parts // 2], every[n_parts // 2:]

    def pair(i, ms):
        ms = step(2 * i, ms, 0, every, (), every)
        return step(2 * i + 1, ms, 1, every, (), every)

    m_init = tuple(jnp.full((1, B_QW), NEG, F32) for _ in chains)
    ms = lax.fori_loop(0, qi, pair, m_init)
    ms = step(2 * qi, ms, 0, every, low, every)
    step(2 * qi + 1, ms, 1, high, high, every)
    for c, (part, _) in enumerate(chains):
        if part in high:
            pending_values(c, 2 * qi + 1, 1)

    lam = _lambda(lamp_ref)
    for c in range(0, len(chains), 2):
        part, (h, _) = chains[c]
        a0 = acc_ref[c]
        a1 = acc_ref[c + 1]
        o = a0[0:LANES] / a0[LANES:LANES + 1] - lam * (a1[0:LANES] / a1[LANES:LANES + 1])
        o = o * lax.rsqrt(jnp.mean(o * o, axis=0, keepdims=True) + LN_EPS) * gcol_ref[...] * (1.0 - LAM_INIT)
        o_ref[part * B_QW:(part + 1) * B_QW, h * LANES:(h + 1) * LANES] = o.T.astype(BF16)

    step_id = pl.program_id(0) * pl.num_programs(1) + pl.program_id(1)
    for w_ref, wbf_ref, n_steps in zip(w_refs, wbf_refs, cast_steps):
        def cast(w_ref=w_ref, wbf_ref=wbf_ref):
            if len(w_ref.shape) == 2:
                wbf_ref[...] = w_ref[...].astype(BF16)
            else:
                width = w_ref.shape[3]
                for h in range(w_ref.shape[2]):
                    wbf_ref[0, :, h * width:(h + 1) * width] = w_ref[0, :, h, :].astype(BF16)
        if n_steps == total_steps:
            cast()
        else:
            pl.when(step_id < n_steps)(cast)


def _attn_b_prompt(qbt, kb, vbt, lamp, gcol, weights):
    rows = kb.shape[0]
    assert rows % B_TQ == 0 and B_TQ == 2 * B_TK and B_TK == 2 * B_QW and B_QW % CHUNK == 0
    assert H_B % B_HEADS == 0
    n_chains = (B_TQ // B_QW) * B_HEADS * 2
    n_q = rows // B_TQ
    total_steps = (H_B // B_HEADS) * n_q
    single = pl.Buffered(1)

    def cast_plan(w):
        step = lambda h, i, n: jnp.minimum(h * n_q + i, n - 1)
        if w.ndim == 2:
            n_steps = total_steps
            while w.shape[0] % (n_steps * BF16_SUBLANES):
                n_steps //= 2
            spec = pl.BlockSpec((w.shape[0] // n_steps, w.shape[1]), lambda h, i: (step(h, i, n_steps), 0))
            return n_steps, spec, spec, w.shape
        sets, n_rows, heads, width = w.shape
        per_set = total_steps // sets
        rows = n_rows // per_set
        assert per_set * sets == total_steps and rows * per_set == n_rows and rows % BF16_SUBLANES == 0
        in_spec = pl.BlockSpec((1, rows, heads, width),
                               lambda h, i: (step(h, i, total_steps) // per_set, step(h, i, total_steps) % per_set, 0, 0))
        out_spec = pl.BlockSpec((1, rows, heads * width),
                                lambda h, i: (step(h, i, total_steps) // per_set, step(h, i, total_steps) % per_set, 0))
        return total_steps, in_spec, out_spec, (sets, n_rows, heads * width)

    plans = [cast_plan(w) for w in weights]
    outs = pl.pallas_call(
        functools.partial(_attn_b_kernel, cast_steps=tuple(p[0] for p in plans), total_steps=total_steps),
        grid=(H_B // B_HEADS, n_q),
        in_specs=[pl.BlockSpec((B_HEADS, LANES, B_TQ), lambda h, i: (h, 0, i)),
                  pl.BlockSpec((rows, B_HEADS * LANES), lambda h, i: (0, h), pipeline_mode=single),
                  pl.BlockSpec((B_HEADS,) + vbt.shape[1:], lambda h, i: (h, 0, 0, 0), pipeline_mode=single),
                  _const_spec(lamp.shape), _const_spec(gcol.shape)] + [p[1] for p in plans],
        out_specs=[pl.BlockSpec((B_TQ, B_HEADS * LANES), lambda h, i: (i, h))] + [p[2] for p in plans],
        out_shape=[jax.ShapeDtypeStruct((rows, B_WIDTH), BF16)]
                  + [jax.ShapeDtypeStruct(p[3], BF16) for p in plans],
        scratch_shapes=[pltpu.VMEM((n_chains, V_ROWS, B_QW), F32),
                        pltpu.VMEM((2, n_chains, B_TK, B_QW), BF16), pltpu.VMEM((2, n_chains, 1, B_QW), F32)],
        compiler_params=_arbitrary(2),
        name="attn_b_prompt",
    )(qbt, kb, vbt, lamp, gcol, *weights)
    return outs[0], outs[1:]


def _attn_b_sample_kernel(q_ref, kn_ref, vn_ref, kc_ref, vc_ref, lamp_ref, g_ref, o_ref):
    t = q_ref.shape[0]
    lam = _lambda(lamp_ref)
    g = g_ref[...]
    for h in range(H_B):
        sl = slice(h * LANES, (h + 1) * LANES)
        qz, _ = _pair_queries(q_ref[:, sl])
        s_past = _dot_nt(qz, kc_ref[0, :, sl].astype(BF16))
        s_new = _dot_nt(qz, kn_ref[:, sl])
        m = jnp.maximum(jnp.max(s_past, axis=-1, keepdims=True), jnp.max(s_new, axis=-1, keepdims=True))
        e_past = jnp.exp(s_past - m)
        e_new = jnp.exp(s_new - m)
        l = jnp.sum(e_past, axis=-1, keepdims=True) + jnp.sum(e_new, axis=-1, keepdims=True)
        o = (_dot(e_past.astype(BF16), vc_ref[0, :, h, :].astype(BF16))
             + _dot(e_new.astype(BF16), vn_ref[:, sl])) / l
        o_ref[:, sl] = _diff_finish(o[:t], o[t:], lam, g)


def _attn_b_sample(qb, kb, vb, cache_k, cache_v, lamp, g, t):
    rows = qb.shape[0]
    streams, past, width = cache_k.shape
    assert past % CHUNK == 0 and t <= CHUNK
    row = pl.BlockSpec((t, B_WIDTH), lambda s: (s, 0))
    cache = pl.BlockSpec((1, past, width), lambda s: (s, 0, 0))
    cache_v_spec = pl.BlockSpec((1,) + cache_v.shape[1:], lambda s: (s, 0, 0, 0))
    return pl.pallas_call(
        _attn_b_sample_kernel,
        grid=(streams,),
        in_specs=[row, row, row, cache, cache_v_spec, _const_spec(lamp.shape), _const_spec(g.shape)],
        out_specs=row,
        out_shape=jax.ShapeDtypeStruct((rows, B_WIDTH), BF16),
        compiler_params=_arbitrary(1),
        name="attn_b_sample",
    )(qb, kb, vb, cache_k, cache_v, lamp, g)


def _mem_kv_kernel(x_ref, wk_ref, wv_ref, kf_ref, vf_ref):
    xb = x_ref[...].astype(BF16)
    kf_ref[...] = _dot(xb, wk_ref[...])
    vf_ref[...] = _dot(xb, wv_ref[...])


def _mem_kv(mem, wk_bf, wv_bf):
    f = jax.ShapeDtypeStruct(mem.shape, F32)
    return pl.pallas_call(
        _mem_kv_kernel,
        out_shape=[f, f],
        name="mem_kv",
    )(mem, wk_bf, wv_bf)


def _layer_norm(x, g, b):
    mu = jnp.mean(x, axis=-1, keepdims=True)
    xc = x - mu
    var = jnp.mean(xc * xc, axis=-1, keepdims=True)
    return xc * lax.rsqrt(var + LN_EPS) * g + b


def _tail_kernel(x_ref, oa_ref, ob_ref, mk_ref, mv_ref, cin_ref,
                 wo_ref, ln1g_ref, ln1b_ref, wmq_ref, wmo_ref, ln2g_ref, ln2b_ref,
                 wup_ref, cw_ref, cb_ref, wdn_ref, ln3g_ref, ln3b_ref,
                 y_ref, carry_ref, *, seg, sub):
    tm = x_ref.shape[0]
    n_seg = tm // seg
    n_sub = tm // sub
    assert n_seg == 1 or n_sub == 1
    subs = [slice(i * sub, (i + 1) * sub) for i in range(n_sub)]

    @pl.when(pl.program_id(0) == 0)
    def _():
        carry_ref[...] = cin_ref[...]

    def mem_attend(q, mk, mv):
        heads = []
        for h in range(H_M):
            sl = slice(h * HD_M, (h + 1) * HD_M)
            s = _dot_nt(q[:, sl], mk[:, sl])
            e = jnp.exp(s - jnp.max(s, axis=-1, keepdims=True))
            l = jnp.sum(e, axis=-1, keepdims=True)
            heads.append((_dot(e.astype(BF16), mv[:, sl]) / l).astype(BF16))
        return jnp.concatenate(heads, axis=1)

    def mem_attend_rows(q):
        if mk_ref.shape[0] == 1:
            return mem_attend(q, mk_ref[0].astype(BF16), mv_ref[0].astype(BF16))
        return jnp.concatenate([mem_attend(q[sg * seg:(sg + 1) * seg],
                                           mk_ref[sg].astype(BF16), mv_ref[sg].astype(BF16))
                                for sg in range(n_seg)], axis=0)

    mix = [_dot(oa_ref[sl, :], wo_ref[0:A_WIDTH, :]) + _dot(ob_ref[sl, :], wo_ref[A_WIDTH:A_WIDTH + B_WIDTH, :])
           for sl in subs]
    x1 = [_layer_norm(DEEPNORM_ALPHA * x_ref[sl, :] + m, ln1g_ref[...], ln1b_ref[...]) for sl, m in zip(subs, mix)]
    qm = [(_dot(v.astype(BF16), wmq_ref[...]) * (HD_M ** -0.5)).astype(BF16) for v in x1]
    om = [mem_attend_rows(q) for q in qm]
    mo = [_dot(v, wmo_ref[...]) for v in om]
    x2 = [_layer_norm(DEEPNORM_ALPHA * a + b, ln2g_ref[...], ln2b_ref[...]) for a, b in zip(x1, mo)]
    x2b = [v.astype(BF16) for v in x2]

    def conv(us, col, width):
        cols = slice(col, col + width)
        w0, w1, w2, b = cw_ref[0:1, cols], cw_ref[1:2, cols], cw_ref[2:3, cols], cb_ref[0:1, cols]

        def taps(u, p2, p1):
            row = lax.broadcasted_iota(jnp.int32, u.shape, 0)
            u1 = jnp.where(row == 0, p1, pltpu.roll(u, 1, axis=0))
            u2 = jnp.where(row == 0, p2, jnp.where(row == 1, p1, pltpu.roll(u, 2, axis=0)))
            return u2 * w0 + u1 * w1 + u * w2 + b

        def state(sg):
            base = SUBLANES * sg
            return carry_ref[base + 6:base + 7, cols], carry_ref[base + 7:base + 8, cols]

        if n_seg == 1:
            p2, p1 = state(0)
            outs = []
            for u in us:
                outs.append(taps(u, p2, p1))
                p2, p1 = u[sub - 2:sub - 1], u[sub - 1:sub]
            carry_ref[0:SUBLANES, cols] = us[-1][sub - SUBLANES:sub]
            return outs
        (u,) = us
        outs = []
        for sg in range(n_seg):
            useg = u[sg * seg:(sg + 1) * seg]
            outs.append(taps(useg, *state(sg)))
            carry_ref[SUBLANES * sg:SUBLANES * (sg + 1), cols] = useg[seg - SUBLANES:seg]
        return [jnp.concatenate(outs, axis=0)]

    f = [jnp.zeros((sub, D_MODEL), F32) for _ in subs]
    c0 = 0
    for width in FF_BLOCKS:
        gate = conv([_dot(v, wup_ref[:, c0:c0 + width]) for v in x2b], c0, width)
        val = conv([_dot(v, wup_ref[:, D_FF + c0:D_FF + c0 + width]) for v in x2b], D_FF + c0, width)
        hid = [(g * (1.0 / (1.0 + jnp.exp(-g))) * v).astype(BF16) for g, v in zip(gate, val)]
        f = [a + _dot(h, wdn_ref[c0:c0 + width, :]) for a, h in zip(f, hid)]
        c0 += width
    for sl, a, b in zip(subs, x2, f):
        y_ref[sl, :] = _layer_norm(DEEPNORM_ALPHA * a + b, ln3g_ref[...], ln3b_ref[...])


def _tail(x, oa, ob, mk, mv, conv_in, params, tm, seg, sub):
    rows = x.shape[0]
    n_seg = tm // seg
    assert tm % sub == 0 and sub % SUBLANES == 0
    assert conv_in.shape == (SUBLANES * n_seg, 2 * D_FF)
    assert n_seg == 1 or rows == tm
    assert mk.shape[0] in (1, n_seg)
    row_spec = lambda w: pl.BlockSpec((tm, w), lambda i: (i, 0))
    carry_spec = pl.BlockSpec(conv_in.shape, lambda i: (0, 0))
    return pl.pallas_call(
        functools.partial(_tail_kernel, seg=seg, sub=sub),
        grid=(rows // tm,),
        in_specs=[row_spec(D_MODEL), row_spec(A_WIDTH), row_spec(B_WIDTH),
                  _const_spec(mk.shape), _const_spec(mv.shape),
                  _const_spec(conv_in.shape)] + [_const_spec(p.shape) for p in params],
        out_specs=[row_spec(D_MODEL), carry_spec],
        out_shape=[jax.ShapeDtypeStruct((rows, D_MODEL), F32),
                   jax.ShapeDtypeStruct(conv_in.shape, F32)],
        compiler_params=_arbitrary(1),
        name="tail",
    )(x, oa, ob, mk, mv, conv_in, *params)


def kernel(x_prompt, x_sample, mem_prompt, cache_a_k, cache_a_v, cache_b_k, cache_b_v, cache_mem_k, cache_mem_v,
           state_conv, w_qkv, rel_bias, lambda_q1, lambda_k1, lambda_q2, lambda_k2, subln_g, w_o, ln1_g, ln1_b,
           w_mq, w_mk, w_mv, w_mo, ln2_g, ln2_b, w_up, conv_w, conv_b, w_down, ln3_g, ln3_b):
    assert w_qkv.shape[0] == DEPTH == 1
    batch, seq, _ = x_prompt.shape
    streams, t_new, _ = x_sample.shape
    past_b = cache_b_k.shape[2]
    assert batch == 1 and seq % QKV_TM == 0 and QKV_TM == BAND_PREV * CHUNK

    row = lambda v: v[0].astype(F32)[None, :]
    w_qkv_bf = w_qkv[0].astype(BF16)
    lamp = jnp.stack([lambda_q1[0], lambda_k1[0], lambda_q2[0], lambda_k2[0]]).astype(F32)
    g = row(subln_g)
    gcol = jnp.broadcast_to(subln_g[0].astype(F32)[:, None], (2 * HD_B, B_QW))
    bias = _bias_row(rel_bias[0])

    xp = x_prompt.reshape(seq, D_MODEL)
    qat, ka, vat, qbt, kb, vbt, ka_f, va_f, kb_f, vb_f = _qkv(xp, w_qkv_bf, jnp.arange(seq), QKV_TM, True)
    bias_rev = jnp.roll(bias[:, ::-1], 1, axis=1)
    oa = _attn_a_prompt(qat, ka, vat, bias_rev)
    later = [w[0].astype(F32) for w in (w_o, w_mq, w_mo, w_up, w_down, w_mk, w_mv, cache_mem_k, cache_mem_v)]
    ob, (w_o_bf, w_mq_bf, w_mo_bf, w_up_bf, w_down_bf, w_mk_bf, w_mv_bf, mem_k_s, mem_v_s) = _attn_b_prompt(
        qbt, kb, vbt, lamp, gcol, later)
    tail_params = (w_o_bf, row(ln1_g), row(ln1_b), w_mq_bf, w_mo_bf, row(ln2_g), row(ln2_b),
                   w_up_bf, conv_w[0].astype(F32), row(conv_b), w_down_bf, row(ln3_g), row(ln3_b))
    mk_p, mv_p = _mem_kv(mem_prompt.reshape(N_MEM, D_MODEL), w_mk_bf, w_mv_bf)
    conv0 = jnp.zeros((SUBLANES, 2 * D_FF), F32)
    y_p, conv_p = _tail(xp, oa, ob, mk_p[None], mv_p[None], conv0, tail_params, TAIL_TM, TAIL_TM, TAIL_SUB)

    rows_s = streams * t_new
    xs = x_sample.reshape(rows_s, D_MODEL)
    pos_s = jnp.tile(past_b + jnp.arange(t_new), streams)
    qa_s, ka_s, va_s, qb_s, kb_s, vb_s, ka_sf, va_sf, kb_sf, vb_sf = _qkv(xs, w_qkv_bf, pos_s, rows_s, False)
    oa_s = _attn_a_sample(qa_s, ka_s, va_s,
                          cache_a_k[0].reshape(streams, -1, A_WIDTH), cache_a_v[0].reshape(streams, -1, A_WIDTH),
                          bias, t_new)
    ob_s = _attn_b_sample(qb_s, kb_s, vb_s,
                          cache_b_k[0].reshape(streams, past_b, B_WIDTH),
                          cache_b_v[0], lamp, g, t_new)
    conv_in_s = jnp.pad(state_conv[0].astype(F32), ((0, 0), (SUBLANES - (CONV_W - 1), 0), (0, 0)))
    y_s, conv_s = _tail(xs, oa_s, ob_s, mem_k_s, mem_v_s,
                        conv_in_s.reshape(streams * SUBLANES, 2 * D_FF), tail_params, rows_s, t_new, rows_s)

    keep = CONV_W - 1
    return (
        y_p.reshape(batch, seq, D_MODEL),
        y_s.reshape(streams, t_new, D_MODEL),
        ka_f.reshape(1, batch, QKV_TM, H_A, HD_A),
        va_f.reshape(1, batch, QKV_TM, H_A, HD_A),
        kb_f.reshape(1, batch, seq, H_B, 2, HD_B),
        vb_f.reshape(1, batch, seq, H_B, 2 * HD_B),
        mk_p.reshape(1, batch, N_MEM, H_M, HD_M),
        mv_p.reshape(1, batch, N_MEM, H_M, HD_M),
        conv_p[SUBLANES - keep:].reshape(1, batch, keep, 2 * D_FF),
        ka_sf.reshape(1, streams, t_new, H_A, HD_A),
        va_sf.reshape(1, streams, t_new, H_A, HD_A),
        kb_sf.reshape(1, streams, t_new, H_B, 2, HD_B),
        vb_sf.reshape(1, streams, t_new, H_B, 2 * HD_B),
        conv_s.reshape(streams, SUBLANES, 2 * D_FF)[:, SUBLANES - keep:].reshape(1, streams, keep, 2 * D_FF),
    )
```

```python
import functools
import math

import jax
import jax.numpy as jnp
from jax import lax
from jax.experimental import pallas as pl
from jax.experimental.pallas import tpu as pltpu

F32 = jnp.float32
BF16 = jnp.bfloat16

D_MODEL = 1024
CHUNK = 64
BAND_PREV = 8
REL_CLIP = 128
H_A = 8
HD_A = 64
H_B = 4
HD_B = 64
ROT_DIM = HD_B // 4
ROPE_THETA = 500000.0
N_MEM = 256
H_M = 4
HD_M = D_MODEL // H_M
D_FF = 2816
CONV_W = 3
LN_EPS = 1e-5
DEPTH = 1
DEEPNORM_ALPHA = (2.0 * DEPTH) ** 0.25
A_WIDTH = H_A * HD_A
B_WIDTH = H_B * 2 * HD_B
LAM_INIT = 0.8 - 0.6 * math.exp(-0.3 * 0)

LANES = 128
SUBLANES = 8
BF16_SUBLANES = 16
NEG = -1e30
LOG2_E = math.log2(math.e)

QKV_TM = 512
A_TQ = 256
B_TQ = 1024
B_QW = 256
B_HEADS = 2
B_TK = 512
ONES_ROWS = 16
V_ROWS = LANES + ONES_ROWS
VA_ROWS = HD_A + ONES_ROWS
BIAS_SPAN = 1024
TAIL_TM = 512
TAIL_SUB = 256
MXU_DIM = 256
FF_BLOCKS = (4 * MXU_DIM, 4 * MXU_DIM, D_FF - 8 * MXU_DIM)
assert sum(FF_BLOCKS) == D_FF and all(w % MXU_DIM == 0 for w in FF_BLOCKS)

_NT = (((1,), (1,)), ((), ()))


def _dot(a, b):
    return jnp.dot(a, b, preferred_element_type=F32)


def _dot_nt(a, b):
    return lax.dot_general(a, b, _NT, preferred_element_type=F32)


def _const_spec(shape):
    nd = len(shape)
    return pl.BlockSpec(shape, lambda *_: (0,) * nd, pipeline_mode=pl.Buffered(1))


def _arbitrary(n):
    return pltpu.CompilerParams(dimension_semantics=("arbitrary",) * n)


def _rope(z, cos, sin):
    lane = lax.broadcasted_iota(jnp.int32, (z.shape[0], LANES), 1)
    first_half = (lane % HD_B) < (ROT_DIM // 2)
    outs = []
    for c in range(z.shape[1] // LANES):
        zc = z[:, c * LANES:(c + 1) * LANES]
        partner = jnp.where(first_half,
                            pltpu.roll(zc, LANES - ROT_DIM // 2, axis=1),
                            pltpu.roll(zc, ROT_DIM // 2, axis=1))
        outs.append(zc * cos + partner * sin)
    return jnp.concatenate(outs, axis=1)


def _qkv_kernel(x_ref, w_ref, rope_ref,
                qa_ref, ka_ref, va_ref, qb_ref, kb_ref, vb_ref,
                kaf_ref, vaf_ref, kbf_ref, vbf_ref, *, transposed):
    xb = x_ref[...].astype(BF16)
    cos, sin = _spread_rope(rope_ref[...])
    tm = xb.shape[0]

    def section(i):
        return _dot(xb, w_ref[:, i * A_WIDTH:(i + 1) * A_WIDTH])

    vb = section(5)
    for h in range(H_B):
        vbf_ref[:, h, :] = vb[:, h * LANES:(h + 1) * LANES]
    def ones_rows(width):
        return (lax.broadcasted_iota(jnp.int32, (ONES_ROWS, width), 0) == 0).astype(BF16)

    if transposed:
        for h in range(H_B):
            vb_ref[h, 0, 0:LANES, :] = vb[:, h * LANES:(h + 1) * LANES].T.astype(BF16)
            vb_ref[h, 0, LANES:V_ROWS, :] = ones_rows(tm)
    else:
        vb_ref[...] = vb.astype(BF16)
    qb = _rope(section(3), cos, sin) * (HD_B ** -0.5 * (LOG2_E if transposed else 1.0))
    if transposed:
        for h in range(H_B):
            qb_ref[h] = qb[:, h * LANES:(h + 1) * LANES].T.astype(BF16)
    else:
        qb_ref[...] = qb.astype(BF16)
    kb = _rope(section(4), cos, sin)
    kb_ref[...] = kb.astype(BF16)
    kbf_ref[...] = kb
    va = section(2)
    vaf_ref[...] = va
    if transposed:
        for p in range(H_A // 2):
            vt = va[:, p * LANES:(p + 1) * LANES].T.astype(BF16)
            for hh in range(2):
                for t in range(tm // A_TQ):
                    cols = slice(t * A_TQ, (t + 1) * A_TQ)
                    va_ref[2 * p + hh, t, 0:HD_A, :] = vt[hh * HD_A:(hh + 1) * HD_A, cols]
                    va_ref[2 * p + hh, t, HD_A:VA_ROWS, :] = ones_rows(A_TQ)
    else:
        va_ref[...] = va.astype(BF16)
    qa = section(0) * (HD_A ** -0.5 * LOG2_E)
    if transposed:
        for p in range(H_A // 2):
            qa_ref[p] = qa[:, p * LANES:(p + 1) * LANES].T.astype(BF16)
    else:
        qa_ref[...] = qa.astype(BF16)
    ka = section(1)
    ka_ref[...] = ka.astype(BF16)
    kaf_ref[...] = ka


def _rope_table(pos):
    inv = ROPE_THETA ** (-jnp.arange(0, ROT_DIM, 2, dtype=F32) / ROT_DIM)
    ang = pos.astype(F32)[:, None] * inv[None, :]
    cs = lax.optimization_barrier(jnp.concatenate([jnp.cos(ang), jnp.sin(ang)], axis=1))
    return jnp.pad(cs, ((0, 0), (0, LANES - ROT_DIM)))


def _spread_rope(tab):
    half = ROT_DIM // 2
    lane = lax.broadcasted_iota(jnp.int32, tab.shape, 1)
    c = jnp.where(lane < half, tab, 0.0)
    s = jnp.where((lane >= half) & (lane < ROT_DIM), tab, 0.0)
    c = c + pltpu.roll(c, half, axis=1)
    s = s - pltpu.roll(s, LANES - half, axis=1)
    c = c + pltpu.roll(c, HD_B, axis=1)
    s = s + pltpu.roll(s, HD_B, axis=1)
    return jnp.where(lane % HD_B < ROT_DIM, c, 1.0), s


def _qkv(x, w_bf, pos, tm, transposed):
    rows = x.shape[0]
    nt = rows // tm
    rope = _rope_table(pos)
    row_spec = lambda w: pl.BlockSpec((tm, w), lambda i: (i, 0))
    last_spec = pl.BlockSpec((tm, A_WIDTH), lambda i: (0, 0))
    bf = jax.ShapeDtypeStruct((rows, A_WIDTH), BF16)
    if transposed:
        assert tm == B_TK and tm % A_TQ == 0
        qa_spec = pl.BlockSpec((H_A // 2, LANES, tm), lambda i: (0, 0, i))
        qa_shape = jax.ShapeDtypeStruct((H_A // 2, LANES, rows), BF16)
        va_spec = pl.BlockSpec((H_A, tm // A_TQ, VA_ROWS, A_TQ), lambda i: (0, i, 0, 0))
        va_shape = jax.ShapeDtypeStruct((H_A, rows // A_TQ, VA_ROWS, A_TQ), BF16)
        qb_spec = pl.BlockSpec((H_B, LANES, tm), lambda i: (0, 0, i))
        qb_shape = jax.ShapeDtypeStruct((H_B, LANES, rows), BF16)
        vb_spec = pl.BlockSpec((H_B, 1, V_ROWS, tm), lambda i: (0, i, 0, 0))
        vb_shape = jax.ShapeDtypeStruct((H_B, nt, V_ROWS, tm), BF16)
    else:
        qa_spec = va_spec = qb_spec = vb_spec = row_spec(B_WIDTH)
        qa_shape = va_shape = qb_shape = vb_shape = bf
    return pl.pallas_call(
        functools.partial(_qkv_kernel, transposed=transposed),
        grid=(nt,),
        in_specs=[row_spec(D_MODEL), _const_spec(w_bf.shape), row_spec(LANES)],
        out_specs=[qa_spec, row_spec(A_WIDTH), va_spec, qb_spec, row_spec(B_WIDTH), vb_spec]
                  + [last_spec, last_spec, row_spec(B_WIDTH),
                     pl.BlockSpec((tm, H_B, 2 * HD_B), lambda i: (i, 0, 0))],
        out_shape=[qa_shape, bf, va_shape, qb_shape, bf, vb_shape]
                  + [jax.ShapeDtypeStruct((tm, A_WIDTH), F32)] * 2
                  + [jax.ShapeDtypeStruct((rows, B_WIDTH), F32),
                     jax.ShapeDtypeStruct((rows, H_B, 2 * HD_B), F32)],
        compiler_params=_arbitrary(1),
        name="qkv_proj",
    )(x, w_bf, rope)


def _bias_row(rel_bias):
    table = rel_bias.astype(F32)
    far = table[:, 2 * REL_CLIP:]
    near = table[:, :1]
    n_far = 2 * A_TQ - REL_CLIP
    n_near = 3 * A_TQ - (n_far + 2 * REL_CLIP + 1)
    h = table.shape[0]
    return jnp.concatenate([jnp.broadcast_to(far, (h, n_far)), table[:, ::-1],
                            jnp.broadcast_to(near, (h, n_near)),
                            jnp.broadcast_to(far, (h, BIAS_SPAN - 3 * A_TQ))], axis=1)


def _build_band_bias(row_ref, bias_ref):
    _, rows, cols = bias_ref.shape
    q = lax.broadcasted_iota(jnp.int32, (rows, cols), 0)
    k = lax.broadcasted_iota(jnp.int32, (rows, cols), 1)
    qc = q // CHUNK
    kc = k // CHUNK - (2 * A_TQ) // CHUNK
    band = (kc >= qc - BAND_PREV) & (kc <= qc)
    for h in range(H_A):
        r = jnp.broadcast_to(row_ref[h:h + 1, :], (rows, BIAS_SPAN))
        toeplitz = pltpu.roll(r, 0, axis=1, stride=1, stride_axis=0)
        bias_ref[h] = jnp.where(band, toeplitz[:, :cols] * LOG2_E, NEG)


def _pair_queries(q2):
    lane = lax.broadcasted_iota(jnp.int32, q2.shape, 1)
    lo = lane < (LANES // 2)
    zero = jnp.zeros_like(q2)
    return jnp.concatenate([jnp.where(lo, q2, zero), jnp.where(lo, zero, q2)], axis=0), lo


def _build_band_bias_t(row_ref, bias_ref):
    _, n_k, n_q = bias_ref.shape
    k = lax.broadcasted_iota(jnp.int32, (n_k, n_q), 0)
    q = lax.broadcasted_iota(jnp.int32, (n_k, n_q), 1)
    qc = q // CHUNK
    kc = k // CHUNK - (2 * A_TQ) // CHUNK
    band = (kc >= qc - BAND_PREV) & (kc <= qc)
    for h in range(H_A):
        r = jnp.broadcast_to(row_ref[h:h + 1, :], (n_k, BIAS_SPAN))
        toeplitz = pltpu.roll(r, 0, axis=1, stride=1, stride_axis=0)
        bias_ref[h] = jnp.where(band, toeplitz[:, :n_q] * LOG2_E, NEG)


def _attn_a_kernel(qt_ref, ka_ref, kb_ref, kc_ref, kd_ref, va_ref, vb_ref, vc_ref, vd_ref, row_ref,
                   o_even_ref, o_odd_ref, bias_ref, p_ref, *, n_steps):
    g = pl.program_id(0)
    k_refs = (ka_ref, kb_ref, kc_ref, kd_ref)
    v_refs = (va_ref, vb_ref, vc_ref, vd_ref)

    @pl.when(g == 0)
    def _():
        _build_band_bias_t(row_ref, bias_ref)
        p_ref[1] = jnp.zeros(p_ref.shape[1:], BF16)

    def body(check_keys):
        kidx = lax.broadcasted_iota(jnp.int32, (3 * A_TQ, A_TQ), 0)
        dim = lax.broadcasted_iota(jnp.int32, (LANES, A_TQ), 0)
        zero = jnp.zeros((LANES, A_TQ), BF16)

        def scores(t, h):
            pair = h // 2
            sl = slice(pair * LANES, (pair + 1) * LANES)
            qz = jnp.where(dim < HD_A if h % 2 == 0 else dim >= HD_A,
                           qt_ref[pair, :, t * A_TQ:(t + 1) * A_TQ], zero)
            kc = jnp.concatenate([k_refs[t][:, sl], k_refs[t + 1][:, sl], k_refs[t + 2][:, sl]], axis=0)
            s = _dot(kc, qz) + bias_ref[h]
            if check_keys:
                s = jnp.where(kidx >= (2 - t) * A_TQ, s, NEG)
            p_ref[t, h] = jnp.exp2(s - jnp.max(s, axis=0, keepdims=True)).astype(BF16)

        def values(t, o_ref, keep):
            o_t = [None] * H_A

            def head(h):
                e = p_ref[1 - t, h]
                r = (_dot(v_refs[t][h, 0], e[0:A_TQ]) + _dot(v_refs[t + 1][h, 0], e[A_TQ:2 * A_TQ])
                     + _dot(v_refs[t + 2][h, 0], e[2 * A_TQ:3 * A_TQ]))
                o_t[h] = r[0:HD_A] / r[HD_A:HD_A + 1]
                if h % 2 == 1:
                    sl = slice((h // 2) * LANES, (h // 2 + 1) * LANES)
                    o = jnp.concatenate([o_t[h - 1], o_t[h]], axis=0).T.astype(BF16)
                    o_ref[:, sl] = o if keep is None else jnp.where(keep, o, o_ref[:, sl])
            return head

        odd_head = values(0, o_odd_ref, None)
        even_head = values(1, o_even_ref, g < n_steps)
        for h in range(H_A):
            scores(0, h)
            odd_head(h)
        for h in range(H_A):
            scores(1, h)
            even_head(h)

    pl.when(g == 0)(lambda: body(True))
    pl.when(g > 0)(lambda: body(False))


def _attn_a_prompt(qat, ka, vat, bias_row_rev):
    rows = ka.shape[0]
    n_tiles = rows // A_TQ
    n_steps = n_tiles // 2
    assert n_tiles % 2 == 0 and BAND_PREV * CHUNK == 2 * A_TQ
    tile = lambda g, d: jnp.clip(2 * g + d, 0, n_tiles - 1)
    kspec = lambda d: pl.BlockSpec((A_TQ, A_WIDTH), lambda g: (tile(g, d), 0))
    vspec = lambda d: pl.BlockSpec((H_A, 1, VA_ROWS, A_TQ), lambda g: (0, tile(g, d), 0, 0))
    half = jax.ShapeDtypeStruct((rows // 2, A_WIDTH), BF16)
    return pl.pallas_call(
        functools.partial(_attn_a_kernel, n_steps=n_steps),
        grid=(n_steps + 1,),
        in_specs=[pl.BlockSpec((H_A // 2, LANES, 2 * A_TQ), lambda g: (0, 0, jnp.minimum(g, n_steps - 1)))]
                 + [kspec(d) for d in (-2, -1, 0, 1)] + [vspec(d) for d in (-3, -2, -1, 0)]
                 + [_const_spec(bias_row_rev.shape)],
        out_specs=[pl.BlockSpec((A_TQ, A_WIDTH), lambda g: (jnp.minimum(g, n_steps - 1), 0)),
                   pl.BlockSpec((A_TQ, A_WIDTH), lambda g: (jnp.clip(g - 1, 0, n_steps - 1), 0))],
        out_shape=[half, half],
        scratch_shapes=[pltpu.VMEM((H_A, 3 * A_TQ, A_TQ), F32), pltpu.VMEM((2, H_A, 3 * A_TQ, A_TQ), BF16)],
        compiler_params=_arbitrary(1),
        name="attn_a_prompt",
    )(qat, *([ka] * 4), *([vat] * 4), bias_row_rev)


def _attn_a_sample_kernel(q_ref, kn_ref, vn_ref, kc_ref, vc_ref, row_ref, o_ref, bias_ref):
    t = q_ref.shape[0]
    past = kc_ref.shape[1]

    @pl.when(pl.program_id(0) == 0)
    def _():
        _build_band_bias(row_ref, bias_ref)

    kc = kc_ref[0].astype(BF16)
    vc = vc_ref[0].astype(BF16)
    for p in range(H_A // 2):
        sl = slice(p * LANES, (p + 1) * LANES)
        qz, lo = _pair_queries(q_ref[:, sl])
        b_past = jnp.concatenate([bias_ref[2 * p, 0:t, 0:past], bias_ref[2 * p + 1, 0:t, 0:past]], axis=0)
        b_new = jnp.concatenate([bias_ref[2 * p, 0:t, past:past + t],
                                 bias_ref[2 * p + 1, 0:t, past:past + t]], axis=0)
        s_past = _dot_nt(qz, kc[:, sl]) + b_past
        s_new = _dot_nt(qz, kn_ref[:, sl]) + b_new
        m = jnp.maximum(jnp.max(s_past, axis=-1, keepdims=True), jnp.max(s_new, axis=-1, keepdims=True))
        e_past = jnp.exp2(s_past - m)
        e_new = jnp.exp2(s_new - m)
        l = jnp.sum(e_past, axis=-1, keepdims=True) + jnp.sum(e_new, axis=-1, keepdims=True)
        o = (_dot(e_past.astype(BF16), vc[:, sl]) + _dot(e_new.astype(BF16), vn_ref[:, sl])) / l
        o_ref[:, sl] = jnp.where(lo, o[:t], o[t:]).astype(BF16)


def _attn_a_sample(qa, ka, va, cache_k, cache_v, bias_row, t):
    rows = qa.shape[0]
    streams, past, width = cache_k.shape
    assert past == 2 * A_TQ and t <= CHUNK and t % SUBLANES == 0
    row = pl.BlockSpec((t, A_WIDTH), lambda s: (s, 0))
    cache = pl.BlockSpec((1, past, width), lambda s: (s, 0, 0))
    return pl.pallas_call(
        _attn_a_sample_kernel,
        grid=(streams,),
        in_specs=[row, row, row, cache, cache, _const_spec(bias_row.shape)],
        out_specs=row,
        out_shape=jax.ShapeDtypeStruct((rows, A_WIDTH), BF16),
        scratch_shapes=[pltpu.VMEM((H_A, t, past + LANES), F32)],
        compiler_params=_arbitrary(1),
        name="attn_a_sample",
    )(qa, ka, va, cache_k, cache_v, bias_row)


def _lambda(lamp_ref):
    lp = lamp_ref[...]
    a = jnp.sum(lp[0:1] * lp[1:2], axis=-1, keepdims=True)
    b = jnp.sum(lp[2:3] * lp[3:4], axis=-1, keepdims=True)
    return jnp.exp(a) - jnp.exp(b) + LAM_INIT


def _diff_finish(o0, o1, lam, g):
    o = o0 - lam * o1
    o = o * lax.rsqrt(jnp.mean(o * o, axis=-1, keepdims=True) + LN_EPS) * g * (1.0 - LAM_INIT)
    return o.astype(BF16)


def _attn_b_kernel(*refs, cast_steps, total_steps):
    n_cast = len(cast_steps)
    qt_ref, k_ref, vt_ref, lamp_ref, gcol_ref = refs[:5]
    w_refs = refs[5:5 + n_cast]
    o_ref = refs[5 + n_cast]
    wbf_refs = refs[6 + n_cast:6 + 2 * n_cast]
    acc_ref, p_ref, alpha_ref = refs[6 + 2 * n_cast:]

    qi = pl.program_id(1)
    n_parts = B_TQ // B_QW
    dim = lax.broadcasted_iota(jnp.int32, (LANES, B_QW), 0)
    zero = jnp.zeros((LANES, B_QW), BF16)
    chains = [(part, (h, mp)) for part in range(n_parts) for h in range(B_HEADS) for mp in range(2)]
    qz = []
    for part, (h, mp) in chains:
        qt = qt_ref[h, :, part * B_QW:(part + 1) * B_QW]
        qz.append(jnp.where(dim < HD_B if mp == 0 else dim >= HD_B, qt, zero))
    acc_ref[...] = jnp.zeros(acc_ref.shape, F32)
    p_ref[1] = jnp.zeros(p_ref.shape[1:], BF16)
    alpha_ref[1] = jnp.ones(alpha_ref.shape[1:], F32)

    def pending_values(c, j_prev, buf):
        h = chains[c][1][0]
        acc_ref[c] = alpha_ref[buf, c] * acc_ref[c] + _dot(vt_ref[h, j_prev], p_ref[buf, c])

    def step(j, ms, buf, parts, diag_parts, pending_parts):
        start = pl.multiple_of(j * B_TK, B_TK)
        j_prev = jnp.maximum(j - 1, 0)
        kpos = start + lax.broadcasted_iota(jnp.int32, (B_TK, B_QW), 0)
        lane = lax.broadcasted_iota(jnp.int32, (B_TK, B_QW), 1)
        out = list(ms)
        for c, (part, (h, _)) in enumerate(chains):
            if part in parts:
                s = _dot(k_ref[pl.ds(start, B_TK), h * LANES:(h + 1) * LANES], qz[c])
                if part in diag_parts:
                    qpos = qi * B_TQ + part * B_QW + lane
                    s = jnp.where(kpos < (qpos // CHUNK + 1) * CHUNK, s, NEG)
                out[c] = jnp.maximum(ms[c], jnp.max(s, axis=0, keepdims=True))
                alpha_ref[buf, c] = jnp.exp2(ms[c] - out[c])
                p_ref[buf, c] = jnp.exp2(s - out[c]).astype(BF16)
            if part in pending_parts:
                pending_values(c, j_prev, 1 - buf)
        return tuple(out)

    every = tuple(range(n_parts))
    low, high = every[:n_parts // 2], every[n_parts // 2:]

    def pair(i, ms):
        ms = step(2 * i, ms, 0, every, (), every)
        return step(2 * i + 1, ms, 1, every, (), every)

    m_init = tuple(jnp.full((1, B_QW), NEG, F32) for _ in chains)
    ms = lax.fori_loop(0, qi, pair, m_init)
    ms = step(2 * qi, ms, 0, every, low, every)
    step(2 * qi + 1, ms, 1, high, high, every)
    for c, (part, _) in enumerate(chains):
        if part in high:
            pending_values(c, 2 * qi + 1, 1)

    lam = _lambda(lamp_ref)
    for c in range(0, len(chains), 2):
        part, (h, _) = chains[c]
        a0 = acc_ref[c]
        a1 = acc_ref[c + 1]
        o = a0[0:LANES] / a0[LANES:LANES + 1] - lam * (a1[0:LANES] / a1[LANES:LANES + 1])
        o = o * lax.rsqrt(jnp.mean(o * o, axis=0, keepdims=True) + LN_EPS) * gcol_ref[...] * (1.0 - LAM_INIT)
        o_ref[part * B_QW:(part + 1) * B_QW, h * LANES:(h + 1) * LANES] = o.T.astype(BF16)

    step_id = pl.program_id(0) * pl.num_programs(1) + pl.program_id(1)
    for w_ref, wbf_ref, n_steps in zip(w_refs, wbf_refs, cast_steps):
        def cast(w_ref=w_ref, wbf_ref=wbf_ref):
            if len(w_ref.shape) == 2:
                wbf_ref[...] = w_ref[...].astype(BF16)
            else:
                width = w_ref.shape[3]
                for h in range(w_ref.shape[2]):
                    wbf_ref[0, :, h * width:(h + 1) * width] = w_ref[0, :, h, :].astype(BF16)
        if n_steps == total_steps:
            cast()
        else:
            pl.when(step_id < n_steps)(cast)


def _attn_b_prompt(qbt, kb, vbt, lamp, gcol, weights):
    rows = kb.shape[0]
    assert rows % B_TQ == 0 and B_TQ == 2 * B_TK and B_TK == 2 * B_QW and B_QW % CHUNK == 0
    assert H_B % B_HEADS == 0
    n_chains = (B_TQ // B_QW) * B_HEADS * 2
    n_q = rows // B_TQ
    total_steps = (H_B // B_HEADS) * n_q
    single = pl.Buffered(1)

    def cast_plan(w):
        step = lambda h, i, n: jnp.minimum(h * n_q + i, n - 1)
        if w.ndim == 2:
            n_steps = total_steps
            while w.shape[0] % (n_steps * BF16_SUBLANES):
                n_steps //= 2
            spec = pl.BlockSpec((w.shape[0] // n_steps, w.shape[1]), lambda h, i: (step(h, i, n_steps), 0))
            return n_steps, spec, spec, w.shape
        sets, n_rows, heads, width = w.shape
        per_set = total_steps // sets
        rows = n_rows // per_set
        assert per_set * sets == total_steps and rows * per_set == n_rows and rows % BF16_SUBLANES == 0
        in_spec = pl.BlockSpec((1, rows, heads, width),
                               lambda h, i: (step(h, i, total_steps) // per_set, step(h, i, total_steps) % per_set, 0, 0))
        out_spec = pl.BlockSpec((1, rows, heads * width),
                                lambda h, i: (step(h, i, total_steps) // per_set, step(h, i, total_steps) % per_set, 0))
        return total_steps, in_spec, out_spec, (sets, n_rows, heads * width)

    plans = [cast_plan(w) for w in weights]
    outs = pl.pallas_call(
        functools.partial(_attn_b_kernel, cast_steps=tuple(p[0] for p in plans), total_steps=total_steps),
        grid=(H_B // B_HEADS, n_q),
        in_specs=[pl.BlockSpec((B_HEADS, LANES, B_TQ), lambda h, i: (h, 0, i)),
                  pl.BlockSpec((rows, B_HEADS * LANES), lambda h, i: (0, h), pipeline_mode=single),
                  pl.BlockSpec((B_HEADS,) + vbt.shape[1:], lambda h, i: (h, 0, 0, 0), pipeline_mode=single),
                  _const_spec(lamp.shape), _const_spec(gcol.shape)] + [p[1] for p in plans],
        out_specs=[pl.BlockSpec((B_TQ, B_HEADS * LANES), lambda h, i: (i, h))] + [p[2] for p in plans],
        out_shape=[jax.ShapeDtypeStruct((rows, B_WIDTH), BF16)]
                  + [jax.ShapeDtypeStruct(p[3], BF16) for p in plans],
        scratch_shapes=[pltpu.VMEM((n_chains, V_ROWS, B_QW), F32),
                        pltpu.VMEM((2, n_chains, B_TK, B_QW), BF16), pltpu.VMEM((2, n_chains, 1, B_QW), F32)],
        compiler_params=_arbitrary(2),
        name="attn_b_prompt",
    )(qbt, kb, vbt, lamp, gcol, *weights)
    return outs[0], outs[1:]


def _attn_b_sample_kernel(q_ref, kn_ref, vn_ref, kc_ref, vc_ref, lamp_ref, g_ref, o_ref):
    t = q_ref.shape[0]
    lam = _lambda(lamp_ref)
    g = g_ref[...]
    for h in range(H_B):
        sl = slice(h * LANES, (h + 1) * LANES)
        qz, _ = _pair_queries(q_ref[:, sl])
        s_past = _dot_nt(qz, kc_ref[0, :, sl].astype(BF16))
        s_new = _dot_nt(qz, kn_ref[:, sl])
        m = jnp.maximum(jnp.max(s_past, axis=-1, keepdims=True), jnp.max(s_new, axis=-1, keepdims=True))
        e_past = jnp.exp(s_past - m)
        e_new = jnp.exp(s_new - m)
        l = jnp.sum(e_past, axis=-1, keepdims=True) + jnp.sum(e_new, axis=-1, keepdims=True)
        o = (_dot(e_past.astype(BF16), vc_ref[0, :, h, :].astype(BF16))
             + _dot(e_new.astype(BF16), vn_ref[:, sl])) / l
        o_ref[:, sl] = _diff_finish(o[:t], o[t:], lam, g)


def _attn_b_sample(qb, kb, vb, cache_k, cache_v, lamp, g, t):
    rows = qb.shape[0]
    streams, past, width = cache_k.shape
    assert past % CHUNK == 0 and t <= CHUNK
    row = pl.BlockSpec((t, B_WIDTH), lambda s: (s, 0))
    cache = pl.BlockSpec((1, past, width), lambda s: (s, 0, 0))
    cache_v_spec = pl.BlockSpec((1,) + cache_v.shape[1:], lambda s: (s, 0, 0, 0))
    return pl.pallas_call(
        _attn_b_sample_kernel,
        grid=(streams,),
        in_specs=[row, row, row, cache, cache_v_spec, _const_spec(lamp.shape), _const_spec(g.shape)],
        out_specs=row,
        out_shape=jax.ShapeDtypeStruct((rows, B_WIDTH), BF16),
        compiler_params=_arbitrary(1),
        name="attn_b_sample",
    )(qb, kb, vb, cache_k, cache_v, lamp, g)


def _mem_kv_kernel(x_ref, wk_ref, wv_ref, kf_ref, vf_ref):
    xb = x_ref[...].astype(BF16)
    kf_ref[...] = _dot(xb, wk_ref[...])
    vf_ref[...] = _dot(xb, wv_ref[...])


def _mem_kv(mem, wk_bf, wv_bf):
    f = jax.ShapeDtypeStruct(mem.shape, F32)
    return pl.pallas_call(
        _mem_kv_kernel,
        out_shape=[f, f],
        name="mem_kv",
    )(mem, wk_bf, wv_bf)


def _layer_norm(x, g, b):
    mu = jnp.mean(x, axis=-1, keepdims=True)
    xc = x - mu
    var = jnp.mean(xc * xc, axis=-1, keepdims=True)
    return xc * lax.rsqrt(var + LN_EPS) * g + b


def _tail_kernel(*refs, seg, sub):
    x_ref = refs[0]
    tm = x_ref.shape[0]
    n_seg = tm // seg
    n_sub = tm // sub
    assert n_seg == 1 or n_sub == 1
    subs = [slice(i * sub, (i + 1) * sub) for i in range(n_sub)]
    oa_refs = refs[1:1 + n_sub]
    (ob_ref, mk_ref, mv_ref, cin_ref, wo_ref, ln1g_ref, ln1b_ref, wmq_ref, wmo_ref, ln2g_ref, ln2b_ref,
     wup_ref, cw_ref, cb_ref, wdn_ref, ln3g_ref, ln3b_ref, y_ref, carry_ref) = refs[1 + n_sub:]

    @pl.when(pl.program_id(0) == 0)
    def _():
        carry_ref[...] = cin_ref[...]

    def mem_attend(q, mk, mv):
        heads = []
        for h in range(H_M):
            sl = slice(h * HD_M, (h + 1) * HD_M)
            s = _dot_nt(q[:, sl], mk[:, sl])
            e = jnp.exp(s - jnp.max(s, axis=-1, keepdims=True))
            l = jnp.sum(e, axis=-1, keepdims=True)
            heads.append((_dot(e.astype(BF16), mv[:, sl]) / l).astype(BF16))
        return jnp.concatenate(heads, axis=1)

    def mem_attend_rows(q):
        if mk_ref.shape[0] == 1:
            return mem_attend(q, mk_ref[0].astype(BF16), mv_ref[0].astype(BF16))
        return jnp.concatenate([mem_attend(q[sg * seg:(sg + 1) * seg],
                                           mk_ref[sg].astype(BF16), mv_ref[sg].astype(BF16))
                                for sg in range(n_seg)], axis=0)

    mix = [_dot(oa_ref[...], wo_ref[0:A_WIDTH, :]) + _dot(ob_ref[sl, :], wo_ref[A_WIDTH:A_WIDTH + B_WIDTH, :])
           for oa_ref, sl in zip(oa_refs, subs)]
    x1 = [_layer_norm(DEEPNORM_ALPHA * x_ref[sl, :] + m, ln1g_ref[...], ln1b_ref[...]) for sl, m in zip(subs, mix)]
    qm = [(_dot(v.astype(BF16), wmq_ref[...]) * (HD_M ** -0.5)).astype(BF16) for v in x1]
    om = [mem_attend_rows(q) for q in qm]
    mo = [_dot(v, wmo_ref[...]) for v in om]
    x2 = [_layer_norm(DEEPNORM_ALPHA * a + b, ln2g_ref[...], ln2b_ref[...]) for a, b in zip(x1, mo)]
    x2b = [v.astype(BF16) for v in x2]

    def conv(us, col, width):
        cols = slice(col, col + width)
        w0, w1, w2, b = cw_ref[0:1, cols], cw_ref[1:2, cols], cw_ref[2:3, cols], cb_ref[0:1, cols]

        def taps(u, p2, p1):
            row = lax.broadcasted_iota(jnp.int32, u.shape, 0)
            u1 = jnp.where(row == 0, p1, pltpu.roll(u, 1, axis=0))
            u2 = jnp.where(row == 0, p2, jnp.where(row == 1, p1, pltpu.roll(u, 2, axis=0)))
            return u2 * w0 + u1 * w1 + u * w2 + b

        def state(sg):
            base = SUBLANES * sg
            return carry_ref[base + 6:base + 7, cols], carry_ref[base + 7:base + 8, cols]

        if n_seg == 1:
            p2, p1 = state(0)
            outs = []
            for u in us:
                outs.append(taps(u, p2, p1))
                p2, p1 = u[sub - 2:sub - 1], u[sub - 1:sub]
            carry_ref[0:SUBLANES, cols] = us[-1][sub - SUBLANES:sub]
            return outs
        (u,) = us
        outs = []
        for sg in range(n_seg):
            useg = u[sg * seg:(sg + 1) * seg]
            outs.append(taps(useg, *state(sg)))
            carry_ref[SUBLANES * sg:SUBLANES * (sg + 1), cols] = useg[seg - SUBLANES:seg]
        return [jnp.concatenate(outs, axis=0)]

    f = [jnp.zeros((sub, D_MODEL), F32) for _ in subs]
    c0 = 0
    for width in FF_BLOCKS:
        gate = conv([_dot(v, wup_ref[:, c0:c0 + width]) for v in x2b], c0, width)
        val = conv([_dot(v, wup_ref[:, D_FF + c0:D_FF + c0 + width]) for v in x2b], D_FF + c0, width)
        hid = [(g * (1.0 / (1.0 + jnp.exp(-g))) * v).astype(BF16) for g, v in zip(gate, val)]
        f = [a + _dot(h, wdn_ref[c0:c0 + width, :]) for a, h in zip(f, hid)]
        c0 += width
    for sl, a, b in zip(subs, x2, f):
        y_ref[sl, :] = _layer_norm(DEEPNORM_ALPHA * a + b, ln3g_ref[...], ln3b_ref[...])


def _tail(x, oas, ob, mk, mv, conv_in, params, tm, seg, sub):
    rows = x.shape[0]
    n_seg = tm // seg
    assert tm % sub == 0 and sub % SUBLANES == 0 and len(oas) == tm // sub
    assert conv_in.shape == (SUBLANES * n_seg, 2 * D_FF)
    assert n_seg == 1 or rows == tm
    assert mk.shape[0] in (1, n_seg)
    row_spec = lambda w: pl.BlockSpec((tm, w), lambda i: (i, 0))
    carry_spec = pl.BlockSpec(conv_in.shape, lambda i: (0, 0))
    return pl.pallas_call(
        functools.partial(_tail_kernel, seg=seg, sub=sub),
        grid=(rows // tm,),
        in_specs=[row_spec(D_MODEL)] + [pl.BlockSpec((sub, A_WIDTH), lambda i: (i, 0)) for _ in oas]
                 + [row_spec(B_WIDTH), _const_spec(mk.shape), _const_spec(mv.shape),
                  _const_spec(conv_in.shape)] + [_const_spec(p.shape) for p in params],
        out_specs=[row_spec(D_MODEL), carry_spec],
        out_shape=[jax.ShapeDtypeStruct((rows, D_MODEL), F32),
                   jax.ShapeDtypeStruct(conv_in.shape, F32)],
        compiler_params=_arbitrary(1),
        name="tail",
    )(x, *oas, ob, mk, mv, conv_in, *params)


def kernel(x_prompt, x_sample, mem_prompt, cache_a_k, cache_a_v, cache_b_k, cache_b_v, cache_mem_k, cache_mem_v,
           state_conv, w_qkv, rel_bias, lambda_q1, lambda_k1, lambda_q2, lambda_k2, subln_g, w_o, ln1_g, ln1_b,
           w_mq, w_mk, w_mv, w_mo, ln2_g, ln2_b, w_up, conv_w, conv_b, w_down, ln3_g, ln3_b):
    assert w_qkv.shape[0] == DEPTH == 1
    batch, seq, _ = x_prompt.shape
    streams, t_new, _ = x_sample.shape
    past_b = cache_b_k.shape[2]
    assert batch == 1 and seq % QKV_TM == 0 and QKV_TM == BAND_PREV * CHUNK

    row = lambda v: v[0].astype(F32)[None, :]
    w_qkv_bf = w_qkv[0].astype(BF16)
    lamp = jnp.stack([lambda_q1[0], lambda_k1[0], lambda_q2[0], lambda_k2[0]]).astype(F32)
    g = row(subln_g)
    gcol = jnp.broadcast_to(subln_g[0].astype(F32)[:, None], (2 * HD_B, B_QW))
    bias = _bias_row(rel_bias[0])

    xp = x_prompt.reshape(seq, D_MODEL)
    qat, ka, vat, qbt, kb, vbt, ka_f, va_f, kb_f, vb_f = _qkv(xp, w_qkv_bf, jnp.arange(seq), QKV_TM, True)
    bias_rev = jnp.roll(bias[:, ::-1], 1, axis=1)
    assert TAIL_TM == 2 * A_TQ and TAIL_SUB == A_TQ
    oa_even, oa_odd = _attn_a_prompt(qat, ka, vat, bias_rev)
    later = [w[0].astype(F32) for w in (w_o, w_mq, w_mo, w_up, w_down, w_mk, w_mv, cache_mem_k, cache_mem_v)]
    ob, (w_o_bf, w_mq_bf, w_mo_bf, w_up_bf, w_down_bf, w_mk_bf, w_mv_bf, mem_k_s, mem_v_s) = _attn_b_prompt(
        qbt, kb, vbt, lamp, gcol, later)
    tail_params = (w_o_bf, row(ln1_g), row(ln1_b), w_mq_bf, w_mo_bf, row(ln2_g), row(ln2_b),
                   w_up_bf, conv_w[0].astype(F32), row(conv_b), w_down_bf, row(ln3_g), row(ln3_b))
    mk_p, mv_p = _mem_kv(mem_prompt.reshape(N_MEM, D_MODEL), w_mk_bf, w_mv_bf)
    conv0 = jnp.zeros((SUBLANES, 2 * D_FF), F32)
    y_p, conv_p = _tail(xp, [oa_even, oa_odd], ob, mk_p[None], mv_p[None], conv0, tail_params,
                        TAIL_TM, TAIL_TM, TAIL_SUB)

    rows_s = streams * t_new
    xs = x_sample.reshape(rows_s, D_MODEL)
    pos_s = jnp.tile(past_b + jnp.arange(t_new), streams)
    qa_s, ka_s, va_s, qb_s, kb_s, vb_s, ka_sf, va_sf, kb_sf, vb_sf = _qkv(xs, w_qkv_bf, pos_s, rows_s, False)
    oa_s = _attn_a_sample(qa_s, ka_s, va_s,
                          cache_a_k[0].reshape(streams, -1, A_WIDTH), cache_a_v[0].reshape(streams, -1, A_WIDTH),
                          bias, t_new)
    ob_s = _attn_b_sample(qb_s, kb_s, vb_s,
                          cache_b_k[0].reshape(streams, past_b, B_WIDTH),
                          cache_b_v[0], lamp, g, t_new)
    conv_in_s = jnp.pad(state_conv[0].astype(F32), ((0, 0), (SUBLANES - (CONV_W - 1), 0), (0, 0)))
    y_s, conv_s = _tail(xs, [oa_s], ob_s, mem_k_s, mem_v_s,
                        conv_in_s.reshape(streams * SUBLANES, 2 * D_FF), tail_params, rows_s, t_new, rows_s)

    keep = CONV_W - 1
    return (
        y_p.reshape(batch, seq, D_MODEL),
        y_s.reshape(streams, t_new, D_MODEL),
        ka_f.reshape(1, batch, QKV_TM, H_A, HD_A),
        va_f.reshape(1, batch, QKV_TM, H_A, HD_A),
        kb_f.reshape(1, batch, seq, H_B, 2, HD_B),
        vb_f.reshape(1, batch, seq, H_B, 2 * HD_B),
        mk_p.reshape(1, batch, N_MEM, H_M, HD_M),
        mv_p.reshape(1, batch, N_MEM, H_M, HD_M),
        conv_p[SUBLANES - keep:].reshape(1, batch, keep, 2 * D_FF),
        ka_sf.reshape(1, streams, t_new, H_A, HD_A),
        va_sf.reshape(1, streams, t_new, H_A, HD_A),
        kb_sf.reshape(1, streams, t_new, H_B, 2, HD_B),
        vb_sf.reshape(1, streams, t_new, H_B, 2 * HD_B),
        conv_s.reshape(streams, SUBLANES, 2 * D_FF)[:, SUBLANES - keep:].reshape(1, streams, keep, 2 * D_FF),
    )
```

```python
import functools
import math

import jax
import jax.numpy as jnp
from jax import lax
from jax.experimental import pallas as pl
from jax.experimental.pallas import tpu as pltpu

F32 = jnp.float32
BF16 = jnp.bfloat16

D_MODEL = 1024
CHUNK = 64
BAND_PREV = 8
REL_CLIP = 128
H_A = 8
HD_A = 64
H_B = 4
HD_B = 64
ROT_DIM = HD_B // 4
ROPE_THETA = 500000.0
N_MEM = 256
H_M = 4
HD_M = D_MODEL // H_M
D_FF = 2816
CONV_W = 3
LN_EPS = 1e-5
DEPTH = 1
DEEPNORM_ALPHA = (2.0 * DEPTH) ** 0.25
A_WIDTH = H_A * HD_A
B_WIDTH = H_B * 2 * HD_B
LAM_INIT = 0.8 - 0.6 * math.exp(-0.3 * 0)

LANES = 128
SUBLANES = 8
BF16_SUBLANES = 16
NEG = -1e30
LOG2_E = math.log2(math.e)

QKV_TM = 512
A_TQ = 256
B_TQ = 1024
B_QW = 256
B_HEADS = 2
B_TK = 512
ONES_ROWS = 16
V_ROWS = LANES + ONES_ROWS
VA_ROWS = HD_A + ONES_ROWS
BIAS_SPAN = 1024
TAIL_TM = 512
TAIL_SUB = 256
MXU_DIM = 256
FF_BLOCKS = (4 * MXU_DIM, 4 * MXU_DIM, D_FF - 8 * MXU_DIM)
assert sum(FF_BLOCKS) == D_FF and all(w % MXU_DIM == 0 for w in FF_BLOCKS)

_NT = (((1,), (1,)), ((), ()))


def _dot(a, b):
    return jnp.dot(a, b, preferred_element_type=F32)


def _dot_nt(a, b):
    return lax.dot_general(a, b, _NT, preferred_element_type=F32)


def _const_spec(shape):
    nd = len(shape)
    return pl.BlockSpec(shape, lambda *_: (0,) * nd, pipeline_mode=pl.Buffered(1))


def _arbitrary(n):
    return pltpu.CompilerParams(dimension_semantics=("arbitrary",) * n)


def _rope(z, cos, sin):
    lane = lax.broadcasted_iota(jnp.int32, (z.shape[0], LANES), 1)
    first_half = (lane % HD_B) < (ROT_DIM // 2)
    outs = []
    for c in range(z.shape[1] // LANES):
        zc = z[:, c * LANES:(c + 1) * LANES]
        partner = jnp.where(first_half,
                            pltpu.roll(zc, LANES - ROT_DIM // 2, axis=1),
                            pltpu.roll(zc, ROT_DIM // 2, axis=1))
        outs.append(zc * cos + partner * sin)
    return jnp.concatenate(outs, axis=1)


def _cast_plans(arrays, total_steps):
    plans = []
    for w in arrays:
        if w.ndim == 2:
            n_steps = total_steps
            while w.shape[0] % (n_steps * BF16_SUBLANES):
                n_steps //= 2
            spec = pl.BlockSpec((w.shape[0] // n_steps, w.shape[1]), lambda i, n=n_steps: (jnp.minimum(i, n - 1), 0))
            plans.append((n_steps, spec, spec, w.shape))
        else:
            sets, n_rows, heads, width = w.shape
            per_set = total_steps // sets
            rows = n_rows // per_set
            assert per_set * sets == total_steps and rows * per_set == n_rows and rows % BF16_SUBLANES == 0
            plans.append((total_steps,
                          pl.BlockSpec((1, rows, heads, width), lambda i, p=per_set: (i // p, i % p, 0, 0)),
                          pl.BlockSpec((1, rows, heads * width), lambda i, p=per_set: (i // p, i % p, 0)),
                          (sets, n_rows, heads * width)))
    return plans


def _run_casts(step_id, total_steps, cast_steps, w_refs, wbf_refs):
    for w_ref, wbf_ref, n_steps in zip(w_refs, wbf_refs, cast_steps):
        def cast(w_ref=w_ref, wbf_ref=wbf_ref):
            if len(w_ref.shape) == 2:
                wbf_ref[...] = w_ref[...].astype(BF16)
            else:
                width = w_ref.shape[3]
                for h in range(w_ref.shape[2]):
                    wbf_ref[0, :, h * width:(h + 1) * width] = w_ref[0, :, h, :].astype(BF16)
        if n_steps == total_steps:
            cast()
        else:
            pl.when(step_id < n_steps)(cast)


def _qkv_kernel(*refs, transposed, cast_steps, total_steps):
    n_cast = len(cast_steps)
    x_ref, w_ref, rope_ref = refs[:3]
    (qa_ref, ka_ref, va_ref, qb_ref, kb_ref, vb_ref,
     kaf_ref, vaf_ref, kbf_ref, vbf_ref) = refs[3 + n_cast:13 + n_cast]
    _run_casts(pl.program_id(0), total_steps, cast_steps, refs[3:3 + n_cast], refs[13 + n_cast:])
    xb = x_ref[...].astype(BF16)
    cos, sin = _spread_rope(rope_ref[...])
    tm = xb.shape[0]

    def section(i):
        return _dot(xb, w_ref[:, i * A_WIDTH:(i + 1) * A_WIDTH])

    vb = section(5)
    for h in range(H_B):
        vbf_ref[:, h, :] = vb[:, h * LANES:(h + 1) * LANES]
    def ones_rows(width):
        return (lax.broadcasted_iota(jnp.int32, (ONES_ROWS, width), 0) == 0).astype(BF16)

    if transposed:
        for h in range(H_B):
            vb_ref[h, 0, 0:LANES, :] = vb[:, h * LANES:(h + 1) * LANES].T.astype(BF16)
            vb_ref[h, 0, LANES:V_ROWS, :] = ones_rows(tm)
    else:
        vb_ref[...] = vb.astype(BF16)
    qb = _rope(section(3), cos, sin) * (HD_B ** -0.5 * (LOG2_E if transposed else 1.0))
    if transposed:
        for h in range(H_B):
            qb_ref[h] = qb[:, h * LANES:(h + 1) * LANES].T.astype(BF16)
    else:
        qb_ref[...] = qb.astype(BF16)
    kb = _rope(section(4), cos, sin)
    kb_ref[...] = kb.astype(BF16)
    kbf_ref[...] = kb
    va = section(2)
    vaf_ref[...] = va
    if transposed:
        for p in range(H_A // 2):
            vt = va[:, p * LANES:(p + 1) * LANES].T.astype(BF16)
            for hh in range(2):
                for t in range(tm // A_TQ):
                    cols = slice(t * A_TQ, (t + 1) * A_TQ)
                    va_ref[2 * p + hh, t, 0:HD_A, :] = vt[hh * HD_A:(hh + 1) * HD_A, cols]
                    va_ref[2 * p + hh, t, HD_A:VA_ROWS, :] = ones_rows(A_TQ)
    else:
        va_ref[...] = va.astype(BF16)
    qa = section(0) * (HD_A ** -0.5 * LOG2_E)
    if transposed:
        for p in range(H_A // 2):
            qa_ref[p] = qa[:, p * LANES:(p + 1) * LANES].T.astype(BF16)
    else:
        qa_ref[...] = qa.astype(BF16)
    ka = section(1)
    ka_ref[...] = ka.astype(BF16)
    kaf_ref[...] = ka


def _rope_table(pos):
    inv = ROPE_THETA ** (-jnp.arange(0, ROT_DIM, 2, dtype=F32) / ROT_DIM)
    ang = pos.astype(F32)[:, None] * inv[None, :]
    cs = lax.optimization_barrier(jnp.concatenate([jnp.cos(ang), jnp.sin(ang)], axis=1))
    return jnp.pad(cs, ((0, 0), (0, LANES - ROT_DIM)))


def _spread_rope(tab):
    half = ROT_DIM // 2
    lane = lax.broadcasted_iota(jnp.int32, tab.shape, 1)
    c = jnp.where(lane < half, tab, 0.0)
    s = jnp.where((lane >= half) & (lane < ROT_DIM), tab, 0.0)
    c = c + pltpu.roll(c, half, axis=1)
    s = s - pltpu.roll(s, LANES - half, axis=1)
    c = c + pltpu.roll(c, HD_B, axis=1)
    s = s + pltpu.roll(s, HD_B, axis=1)
    return jnp.where(lane % HD_B < ROT_DIM, c, 1.0), s


def _qkv(x, w_bf, pos, tm, transposed, to_cast=()):
    rows = x.shape[0]
    nt = rows // tm
    rope = _rope_table(pos)
    plans = _cast_plans(to_cast, nt)
    row_spec = lambda w: pl.BlockSpec((tm, w), lambda i: (i, 0))
    last_spec = pl.BlockSpec((tm, A_WIDTH), lambda i: (0, 0))
    bf = jax.ShapeDtypeStruct((rows, A_WIDTH), BF16)
    if transposed:
        assert tm == B_TK and tm % A_TQ == 0
        qa_spec = pl.BlockSpec((H_A // 2, LANES, tm), lambda i: (0, 0, i))
        qa_shape = jax.ShapeDtypeStruct((H_A // 2, LANES, rows), BF16)
        va_spec = pl.BlockSpec((H_A, tm // A_TQ, VA_ROWS, A_TQ), lambda i: (0, i, 0, 0))
        va_shape = jax.ShapeDtypeStruct((H_A, rows // A_TQ, VA_ROWS, A_TQ), BF16)
        qb_spec = pl.BlockSpec((H_B, LANES, tm), lambda i: (0, 0, i))
        qb_shape = jax.ShapeDtypeStruct((H_B, LANES, rows), BF16)
        vb_spec = pl.BlockSpec((H_B, 1, V_ROWS, tm), lambda i: (0, i, 0, 0))
        vb_shape = jax.ShapeDtypeStruct((H_B, nt, V_ROWS, tm), BF16)
    else:
        qa_spec = va_spec = qb_spec = vb_spec = row_spec(B_WIDTH)
        qa_shape = va_shape = qb_shape = vb_shape = bf
    return pl.pallas_call(
        functools.partial(_qkv_kernel, transposed=transposed, cast_steps=tuple(p[0] for p in plans),
                          total_steps=nt),
        grid=(nt,),
        in_specs=[row_spec(D_MODEL), _const_spec(w_bf.shape), row_spec(LANES)] + [p[1] for p in plans],
        out_specs=[qa_spec, row_spec(A_WIDTH), va_spec, qb_spec, row_spec(B_WIDTH), vb_spec]
                  + [last_spec, last_spec, row_spec(B_WIDTH),
                     pl.BlockSpec((tm, H_B, 2 * HD_B), lambda i: (i, 0, 0))] + [p[2] for p in plans],
        out_shape=[qa_shape, bf, va_shape, qb_shape, bf, vb_shape]
                  + [jax.ShapeDtypeStruct((tm, A_WIDTH), F32)] * 2
                  + [jax.ShapeDtypeStruct((rows, B_WIDTH), F32),
                     jax.ShapeDtypeStruct((rows, H_B, 2 * HD_B), F32)]
                  + [jax.ShapeDtypeStruct(p[3], BF16) for p in plans],
        compiler_params=_arbitrary(1),
        name="qkv_proj",
    )(x, w_bf, rope, *to_cast)


def _bias_row(rel_bias):
    table = rel_bias.astype(F32)
    far = table[:, 2 * REL_CLIP:]
    near = table[:, :1]
    n_far = 2 * A_TQ - REL_CLIP
    n_near = 3 * A_TQ - (n_far + 2 * REL_CLIP + 1)
    h = table.shape[0]
    return jnp.concatenate([jnp.broadcast_to(far, (h, n_far)), table[:, ::-1],
                            jnp.broadcast_to(near, (h, n_near)),
                            jnp.broadcast_to(far, (h, BIAS_SPAN - 3 * A_TQ))], axis=1)


def _build_band_bias(row_ref, bias_ref):
    _, rows, cols = bias_ref.shape
    q = lax.broadcasted_iota(jnp.int32, (rows, cols), 0)
    k = lax.broadcasted_iota(jnp.int32, (rows, cols), 1)
    qc = q // CHUNK
    kc = k // CHUNK - (2 * A_TQ) // CHUNK
    band = (kc >= qc - BAND_PREV) & (kc <= qc)
    for h in range(H_A):
        r = jnp.broadcast_to(row_ref[h:h + 1, :], (rows, BIAS_SPAN))
        toeplitz = pltpu.roll(r, 0, axis=1, stride=1, stride_axis=0)
        bias_ref[h] = jnp.where(band, toeplitz[:, :cols] * LOG2_E, NEG)


def _pair_queries(q2):
    lane = lax.broadcasted_iota(jnp.int32, q2.shape, 1)
    lo = lane < (LANES // 2)
    zero = jnp.zeros_like(q2)
    return jnp.concatenate([jnp.where(lo, q2, zero), jnp.where(lo, zero, q2)], axis=0), lo


def _build_band_bias_t(row_ref, bias_ref):
    _, n_k, n_q = bias_ref.shape
    k = lax.broadcasted_iota(jnp.int32, (n_k, n_q), 0)
    q = lax.broadcasted_iota(jnp.int32, (n_k, n_q), 1)
    qc = q // CHUNK
    kc = k // CHUNK - (2 * A_TQ) // CHUNK
    band = (kc >= qc - BAND_PREV) & (kc <= qc)
    for h in range(H_A):
        r = jnp.broadcast_to(row_ref[h:h + 1, :], (n_k, BIAS_SPAN))
        toeplitz = pltpu.roll(r, 0, axis=1, stride=1, stride_axis=0)
        bias_ref[h] = jnp.where(band, toeplitz[:, :n_q] * LOG2_E, NEG)


def _attn_a_kernel(qt_ref, ka_ref, kb_ref, kc_ref, kd_ref, va_ref, vb_ref, vc_ref, vd_ref, row_ref,
                   o_even_ref, o_odd_ref, bias_ref, p_ref, *, n_steps):
    g = pl.program_id(0)
    k_refs = (ka_ref, kb_ref, kc_ref, kd_ref)
    v_refs = (va_ref, vb_ref, vc_ref, vd_ref)

    @pl.when(g == 0)
    def _():
        _build_band_bias_t(row_ref, bias_ref)
        p_ref[1] = jnp.zeros(p_ref.shape[1:], BF16)

    def body(check_keys):
        kidx = lax.broadcasted_iota(jnp.int32, (3 * A_TQ, A_TQ), 0)
        dim = lax.broadcasted_iota(jnp.int32, (LANES, A_TQ), 0)
        zero = jnp.zeros((LANES, A_TQ), BF16)

        def scores(t, h):
            pair = h // 2
            sl = slice(pair * LANES, (pair + 1) * LANES)
            qz = jnp.where(dim < HD_A if h % 2 == 0 else dim >= HD_A,
                           qt_ref[pair, :, t * A_TQ:(t + 1) * A_TQ], zero)
            kc = jnp.concatenate([k_refs[t][:, sl], k_refs[t + 1][:, sl], k_refs[t + 2][:, sl]], axis=0)
            s = _dot(kc, qz) + bias_ref[h]
            if check_keys:
                s = jnp.where(kidx >= (2 - t) * A_TQ, s, NEG)
            p_ref[t, h] = jnp.exp2(s - jnp.max(s, axis=0, keepdims=True)).astype(BF16)

        def values(t, o_ref, keep):
            o_t = [None] * H_A

            def head(h):
                e = p_ref[1 - t, h]
                r = (_dot(v_refs[t][h, 0], e[0:A_TQ]) + _dot(v_refs[t + 1][h, 0], e[A_TQ:2 * A_TQ])
                     + _dot(v_refs[t + 2][h, 0], e[2 * A_TQ:3 * A_TQ]))
                o_t[h] = r[0:HD_A] / r[HD_A:HD_A + 1]
                if h % 2 == 1:
                    sl = slice((h // 2) * LANES, (h // 2 + 1) * LANES)
                    o = jnp.concatenate([o_t[h - 1], o_t[h]], axis=0).T.astype(BF16)
                    o_ref[:, sl] = o if keep is None else jnp.where(keep, o, o_ref[:, sl])
            return head

        odd_head = values(0, o_odd_ref, None)
        even_head = values(1, o_even_ref, g < n_steps)
        for h in range(H_A):
            scores(0, h)
            odd_head(h)
        for h in range(H_A):
            scores(1, h)
            even_head(h)

    pl.when(g == 0)(lambda: body(True))
    pl.when(g > 0)(lambda: body(False))


def _attn_a_prompt(qat, ka, vat, bias_row_rev):
    rows = ka.shape[0]
    n_tiles = rows // A_TQ
    n_steps = n_tiles // 2
    assert n_tiles % 2 == 0 and BAND_PREV * CHUNK == 2 * A_TQ
    tile = lambda g, d: jnp.clip(2 * g + d, 0, n_tiles - 1)
    kspec = lambda d: pl.BlockSpec((A_TQ, A_WIDTH), lambda g: (tile(g, d), 0))
    vspec = lambda d: pl.BlockSpec((H_A, 1, VA_ROWS, A_TQ), lambda g: (0, tile(g, d), 0, 0))
    half = jax.ShapeDtypeStruct((rows // 2, A_WIDTH), BF16)
    return pl.pallas_call(
        functools.partial(_attn_a_kernel, n_steps=n_steps),
        grid=(n_steps + 1,),
        in_specs=[pl.BlockSpec((H_A // 2, LANES, 2 * A_TQ), lambda g: (0, 0, jnp.minimum(g, n_steps - 1)))]
                 + [kspec(d) for d in (-2, -1, 0, 1)] + [vspec(d) for d in (-3, -2, -1, 0)]
                 + [_const_spec(bias_row_rev.shape)],
        out_specs=[pl.BlockSpec((A_TQ, A_WIDTH), lambda g: (jnp.minimum(g, n_steps - 1), 0)),
                   pl.BlockSpec((A_TQ, A_WIDTH), lambda g: (jnp.clip(g - 1, 0, n_steps - 1), 0))],
        out_shape=[half, half],
        scratch_shapes=[pltpu.VMEM((H_A, 3 * A_TQ, A_TQ), F32), pltpu.VMEM((2, H_A, 3 * A_TQ, A_TQ), BF16)],
        compiler_params=_arbitrary(1),
        name="attn_a_prompt",
    )(qat, *([ka] * 4), *([vat] * 4), bias_row_rev)


def _attn_a_sample_kernel(q_ref, kn_ref, vn_ref, kc_ref, vc_ref, row_ref, o_ref, bias_ref):
    t = q_ref.shape[0]
    past = kc_ref.shape[1]

    @pl.when(pl.program_id(0) == 0)
    def _():
        _build_band_bias(row_ref, bias_ref)

    kc = kc_ref[0].astype(BF16)
    vc = vc_ref[0].astype(BF16)
    for p in range(H_A // 2):
        sl = slice(p * LANES, (p + 1) * LANES)
        qz, lo = _pair_queries(q_ref[:, sl])
        b_past = jnp.concatenate([bias_ref[2 * p, 0:t, 0:past], bias_ref[2 * p + 1, 0:t, 0:past]], axis=0)
        b_new = jnp.concatenate([bias_ref[2 * p, 0:t, past:past + t],
                                 bias_ref[2 * p + 1, 0:t, past:past + t]], axis=0)
        s_past = _dot_nt(qz, kc[:, sl]) + b_past
        s_new = _dot_nt(qz, kn_ref[:, sl]) + b_new
        m = jnp.maximum(jnp.max(s_past, axis=-1, keepdims=True), jnp.max(s_new, axis=-1, keepdims=True))
        e_past = jnp.exp2(s_past - m)
        e_new = jnp.exp2(s_new - m)
        l = jnp.sum(e_past, axis=-1, keepdims=True) + jnp.sum(e_new, axis=-1, keepdims=True)
        o = (_dot(e_past.astype(BF16), vc[:, sl]) + _dot(e_new.astype(BF16), vn_ref[:, sl])) / l
        o_ref[:, sl] = jnp.where(lo, o[:t], o[t:]).astype(BF16)


def _attn_a_sample(qa, ka, va, cache_k, cache_v, bias_row, t):
    rows = qa.shape[0]
    streams, past, width = cache_k.shape
    assert past == 2 * A_TQ and t <= CHUNK and t % SUBLANES == 0
    row = pl.BlockSpec((t, A_WIDTH), lambda s: (s, 0))
    cache = pl.BlockSpec((1, past, width), lambda s: (s, 0, 0))
    return pl.pallas_call(
        _attn_a_sample_kernel,
        grid=(streams,),
        in_specs=[row, row, row, cache, cache, _const_spec(bias_row.shape)],
        out_specs=row,
        out_shape=jax.ShapeDtypeStruct((rows, A_WIDTH), BF16),
        scratch_shapes=[pltpu.VMEM((H_A, t, past + LANES), F32)],
        compiler_params=_arbitrary(1),
        name="attn_a_sample",
    )(qa, ka, va, cache_k, cache_v, bias_row)


def _lambda(lamp_ref):
    lp = lamp_ref[...]
    a = jnp.sum(lp[0:1] * lp[1:2], axis=-1, keepdims=True)
    b = jnp.sum(lp[2:3] * lp[3:4], axis=-1, keepdims=True)
    return jnp.exp(a) - jnp.exp(b) + LAM_INIT


def _diff_finish(o0, o1, lam, g):
    o = o0 - lam * o1
    o = o * lax.rsqrt(jnp.mean(o * o, axis=-1, keepdims=True) + LN_EPS) * g * (1.0 - LAM_INIT)
    return o.astype(BF16)


def _attn_b_kernel(qt_ref, k_ref, vt_ref, lamp_ref, gcol_ref, o_ref, acc_ref, p_ref, alpha_ref):
    qi = pl.program_id(1)
    n_parts = B_TQ // B_QW
    dim = lax.broadcasted_iota(jnp.int32, (LANES, B_QW), 0)
    zero = jnp.zeros((LANES, B_QW), BF16)
    chains = [(part, (h, mp)) for part in range(n_parts) for h in range(B_HEADS) for mp in range(2)]
    qz = []
    for part, (h, mp) in chains:
        qt = qt_ref[h, :, part * B_QW:(part + 1) * B_QW]
        qz.append(jnp.where(dim < HD_B if mp == 0 else dim >= HD_B, qt, zero))
    acc_ref[...] = jnp.zeros(acc_ref.shape, F32)
    p_ref[1] = jnp.zeros(p_ref.shape[1:], BF16)
    alpha_ref[1] = jnp.ones(alpha_ref.shape[1:], F32)

    def pending_values(c, j_prev, buf):
        h = chains[c][1][0]
        acc_ref[c] = alpha_ref[buf, c] * acc_ref[c] + _dot(vt_ref[h, j_prev], p_ref[buf, c])

    def step(j, ms, buf, parts, diag_parts, pending_parts):
        start = pl.multiple_of(j * B_TK, B_TK)
        j_prev = jnp.maximum(j - 1, 0)
        kpos = start + lax.broadcasted_iota(jnp.int32, (B_TK, B_QW), 0)
        lane = lax.broadcasted_iota(jnp.int32, (B_TK, B_QW), 1)
        out = list(ms)
        for c, (part, (h, _)) in enumerate(chains):
            if part in parts:
                s = _dot(k_ref[pl.ds(start, B_TK), h * LANES:(h + 1) * LANES], qz[c])
                if part in diag_parts:
                    qpos = qi * B_TQ + part * B_QW + lane
                    s = jnp.where(kpos < (qpos // CHUNK + 1) * CHUNK, s, NEG)
                out[c] = jnp.maximum(ms[c], jnp.max(s, axis=0, keepdims=True))
                alpha_ref[buf, c] = jnp.exp2(ms[c] - out[c])
                p_ref[buf, c] = jnp.exp2(s - out[c]).astype(BF16)
            if part in pending_parts:
                pending_values(c, j_prev, 1 - buf)
        return tuple(out)

    every = tuple(range(n_parts))
    low, high = every[:n_parts // 2], every[n_parts // 2:]

    def pair(i, ms):
        ms = step(2 * i, ms, 0, every, (), every)
        return step(2 * i + 1, ms, 1, every, (), every)

    m_init = tuple(jnp.full((1, B_QW), NEG, F32) for _ in chains)
    ms = lax.fori_loop(0, qi, pair, m_init)
    ms = step(2 * qi, ms, 0, every, low, every)
    step(2 * qi + 1, ms, 1, high, high, every)
    for c, (part, _) in enumerate(chains):
        if part in high:
            pending_values(c, 2 * qi + 1, 1)

    lam = _lambda(lamp_ref)
    for c in range(0, len(chains), 2):
        part, (h, _) = chains[c]
        a0 = acc_ref[c]
        a1 = acc_ref[c + 1]
        o = a0[0:LANES] / a0[LANES:LANES + 1] - lam * (a1[0:LANES] / a1[LANES:LANES + 1])
        o = o * lax.rsqrt(jnp.mean(o * o, axis=0, keepdims=True) + LN_EPS) * gcol_ref[...] * (1.0 - LAM_INIT)
        o_ref[part * B_QW:(part + 1) * B_QW, h * LANES:(h + 1) * LANES] = o.T.astype(BF16)


def _attn_b_prompt(qbt, kb, vbt, lamp, gcol):
    rows = kb.shape[0]
    assert rows % B_TQ == 0 and B_TQ == 2 * B_TK and B_TK == 2 * B_QW and B_QW % CHUNK == 0
    assert H_B % B_HEADS == 0
    n_chains = (B_TQ // B_QW) * B_HEADS * 2
    single = pl.Buffered(1)
    return pl.pallas_call(
        _attn_b_kernel,
        grid=(H_B // B_HEADS, rows // B_TQ),
        in_specs=[pl.BlockSpec((B_HEADS, LANES, B_TQ), lambda h, i: (h, 0, i)),
                  pl.BlockSpec((rows, B_HEADS * LANES), lambda h, i: (0, h), pipeline_mode=single),
                  pl.BlockSpec((B_HEADS,) + vbt.shape[1:], lambda h, i: (h, 0, 0, 0), pipeline_mode=single),
                  _const_spec(lamp.shape), _const_spec(gcol.shape)],
        out_specs=pl.BlockSpec((B_TQ, B_HEADS * LANES), lambda h, i: (i, h)),
        out_shape=jax.ShapeDtypeStruct((rows, B_WIDTH), BF16),
        scratch_shapes=[pltpu.VMEM((n_chains, V_ROWS, B_QW), F32),
                        pltpu.VMEM((2, n_chains, B_TK, B_QW), BF16), pltpu.VMEM((2, n_chains, 1, B_QW), F32)],
        compiler_params=_arbitrary(2),
        name="attn_b_prompt",
    )(qbt, kb, vbt, lamp, gcol)


def _attn_b_sample_kernel(q_ref, kn_ref, vn_ref, kc_ref, vc_ref, lamp_ref, g_ref, o_ref):
    t = q_ref.shape[0]
    lam = _lambda(lamp_ref)
    g = g_ref[...]
    for h in range(H_B):
        sl = slice(h * LANES, (h + 1) * LANES)
        qz, _ = _pair_queries(q_ref[:, sl])
        s_past = _dot_nt(qz, kc_ref[0, :, sl].astype(BF16))
        s_new = _dot_nt(qz, kn_ref[:, sl])
        m = jnp.maximum(jnp.max(s_past, axis=-1, keepdims=True), jnp.max(s_new, axis=-1, keepdims=True))
        e_past = jnp.exp(s_past - m)
        e_new = jnp.exp(s_new - m)
        l = jnp.sum(e_past, axis=-1, keepdims=True) + jnp.sum(e_new, axis=-1, keepdims=True)
        o = (_dot(e_past.astype(BF16), vc_ref[0, :, h, :].astype(BF16))
             + _dot(e_new.astype(BF16), vn_ref[:, sl])) / l
        o_ref[:, sl] = _diff_finish(o[:t], o[t:], lam, g)


def _attn_b_sample(qb, kb, vb, cache_k, cache_v, lamp, g, t):
    rows = qb.shape[0]
    streams, past, width = cache_k.shape
    assert past % CHUNK == 0 and t <= CHUNK
    row = pl.BlockSpec((t, B_WIDTH), lambda s: (s, 0))
    cache = pl.BlockSpec((1, past, width), lambda s: (s, 0, 0))
    cache_v_spec = pl.BlockSpec((1,) + cache_v.shape[1:], lambda s: (s, 0, 0, 0))
    return pl.pallas_call(
        _attn_b_sample_kernel,
        grid=(streams,),
        in_specs=[row, row, row, cache, cache_v_spec, _const_spec(lamp.shape), _const_spec(g.shape)],
        out_specs=row,
        out_shape=jax.ShapeDtypeStruct((rows, B_WIDTH), BF16),
        compiler_params=_arbitrary(1),
        name="attn_b_sample",
    )(qb, kb, vb, cache_k, cache_v, lamp, g)


def _mem_kv_kernel(x_ref, wk_ref, wv_ref, kf_ref, vf_ref):
    xb = x_ref[...].astype(BF16)
    kf_ref[...] = _dot(xb, wk_ref[...])
    vf_ref[...] = _dot(xb, wv_ref[...])


def _mem_kv(mem, wk_bf, wv_bf):
    f = jax.ShapeDtypeStruct(mem.shape, F32)
    return pl.pallas_call(
        _mem_kv_kernel,
        out_shape=[f, f],
        name="mem_kv",
    )(mem, wk_bf, wv_bf)


def _layer_norm(x, g, b):
    mu = jnp.mean(x, axis=-1, keepdims=True)
    xc = x - mu
    var = jnp.mean(xc * xc, axis=-1, keepdims=True)
    return xc * lax.rsqrt(var + LN_EPS) * g + b


def _tail_kernel(*refs, seg, sub):
    x_ref = refs[0]
    tm = x_ref.shape[0]
    n_seg = tm // seg
    n_sub = tm // sub
    assert n_seg == 1 or n_sub == 1
    subs = [slice(i * sub, (i + 1) * sub) for i in range(n_sub)]
    oa_refs = refs[1:1 + n_sub]
    (ob_ref, mk_ref, mv_ref, cin_ref, wo_ref, ln1g_ref, ln1b_ref, wmq_ref, wmo_ref, ln2g_ref, ln2b_ref,
     wup_ref, cw_ref, cb_ref, wdn_ref, ln3g_ref, ln3b_ref, y_ref, carry_ref) = refs[1 + n_sub:]

    @pl.when(pl.program_id(0) == 0)
    def _():
        carry_ref[...] = cin_ref[...]

    def mem_attend(q, mk, mv):
        heads = []
        for h in range(H_M):
            sl = slice(h * HD_M, (h + 1) * HD_M)
            s = _dot_nt(q[:, sl], mk[:, sl])
            e = jnp.exp(s - jnp.max(s, axis=-1, keepdims=True))
            l = jnp.sum(e, axis=-1, keepdims=True)
            heads.append((_dot(e.astype(BF16), mv[:, sl]) / l).astype(BF16))
        return jnp.concatenate(heads, axis=1)

    def mem_attend_rows(q):
        if mk_ref.shape[0] == 1:
            return mem_attend(q, mk_ref[0].astype(BF16), mv_ref[0].astype(BF16))
        return jnp.concatenate([mem_attend(q[sg * seg:(sg + 1) * seg],
                                           mk_ref[sg].astype(BF16), mv_ref[sg].astype(BF16))
                                for sg in range(n_seg)], axis=0)

    mix = [_dot(oa_ref[...], wo_ref[0:A_WIDTH, :]) + _dot(ob_ref[sl, :], wo_ref[A_WIDTH:A_WIDTH + B_WIDTH, :])
           for oa_ref, sl in zip(oa_refs, subs)]
    x1 = [_layer_norm(DEEPNORM_ALPHA * x_ref[sl, :] + m, ln1g_ref[...], ln1b_ref[...]) for sl, m in zip(subs, mix)]
    qm = [(_dot(v.astype(BF16), wmq_ref[...]) * (HD_M ** -0.5)).astype(BF16) for v in x1]
    om = [mem_attend_rows(q) for q in qm]
    mo = [_dot(v, wmo_ref[...]) for v in om]
    x2 = [_layer_norm(DEEPNORM_ALPHA * a + b, ln2g_ref[...], ln2b_ref[...]) for a, b in zip(x1, mo)]
    x2b = [v.astype(BF16) for v in x2]

    def conv(us, col, width):
        cols = slice(col, col + width)
        w0, w1, w2, b = cw_ref[0:1, cols], cw_ref[1:2, cols], cw_ref[2:3, cols], cb_ref[0:1, cols]

        def taps(u, p2, p1):
            row = lax.broadcasted_iota(jnp.int32, u.shape, 0)
            u1 = jnp.where(row == 0, p1, pltpu.roll(u, 1, axis=0))
            u2 = jnp.where(row == 0, p2, jnp.where(row == 1, p1, pltpu.roll(u, 2, axis=0)))
            return u2 * w0 + u1 * w1 + u * w2 + b

        def state(sg):
            base = SUBLANES * sg
            return carry_ref[base + 6:base + 7, cols], carry_ref[base + 7:base + 8, cols]

        if n_seg == 1:
            p2, p1 = state(0)
            outs = []
            for u in us:
                outs.append(taps(u, p2, p1))
                p2, p1 = u[sub - 2:sub - 1], u[sub - 1:sub]
            carry_ref[0:SUBLANES, cols] = us[-1][sub - SUBLANES:sub]
            return outs
        (u,) = us
        outs = []
        for sg in range(n_seg):
            useg = u[sg * seg:(sg + 1) * seg]
            outs.append(taps(useg, *state(sg)))
            carry_ref[SUBLANES * sg:SUBLANES * (sg + 1), cols] = useg[seg - SUBLANES:seg]
        return [jnp.concatenate(outs, axis=0)]

    f = [jnp.zeros((sub, D_MODEL), F32) for _ in subs]
    c0 = 0
    for width in FF_BLOCKS:
        gate = conv([_dot(v, wup_ref[:, c0:c0 + width]) for v in x2b], c0, width)
        val = conv([_dot(v, wup_ref[:, D_FF + c0:D_FF + c0 + width]) for v in x2b], D_FF + c0, width)
        hid = [(g * (1.0 / (1.0 + jnp.exp(-g))) * v).astype(BF16) for g, v in zip(gate, val)]
        f = [a + _dot(h, wdn_ref[c0:c0 + width, :]) for a, h in zip(f, hid)]
        c0 += width
    for sl, a, b in zip(subs, x2, f):
        y_ref[sl, :] = _layer_norm(DEEPNORM_ALPHA * a + b, ln3g_ref[...], ln3b_ref[...])


def _tail(x, oas, ob, mk, mv, conv_in, params, tm, seg, sub):
    rows = x.shape[0]
    n_seg = tm // seg
    assert tm % sub == 0 and sub % SUBLANES == 0 and len(oas) == tm // sub
    assert conv_in.shape == (SUBLANES * n_seg, 2 * D_FF)
    assert n_seg == 1 or rows == tm
    assert mk.shape[0] in (1, n_seg)
    row_spec = lambda w: pl.BlockSpec((tm, w), lambda i: (i, 0))
    carry_spec = pl.BlockSpec(conv_in.shape, lambda i: (0, 0))
    return pl.pallas_call(
        functools.partial(_tail_kernel, seg=seg, sub=sub),
        grid=(rows // tm,),
        in_specs=[row_spec(D_MODEL)] + [pl.BlockSpec((sub, A_WIDTH), lambda i: (i, 0)) for _ in oas]
                 + [row_spec(B_WIDTH), _const_spec(mk.shape), _const_spec(mv.shape),
                  _const_spec(conv_in.shape)] + [_const_spec(p.shape) for p in params],
        out_specs=[row_spec(D_MODEL), carry_spec],
        out_shape=[jax.ShapeDtypeStruct((rows, D_MODEL), F32),
                   jax.ShapeDtypeStruct(conv_in.shape, F32)],
        compiler_params=_arbitrary(1),
        name="tail",
    )(x, *oas, ob, mk, mv, conv_in, *params)


def kernel(x_prompt, x_sample, mem_prompt, cache_a_k, cache_a_v, cache_b_k, cache_b_v, cache_mem_k, cache_mem_v,
           state_conv, w_qkv, rel_bias, lambda_q1, lambda_k1, lambda_q2, lambda_k2, subln_g, w_o, ln1_g, ln1_b,
           w_mq, w_mk, w_mv, w_mo, ln2_g, ln2_b, w_up, conv_w, conv_b, w_down, ln3_g, ln3_b):
    assert w_qkv.shape[0] == DEPTH == 1
    batch, seq, _ = x_prompt.shape
    streams, t_new, _ = x_sample.shape
    past_b = cache_b_k.shape[2]
    assert batch == 1 and seq % QKV_TM == 0 and QKV_TM == BAND_PREV * CHUNK

    row = lambda v: v[0].astype(F32)[None, :]
    w_qkv_bf = w_qkv[0].astype(BF16)
    lamp = jnp.stack([lambda_q1[0], lambda_k1[0], lambda_q2[0], lambda_k2[0]]).astype(F32)
    g = row(subln_g)
    gcol = jnp.broadcast_to(subln_g[0].astype(F32)[:, None], (2 * HD_B, B_QW))
    bias = _bias_row(rel_bias[0])

    xp = x_prompt.reshape(seq, D_MODEL)
    later = [w[0].astype(F32) for w in (w_o, w_mq, w_mo, w_up, w_down, w_mk, w_mv, cache_mem_k, cache_mem_v)]
    (qat, ka, vat, qbt, kb, vbt, ka_f, va_f, kb_f, vb_f,
     w_o_bf, w_mq_bf, w_mo_bf, w_up_bf, w_down_bf, w_mk_bf, w_mv_bf, mem_k_s, mem_v_s) = _qkv(
        xp, w_qkv_bf, jnp.arange(seq), QKV_TM, True, later)
    bias_rev = jnp.roll(bias[:, ::-1], 1, axis=1)
    assert TAIL_TM == 2 * A_TQ and TAIL_SUB == A_TQ
    oa_even, oa_odd = _attn_a_prompt(qat, ka, vat, bias_rev)
    ob = _attn_b_prompt(qbt, kb, vbt, lamp, gcol)
    tail_params = (w_o_bf, row(ln1_g), row(ln1_b), w_mq_bf, w_mo_bf, row(ln2_g), row(ln2_b),
                   w_up_bf, conv_w[0].astype(F32), row(conv_b), w_down_bf, row(ln3_g), row(ln3_b))
    mk_p, mv_p = _mem_kv(mem_prompt.reshape(N_MEM, D_MODEL), w_mk_bf, w_mv_bf)
    conv0 = jnp.zeros((SUBLANES, 2 * D_FF), F32)
    y_p, conv_p = _tail(xp, [oa_even, oa_odd], ob, mk_p[None], mv_p[None], conv0, tail_params,
                        TAIL_TM, TAIL_TM, TAIL_SUB)

    rows_s = streams * t_new
    xs = x_sample.reshape(rows_s, D_MODEL)
    pos_s = jnp.tile(past_b + jnp.arange(t_new), streams)
    qa_s, ka_s, va_s, qb_s, kb_s, vb_s, ka_sf, va_sf, kb_sf, vb_sf = _qkv(xs, w_qkv_bf, pos_s, rows_s, False)
    oa_s = _attn_a_sample(qa_s, ka_s, va_s,
                          cache_a_k[0].reshape(streams, -1, A_WIDTH), cache_a_v[0].reshape(streams, -1, A_WIDTH),
                          bias, t_new)
    ob_s = _attn_b_sample(qb_s, kb_s, vb_s,
                          cache_b_k[0].reshape(streams, past_b, B_WIDTH),
                          cache_b_v[0], lamp, g, t_new)
    conv_in_s = jnp.pad(state_conv[0].astype(F32), ((0, 0), (SUBLANES - (CONV_W - 1), 0), (0, 0)))
    y_s, conv_s = _tail(xs, [oa_s], ob_s, mem_k_s, mem_v_s,
                        conv_in_s.reshape(streams * SUBLANES, 2 * D_FF), tail_params, rows_s, t_new, rows_s)

    keep = CONV_W - 1
    return (
        y_p.reshape(batch, seq, D_MODEL),
        y_s.reshape(streams, t_new, D_MODEL),
        ka_f.reshape(1, batch, QKV_TM, H_A, HD_A),
        va_f.reshape(1, batch, QKV_TM, H_A, HD_A),
        kb_f.reshape(1, batch, seq, H_B, 2, HD_B),
        vb_f.reshape(1, batch, seq, H_B, 2 * HD_B),
        mk_p.reshape(1, batch, N_MEM, H_M, HD_M),
        mv_p.reshape(1, batch, N_MEM, H_M, HD_M),
        conv_p[SUBLANES - keep:].reshape(1, batch, keep, 2 * D_FF),
        ka_sf.reshape(1, streams, t_new, H_A, HD_A),
        va_sf.reshape(1, streams, t_new, H_A, HD_A),
        kb_sf.reshape(1, streams, t_new, H_B, 2, HD_B),
        vb_sf.reshape(1, streams, t_new, H_B, 2 * HD_B),
        conv_s.reshape(streams, SUBLANES, 2 * D_FF)[:, SUBLANES - keep:].reshape(1, streams, keep, 2 * D_FF),
    )
```

```python
import functools
import math

import jax
import jax.numpy as jnp
from jax import lax
from jax.experimental import pallas as pl
from jax.experimental.pallas import tpu as pltpu

F32 = jnp.float32
BF16 = jnp.bfloat16

D_MODEL = 1024
CHUNK = 64
BAND_PREV = 8
REL_CLIP = 128
H_A = 8
HD_A = 64
H_B = 4
HD_B = 64
ROT_DIM = HD_B // 4
ROPE_THETA = 500000.0
N_MEM = 256
H_M = 4
HD_M = D_MODEL // H_M
D_FF = 2816
CONV_W = 3
LN_EPS = 1e-5
DEPTH = 1
DEEPNORM_ALPHA = (2.0 * DEPTH) ** 0.25
A_WIDTH = H_A * HD_A
B_WIDTH = H_B * 2 * HD_B
LAM_INIT = 0.8 - 0.6 * math.exp(-0.3 * 0)

LANES = 128
SUBLANES = 8
BF16_SUBLANES = 16
NEG = -1e30
LOG2_E = math.log2(math.e)

QKV_TM = 512
A_TQ = 256
B_TQ = 1024
B_QW = 256
B_HEADS = 2
B_TK = 512
ONES_ROWS = 16
V_ROWS = LANES + ONES_ROWS
VA_ROWS = HD_A + ONES_ROWS
BIAS_SPAN = 1024
TAIL_TM = 512
TAIL_SUB = 256
MXU_DIM = 256
FF_BLOCKS = (D_FF,)
assert sum(FF_BLOCKS) == D_FF and all(w % MXU_DIM == 0 for w in FF_BLOCKS)

_NT = (((1,), (1,)), ((), ()))


def _dot(a, b):
    return jnp.dot(a, b, preferred_element_type=F32)


def _dot_nt(a, b):
    return lax.dot_general(a, b, _NT, preferred_element_type=F32)


def _const_spec(shape):
    nd = len(shape)
    return pl.BlockSpec(shape, lambda *_: (0,) * nd, pipeline_mode=pl.Buffered(1))


def _arbitrary(n):
    return pltpu.CompilerParams(dimension_semantics=("arbitrary",) * n)


def _rope(z, cos, sin):
    lane = lax.broadcasted_iota(jnp.int32, (z.shape[0], LANES), 1)
    first_half = (lane % HD_B) < (ROT_DIM // 2)
    outs = []
    for c in range(z.shape[1] // LANES):
        zc = z[:, c * LANES:(c + 1) * LANES]
        partner = jnp.where(first_half,
                            pltpu.roll(zc, LANES - ROT_DIM // 2, axis=1),
                            pltpu.roll(zc, ROT_DIM // 2, axis=1))
        outs.append(zc * cos + partner * sin)
    return jnp.concatenate(outs, axis=1)


def _qkv_kernel(x_ref, w_ref, rope_ref,
                qa_ref, ka_ref, va_ref, qb_ref, kb_ref, vb_ref,
                kaf_ref, vaf_ref, kbf_ref, vbf_ref, *, transposed):
    xb = x_ref[...].astype(BF16)
    cos, sin = _spread_rope(rope_ref[...])
    tm = xb.shape[0]

    def section(i):
        return _dot(xb, w_ref[:, i * A_WIDTH:(i + 1) * A_WIDTH])

    vb = section(5)
    for h in range(H_B):
        vbf_ref[:, h, :] = vb[:, h * LANES:(h + 1) * LANES]

    def ones_rows(width):
        return (lax.broadcasted_iota(jnp.int32, (ONES_ROWS, width), 0) == 0).astype(BF16)

    if transposed:
        for h in range(H_B):
            vb_ref[h, 0, 0:LANES, :] = vb[:, h * LANES:(h + 1) * LANES].T.astype(BF16)
            vb_ref[h, 0, LANES:V_ROWS, :] = ones_rows(tm)
    else:
        vb_ref[...] = vb.astype(BF16)
    qb = _rope(section(3), cos, sin) * (HD_B ** -0.5 * (LOG2_E if transposed else 1.0))
    if transposed:
        for h in range(H_B):
            qb_ref[h] = qb[:, h * LANES:(h + 1) * LANES].T.astype(BF16)
    else:
        qb_ref[...] = qb.astype(BF16)
    kb = _rope(section(4), cos, sin)
    kb_ref[...] = kb.astype(BF16)
    kbf_ref[...] = kb
    va = section(2)
    vaf_ref[...] = va
    if transposed:
        for p in range(H_A // 2):
            vt = va[:, p * LANES:(p + 1) * LANES].T.astype(BF16)
            for hh in range(2):
                for t in range(tm // A_TQ):
                    cols = slice(t * A_TQ, (t + 1) * A_TQ)
                    va_ref[2 * p + hh, t, 0:HD_A, :] = vt[hh * HD_A:(hh + 1) * HD_A, cols]
                    va_ref[2 * p + hh, t, HD_A:VA_ROWS, :] = ones_rows(A_TQ)
    else:
        va_ref[...] = va.astype(BF16)
    qa = section(0) * (HD_A ** -0.5 * LOG2_E)
    if transposed:
        for p in range(H_A // 2):
            qa_ref[p] = qa[:, p * LANES:(p + 1) * LANES].T.astype(BF16)
    else:
        qa_ref[...] = qa.astype(BF16)
    ka = section(1)
    ka_ref[...] = ka.astype(BF16)
    kaf_ref[...] = ka


def _rope_table(pos):
    inv = ROPE_THETA ** (-jnp.arange(0, ROT_DIM, 2, dtype=F32) / ROT_DIM)
    ang = pos.astype(F32)[:, None] * inv[None, :]
    cs = lax.optimization_barrier(jnp.concatenate([jnp.cos(ang), jnp.sin(ang)], axis=1))
    return jnp.pad(cs, ((0, 0), (0, LANES - ROT_DIM)))


def _spread_rope(tab):
    half = ROT_DIM // 2
    lane = lax.broadcasted_iota(jnp.int32, tab.shape, 1)
    c = jnp.where(lane < half, tab, 0.0)
    s = jnp.where((lane >= half) & (lane < ROT_DIM), tab, 0.0)
    c = c + pltpu.roll(c, half, axis=1)
    s = s - pltpu.roll(s, LANES - half, axis=1)
    c = c + pltpu.roll(c, HD_B, axis=1)
    s = s + pltpu.roll(s, HD_B, axis=1)
    return jnp.where(lane % HD_B < ROT_DIM, c, 1.0), s


def _qkv(x, w_bf, pos, tm, transposed):
    rows = x.shape[0]
    nt = rows // tm
    rope = _rope_table(pos)
    row_spec = lambda w: pl.BlockSpec((tm, w), lambda i: (i, 0))
    last_spec = pl.BlockSpec((tm, A_WIDTH), lambda i: (0, 0))
    bf = jax.ShapeDtypeStruct((rows, A_WIDTH), BF16)
    if transposed:
        assert tm == B_TK and tm % A_TQ == 0
        qa_spec = pl.BlockSpec((H_A // 2, LANES, tm), lambda i: (0, 0, i))
        qa_shape = jax.ShapeDtypeStruct((H_A // 2, LANES, rows), BF16)
        va_spec = pl.BlockSpec((H_A, tm // A_TQ, VA_ROWS, A_TQ), lambda i: (0, i, 0, 0))
        va_shape = jax.ShapeDtypeStruct((H_A, rows // A_TQ, VA_ROWS, A_TQ), BF16)
        qb_spec = pl.BlockSpec((H_B, LANES, tm), lambda i: (0, 0, i))
        qb_shape = jax.ShapeDtypeStruct((H_B, LANES, rows), BF16)
        vb_spec = pl.BlockSpec((H_B, 1, V_ROWS, tm), lambda i: (0, i, 0, 0))
        vb_shape = jax.ShapeDtypeStruct((H_B, nt, V_ROWS, tm), BF16)
    else:
        qa_spec = va_spec = qb_spec = vb_spec = row_spec(B_WIDTH)
        qa_shape = va_shape = qb_shape = vb_shape = bf
    return pl.pallas_call(
        functools.partial(_qkv_kernel, transposed=transposed),
        grid=(nt,),
        in_specs=[row_spec(D_MODEL), _const_spec(w_bf.shape), row_spec(LANES)],
        out_specs=[qa_spec, row_spec(A_WIDTH), va_spec, qb_spec, row_spec(B_WIDTH), vb_spec]
                  + [last_spec, last_spec, row_spec(B_WIDTH),
                     pl.BlockSpec((tm, H_B, 2 * HD_B), lambda i: (i, 0, 0))],
        out_shape=[qa_shape, bf, va_shape, qb_shape, bf, vb_shape]
                  + [jax.ShapeDtypeStruct((tm, A_WIDTH), F32)] * 2
                  + [jax.ShapeDtypeStruct((rows, B_WIDTH), F32),
                     jax.ShapeDtypeStruct((rows, H_B, 2 * HD_B), F32)],
        compiler_params=_arbitrary(1),
        name="qkv_proj",
    )(x, w_bf, rope)


def _bias_row(rel_bias):
    table = rel_bias.astype(F32)
    far = table[:, 2 * REL_CLIP:]
    near = table[:, :1]
    n_far = 2 * A_TQ - REL_CLIP
    n_near = 3 * A_TQ - (n_far + 2 * REL_CLIP + 1)
    h = table.shape[0]
    return jnp.concatenate([jnp.broadcast_to(far, (h, n_far)), table[:, ::-1],
                            jnp.broadcast_to(near, (h, n_near)),
                            jnp.broadcast_to(far, (h, BIAS_SPAN - 3 * A_TQ))], axis=1)


def _build_band_bias(row_ref, bias_ref):
    _, rows, cols = bias_ref.shape
    q = lax.broadcasted_iota(jnp.int32, (rows, cols), 0)
    k = lax.broadcasted_iota(jnp.int32, (rows, cols), 1)
    qc = q // CHUNK
    kc = k // CHUNK - (2 * A_TQ) // CHUNK
    band = (kc >= qc - BAND_PREV) & (kc <= qc)
    for h in range(H_A):
        r = jnp.broadcast_to(row_ref[h:h + 1, :], (rows, BIAS_SPAN))
        toeplitz = pltpu.roll(r, 0, axis=1, stride=1, stride_axis=0)
        bias_ref[h] = jnp.where(band, toeplitz[:, :cols] * LOG2_E, NEG)


def _pair_queries(q2):
    lane = lax.broadcasted_iota(jnp.int32, q2.shape, 1)
    lo = lane < (LANES // 2)
    zero = jnp.zeros_like(q2)
    return jnp.concatenate([jnp.where(lo, q2, zero), jnp.where(lo, zero, q2)], axis=0), lo


def _build_band_bias_t(row_ref, bias_ref):
    _, n_k, n_q = bias_ref.shape
    k = lax.broadcasted_iota(jnp.int32, (n_k, n_q), 0)
    q = lax.broadcasted_iota(jnp.int32, (n_k, n_q), 1)
    qc = q // CHUNK
    kc = k // CHUNK - (2 * A_TQ) // CHUNK
    band = (kc >= qc - BAND_PREV) & (kc <= qc)
    for h in range(H_A):
        r = jnp.broadcast_to(row_ref[h:h + 1, :], (n_k, BIAS_SPAN))
        toeplitz = pltpu.roll(r, 0, axis=1, stride=1, stride_axis=0)
        bias_ref[h] = jnp.where(band, toeplitz[:, :n_q] * LOG2_E, NEG)


def _attn_a_kernel(qt_ref, ka_ref, kb_ref, kc_ref, kd_ref, va_ref, vb_ref, vc_ref, vd_ref, row_ref,
                   o_even_ref, o_odd_ref, bias_ref, p_ref, *, n_steps):
    g = pl.program_id(0)
    k_refs = (ka_ref, kb_ref, kc_ref, kd_ref)
    v_refs = (va_ref, vb_ref, vc_ref, vd_ref)

    @pl.when(g == 0)
    def _():
        _build_band_bias_t(row_ref, bias_ref)
        p_ref[1] = jnp.zeros(p_ref.shape[1:], BF16)

    def body(check_keys):
        kidx = lax.broadcasted_iota(jnp.int32, (3 * A_TQ, A_TQ), 0)
        dim = lax.broadcasted_iota(jnp.int32, (LANES, A_TQ), 0)
        zero = jnp.zeros((LANES, A_TQ), BF16)

        def scores(t, h):
            pair = h // 2
            sl = slice(pair * LANES, (pair + 1) * LANES)
            qz = jnp.where(dim < HD_A if h % 2 == 0 else dim >= HD_A,
                           qt_ref[pair, :, t * A_TQ:(t + 1) * A_TQ], zero)
            kc = jnp.concatenate([k_refs[t][:, sl], k_refs[t + 1][:, sl], k_refs[t + 2][:, sl]], axis=0)
            s = _dot(kc, qz) + bias_ref[h]
            if check_keys:
                s = jnp.where(kidx >= (2 - t) * A_TQ, s, NEG)
            p_ref[t, h] = jnp.exp2(s - jnp.max(s, axis=0, keepdims=True)).astype(BF16)

        def values(t, o_ref, keep):
            o_t = [None] * H_A

            def head(h):
                e = p_ref[1 - t, h]
                r = (_dot(v_refs[t][h, 0], e[0:A_TQ]) + _dot(v_refs[t + 1][h, 0], e[A_TQ:2 * A_TQ])
                     + _dot(v_refs[t + 2][h, 0], e[2 * A_TQ:3 * A_TQ]))
                o_t[h] = r[0:HD_A] / r[HD_A:HD_A + 1]
                if h % 2 == 1:
                    sl = slice((h // 2) * LANES, (h // 2 + 1) * LANES)
                    o = jnp.concatenate([o_t[h - 1], o_t[h]], axis=0).T.astype(BF16)
                    o_ref[:, sl] = o if keep is None else jnp.where(keep, o, o_ref[:, sl])
            return head

        odd_head = values(0, o_odd_ref, None)
        even_head = values(1, o_even_ref, g < n_steps)
        for h in range(H_A):
            scores(0, h)
            odd_head(h)
        for h in range(H_A):
            scores(1, h)
            even_head(h)

    pl.when(g == 0)(lambda: body(True))
    pl.when(g > 0)(lambda: body(False))


def _attn_a_prompt(qat, ka, vat, bias_row_rev):
    rows = ka.shape[0]
    n_tiles = rows // A_TQ
    n_steps = n_tiles // 2
    assert n_tiles % 2 == 0 and BAND_PREV * CHUNK == 2 * A_TQ
    tile = lambda g, d: jnp.clip(2 * g + d, 0, n_tiles - 1)
    kspec = lambda d: pl.BlockSpec((A_TQ, A_WIDTH), lambda g: (tile(g, d), 0))
    vspec = lambda d: pl.BlockSpec((H_A, 1, VA_ROWS, A_TQ), lambda g: (0, tile(g, d), 0, 0))
    half = jax.ShapeDtypeStruct((rows // 2, A_WIDTH), BF16)
    return pl.pallas_call(
        functools.partial(_attn_a_kernel, n_steps=n_steps),
        grid=(n_steps + 1,),
        in_specs=[pl.BlockSpec((H_A // 2, LANES, 2 * A_TQ), lambda g: (0, 0, jnp.minimum(g, n_steps - 1)))]
                 + [kspec(d) for d in (-2, -1, 0, 1)] + [vspec(d) for d in (-3, -2, -1, 0)]
                 + [_const_spec(bias_row_rev.shape)],
        out_specs=[pl.BlockSpec((A_TQ, A_WIDTH), lambda g: (jnp.minimum(g, n_steps - 1), 0)),
                   pl.BlockSpec((A_TQ, A_WIDTH), lambda g: (jnp.clip(g - 1, 0, n_steps - 1), 0))],
        out_shape=[half, half],
        scratch_shapes=[pltpu.VMEM((H_A, 3 * A_TQ, A_TQ), F32), pltpu.VMEM((2, H_A, 3 * A_TQ, A_TQ), BF16)],
        compiler_params=_arbitrary(1),
        name="attn_a_prompt",
    )(qat, *([ka] * 4), *([vat] * 4), bias_row_rev)


def _attn_a_sample_kernel(q_ref, kn_ref, vn_ref, kc_ref, vc_ref, row_ref, o_ref, bias_ref):
    t = q_ref.shape[0]
    past = kc_ref.shape[1]

    @pl.when(pl.program_id(0) == 0)
    def _():
        _build_band_bias(row_ref, bias_ref)

    kc = kc_ref[0].astype(BF16)
    vc = vc_ref[0].astype(BF16)
    for p in range(H_A // 2):
        sl = slice(p * LANES, (p + 1) * LANES)
        qz, lo = _pair_queries(q_ref[:, sl])
        b_past = jnp.concatenate([bias_ref[2 * p, 0:t, 0:past], bias_ref[2 * p + 1, 0:t, 0:past]], axis=0)
        b_new = jnp.concatenate([bias_ref[2 * p, 0:t, past:past + t],
                                 bias_ref[2 * p + 1, 0:t, past:past + t]], axis=0)
        s_past = _dot_nt(qz, kc[:, sl]) + b_past
        s_new = _dot_nt(qz, kn_ref[:, sl]) + b_new
        m = jnp.maximum(jnp.max(s_past, axis=-1, keepdims=True), jnp.max(s_new, axis=-1, keepdims=True))
        e_past = jnp.exp2(s_past - m)
        e_new = jnp.exp2(s_new - m)
        l = jnp.sum(e_past, axis=-1, keepdims=True) + jnp.sum(e_new, axis=-1, keepdims=True)
        o = (_dot(e_past.astype(BF16), vc[:, sl]) + _dot(e_new.astype(BF16), vn_ref[:, sl])) / l
        o_ref[:, sl] = jnp.where(lo, o[:t], o[t:]).astype(BF16)


def _attn_a_sample(qa, ka, va, cache_k, cache_v, bias_row, t):
    rows = qa.shape[0]
    streams, past, width = cache_k.shape
    assert past == 2 * A_TQ and t <= CHUNK and t % SUBLANES == 0
    row = pl.BlockSpec((t, A_WIDTH), lambda s: (s, 0))
    cache = pl.BlockSpec((1, past, width), lambda s: (s, 0, 0))
    return pl.pallas_call(
        _attn_a_sample_kernel,
        grid=(streams,),
        in_specs=[row, row, row, cache, cache, _const_spec(bias_row.shape)],
        out_specs=row,
        out_shape=jax.ShapeDtypeStruct((rows, A_WIDTH), BF16),
        scratch_shapes=[pltpu.VMEM((H_A, t, past + LANES), F32)],
        compiler_params=_arbitrary(1),
        name="attn_a_sample",
    )(qa, ka, va, cache_k, cache_v, bias_row)


def _lambda(lamp_ref):
    lp = lamp_ref[...]
    a = jnp.sum(lp[0:1] * lp[1:2], axis=-1, keepdims=True)
    b = jnp.sum(lp[2:3] * lp[3:4], axis=-1, keepdims=True)
    return jnp.exp(a) - jnp.exp(b) + LAM_INIT


def _diff_finish(o0, o1, lam, g):
    o = o0 - lam * o1
    o = o * lax.rsqrt(jnp.mean(o * o, axis=-1, keepdims=True) + LN_EPS) * g * (1.0 - LAM_INIT)
    return o.astype(BF16)


def _attn_b_kernel(*refs, cast_steps, total_steps):
    n_cast = len(cast_steps)
    qt_ref, k_ref, vt_ref, lamp_ref, gcol_ref = refs[:5]
    w_refs = refs[5:5 + n_cast]
    o_ref = refs[5 + n_cast]
    wbf_refs = refs[6 + n_cast:6 + 2 * n_cast]
    acc_ref, p_ref, alpha_ref = refs[6 + 2 * n_cast:]

    qi = pl.program_id(1)
    n_parts = B_TQ // B_QW
    dim = lax.broadcasted_iota(jnp.int32, (LANES, B_QW), 0)
    zero = jnp.zeros((LANES, B_QW), BF16)
    chains = [(part, (h, mp)) for part in range(n_parts) for h in range(B_HEADS) for mp in range(2)]
    qz = []
    for part, (h, mp) in chains:
        qt = qt_ref[h, :, part * B_QW:(part + 1) * B_QW]
        qz.append(jnp.where(dim < HD_B if mp == 0 else dim >= HD_B, qt, zero))
    acc_ref[...] = jnp.zeros(acc_ref.shape, F32)
    p_ref[1] = jnp.zeros(p_ref.shape[1:], BF16)
    alpha_ref[1] = jnp.ones(alpha_ref.shape[1:], F32)

    def pending_values(c, j_prev, buf):
        h = chains[c][1][0]
        acc_ref[c] = alpha_ref[buf, c] * acc_ref[c] + _dot(vt_ref[h, j_prev], p_ref[buf, c])

    def step(j, ms, buf, parts, diag_parts, pending_parts):
        start = pl.multiple_of(j * B_TK, B_TK)
        j_prev = jnp.maximum(j - 1, 0)
        kpos = start + lax.broadcasted_iota(jnp.int32, (B_TK, B_QW), 0)
        lane = lax.broadcasted_iota(jnp.int32, (B_TK, B_QW), 1)
        out = list(ms)
        for c, (part, (h, _)) in enumerate(chains):
            if part in parts:
                s = _dot(k_ref[pl.ds(start, B_TK), h * LANES:(h + 1) * LANES], qz[c])
                if part in diag_parts:
                    qpos = qi * B_TQ + part * B_QW + lane
                    s = jnp.where(kpos < (qpos // CHUNK + 1) * CHUNK, s, NEG)
                out[c] = jnp.maximum(ms[c], jnp.max(s, axis=0, keepdims=True))
                alpha_ref[buf, c] = jnp.exp2(ms[c] - out[c])
                p_ref[buf, c] = jnp.exp2(s - out[c]).astype(BF16)
            if part in pending_parts:
                pending_values(c, j_prev, 1 - buf)
        return tuple(out)

    every = tuple(range(n_parts))
    low, high = every[:n_parts // 2], every[n_parts // 2:]

    def pair(i, ms):
        ms = step(2 * i, ms, 0, every, (), every)
        return step(2 * i + 1, ms, 1, every, (), every)

    m_init = tuple(jnp.full((1, B_QW), NEG, F32) for _ in chains)
    ms = lax.fori_loop(0, qi, pair, m_init)
    ms = step(2 * qi, ms, 0, every, low, every)
    step(2 * qi + 1, ms, 1, high, high, every)
    for c, (part, _) in enumerate(chains):
        if part in high:
            pending_values(c, 2 * qi + 1, 1)

    lam = _lambda(lamp_ref)
    for c in range(0, len(chains), 2):
        part, (h, _) = chains[c]
        a0 = acc_ref[c]
        a1 = acc_ref[c + 1]
        o = a0[0:LANES] / a0[LANES:LANES + 1] - lam * (a1[0:LANES] / a1[LANES:LANES + 1])
        o = o * lax.rsqrt(jnp.mean(o * o, axis=0, keepdims=True) + LN_EPS) * gcol_ref[...] * (1.0 - LAM_INIT)
        o_ref[part * B_QW:(part + 1) * B_QW, h * LANES:(h + 1) * LANES] = o.T.astype(BF16)

    step_id = pl.program_id(0) * pl.num_programs(1) + pl.program_id(1)
    for w_ref, wbf_ref, n_steps in zip(w_refs, wbf_refs, cast_steps):
        def cast(w_ref=w_ref, wbf_ref=wbf_ref):
            if len(w_ref.shape) == 2:
                wbf_ref[...] = w_ref[...].astype(BF16)
            else:
                width = w_ref.shape[3]
                for h in range(w_ref.shape[2]):
                    wbf_ref[0, :, h * width:(h + 1) * width] = w_ref[0, :, h, :].astype(BF16)
        if n_steps == total_steps:
            cast()
        else:
            pl.when(step_id < n_steps)(cast)


def _attn_b_prompt(qbt, kb, vbt, lamp, gcol, weights):
    rows = kb.shape[0]
    assert rows % B_TQ == 0 and B_TQ == 2 * B_TK and B_TK == 2 * B_QW and B_QW % CHUNK == 0
    assert H_B % B_HEADS == 0
    n_chains = (B_TQ // B_QW) * B_HEADS * 2
    n_q = rows // B_TQ
    total_steps = (H_B // B_HEADS) * n_q
    single = pl.Buffered(1)

    def cast_plan(w):
        step = lambda h, i, n: jnp.minimum(h * n_q + i, n - 1)
        if w.ndim == 2:
            n_steps = total_steps
            while w.shape[0] % (n_steps * BF16_SUBLANES):
                n_steps //= 2
            spec = pl.BlockSpec((w.shape[0] // n_steps, w.shape[1]), lambda h, i: (step(h, i, n_steps), 0))
            return n_steps, spec, spec, w.shape
        sets, n_rows, heads, width = w.shape
        per_set = total_steps // sets
        rows = n_rows // per_set
        assert per_set * sets == total_steps and rows * per_set == n_rows and rows % BF16_SUBLANES == 0
        in_spec = pl.BlockSpec((1, rows, heads, width),
                               lambda h, i: (step(h, i, total_steps) // per_set, step(h, i, total_steps) % per_set, 0, 0))
        out_spec = pl.BlockSpec((1, rows, heads * width),
                                lambda h, i: (step(h, i, total_steps) // per_set, step(h, i, total_steps) % per_set, 0))
        return total_steps, in_spec, out_spec, (sets, n_rows, heads * width)

    plans = [cast_plan(w) for w in weights]
    outs = pl.pallas_call(
        functools.partial(_attn_b_kernel, cast_steps=tuple(p[0] for p in plans), total_steps=total_steps),
        grid=(H_B // B_HEADS, n_q),
        in_specs=[pl.BlockSpec((B_HEADS, LANES, B_TQ), lambda h, i: (h, 0, i)),
                  pl.BlockSpec((rows, B_HEADS * LANES), lambda h, i: (0, h), pipeline_mode=single),
                  pl.BlockSpec((B_HEADS,) + vbt.shape[1:], lambda h, i: (h, 0, 0, 0), pipeline_mode=single),
                  _const_spec(lamp.shape), _const_spec(gcol.shape)] + [p[1] for p in plans],
        out_specs=[pl.BlockSpec((B_TQ, B_HEADS * LANES), lambda h, i: (i, h))] + [p[2] for p in plans],
        out_shape=[jax.ShapeDtypeStruct((rows, B_WIDTH), BF16)]
                  + [jax.ShapeDtypeStruct(p[3], BF16) for p in plans],
        scratch_shapes=[pltpu.VMEM((n_chains, V_ROWS, B_QW), F32),
                        pltpu.VMEM((2, n_chains, B_TK, B_QW), BF16), pltpu.VMEM((2, n_chains, 1, B_QW), F32)],
        compiler_params=_arbitrary(2),
        name="attn_b_prompt",
    )(qbt, kb, vbt, lamp, gcol, *weights)
    return outs[0], outs[1:]


def _attn_b_sample_kernel(q_ref, kn_ref, vn_ref, kc_ref, vc_ref, lamp_ref, g_ref, o_ref):
    t = q_ref.shape[0]
    lam = _lambda(lamp_ref)
    g = g_ref[...]
    for h in range(H_B):
        sl = slice(h * LANES, (h + 1) * LANES)
        qz, _ = _pair_queries(q_ref[:, sl])
        s_past = _dot_nt(qz, kc_ref[0, :, sl].astype(BF16))
        s_new = _dot_nt(qz, kn_ref[:, sl])
        m = jnp.maximum(jnp.max(s_past, axis=-1, keepdims=True), jnp.max(s_new, axis=-1, keepdims=True))
        e_past = jnp.exp(s_past - m)
        e_new = jnp.exp(s_new - m)
        l = jnp.sum(e_past, axis=-1, keepdims=True) + jnp.sum(e_new, axis=-1, keepdims=True)
        o = (_dot(e_past.astype(BF16), vc_ref[0, :, h, :].astype(BF16))
             + _dot(e_new.astype(BF16), vn_ref[:, sl])) / l
        o_ref[:, sl] = _diff_finish(o[:t], o[t:], lam, g)


def _attn_b_sample(qb, kb, vb, cache_k, cache_v, lamp, g, t):
    rows = qb.shape[0]
    streams, past, width = cache_k.shape
    assert past % CHUNK == 0 and t <= CHUNK
    row = pl.BlockSpec((t, B_WIDTH), lambda s: (s, 0))
    cache = pl.BlockSpec((1, past, width), lambda s: (s, 0, 0))
    cache_v_spec = pl.BlockSpec((1,) + cache_v.shape[1:], lambda s: (s, 0, 0, 0))
    return pl.pallas_call(
        _attn_b_sample_kernel,
        grid=(streams,),
        in_specs=[row, row, row, cache, cache_v_spec, _const_spec(lamp.shape), _const_spec(g.shape)],
        out_specs=row,
        out_shape=jax.ShapeDtypeStruct((rows, B_WIDTH), BF16),
        compiler_params=_arbitrary(1),
        name="attn_b_sample",
    )(qb, kb, vb, cache_k, cache_v, lamp, g)


def _mem_kv_kernel(x_ref, wk_ref, wv_ref, kf_ref, vf_ref):
    xb = x_ref[...].astype(BF16)
    kf_ref[...] = _dot(xb, wk_ref[...])
    vf_ref[...] = _dot(xb, wv_ref[...])


def _mem_kv(mem, wk_bf, wv_bf):
    f = jax.ShapeDtypeStruct(mem.shape, F32)
    return pl.pallas_call(
        _mem_kv_kernel,
        out_shape=[f, f],
        name="mem_kv",
    )(mem, wk_bf, wv_bf)


def _layer_norm(x, g, b):
    mu = jnp.mean(x, axis=-1, keepdims=True)
    xc = x - mu
    var = jnp.mean(xc * xc, axis=-1, keepdims=True)
    return xc * lax.rsqrt(var + LN_EPS) * g + b


def _tail_kernel(*refs, seg, sub):
    x_ref = refs[0]
    tm = x_ref.shape[0]
    n_seg = tm // seg
    n_sub = tm // sub
    assert n_seg == 1 or n_sub == 1
    subs = [slice(i * sub, (i + 1) * sub) for i in range(n_sub)]
    oa_refs = refs[1:1 + n_sub]
    (ob_ref, mk_ref, mv_ref, cin_ref, wo_ref, ln1g_ref, ln1b_ref, wmq_ref, wmo_ref, ln2g_ref, ln2b_ref,
     wup_ref, cw_ref, cb_ref, wdn_ref, ln3g_ref, ln3b_ref, y_ref, carry_ref) = refs[1 + n_sub:]

    @pl.when(pl.program_id(0) == 0)
    def _():
        carry_ref[...] = cin_ref[...]

    def mem_attend(q, mk, mv):
        heads = []
        for h in range(H_M):
            sl = slice(h * HD_M, (h + 1) * HD_M)
            s = _dot_nt(q[:, sl], mk[:, sl])
            e = jnp.exp(s - jnp.max(s, axis=-1, keepdims=True))
            l = jnp.sum(e, axis=-1, keepdims=True)
            heads.append((_dot(e.astype(BF16), mv[:, sl]) / l).astype(BF16))
        return jnp.concatenate(heads, axis=1)

    def mem_attend_rows(q):
        if mk_ref.shape[0] == 1:
            return mem_attend(q, mk_ref[0].astype(BF16), mv_ref[0].astype(BF16))
        return jnp.concatenate([mem_attend(q[sg * seg:(sg + 1) * seg],
                                           mk_ref[sg].astype(BF16), mv_ref[sg].astype(BF16))
                                for sg in range(n_seg)], axis=0)

    mix = [_dot(oa_ref[...], wo_ref[0:A_WIDTH, :]) + _dot(ob_ref[sl, :], wo_ref[A_WIDTH:A_WIDTH + B_WIDTH, :])
           for oa_ref, sl in zip(oa_refs, subs)]
    x1 = [_layer_norm(DEEPNORM_ALPHA * x_ref[sl, :] + m, ln1g_ref[...], ln1b_ref[...]) for sl, m in zip(subs, mix)]
    qm = [(_dot(v.astype(BF16), wmq_ref[...]) * (HD_M ** -0.5)).astype(BF16) for v in x1]
    om = [mem_attend_rows(q) for q in qm]
    mo = [_dot(v, wmo_ref[...]) for v in om]
    x2 = [_layer_norm(DEEPNORM_ALPHA * a + b, ln2g_ref[...], ln2b_ref[...]) for a, b in zip(x1, mo)]
    x2b = [v.astype(BF16) for v in x2]

    def conv(us, col, width):
        cols = slice(col, col + width)
        w0, w1, w2, b = cw_ref[0:1, cols], cw_ref[1:2, cols], cw_ref[2:3, cols], cb_ref[0:1, cols]

        def taps(u, p2, p1):
            row = lax.broadcasted_iota(jnp.int32, u.shape, 0)
            u1 = jnp.where(row == 0, p1, pltpu.roll(u, 1, axis=0))
            u2 = jnp.where(row == 0, p2, jnp.where(row == 1, p1, pltpu.roll(u, 2, axis=0)))
            return u2 * w0 + u1 * w1 + u * w2 + b

        def state(sg):
            base = SUBLANES * sg
            return carry_ref[base + 6:base + 7, cols], carry_ref[base + 7:base + 8, cols]

        if n_seg == 1:
            p2, p1 = state(0)
            outs = []
            for u in us:
                outs.append(taps(u, p2, p1))
                p2, p1 = u[sub - 2:sub - 1], u[sub - 1:sub]
            carry_ref[0:SUBLANES, cols] = us[-1][sub - SUBLANES:sub]
            return outs
        (u,) = us
        outs = []
        for sg in range(n_seg):
            useg = u[sg * seg:(sg + 1) * seg]
            outs.append(taps(useg, *state(sg)))
            carry_ref[SUBLANES * sg:SUBLANES * (sg + 1), cols] = useg[seg - SUBLANES:seg]
        return [jnp.concatenate(outs, axis=0)]

    f = [jnp.zeros((sub, D_MODEL), F32) for _ in subs]
    c0 = 0
    for width in FF_BLOCKS:
        gate = conv([_dot(v, wup_ref[:, c0:c0 + width]) for v in x2b], c0, width)
        val = conv([_dot(v, wup_ref[:, D_FF + c0:D_FF + c0 + width]) for v in x2b], D_FF + c0, width)
        hid = [(g * (1.0 / (1.0 + jnp.exp(-g))) * v).astype(BF16) for g, v in zip(gate, val)]
        f = [a + _dot(h, wdn_ref[c0:c0 + width, :]) for a, h in zip(f, hid)]
        c0 += width
    for sl, a, b in zip(subs, x2, f):
        y_ref[sl, :] = _layer_norm(DEEPNORM_ALPHA * a + b, ln3g_ref[...], ln3b_ref[...])


def _tail(x, oas, ob, mk, mv, conv_in, params, tm, seg, sub):
    rows = x.shape[0]
    n_seg = tm // seg
    assert tm % sub == 0 and sub % SUBLANES == 0 and len(oas) == tm // sub
    assert conv_in.shape == (SUBLANES * n_seg, 2 * D_FF)
    assert n_seg == 1 or rows == tm
    assert mk.shape[0] in (1, n_seg)
    row_spec = lambda w: pl.BlockSpec((tm, w), lambda i: (i, 0))
    carry_spec = pl.BlockSpec(conv_in.shape, lambda i: (0, 0))
    return pl.pallas_call(
        functools.partial(_tail_kernel, seg=seg, sub=sub),
        grid=(rows // tm,),
        in_specs=[row_spec(D_MODEL)] + [pl.BlockSpec((sub, A_WIDTH), lambda i: (i, 0)) for _ in oas]
                 + [row_spec(B_WIDTH), _const_spec(mk.shape), _const_spec(mv.shape),
                  _const_spec(conv_in.shape)] + [_const_spec(p.shape) for p in params],
        out_specs=[row_spec(D_MODEL), carry_spec],
        out_shape=[jax.ShapeDtypeStruct((rows, D_MODEL), F32),
                   jax.ShapeDtypeStruct(conv_in.shape, F32)],
        compiler_params=_arbitrary(1),
        name="tail",
    )(x, *oas, ob, mk, mv, conv_in, *params)


def kernel(x_prompt, x_sample, mem_prompt, cache_a_k, cache_a_v, cache_b_k, cache_b_v, cache_mem_k, cache_mem_v,
           state_conv, w_qkv, rel_bias, lambda_q1, lambda_k1, lambda_q2, lambda_k2, subln_g, w_o, ln1_g, ln1_b,
           w_mq, w_mk, w_mv, w_mo, ln2_g, ln2_b, w_up, conv_w, conv_b, w_down, ln3_g, ln3_b):
    assert w_qkv.shape[0] == DEPTH == 1
    batch, seq, _ = x_prompt.shape
    streams, t_new, _ = x_sample.shape
    past_b = cache_b_k.shape[2]
    assert batch == 1 and seq % QKV_TM == 0 and QKV_TM == BAND_PREV * CHUNK

    row = lambda v: v[0].astype(F32)[None, :]
    w_qkv_bf = w_qkv[0].astype(BF16)
    lamp = jnp.stack([lambda_q1[0], lambda_k1[0], lambda_q2[0], lambda_k2[0]]).astype(F32)
    g = row(subln_g)
    gcol = jnp.broadcast_to(subln_g[0].astype(F32)[:, None], (2 * HD_B, B_QW))
    bias = _bias_row(rel_bias[0])

    xp = x_prompt.reshape(seq, D_MODEL)
    qat, ka, vat, qbt, kb, vbt, ka_f, va_f, kb_f, vb_f = _qkv(xp, w_qkv_bf, jnp.arange(seq), QKV_TM, True)
    bias_rev = jnp.roll(bias[:, ::-1], 1, axis=1)
    assert TAIL_TM == 2 * A_TQ and TAIL_SUB == A_TQ
    oa_even, oa_odd = _attn_a_prompt(qat, ka, vat, bias_rev)
    later = [w[0].astype(F32) for w in (w_o, w_mq, w_mo, w_up, w_down, w_mk, w_mv, cache_mem_k, cache_mem_v)]
    ob, (w_o_bf, w_mq_bf, w_mo_bf, w_up_bf, w_down_bf, w_mk_bf, w_mv_bf, mem_k_s, mem_v_s) = _attn_b_prompt(
        qbt, kb, vbt, lamp, gcol, later)
    tail_params = (w_o_bf, row(ln1_g), row(ln1_b), w_mq_bf, w_mo_bf, row(ln2_g), row(ln2_b),
                   w_up_bf, conv_w[0].astype(F32), row(conv_b), w_down_bf, row(ln3_g), row(ln3_b))
    mk_p, mv_p = _mem_kv(mem_prompt.reshape(N_MEM, D_MODEL), w_mk_bf, w_mv_bf)
    conv0 = jnp.zeros((SUBLANES, 2 * D_FF), F32)
    y_p, conv_p = _tail(xp, [oa_even, oa_odd], ob, mk_p[None], mv_p[None], conv0, tail_params,
                        TAIL_TM, TAIL_TM, TAIL_SUB)

    rows_s = streams * t_new
    xs = x_sample.reshape(rows_s, D_MODEL)
    pos_s = jnp.tile(past_b + jnp.arange(t_new), streams)
    qa_s, ka_s, va_s, qb_s, kb_s, vb_s, ka_sf, va_sf, kb_sf, vb_sf = _qkv(xs, w_qkv_bf, pos_s, rows_s, False)
    oa_s = _attn_a_sample(qa_s, ka_s, va_s,
                          cache_a_k[0].reshape(streams, -1, A_WIDTH), cache_a_v[0].reshape(streams, -1, A_WIDTH),
                          bias, t_new)
    ob_s = _attn_b_sample(qb_s, kb_s, vb_s,
                          cache_b_k[0].reshape(streams, past_b, B_WIDTH),
                          cache_b_v[0], lamp, g, t_new)
    conv_in_s = jnp.pad(state_conv[0].astype(F32), ((0, 0), (SUBLANES - (CONV_W - 1), 0), (0, 0)))
    y_s, conv_s = _tail(xs, [oa_s], ob_s, mem_k_s, mem_v_s,
                        conv_in_s.reshape(streams * SUBLANES, 2 * D_FF), tail_params, rows_s, t_new, rows_s)

    keep = CONV_W - 1
    return (
        y_p.reshape(batch, seq, D_MODEL),
        y_s.reshape(streams, t_new, D_MODEL),
        ka_f.reshape(1, batch, QKV_TM, H_A, HD_A),
        va_f.reshape(1, batch, QKV_TM, H_A, HD_A),
        kb_f.reshape(1, batch, seq, H_B, 2, HD_B),
        vb_f.reshape(1, batch, seq, H_B, 2 * HD_B),
        mk_p.reshape(1, batch, N_MEM, H_M, HD_M),
        mv_p.reshape(1, batch, N_MEM, H_M, HD_M),
        conv_p[SUBLANES - keep:].reshape(1, batch, keep, 2 * D_FF),
        ka_sf.reshape(1, streams, t_new, H_A, HD_A),
        va_sf.reshape(1, streams, t_new, H_A, HD_A),
        kb_sf.reshape(1, streams, t_new, H_B, 2, HD_B),
        vb_sf.reshape(1, streams, t_new, H_B, 2 * HD_B),
        conv_s.reshape(streams, SUBLANES, 2 * D_FF)[:, SUBLANES - keep:].reshape(1, streams, keep, 2 * D_FF),
    )
```

```python
import functools
import math

import jax
import jax.numpy as jnp
from jax import lax
from jax.experimental import pallas as pl
from jax.experimental.pallas import tpu as pltpu

F32 = jnp.float32
BF16 = jnp.bfloat16

D_MODEL = 1024
CHUNK = 64
BAND_PREV = 8
REL_CLIP = 128
H_A = 8
HD_A = 64
H_B = 4
HD_B = 64
ROT_DIM = HD_B // 4
ROPE_THETA = 500000.0
N_MEM = 256
H_M = 4
HD_M = D_MODEL // H_M
D_FF = 2816
CONV_W = 3
LN_EPS = 1e-5
DEPTH = 1
DEEPNORM_ALPHA = (2.0 * DEPTH) ** 0.25
A_WIDTH = H_A * HD_A
B_WIDTH = H_B * 2 * HD_B
LAM_INIT = 0.8 - 0.6 * math.exp(-0.3 * 0)

LANES = 128
SUBLANES = 8
BF16_SUBLANES = 16
NEG = -1e30
LOG2_E = math.log2(math.e)

QKV_TM = 512
A_TQ = 256
B_TQ = 1024
B_QW = 256
B_HEADS = 2
B_TK = 512
ONES_ROWS = 16
V_ROWS = LANES + ONES_ROWS
VA_ROWS = HD_A + ONES_ROWS
BIAS_SPAN = 1024
TAIL_TM = 512
TAIL_SUB = 256
MXU_DIM = 256
FF_BLOCKS = (D_FF,)
assert sum(FF_BLOCKS) == D_FF and all(w % MXU_DIM == 0 for w in FF_BLOCKS)

_NT = (((1,), (1,)), ((), ()))


def _dot(a, b):
    return jnp.dot(a, b, preferred_element_type=F32)


def _dot_nt(a, b):
    return lax.dot_general(a, b, _NT, preferred_element_type=F32)


def _const_spec(shape):
    nd = len(shape)
    return pl.BlockSpec(shape, lambda *_: (0,) * nd, pipeline_mode=pl.Buffered(1))


def _arbitrary(n):
    return pltpu.CompilerParams(dimension_semantics=("arbitrary",) * n)


def _rope(z, cos, sin):
    lane = lax.broadcasted_iota(jnp.int32, (z.shape[0], LANES), 1)
    first_half = (lane % HD_B) < (ROT_DIM // 2)
    outs = []
    for c in range(z.shape[1] // LANES):
        zc = z[:, c * LANES:(c + 1) * LANES]
        partner = jnp.where(first_half,
                            pltpu.roll(zc, LANES - ROT_DIM // 2, axis=1),
                            pltpu.roll(zc, ROT_DIM // 2, axis=1))
        outs.append(zc * cos + partner * sin)
    return jnp.concatenate(outs, axis=1)


def _qkv_kernel(x_ref, w_ref, rope_ref,
                qa_ref, ka_ref, va_ref, qb_ref, kb_ref, vb_ref,
                kaf_ref, vaf_ref, kbf_ref, vbf_ref, *, transposed):
    xb = x_ref[...].astype(BF16)
    cos, sin = _spread_rope(rope_ref[...])
    tm = xb.shape[0]

    def section(i):
        return _dot(xb, w_ref[:, i * A_WIDTH:(i + 1) * A_WIDTH])

    vb = section(5)
    for h in range(H_B):
        vbf_ref[:, h, :] = vb[:, h * LANES:(h + 1) * LANES]

    def ones_rows(width):
        return (lax.broadcasted_iota(jnp.int32, (ONES_ROWS, width), 0) == 0).astype(BF16)

    if transposed:
        for h in range(H_B):
            vb_ref[h, 0, 0:LANES, :] = vb[:, h * LANES:(h + 1) * LANES].T.astype(BF16)
            vb_ref[h, 0, LANES:V_ROWS, :] = ones_rows(tm)
    else:
        vb_ref[...] = vb.astype(BF16)
    qb = _rope(section(3), cos, sin) * (HD_B ** -0.5 * (LOG2_E if transposed else 1.0))
    if transposed:
        for h in range(H_B):
            qb_ref[h] = qb[:, h * LANES:(h + 1) * LANES].T.astype(BF16)
    else:
        qb_ref[...] = qb.astype(BF16)
    kb = _rope(section(4), cos, sin)
    kb_ref[...] = kb.astype(BF16)
    kbf_ref[...] = kb
    va = section(2)
    vaf_ref[...] = va
    if transposed:
        for p in range(H_A // 2):
            vt = va[:, p * LANES:(p + 1) * LANES].T.astype(BF16)
            for hh in range(2):
                for t in range(tm // A_TQ):
                    cols = slice(t * A_TQ, (t + 1) * A_TQ)
                    va_ref[2 * p + hh, t, 0:HD_A, :] = vt[hh * HD_A:(hh + 1) * HD_A, cols]
                    va_ref[2 * p + hh, t, HD_A:VA_ROWS, :] = ones_rows(A_TQ)
    else:
        va_ref[...] = va.astype(BF16)
    qa = section(0) * (HD_A ** -0.5 * LOG2_E)
    if transposed:
        for p in range(H_A // 2):
            qa_ref[p] = qa[:, p * LANES:(p + 1) * LANES].T.astype(BF16)
    else:
        qa_ref[...] = qa.astype(BF16)
    ka = section(1)
    ka_ref[...] = ka.astype(BF16)
    kaf_ref[...] = ka


def _rope_table(pos):
    inv = ROPE_THETA ** (-jnp.arange(0, ROT_DIM, 2, dtype=F32) / ROT_DIM)
    ang = pos.astype(F32)[:, None] * inv[None, :]
    cs = lax.optimization_barrier(jnp.concatenate([jnp.cos(ang), jnp.sin(ang)], axis=1))
    return jnp.pad(cs, ((0, 0), (0, LANES - ROT_DIM)))


def _spread_rope(tab):
    half = ROT_DIM // 2
    lane = lax.broadcasted_iota(jnp.int32, tab.shape, 1)
    c = jnp.where(lane < half, tab, 0.0)
    s = jnp.where((lane >= half) & (lane < ROT_DIM), tab, 0.0)
    c = c + pltpu.roll(c, half, axis=1)
    s = s - pltpu.roll(s, LANES - half, axis=1)
    c = c + pltpu.roll(c, HD_B, axis=1)
    s = s + pltpu.roll(s, HD_B, axis=1)
    return jnp.where(lane % HD_B < ROT_DIM, c, 1.0), s


def _qkv(x, w_bf, pos, tm, transposed):
    rows = x.shape[0]
    nt = rows // tm
    rope = _rope_table(pos)
    row_spec = lambda w: pl.BlockSpec((tm, w), lambda i: (i, 0))
    last_spec = pl.BlockSpec((tm, A_WIDTH), lambda i: (0, 0))
    bf = jax.ShapeDtypeStruct((rows, A_WIDTH), BF16)
    if transposed:
        assert tm == B_TK and tm % A_TQ == 0
        qa_spec = pl.BlockSpec((H_A // 2, LANES, tm), lambda i: (0, 0, i))
        qa_shape = jax.ShapeDtypeStruct((H_A // 2, LANES, rows), BF16)
        va_spec = pl.BlockSpec((H_A, tm // A_TQ, VA_ROWS, A_TQ), lambda i: (0, i, 0, 0))
        va_shape = jax.ShapeDtypeStruct((H_A, rows // A_TQ, VA_ROWS, A_TQ), BF16)
        qb_spec = pl.BlockSpec((H_B, LANES, tm), lambda i: (0, 0, i))
        qb_shape = jax.ShapeDtypeStruct((H_B, LANES, rows), BF16)
        vb_spec = pl.BlockSpec((H_B, 1, V_ROWS, tm), lambda i: (0, i, 0, 0))
        vb_shape = jax.ShapeDtypeStruct((H_B, nt, V_ROWS, tm), BF16)
    else:
        qa_spec = va_spec = qb_spec = vb_spec = row_spec(B_WIDTH)
        qa_shape = va_shape = qb_shape = vb_shape = bf
    return pl.pallas_call(
        functools.partial(_qkv_kernel, transposed=transposed),
        grid=(nt,),
        in_specs=[row_spec(D_MODEL), _const_spec(w_bf.shape), row_spec(LANES)],
        out_specs=[qa_spec, row_spec(A_WIDTH), va_spec, qb_spec, row_spec(B_WIDTH), vb_spec]
                  + [last_spec, last_spec, row_spec(B_WIDTH),
                     pl.BlockSpec((tm, H_B, 2 * HD_B), lambda i: (i, 0, 0))],
        out_shape=[qa_shape, bf, va_shape, qb_shape, bf, vb_shape]
                  + [jax.ShapeDtypeStruct((tm, A_WIDTH), F32)] * 2
                  + [jax.ShapeDtypeStruct((rows, B_WIDTH), F32),
                     jax.ShapeDtypeStruct((rows, H_B, 2 * HD_B), F32)],
        compiler_params=_arbitrary(1),
        name="qkv_proj",
    )(x, w_bf, rope)


def _bias_row(rel_bias):
    table = rel_bias.astype(F32)
    far = table[:, 2 * REL_CLIP:]
    near = table[:, :1]
    n_far = 2 * A_TQ - REL_CLIP
    n_near = 3 * A_TQ - (n_far + 2 * REL_CLIP + 1)
    h = table.shape[0]
    return jnp.concatenate([jnp.broadcast_to(far, (h, n_far)), table[:, ::-1],
                            jnp.broadcast_to(near, (h, n_near)),
                            jnp.broadcast_to(far, (h, BIAS_SPAN - 3 * A_TQ))], axis=1)


def _build_band_bias(row_ref, bias_ref):
    _, rows, cols = bias_ref.shape
    q = lax.broadcasted_iota(jnp.int32, (rows, cols), 0)
    k = lax.broadcasted_iota(jnp.int32, (rows, cols), 1)
    qc = q // CHUNK
    kc = k // CHUNK - (2 * A_TQ) // CHUNK
    band = (kc >= qc - BAND_PREV) & (kc <= qc)
    for h in range(H_A):
        r = jnp.broadcast_to(row_ref[h:h + 1, :], (rows, BIAS_SPAN))
        toeplitz = pltpu.roll(r, 0, axis=1, stride=1, stride_axis=0)
        bias_ref[h] = jnp.where(band, toeplitz[:, :cols] * LOG2_E, NEG)


def _pair_queries(q2):
    lane = lax.broadcasted_iota(jnp.int32, q2.shape, 1)
    lo = lane < (LANES // 2)
    zero = jnp.zeros_like(q2)
    return jnp.concatenate([jnp.where(lo, q2, zero), jnp.where(lo, zero, q2)], axis=0), lo


def _build_band_bias_t(row_ref, bias_ref):
    _, n_k, n_q = bias_ref.shape
    k = lax.broadcasted_iota(jnp.int32, (n_k, n_q), 0)
    q = lax.broadcasted_iota(jnp.int32, (n_k, n_q), 1)
    qc = q // CHUNK
    kc = k // CHUNK - (2 * A_TQ) // CHUNK
    band = (kc >= qc - BAND_PREV) & (kc <= qc)
    for h in range(H_A):
        r = jnp.broadcast_to(row_ref[h:h + 1, :], (n_k, BIAS_SPAN))
        toeplitz = pltpu.roll(r, 0, axis=1, stride=1, stride_axis=0)
        bias_ref[h] = jnp.where(band, toeplitz[:, :n_q] * LOG2_E, NEG)


def _attn_a_kernel(qt_ref, ka_ref, kb_ref, kc_ref, kd_ref, va_ref, vb_ref, vc_ref, vd_ref, row_ref,
                   o_even_ref, o_odd_ref, bias_ref, p_ref, *, n_steps):
    g = pl.program_id(0)
    k_refs = (ka_ref, kb_ref, kc_ref, kd_ref)
    v_refs = (va_ref, vb_ref, vc_ref, vd_ref)

    @pl.when(g == 0)
    def _():
        _build_band_bias_t(row_ref, bias_ref)
        p_ref[1] = jnp.zeros(p_ref.shape[1:], BF16)

    def body(check_keys):
        kidx = lax.broadcasted_iota(jnp.int32, (3 * A_TQ, A_TQ), 0)
        dim = lax.broadcasted_iota(jnp.int32, (LANES, A_TQ), 0)
        zero = jnp.zeros((LANES, A_TQ), BF16)

        def scores(t, h):
            pair = h // 2
            sl = slice(pair * LANES, (pair + 1) * LANES)
            qz = jnp.where(dim < HD_A if h % 2 == 0 else dim >= HD_A,
                           qt_ref[pair, :, t * A_TQ:(t + 1) * A_TQ], zero)
            kc = jnp.concatenate([k_refs[t][:, sl], k_refs[t + 1][:, sl], k_refs[t + 2][:, sl]], axis=0)
            s = _dot(kc, qz) + bias_ref[h]
            if check_keys:
                s = jnp.where(kidx >= (2 - t) * A_TQ, s, NEG)
            p_ref[t, h] = jnp.exp2(s - jnp.max(s, axis=0, keepdims=True)).astype(BF16)

        def values(t, o_ref, keep):
            o_t = [None] * H_A

            def head(h):
                e = p_ref[1 - t, h]
                r = (_dot(v_refs[t][h, 0], e[0:A_TQ]) + _dot(v_refs[t + 1][h, 0], e[A_TQ:2 * A_TQ])
                     + _dot(v_refs[t + 2][h, 0], e[2 * A_TQ:3 * A_TQ]))
                o_t[h] = r[0:HD_A] / r[HD_A:HD_A + 1]
                if h % 2 == 1:
                    sl = slice((h // 2) * LANES, (h // 2 + 1) * LANES)
                    o = jnp.concatenate([o_t[h - 1], o_t[h]], axis=0).T.astype(BF16)
                    o_ref[:, sl] = o if keep is None else jnp.where(keep, o, o_ref[:, sl])
            return head

        odd_head = values(0, o_odd_ref, None)
        even_head = values(1, o_even_ref, g < n_steps)
        for h in range(H_A):
            scores(0, h)
            odd_head(h)
        for h in range(H_A):
            scores(1, h)
            even_head(h)

    pl.when(g == 0)(lambda: body(True))
    pl.when(g > 0)(lambda: body(False))


def _attn_a_prompt(qat, ka, vat, bias_row_rev):
    rows = ka.shape[0]
    n_tiles = rows // A_TQ
    n_steps = n_tiles // 2
    assert n_tiles % 2 == 0 and BAND_PREV * CHUNK == 2 * A_TQ
    tile = lambda g, d: jnp.clip(2 * g + d, 0, n_tiles - 1)
    kspec = lambda d: pl.BlockSpec((A_TQ, A_WIDTH), lambda g: (tile(g, d), 0))
    vspec = lambda d: pl.BlockSpec((H_A, 1, VA_ROWS, A_TQ), lambda g: (0, tile(g, d), 0, 0))
    half = jax.ShapeDtypeStruct((rows // 2, A_WIDTH), BF16)
    return pl.pallas_call(
        functools.partial(_attn_a_kernel, n_steps=n_steps),
        grid=(n_steps + 1,),
        in_specs=[pl.BlockSpec((H_A // 2, LANES, 2 * A_TQ), lambda g: (0, 0, jnp.minimum(g, n_steps - 1)))]
                 + [kspec(d) for d in (-2, -1, 0, 1)] + [vspec(d) for d in (-3, -2, -1, 0)]
                 + [_const_spec(bias_row_rev.shape)],
        out_specs=[pl.BlockSpec((A_TQ, A_WIDTH), lambda g: (jnp.minimum(g, n_steps - 1), 0)),
                   pl.BlockSpec((A_TQ, A_WIDTH), lambda g: (jnp.clip(g - 1, 0, n_steps - 1), 0))],
        out_shape=[half, half],
        scratch_shapes=[pltpu.VMEM((H_A, 3 * A_TQ, A_TQ), F32), pltpu.VMEM((2, H_A, 3 * A_TQ, A_TQ), BF16)],
        compiler_params=_arbitrary(1),
        name="attn_a_prompt",
    )(qat, *([ka] * 4), *([vat] * 4), bias_row_rev)


def _attn_a_sample_kernel(q_ref, kn_ref, vn_ref, kc_ref, vc_ref, row_ref, o_ref, bias_ref):
    t = q_ref.shape[0]
    past = kc_ref.shape[1]

    @pl.when(pl.program_id(0) == 0)
    def _():
        _build_band_bias(row_ref, bias_ref)

    kc = kc_ref[0].astype(BF16)
    vc = vc_ref[0].astype(BF16)
    for p in range(H_A // 2):
        sl = slice(p * LANES, (p + 1) * LANES)
        qz, lo = _pair_queries(q_ref[:, sl])
        b_past = jnp.concatenate([bias_ref[2 * p, 0:t, 0:past], bias_ref[2 * p + 1, 0:t, 0:past]], axis=0)
        b_new = jnp.concatenate([bias_ref[2 * p, 0:t, past:past + t],
                                 bias_ref[2 * p + 1, 0:t, past:past + t]], axis=0)
        s_past = _dot_nt(qz, kc[:, sl]) + b_past
        s_new = _dot_nt(qz, kn_ref[:, sl]) + b_new
        m = jnp.maximum(jnp.max(s_past, axis=-1, keepdims=True), jnp.max(s_new, axis=-1, keepdims=True))
        e_past = jnp.exp2(s_past - m)
        e_new = jnp.exp2(s_new - m)
        l = jnp.sum(e_past, axis=-1, keepdims=True) + jnp.sum(e_new, axis=-1, keepdims=True)
        o = (_dot(e_past.astype(BF16), vc[:, sl]) + _dot(e_new.astype(BF16), vn_ref[:, sl])) / l
        o_ref[:, sl] = jnp.where(lo, o[:t], o[t:]).astype(BF16)


def _attn_a_sample(qa, ka, va, cache_k, cache_v, bias_row, t):
    rows = qa.shape[0]
    streams, past, width = cache_k.shape
    assert past == 2 * A_TQ and t <= CHUNK and t % SUBLANES == 0
    row = pl.BlockSpec((t, A_WIDTH), lambda s: (s, 0))
    cache = pl.BlockSpec((1, past, width), lambda s: (s, 0, 0))
    return pl.pallas_call(
        _attn_a_sample_kernel,
        grid=(streams,),
        in_specs=[row, row, row, cache, cache, _const_spec(bias_row.shape)],
        out_specs=row,
        out_shape=jax.ShapeDtypeStruct((rows, A_WIDTH), BF16),
        scratch_shapes=[pltpu.VMEM((H_A, t, past + LANES), F32)],
        compiler_params=_arbitrary(1),
        name="attn_a_sample",
    )(qa, ka, va, cache_k, cache_v, bias_row)


def _lambda(lamp_ref):
    lp = lamp_ref[...]
    a = jnp.sum(lp[0:1] * lp[1:2], axis=-1, keepdims=True)
    b = jnp.sum(lp[2:3] * lp[3:4], axis=-1, keepdims=True)
    return jnp.exp(a) - jnp.exp(b) + LAM_INIT


def _diff_finish(o0, o1, lam, g):
    o = o0 - lam * o1
    o = o * lax.rsqrt(jnp.mean(o * o, axis=-1, keepdims=True) + LN_EPS) * g * (1.0 - LAM_INIT)
    return o.astype(BF16)


def _attn_b_kernel(*refs, cast_steps, total_steps):
    n_cast = len(cast_steps)
    qt_ref, k_ref, vt_ref, lamp_ref, gcol_ref = refs[:5]
    w_refs = refs[5:5 + n_cast]
    o_ref = refs[5 + n_cast]
    wbf_refs = refs[6 + n_cast:6 + 2 * n_cast]
    acc_ref, p_ref, alpha_ref = refs[6 + 2 * n_cast:]

    qi = pl.program_id(1)
    n_parts = B_TQ // B_QW
    dim = lax.broadcasted_iota(jnp.int32, (LANES, B_QW), 0)
    zero = jnp.zeros((LANES, B_QW), BF16)
    chains = [(part, (h, mp)) for part in range(n_parts) for h in range(B_HEADS) for mp in range(2)]
    qz = []
    for part, (h, mp) in chains:
        qt = qt_ref[h, :, part * B_QW:(part + 1) * B_QW]
        qz.append(jnp.where(dim < HD_B if mp == 0 else dim >= HD_B, qt, zero))
    acc_ref[...] = jnp.zeros(acc_ref.shape, F32)
    p_ref[1] = jnp.zeros(p_ref.shape[1:], BF16)
    alpha_ref[1] = jnp.ones(alpha_ref.shape[1:], F32)

    def pending_values(c, j_prev, buf):
        h = chains[c][1][0]
        acc_ref[c] = alpha_ref[buf, c] * acc_ref[c] + _dot(vt_ref[h, j_prev], p_ref[buf, c])

    def step(j, ms, buf, parts, diag_parts, pending_parts):
        start = pl.multiple_of(j * B_TK, B_TK)
        j_prev = jnp.maximum(j - 1, 0)
        kpos = start + lax.broadcasted_iota(jnp.int32, (B_TK, B_QW), 0)
        lane = lax.broadcasted_iota(jnp.int32, (B_TK, B_QW), 1)
        out = list(ms)
        for c, (part, (h, _)) in enumerate(chains):
            if part in parts:
                s = _dot(k_ref[pl.ds(start, B_TK), h * LANES:(h + 1) * LANES], qz[c])
                if part in diag_parts:
                    qpos = qi * B_TQ + part * B_QW + lane
                    s = jnp.where(kpos < (qpos // CHUNK + 1) * CHUNK, s, NEG)
                out[c] = jnp.maximum(ms[c], jnp.max(s, axis=0, keepdims=True))
                alpha_ref[buf, c] = jnp.exp2(ms[c] - out[c])
                p_ref[buf, c] = jnp.exp2(s - out[c]).astype(BF16)
            if part in pending_parts:
                pending_values(c, j_prev, 1 - buf)
        return tuple(out)

    every = tuple(range(n_parts))
    low, high = every[:n_parts // 2], every[n_parts // 2:]

    def pair(i, ms):
        ms = step(2 * i, ms, 0, every, (), every)
        return step(2 * i + 1, ms, 1, every, (), every)

    m_init = tuple(jnp.full((1, B_QW), NEG, F32) for _ in chains)
    ms = lax.fori_loop(0, qi, pair, m_init)
    ms = step(2 * qi, ms, 0, every, low, every)
    step(2 * qi + 1, ms, 1, high, high, every)
    for c, (part, _) in enumerate(chains):
        if part in high:
            pending_values(c, 2 * qi + 1, 1)

    lam = _lambda(lamp_ref)
    for c in range(0, len(chains), 2):
        part, (h, _) = chains[c]
        a0 = acc_ref[c]
        a1 = acc_ref[c + 1]
        o = a0[0:LANES] / a0[LANES:LANES + 1] - lam * (a1[0:LANES] / a1[LANES:LANES + 1])
        o = o * lax.rsqrt(jnp.mean(o * o, axis=0, keepdims=True) + LN_EPS) * gcol_ref[...] * (1.0 - LAM_INIT)
        o_ref[part * B_QW:(part + 1) * B_QW, h * LANES:(h + 1) * LANES] = o.T.astype(BF16)

    step_id = pl.program_id(0) * pl.num_programs(1) + pl.program_id(1)
    for w_ref, wbf_ref, n_steps in zip(w_refs, wbf_refs, cast_steps):
        def cast(w_ref=w_ref, wbf_ref=wbf_ref):
            if len(w_ref.shape) == 2:
                wbf_ref[...] = w_ref[...].astype(BF16)
            else:
                width = w_ref.shape[3]
                for h in range(w_ref.shape[2]):
                    wbf_ref[0, :, h * width:(h + 1) * width] = w_ref[0, :, h, :].astype(BF16)
        if n_steps == total_steps:
            cast()
        else:
            pl.when(step_id < n_steps)(cast)


def _attn_b_prompt(qbt, kb, vbt, lamp, gcol, weights):
    rows = kb.shape[0]
    assert rows % B_TQ == 0 and B_TQ == 2 * B_TK and B_TK == 2 * B_QW and B_QW % CHUNK == 0
    assert H_B % B_HEADS == 0
    n_chains = (B_TQ // B_QW) * B_HEADS * 2
    n_q = rows // B_TQ
    total_steps = (H_B // B_HEADS) * n_q
    single = pl.Buffered(1)

    def cast_plan(w):
        step = lambda h, i, n: jnp.minimum(h * n_q + i, n - 1)
        if w.ndim == 2:
            n_steps = total_steps
            while w.shape[0] % (n_steps * BF16_SUBLANES):
                n_steps //= 2
            spec = pl.BlockSpec((w.shape[0] // n_steps, w.shape[1]), lambda h, i: (step(h, i, n_steps), 0))
            return n_steps, spec, spec, w.shape
        sets, n_rows, heads, width = w.shape
        per_set = total_steps // sets
        rows = n_rows // per_set
        assert per_set * sets == total_steps and rows * per_set == n_rows and rows % BF16_SUBLANES == 0
        in_spec = pl.BlockSpec((1, rows, heads, width),
                               lambda h, i: (step(h, i, total_steps) // per_set, step(h, i, total_steps) % per_set, 0, 0))
        out_spec = pl.BlockSpec((1, rows, heads * width),
                                lambda h, i: (step(h, i, total_steps) // per_set, step(h, i, total_steps) % per_set, 0))
        return total_steps, in_spec, out_spec, (sets, n_rows, heads * width)

    plans = [cast_plan(w) for w in weights]
    outs = pl.pallas_call(
        functools.partial(_attn_b_kernel, cast_steps=tuple(p[0] for p in plans), total_steps=total_steps),
        grid=(H_B // B_HEADS, n_q),
        in_specs=[pl.BlockSpec((B_HEADS, LANES, B_TQ), lambda h, i: (h, 0, i)),
                  pl.BlockSpec((rows, B_HEADS * LANES), lambda h, i: (0, h), pipeline_mode=single),
                  pl.BlockSpec((B_HEADS,) + vbt.shape[1:], lambda h, i: (h, 0, 0, 0), pipeline_mode=single),
                  _const_spec(lamp.shape), _const_spec(gcol.shape)] + [p[1] for p in plans],
        out_specs=[pl.BlockSpec((B_TQ, B_HEADS * LANES), lambda h, i: (i, h))] + [p[2] for p in plans],
        out_shape=[jax.ShapeDtypeStruct((rows, B_WIDTH), BF16)]
                  + [jax.ShapeDtypeStruct(p[3], BF16) for p in plans],
        scratch_shapes=[pltpu.VMEM((n_chains, V_ROWS, B_QW), F32),
                        pltpu.VMEM((2, n_chains, B_TK, B_QW), BF16), pltpu.VMEM((2, n_chains, 1, B_QW), F32)],
        compiler_params=_arbitrary(2),
        name="attn_b_prompt",
    )(qbt, kb, vbt, lamp, gcol, *weights)
    return outs[0], outs[1:]


def _attn_b_sample_kernel(q_ref, kn_ref, vn_ref, kc_ref, vc_ref, lamp_ref, g_ref, o_ref):
    t = q_ref.shape[0]
    lam = _lambda(lamp_ref)
    g = g_ref[...]
    for h in range(H_B):
        sl = slice(h * LANES, (h + 1) * LANES)
        qz, _ = _pair_queries(q_ref[:, sl])
        s_past = _dot_nt(qz, kc_ref[0, :, sl].astype(BF16))
        s_new = _dot_nt(qz, kn_ref[:, sl])
        m = jnp.maximum(jnp.max(s_past, axis=-1, keepdims=True), jnp.max(s_new, axis=-1, keepdims=True))
        e_past = jnp.exp(s_past - m)
        e_new = jnp.exp(s_new - m)
        l = jnp.sum(e_past, axis=-1, keepdims=True) + jnp.sum(e_new, axis=-1, keepdims=True)
        o = (_dot(e_past.astype(BF16), vc_ref[0, :, h, :].astype(BF16))
             + _dot(e_new.astype(BF16), vn_ref[:, sl])) / l
        o_ref[:, sl] = _diff_finish(o[:t], o[t:], lam, g)


def _attn_b_sample(qb, kb, vb, cache_k, cache_v, lamp, g, t):
    rows = qb.shape[0]
    streams, past, width = cache_k.shape
    assert past % CHUNK == 0 and t <= CHUNK
    row = pl.BlockSpec((t, B_WIDTH), lambda s: (s, 0))
    cache = pl.BlockSpec((1, past, width), lambda s: (s, 0, 0))
    cache_v_spec = pl.BlockSpec((1,) + cache_v.shape[1:], lambda s: (s, 0, 0, 0))
    return pl.pallas_call(
        _attn_b_sample_kernel,
        grid=(streams,),
        in_specs=[row, row, row, cache, cache_v_spec, _const_spec(lamp.shape), _const_spec(g.shape)],
        out_specs=row,
        out_shape=jax.ShapeDtypeStruct((rows, B_WIDTH), BF16),
        compiler_params=_arbitrary(1),
        name="attn_b_sample",
    )(qb, kb, vb, cache_k, cache_v, lamp, g)


def _mem_kv_kernel(x_ref, wk_ref, wv_ref, kf_ref, vf_ref):
    xb = x_ref[...].astype(BF16)
    kf_ref[...] = _dot(xb, wk_ref[...])
    vf_ref[...] = _dot(xb, wv_ref[...])


def _mem_kv(mem, wk_bf, wv_bf):
    f = jax.ShapeDtypeStruct(mem.shape, F32)
    return pl.pallas_call(
        _mem_kv_kernel,
        out_shape=[f, f],
        name="mem_kv",
    )(mem, wk_bf, wv_bf)


def _layer_norm(x, g, b):
    mu = jnp.mean(x, axis=-1, keepdims=True)
    xc = x - mu
    var = jnp.mean(xc * xc, axis=-1, keepdims=True)
    return xc * lax.rsqrt(var + LN_EPS) * g + b


def _tail_kernel(*refs, seg, sub):
    x_ref = refs[0]
    tm = x_ref.shape[0]
    n_seg = tm // seg
    n_sub = tm // sub
    assert n_seg == 1 or n_sub == 1
    subs = [slice(i * sub, (i + 1) * sub) for i in range(n_sub)]
    oa_refs = refs[1:1 + n_sub]
    (ob_ref, mk_ref, mv_ref, cin_ref, wo_ref, ln1g_ref, ln1b_ref, wmq_ref, wmo_ref, ln2g_ref, ln2b_ref,
     wup_ref, cw_ref, cb_ref, wdn_ref, ln3g_ref, ln3b_ref, y_ref, carry_ref) = refs[1 + n_sub:]

    @pl.when(pl.program_id(0) == 0)
    def _():
        carry_ref[...] = cin_ref[...]

    def mem_attend(q, mk, mv):
        heads = []
        for h in range(H_M):
            sl = slice(h * HD_M, (h + 1) * HD_M)
            s = _dot_nt(q[:, sl], mk[:, sl])
            e = jnp.exp(s - jnp.max(s, axis=-1, keepdims=True))
            l = jnp.sum(e, axis=-1, keepdims=True)
            heads.append((_dot(e.astype(BF16), mv[:, sl]) / l).astype(BF16))
        return jnp.concatenate(heads, axis=1)

    def mem_attend_rows(q):
        if mk_ref.shape[0] == 1:
            return mem_attend(q, mk_ref[0].astype(BF16), mv_ref[0].astype(BF16))
        return jnp.concatenate([mem_attend(q[sg * seg:(sg + 1) * seg],
                                           mk_ref[sg].astype(BF16), mv_ref[sg].astype(BF16))
                                for sg in range(n_seg)], axis=0)

    mix = [_dot(oa_ref[...], wo_ref[0:A_WIDTH, :]) + _dot(ob_ref[sl, :], wo_ref[A_WIDTH:A_WIDTH + B_WIDTH, :])
           for oa_ref, sl in zip(oa_refs, subs)]
    x1 = [_layer_norm(DEEPNORM_ALPHA * x_ref[sl, :] + m, ln1g_ref[...], ln1b_ref[...]) for sl, m in zip(subs, mix)]
    qm = [(_dot(v.astype(BF16), wmq_ref[...]) * (HD_M ** -0.5)).astype(BF16) for v in x1]
    om = [mem_attend_rows(q) for q in qm]
    mo = [_dot(v, wmo_ref[...]) for v in om]
    x2 = [_layer_norm(DEEPNORM_ALPHA * a + b, ln2g_ref[...], ln2b_ref[...]) for a, b in zip(x1, mo)]
    x2b = [v.astype(BF16) for v in x2]

    def conv(us, col, width):
        cols = slice(col, col + width)
        w0, w1, w2, b = cw_ref[0:1, cols], cw_ref[1:2, cols], cw_ref[2:3, cols], cb_ref[0:1, cols]

        def taps(u, p2, p1):
            row = lax.broadcasted_iota(jnp.int32, u.shape, 0)
            u1 = jnp.where(row == 0, p1, pltpu.roll(u, 1, axis=0))
            u2 = jnp.where(row == 0, p2, jnp.where(row == 1, p1, pltpu.roll(u, 2, axis=0)))
            return u2 * w0 + u1 * w1 + u * w2 + b

        def state(sg):
            base = SUBLANES * sg
            return carry_ref[base + 6:base + 7, cols], carry_ref[base + 7:base + 8, cols]

        if n_seg == 1:
            p2, p1 = state(0)
            outs = []
            for u in us:
                outs.append(taps(u, p2, p1))
                p2, p1 = u[sub - 2:sub - 1], u[sub - 1:sub]
            carry_ref[0:SUBLANES, cols] = us[-1][sub - SUBLANES:sub]
            return outs
        (u,) = us
        outs = []
        for sg in range(n_seg):
            useg = u[sg * seg:(sg + 1) * seg]
            outs.append(taps(useg, *state(sg)))
            carry_ref[SUBLANES * sg:SUBLANES * (sg + 1), cols] = useg[seg - SUBLANES:seg]
        return [jnp.concatenate(outs, axis=0)]

    f = [jnp.zeros((sub, D_MODEL), F32) for _ in subs]
    c0 = 0
    for width in FF_BLOCKS:
        gate = conv([_dot(v, wup_ref[:, c0:c0 + width]) for v in x2b], c0, width)
        val = conv([_dot(v, wup_ref[:, D_FF + c0:D_FF + c0 + width]) for v in x2b], D_FF + c0, width)
        hid = [(g * (1.0 / (1.0 + jnp.exp(-g))) * v).astype(BF16) for g, v in zip(gate, val)]
        f = [a + _dot(h, wdn_ref[c0:c0 + width, :]) for a, h in zip(f, hid)]
        c0 += width
    for sl, a, b in zip(subs, x2, f):
        y_ref[sl, :] = _layer_norm(DEEPNORM_ALPHA * a + b, ln3g_ref[...], ln3b_ref[...])


def _tail(x, oas, ob, mk, mv, conv_in, params, tm, seg, sub):
    rows = x.shape[0]
    n_seg = tm // seg
    assert tm % sub == 0 and sub % SUBLANES == 0 and len(oas) == tm // sub
    assert conv_in.shape == (SUBLANES * n_seg, 2 * D_FF)
    assert n_seg == 1 or rows == tm
    assert mk.shape[0] in (1, n_seg)
    row_spec = lambda w: pl.BlockSpec((tm, w), lambda i: (i, 0))
    carry_spec = pl.BlockSpec(conv_in.shape, lambda i: (0, 0))
    return pl.pallas_call(
        functools.partial(_tail_kernel, seg=seg, sub=sub),
        grid=(rows // tm,),
        in_specs=[row_spec(D_MODEL)] + [pl.BlockSpec((sub, A_WIDTH), lambda i: (i, 0)) for _ in oas]
                 + [row_spec(B_WIDTH), _const_spec(mk.shape), _const_spec(mv.shape),
                  _const_spec(conv_in.shape)] + [_const_spec(p.shape) for p in params],
        out_specs=[row_spec(D_MODEL), carry_spec],
        out_shape=[jax.ShapeDtypeStruct((rows, D_MODEL), F32),
                   jax.ShapeDtypeStruct(conv_in.shape, F32)],
        compiler_params=_arbitrary(1),
        name="tail",
    )(x, *oas, ob, mk, mv, conv_in, *params)


def kernel(x_prompt, x_sample, mem_prompt, cache_a_k, cache_a_v, cache_b_k, cache_b_v, cache_mem_k, cache_mem_v,
           state_conv, w_qkv, rel_bias, lambda_q1, lambda_k1, lambda_q2, lambda_k2, subln_g, w_o, ln1_g, ln1_b,
           w_mq, w_mk, w_mv, w_mo, ln2_g, ln2_b, w_up, conv_w, conv_b, w_down, ln3_g, ln3_b):
    assert w_qkv.shape[0] == DEPTH == 1
    batch, seq, _ = x_prompt.shape
    streams, t_new, _ = x_sample.shape
    past_b = cache_b_k.shape[2]
    assert batch == 1 and seq % QKV_TM == 0 and QKV_TM == BAND_PREV * CHUNK

    row = lambda v: v[0].astype(F32)[None, :]
    w_qkv_bf = w_qkv[0].astype(BF16)
    lamp = jnp.stack([lambda_q1[0], lambda_k1[0], lambda_q2[0], lambda_k2[0]]).astype(F32)
    g = row(subln_g)
    gcol = jnp.broadcast_to(subln_g[0].astype(F32)[:, None], (2 * HD_B, B_QW))
    bias = _bias_row(rel_bias[0])

    xp = x_prompt.reshape(seq, D_MODEL)
    qat, ka, vat, qbt, kb, vbt, ka_f, va_f, kb_f, vb_f = _qkv(xp, w_qkv_bf, jnp.arange(seq), QKV_TM, True)
    bias_rev = jnp.roll(bias[:, ::-1], 1, axis=1)
    assert TAIL_TM == 2 * A_TQ and TAIL_SUB == A_TQ
    oa_even, oa_odd = _attn_a_prompt(qat, ka, vat, bias_rev)
    later = [w[0].astype(F32) for w in (w_o, w_mq, w_mo, w_up, w_down, w_mk, w_mv, cache_mem_k, cache_mem_v)]
    ob, (w_o_bf, w_mq_bf, w_mo_bf, w_up_bf, w_down_bf, w_mk_bf, w_mv_bf, mem_k_s, mem_v_s) = _attn_b_prompt(
        qbt, kb, vbt, lamp, gcol, later)
    tail_params = (w_o_bf, row(ln1_g), row(ln1_b), w_mq_bf, w_mo_bf, row(ln2_g), row(ln2_b),
                   w_up_bf, conv_w[0].astype(F32), row(conv_b), w_down_bf, row(ln3_g), row(ln3_b))
    mk_p, mv_p = _mem_kv(mem_prompt.reshape(N_MEM, D_MODEL), w_mk_bf, w_mv_bf)
    conv0 = jnp.zeros((SUBLANES, 2 * D_FF), F32)
    y_p, conv_p = _tail(xp, [oa_even, oa_odd], ob, mk_p[None], mv_p[None], conv0, tail_params,
                        TAIL_TM, TAIL_TM, TAIL_SUB)

    rows_s = streams * t_new
    xs = x_sample.reshape(rows_s, D_MODEL)
    pos_s = jnp.tile(past_b + jnp.arange(t_new), streams)
    qa_s, ka_s, va_s, qb_s, kb_s, vb_s, ka_sf, va_sf, kb_sf, vb_sf = _qkv(xs, w_qkv_bf, pos_s, rows_s, False)
    oa_s = _attn_a_sample(qa_s, ka_s, va_s,
                          cache_a_k[0].reshape(streams, -1, A_WIDTH).astype(BF16),
                          cache_a_v[0].reshape(streams, -1, A_WIDTH).astype(BF16), bias, t_new)
    ob_s = _attn_b_sample(qb_s, kb_s, vb_s,
                          cache_b_k[0].reshape(streams, past_b, B_WIDTH).astype(BF16),
                          cache_b_v[0], lamp, g, t_new)
    conv_in_s = jnp.pad(state_conv[0].astype(F32), ((0, 0), (SUBLANES - (CONV_W - 1), 0), (0, 0)))
    y_s, conv_s = _tail(xs, [oa_s], ob_s, mem_k_s, mem_v_s,
                        conv_in_s.reshape(streams * SUBLANES, 2 * D_FF), tail_params, rows_s, t_new, rows_s)

    keep = CONV_W - 1
    return (
        y_p.reshape(batch, seq, D_MODEL),
        y_s.reshape(streams, t_new, D_MODEL),
        ka_f.reshape(1, batch, QKV_TM, H_A, HD_A),
        va_f.reshape(1, batch, QKV_TM, H_A, HD_A),
        kb_f.reshape(1, batch, seq, H_B, 2, HD_B),
        vb_f.reshape(1, batch, seq, H_B, 2 * HD_B),
        mk_p.reshape(1, batch, N_MEM, H_M, HD_M),
        mv_p.reshape(1, batch, N_MEM, H_M, HD_M),
        conv_p[SUBLANES - keep:].reshape(1, batch, keep, 2 * D_FF),
        ka_sf.reshape(1, streams, t_new, H_A, HD_A),
        va_sf.reshape(1, streams, t_new, H_A, HD_A),
        kb_sf.reshape(1, streams, t_new, H_B, 2, HD_B),
        vb_sf.reshape(1, streams, t_new, H_B, 2 * HD_B),
        conv_s.reshape(streams, SUBLANES, 2 * D_FF)[:, SUBLANES - keep:].reshape(1, streams, keep, 2 * D_FF),
    )
```

```python
import functools
import math

import jax
import jax.numpy as jnp
from jax import lax
from jax.experimental import pallas as pl
from jax.experimental.pallas import tpu as pltpu

F32 = jnp.float32
BF16 = jnp.bfloat16

D_MODEL = 1024
CHUNK = 64
BAND_PREV = 8
REL_CLIP = 128
H_A = 8
HD_A = 64
H_B = 4
HD_B = 64
ROT_DIM = HD_B // 4
ROPE_THETA = 500000.0
N_MEM = 256
H_M = 4
HD_M = D_MODEL // H_M
D_FF = 2816
CONV_W = 3
LN_EPS = 1e-5
DEPTH = 1
DEEPNORM_ALPHA = (2.0 * DEPTH) ** 0.25
A_WIDTH = H_A * HD_A
B_WIDTH = H_B * 2 * HD_B
LAM_INIT = 0.8 - 0.6 * math.exp(-0.3 * 0)

LANES = 128
SUBLANES = 8
BF16_SUBLANES = 16
NEG = -1e30
LOG2_E = math.log2(math.e)

QKV_TM = 512
A_TQ = 256
B_TQ = 1024
B_QW = 256
B_HEADS = 2
B_TK = 512
ONES_ROWS = 16
V_ROWS = LANES + ONES_ROWS
VA_ROWS = HD_A + ONES_ROWS
BIAS_SPAN = 1024
TAIL_TM = 512
TAIL_SUB = 256
MXU_DIM = 256
FF_BLOCKS = (D_FF,)
assert sum(FF_BLOCKS) == D_FF and all(w % MXU_DIM == 0 for w in FF_BLOCKS)

_NT = (((1,), (1,)), ((), ()))


def _dot(a, b):
    return jnp.dot(a, b, preferred_element_type=F32)


def _dot_nt(a, b):
    return lax.dot_general(a, b, _NT, preferred_element_type=F32)


def _const_spec(shape):
    nd = len(shape)
    return pl.BlockSpec(shape, lambda *_: (0,) * nd, pipeline_mode=pl.Buffered(1))


def _arbitrary(n):
    return pltpu.CompilerParams(dimension_semantics=("arbitrary",) * n)


def _rope(z, cos, sin):
    lane = lax.broadcasted_iota(jnp.int32, (z.shape[0], LANES), 1)
    first_half = (lane % HD_B) < (ROT_DIM // 2)
    outs = []
    for c in range(z.shape[1] // LANES):
        zc = z[:, c * LANES:(c + 1) * LANES]
        partner = jnp.where(first_half,
                            pltpu.roll(zc, LANES - ROT_DIM // 2, axis=1),
                            pltpu.roll(zc, ROT_DIM // 2, axis=1))
        outs.append(zc * cos + partner * sin)
    return jnp.concatenate(outs, axis=1)


def _qkv_kernel(x_ref, w_ref, rope_ref,
                qa_ref, ka_ref, va_ref, qb_ref, kb_ref, vb_ref,
                kaf_ref, vaf_ref, kbf_ref, vbf_ref, *, transposed):
    xb = x_ref[...].astype(BF16)
    cos, sin = _spread_rope(rope_ref[...])
    tm = xb.shape[0]

    def section(i):
        return _dot(xb, w_ref[:, i * A_WIDTH:(i + 1) * A_WIDTH])

    vb = section(5)
    for h in range(H_B):
        vbf_ref[pl.ds(h, tm, stride=H_B), :] = vb[:, h * LANES:(h + 1) * LANES]

    def ones_rows(width):
        return (lax.broadcasted_iota(jnp.int32, (ONES_ROWS, width), 0) == 0).astype(BF16)

    if transposed:
        for h in range(H_B):
            vb_ref[h, 0, 0:LANES, :] = vb[:, h * LANES:(h + 1) * LANES].T.astype(BF16)
            vb_ref[h, 0, LANES:V_ROWS, :] = ones_rows(tm)
    else:
        vb_ref[...] = vb.astype(BF16)
    qb = _rope(section(3), cos, sin) * (HD_B ** -0.5 * (LOG2_E if transposed else 1.0))
    if transposed:
        for h in range(H_B):
            qb_ref[h] = qb[:, h * LANES:(h + 1) * LANES].T.astype(BF16)
    else:
        qb_ref[...] = qb.astype(BF16)
    kb = _rope(section(4), cos, sin)
    kb_ref[...] = kb.astype(BF16)
    kbf_ref[...] = kb
    va = section(2)
    vaf_ref[...] = va
    if transposed:
        for p in range(H_A // 2):
            vt = va[:, p * LANES:(p + 1) * LANES].T.astype(BF16)
            for hh in range(2):
                for t in range(tm // A_TQ):
                    cols = slice(t * A_TQ, (t + 1) * A_TQ)
                    va_ref[2 * p + hh, t, 0:HD_A, :] = vt[hh * HD_A:(hh + 1) * HD_A, cols]
                    va_ref[2 * p + hh, t, HD_A:VA_ROWS, :] = ones_rows(A_TQ)
    else:
        va_ref[...] = va.astype(BF16)
    qa = section(0) * (HD_A ** -0.5 * LOG2_E)
    if transposed:
        for p in range(H_A // 2):
            qa_ref[p] = qa[:, p * LANES:(p + 1) * LANES].T.astype(BF16)
    else:
        qa_ref[...] = qa.astype(BF16)
    ka = section(1)
    ka_ref[...] = ka.astype(BF16)
    kaf_ref[...] = ka


def _rope_table(pos):
    inv = ROPE_THETA ** (-jnp.arange(0, ROT_DIM, 2, dtype=F32) / ROT_DIM)
    ang = pos.astype(F32)[:, None] * inv[None, :]
    cs = lax.optimization_barrier(jnp.concatenate([jnp.cos(ang), jnp.sin(ang)], axis=1))
    return jnp.pad(cs, ((0, 0), (0, LANES - ROT_DIM)))


def _spread_rope(tab):
    half = ROT_DIM // 2
    lane = lax.broadcasted_iota(jnp.int32, tab.shape, 1)
    c = jnp.where(lane < half, tab, 0.0)
    s = jnp.where((lane >= half) & (lane < ROT_DIM), tab, 0.0)
    c = c + pltpu.roll(c, half, axis=1)
    s = s - pltpu.roll(s, LANES - half, axis=1)
    c = c + pltpu.roll(c, HD_B, axis=1)
    s = s + pltpu.roll(s, HD_B, axis=1)
    return jnp.where(lane % HD_B < ROT_DIM, c, 1.0), s


def _qkv(x, w_bf, pos, tm, transposed):
    rows = x.shape[0]
    nt = rows // tm
    rope = _rope_table(pos)
    row_spec = lambda w: pl.BlockSpec((tm, w), lambda i: (i, 0))
    last_spec = pl.BlockSpec((tm, A_WIDTH), lambda i: (0, 0))
    bf = jax.ShapeDtypeStruct((rows, A_WIDTH), BF16)
    if transposed:
        assert tm == B_TK and tm % A_TQ == 0
        qa_spec = pl.BlockSpec((H_A // 2, LANES, tm), lambda i: (0, 0, i))
        qa_shape = jax.ShapeDtypeStruct((H_A // 2, LANES, rows), BF16)
        va_spec = pl.BlockSpec((H_A, tm // A_TQ, VA_ROWS, A_TQ), lambda i: (0, i, 0, 0))
        va_shape = jax.ShapeDtypeStruct((H_A, rows // A_TQ, VA_ROWS, A_TQ), BF16)
        qb_spec = pl.BlockSpec((H_B, LANES, tm), lambda i: (0, 0, i))
        qb_shape = jax.ShapeDtypeStruct((H_B, LANES, rows), BF16)
        vb_spec = pl.BlockSpec((H_B, 1, V_ROWS, tm), lambda i: (0, i, 0, 0))
        vb_shape = jax.ShapeDtypeStruct((H_B, nt, V_ROWS, tm), BF16)
    else:
        qa_spec = va_spec = qb_spec = vb_spec = row_spec(B_WIDTH)
        qa_shape = va_shape = qb_shape = vb_shape = bf
    return pl.pallas_call(
        functools.partial(_qkv_kernel, transposed=transposed),
        grid=(nt,),
        in_specs=[row_spec(D_MODEL), _const_spec(w_bf.shape), row_spec(LANES)],
        out_specs=[qa_spec, row_spec(A_WIDTH), va_spec, qb_spec, row_spec(B_WIDTH), vb_spec]
                  + [last_spec, last_spec, row_spec(B_WIDTH),
                     pl.BlockSpec((tm * H_B, 2 * HD_B), lambda i: (i, 0))],
        out_shape=[qa_shape, bf, va_shape, qb_shape, bf, vb_shape]
                  + [jax.ShapeDtypeStruct((tm, A_WIDTH), F32)] * 2
                  + [jax.ShapeDtypeStruct((rows, B_WIDTH), F32),
                     jax.ShapeDtypeStruct((rows * H_B, 2 * HD_B), F32)],
        compiler_params=_arbitrary(1),
        name="qkv_proj",
    )(x, w_bf, rope)


def _bias_row(rel_bias):
    table = rel_bias.astype(F32)
    far = table[:, 2 * REL_CLIP:]
    near = table[:, :1]
    n_far = 2 * A_TQ - REL_CLIP
    n_near = 3 * A_TQ - (n_far + 2 * REL_CLIP + 1)
    h = table.shape[0]
    return jnp.concatenate([jnp.broadcast_to(far, (h, n_far)), table[:, ::-1],
                            jnp.broadcast_to(near, (h, n_near)),
                            jnp.broadcast_to(far, (h, BIAS_SPAN - 3 * A_TQ))], axis=1)


def _build_band_bias(row_ref, bias_ref):
    _, rows, cols = bias_ref.shape
    q = lax.broadcasted_iota(jnp.int32, (rows, cols), 0)
    k = lax.broadcasted_iota(jnp.int32, (rows, cols), 1)
    qc = q // CHUNK
    kc = k // CHUNK - (2 * A_TQ) // CHUNK
    band = (kc >= qc - BAND_PREV) & (kc <= qc)
    for h in range(H_A):
        r = jnp.broadcast_to(row_ref[h:h + 1, :], (rows, BIAS_SPAN))
        toeplitz = pltpu.roll(r, 0, axis=1, stride=1, stride_axis=0)
        bias_ref[h] = jnp.where(band, toeplitz[:, :cols] * LOG2_E, NEG)


def _pair_queries(q2):
    lane = lax.broadcasted_iota(jnp.int32, q2.shape, 1)
    lo = lane < (LANES // 2)
    zero = jnp.zeros_like(q2)
    return jnp.concatenate([jnp.where(lo, q2, zero), jnp.where(lo, zero, q2)], axis=0), lo


def _build_band_bias_t(row_ref, bias_ref):
    _, n_k, n_q = bias_ref.shape
    k = lax.broadcasted_iota(jnp.int32, (n_k, n_q), 0)
    q = lax.broadcasted_iota(jnp.int32, (n_k, n_q), 1)
    qc = q // CHUNK
    kc = k // CHUNK - (2 * A_TQ) // CHUNK
    band = (kc >= qc - BAND_PREV) & (kc <= qc)
    for h in range(H_A):
        r = jnp.broadcast_to(row_ref[h:h + 1, :], (n_k, BIAS_SPAN))
        toeplitz = pltpu.roll(r, 0, axis=1, stride=1, stride_axis=0)
        bias_ref[h] = jnp.where(band, toeplitz[:, :n_q] * LOG2_E, NEG)


def _attn_a_kernel(qt_ref, ka_ref, kb_ref, kc_ref, kd_ref, va_ref, vb_ref, vc_ref, vd_ref, row_ref,
                   o_even_ref, o_odd_ref, bias_ref, p_ref, *, n_steps):
    g = pl.program_id(0)
    k_refs = (ka_ref, kb_ref, kc_ref, kd_ref)
    v_refs = (va_ref, vb_ref, vc_ref, vd_ref)

    @pl.when(g == 0)
    def _():
        _build_band_bias_t(row_ref, bias_ref)
        p_ref[1] = jnp.zeros(p_ref.shape[1:], BF16)

    def body(check_keys):
        kidx = lax.broadcasted_iota(jnp.int32, (3 * A_TQ, A_TQ), 0)
        dim = lax.broadcasted_iota(jnp.int32, (LANES, A_TQ), 0)
        zero = jnp.zeros((LANES, A_TQ), BF16)

        def scores(t, h):
            pair = h // 2
            sl = slice(pair * LANES, (pair + 1) * LANES)
            qz = jnp.where(dim < HD_A if h % 2 == 0 else dim >= HD_A,
                           qt_ref[pair, :, t * A_TQ:(t + 1) * A_TQ], zero)
            kc = jnp.concatenate([k_refs[t][:, sl], k_refs[t + 1][:, sl], k_refs[t + 2][:, sl]], axis=0)
            s = _dot(kc, qz) + bias_ref[h]
            if check_keys:
                s = jnp.where(kidx >= (2 - t) * A_TQ, s, NEG)
            p_ref[t, h] = jnp.exp2(s - jnp.max(s, axis=0, keepdims=True)).astype(BF16)

        def values(t, o_ref, keep):
            o_t = [None] * H_A

            def head(h):
                e = p_ref[1 - t, h]
                r = (_dot(v_refs[t][h, 0], e[0:A_TQ]) + _dot(v_refs[t + 1][h, 0], e[A_TQ:2 * A_TQ])
                     + _dot(v_refs[t + 2][h, 0], e[2 * A_TQ:3 * A_TQ]))
                o_t[h] = r[0:HD_A] / r[HD_A:HD_A + 1]
                if h % 2 == 1:
                    sl = slice((h // 2) * LANES, (h // 2 + 1) * LANES)
                    o = jnp.concatenate([o_t[h - 1], o_t[h]], axis=0).T.astype(BF16)
                    o_ref[:, sl] = o if keep is None else jnp.where(keep, o, o_ref[:, sl])
            return head

        odd_head = values(0, o_odd_ref, None)
        even_head = values(1, o_even_ref, g < n_steps)
        for h in range(H_A):
            scores(0, h)
            odd_head(h)
        for h in range(H_A):
            scores(1, h)
            even_head(h)

    pl.when(g == 0)(lambda: body(True))
    pl.when(g > 0)(lambda: body(False))


def _attn_a_prompt(qat, ka, vat, bias_row_rev):
    rows = ka.shape[0]
    n_tiles = rows // A_TQ
    n_steps = n_tiles // 2
    assert n_tiles % 2 == 0 and BAND_PREV * CHUNK == 2 * A_TQ
    tile = lambda g, d: jnp.clip(2 * g + d, 0, n_tiles - 1)
    kspec = lambda d: pl.BlockSpec((A_TQ, A_WIDTH), lambda g: (tile(g, d), 0))
    vspec = lambda d: pl.BlockSpec((H_A, 1, VA_ROWS, A_TQ), lambda g: (0, tile(g, d), 0, 0))
    half = jax.ShapeDtypeStruct((rows // 2, A_WIDTH), BF16)
    return pl.pallas_call(
        functools.partial(_attn_a_kernel, n_steps=n_steps),
        grid=(n_steps + 1,),
        in_specs=[pl.BlockSpec((H_A // 2, LANES, 2 * A_TQ), lambda g: (0, 0, jnp.minimum(g, n_steps - 1)))]
                 + [kspec(d) for d in (-2, -1, 0, 1)] + [vspec(d) for d in (-3, -2, -1, 0)]
                 + [_const_spec(bias_row_rev.shape)],
        out_specs=[pl.BlockSpec((A_TQ, A_WIDTH), lambda g: (jnp.minimum(g, n_steps - 1), 0)),
                   pl.BlockSpec((A_TQ, A_WIDTH), lambda g: (jnp.clip(g - 1, 0, n_steps - 1), 0))],
        out_shape=[half, half],
        scratch_shapes=[pltpu.VMEM((H_A, 3 * A_TQ, A_TQ), F32), pltpu.VMEM((2, H_A, 3 * A_TQ, A_TQ), BF16)],
        compiler_params=_arbitrary(1),
        name="attn_a_prompt",
    )(qat, *([ka] * 4), *([vat] * 4), bias_row_rev)


def _attn_a_sample_kernel(q_ref, kn_ref, vn_ref, kc_ref, vc_ref, row_ref, o_ref, bias_ref):
    t = q_ref.shape[0]
    past = kc_ref.shape[1]

    @pl.when(pl.program_id(0) == 0)
    def _():
        _build_band_bias(row_ref, bias_ref)

    kc = kc_ref[0].astype(BF16)
    vc = vc_ref[0].astype(BF16)
    for p in range(H_A // 2):
        sl = slice(p * LANES, (p + 1) * LANES)
        qz, lo = _pair_queries(q_ref[:, sl])
        b_past = jnp.concatenate([bias_ref[2 * p, 0:t, 0:past], bias_ref[2 * p + 1, 0:t, 0:past]], axis=0)
        b_new = jnp.concatenate([bias_ref[2 * p, 0:t, past:past + t],
                                 bias_ref[2 * p + 1, 0:t, past:past + t]], axis=0)
        s_past = _dot_nt(qz, kc[:, sl]) + b_past
        s_new = _dot_nt(qz, kn_ref[:, sl]) + b_new
        m = jnp.maximum(jnp.max(s_past, axis=-1, keepdims=True), jnp.max(s_new, axis=-1, keepdims=True))
        e_past = jnp.exp2(s_past - m)
        e_new = jnp.exp2(s_new - m)
        l = jnp.sum(e_past, axis=-1, keepdims=True) + jnp.sum(e_new, axis=-1, keepdims=True)
        o = (_dot(e_past.astype(BF16), vc[:, sl]) + _dot(e_new.astype(BF16), vn_ref[:, sl])) / l
        o_ref[:, sl] = jnp.where(lo, o[:t], o[t:]).astype(BF16)


def _attn_a_sample(qa, ka, va, cache_k, cache_v, bias_row, t):
    rows = qa.shape[0]
    streams, past, width = cache_k.shape
    assert past == 2 * A_TQ and t <= CHUNK and t % SUBLANES == 0
    row = pl.BlockSpec((t, A_WIDTH), lambda s: (s, 0))
    cache = pl.BlockSpec((1, past, width), lambda s: (s, 0, 0))
    return pl.pallas_call(
        _attn_a_sample_kernel,
        grid=(streams,),
        in_specs=[row, row, row, cache, cache, _const_spec(bias_row.shape)],
        out_specs=row,
        out_shape=jax.ShapeDtypeStruct((rows, A_WIDTH), BF16),
        scratch_shapes=[pltpu.VMEM((H_A, t, past + LANES), F32)],
        compiler_params=_arbitrary(1),
        name="attn_a_sample",
    )(qa, ka, va, cache_k, cache_v, bias_row)


def _lambda(lamp_ref):
    lp = lamp_ref[...]
    a = jnp.sum(lp[0:1] * lp[1:2], axis=-1, keepdims=True)
    b = jnp.sum(lp[2:3] * lp[3:4], axis=-1, keepdims=True)
    return jnp.exp(a) - jnp.exp(b) + LAM_INIT


def _diff_finish(o0, o1, lam, g):
    o = o0 - lam * o1
    o = o * lax.rsqrt(jnp.mean(o * o, axis=-1, keepdims=True) + LN_EPS) * g * (1.0 - LAM_INIT)
    return o.astype(BF16)


def _attn_b_kernel(*refs, cast_steps, total_steps):
    n_cast = len(cast_steps)
    qt_ref, k_ref, vt_ref, lamp_ref, gcol_ref = refs[:5]
    w_refs = refs[5:5 + n_cast]
    o_ref = refs[5 + n_cast]
    wbf_refs = refs[6 + n_cast:6 + 2 * n_cast]
    acc_ref, p_ref, alpha_ref = refs[6 + 2 * n_cast:]

    qi = pl.program_id(1)
    n_parts = B_TQ // B_QW
    dim = lax.broadcasted_iota(jnp.int32, (LANES, B_QW), 0)
    zero = jnp.zeros((LANES, B_QW), BF16)
    chains = [(part, (h, mp)) for part in range(n_parts) for h in range(B_HEADS) for mp in range(2)]
    qz = []
    for part, (h, mp) in chains:
        qt = qt_ref[h, :, part * B_QW:(part + 1) * B_QW]
        qz.append(jnp.where(dim < HD_B if mp == 0 else dim >= HD_B, qt, zero))
    acc_ref[...] = jnp.zeros(acc_ref.shape, F32)
    p_ref[1] = jnp.zeros(p_ref.shape[1:], BF16)
    alpha_ref[1] = jnp.ones(alpha_ref.shape[1:], F32)

    def pending_values(c, j_prev, buf):
        h = chains[c][1][0]
        acc_ref[c] = alpha_ref[buf, c] * acc_ref[c] + _dot(vt_ref[h, j_prev], p_ref[buf, c])

    def step(j, ms, buf, parts, diag_parts, pending_parts):
        start = pl.multiple_of(j * B_TK, B_TK)
        j_prev = jnp.maximum(j - 1, 0)
        kpos = start + lax.broadcasted_iota(jnp.int32, (B_TK, B_QW), 0)
        lane = lax.broadcasted_iota(jnp.int32, (B_TK, B_QW), 1)
        out = list(ms)
        for c, (part, (h, _)) in enumerate(chains):
            if part in parts:
                s = _dot(k_ref[pl.ds(start, B_TK), h * LANES:(h + 1) * LANES], qz[c])
                if part in diag_parts:
                    qpos = qi * B_TQ + part * B_QW + lane
                    s = jnp.where(kpos < (qpos // CHUNK + 1) * CHUNK, s, NEG)
                out[c] = jnp.maximum(ms[c], jnp.max(s, axis=0, keepdims=True))
                alpha_ref[buf, c] = jnp.exp2(ms[c] - out[c])
                p_ref[buf, c] = jnp.exp2(s - out[c]).astype(BF16)
            if part in pending_parts:
                pending_values(c, j_prev, 1 - buf)
        return tuple(out)

    every = tuple(range(n_parts))
    low, high = every[:n_parts // 2], every[n_parts // 2:]

    def pair(i, ms):
        ms = step(2 * i, ms, 0, every, (), every)
        return step(2 * i + 1, ms, 1, every, (), every)

    m_init = tuple(jnp.full((1, B_QW), NEG, F32) for _ in chains)
    ms = lax.fori_loop(0, qi, pair, m_init)
    ms = step(2 * qi, ms, 0, every, low, every)
    step(2 * qi + 1, ms, 1, high, high, every)
    for c, (part, _) in enumerate(chains):
        if part in high:
            pending_values(c, 2 * qi + 1, 1)

    lam = _lambda(lamp_ref)
    for c in range(0, len(chains), 2):
        part, (h, _) = chains[c]
        a0 = acc_ref[c]
        a1 = acc_ref[c + 1]
        o = a0[0:LANES] / a0[LANES:LANES + 1] - lam * (a1[0:LANES] / a1[LANES:LANES + 1])
        o = o * lax.rsqrt(jnp.mean(o * o, axis=0, keepdims=True) + LN_EPS) * gcol_ref[...] * (1.0 - LAM_INIT)
        o_ref[part * B_QW:(part + 1) * B_QW, h * LANES:(h + 1) * LANES] = o.T.astype(BF16)

    step_id = pl.program_id(0) * pl.num_programs(1) + pl.program_id(1)
    for w_ref, wbf_ref, n_steps in zip(w_refs, wbf_refs, cast_steps):
        def cast(w_ref=w_ref, wbf_ref=wbf_ref):
            if len(w_ref.shape) == 2:
                wbf_ref[...] = w_ref[...].astype(BF16)
            else:
                width = w_ref.shape[3]
                for h in range(w_ref.shape[2]):
                    wbf_ref[0, :, h * width:(h + 1) * width] = w_ref[0, :, h, :].astype(BF16)
        if n_steps == total_steps:
            cast()
        else:
            pl.when(step_id < n_steps)(cast)


def _attn_b_prompt(qbt, kb, vbt, lamp, gcol, weights):
    rows = kb.shape[0]
    assert rows % B_TQ == 0 and B_TQ == 2 * B_TK and B_TK == 2 * B_QW and B_QW % CHUNK == 0
    assert H_B % B_HEADS == 0
    n_chains = (B_TQ // B_QW) * B_HEADS * 2
    n_q = rows // B_TQ
    total_steps = (H_B // B_HEADS) * n_q
    single = pl.Buffered(1)

    def cast_plan(w):
        step = lambda h, i, n: jnp.minimum(h * n_q + i, n - 1)
        if w.ndim == 2:
            n_steps = total_steps
            while w.shape[0] % (n_steps * BF16_SUBLANES):
                n_steps //= 2
            spec = pl.BlockSpec((w.shape[0] // n_steps, w.shape[1]), lambda h, i: (step(h, i, n_steps), 0))
            return n_steps, spec, spec, w.shape
        sets, n_rows, heads, width = w.shape
        per_set = total_steps // sets
        rows = n_rows // per_set
        assert per_set * sets == total_steps and rows * per_set == n_rows and rows % BF16_SUBLANES == 0
        in_spec = pl.BlockSpec((1, rows, heads, width),
                               lambda h, i: (step(h, i, total_steps) // per_set, step(h, i, total_steps) % per_set, 0, 0))
        out_spec = pl.BlockSpec((1, rows, heads * width),
                                lambda h, i: (step(h, i, total_steps) // per_set, step(h, i, total_steps) % per_set, 0))
        return total_steps, in_spec, out_spec, (sets, n_rows, heads * width)

    plans = [cast_plan(w) for w in weights]
    outs = pl.pallas_call(
        functools.partial(_attn_b_kernel, cast_steps=tuple(p[0] for p in plans), total_steps=total_steps),
        grid=(H_B // B_HEADS, n_q),
        in_specs=[pl.BlockSpec((B_HEADS, LANES, B_TQ), lambda h, i: (h, 0, i)),
                  pl.BlockSpec((rows, B_HEADS * LANES), lambda h, i: (0, h), pipeline_mode=single),
                  pl.BlockSpec((B_HEADS,) + vbt.shape[1:], lambda h, i: (h, 0, 0, 0), pipeline_mode=single),
                  _const_spec(lamp.shape), _const_spec(gcol.shape)] + [p[1] for p in plans],
        out_specs=[pl.BlockSpec((B_TQ, B_HEADS * LANES), lambda h, i: (i, h))] + [p[2] for p in plans],
        out_shape=[jax.ShapeDtypeStruct((rows, B_WIDTH), BF16)]
                  + [jax.ShapeDtypeStruct(p[3], BF16) for p in plans],
        scratch_shapes=[pltpu.VMEM((n_chains, V_ROWS, B_QW), F32),
                        pltpu.VMEM((2, n_chains, B_TK, B_QW), BF16), pltpu.VMEM((2, n_chains, 1, B_QW), F32)],
        compiler_params=_arbitrary(2),
        name="attn_b_prompt",
    )(qbt, kb, vbt, lamp, gcol, *weights)
    return outs[0], outs[1:]


def _attn_b_sample_kernel(q_ref, kn_ref, vn_ref, kc_ref, vc_ref, lamp_ref, g_ref, o_ref):
    t = q_ref.shape[0]
    lam = _lambda(lamp_ref)
    g = g_ref[...]
    for h in range(H_B):
        sl = slice(h * LANES, (h + 1) * LANES)
        qz, _ = _pair_queries(q_ref[:, sl])
        s_past = _dot_nt(qz, kc_ref[0, :, sl].astype(BF16))
        s_new = _dot_nt(qz, kn_ref[:, sl])
        m = jnp.maximum(jnp.max(s_past, axis=-1, keepdims=True), jnp.max(s_new, axis=-1, keepdims=True))
        e_past = jnp.exp(s_past - m)
        e_new = jnp.exp(s_new - m)
        l = jnp.sum(e_past, axis=-1, keepdims=True) + jnp.sum(e_new, axis=-1, keepdims=True)
        o = (_dot(e_past.astype(BF16), vc_ref[0, :, h, :].astype(BF16))
             + _dot(e_new.astype(BF16), vn_ref[:, sl])) / l
        o_ref[:, sl] = _diff_finish(o[:t], o[t:], lam, g)


def _attn_b_sample(qb, kb, vb, cache_k, cache_v, lamp, g, t):
    rows = qb.shape[0]
    streams, past, width = cache_k.shape
    assert past % CHUNK == 0 and t <= CHUNK
    row = pl.BlockSpec((t, B_WIDTH), lambda s: (s, 0))
    cache = pl.BlockSpec((1, past, width), lambda s: (s, 0, 0))
    cache_v_spec = pl.BlockSpec((1,) + cache_v.shape[1:], lambda s: (s, 0, 0, 0))
    return pl.pallas_call(
        _attn_b_sample_kernel,
        grid=(streams,),
        in_specs=[row, row, row, cache, cache_v_spec, _const_spec(lamp.shape), _const_spec(g.shape)],
        out_specs=row,
        out_shape=jax.ShapeDtypeStruct((rows, B_WIDTH), BF16),
        compiler_params=_arbitrary(1),
        name="attn_b_sample",
    )(qb, kb, vb, cache_k, cache_v, lamp, g)


def _mem_kv_kernel(x_ref, wk_ref, wv_ref, kf_ref, vf_ref):
    xb = x_ref[...].astype(BF16)
    kf_ref[...] = _dot(xb, wk_ref[...])
    vf_ref[...] = _dot(xb, wv_ref[...])


def _mem_kv(mem, wk_bf, wv_bf):
    f = jax.ShapeDtypeStruct(mem.shape, F32)
    return pl.pallas_call(
        _mem_kv_kernel,
        out_shape=[f, f],
        name="mem_kv",
    )(mem, wk_bf, wv_bf)


def _layer_norm(x, g, b):
    mu = jnp.mean(x, axis=-1, keepdims=True)
    xc = x - mu
    var = jnp.mean(xc * xc, axis=-1, keepdims=True)
    return xc * lax.rsqrt(var + LN_EPS) * g + b


def _tail_kernel(*refs, seg, sub):
    x_ref = refs[0]
    tm = x_ref.shape[0]
    n_seg = tm // seg
    n_sub = tm // sub
    assert n_seg == 1 or n_sub == 1
    subs = [slice(i * sub, (i + 1) * sub) for i in range(n_sub)]
    oa_refs = refs[1:1 + n_sub]
    (ob_ref, mk_ref, mv_ref, cin_ref, wo_ref, ln1g_ref, ln1b_ref, wmq_ref, wmo_ref, ln2g_ref, ln2b_ref,
     wup_ref, cw_ref, cb_ref, wdn_ref, ln3g_ref, ln3b_ref, y_ref, carry_ref) = refs[1 + n_sub:]

    @pl.when(pl.program_id(0) == 0)
    def _():
        carry_ref[...] = cin_ref[...]

    def mem_attend(q, mk, mv):
        heads = []
        for h in range(H_M):
            sl = slice(h * HD_M, (h + 1) * HD_M)
            s = _dot_nt(q[:, sl], mk[:, sl])
            e = jnp.exp(s - jnp.max(s, axis=-1, keepdims=True))
            l = jnp.sum(e, axis=-1, keepdims=True)
            heads.append((_dot(e.astype(BF16), mv[:, sl]) / l).astype(BF16))
        return jnp.concatenate(heads, axis=1)

    def mem_attend_rows(q):
        if mk_ref.shape[0] == 1:
            return mem_attend(q, mk_ref[0].astype(BF16), mv_ref[0].astype(BF16))
        return jnp.concatenate([mem_attend(q[sg * seg:(sg + 1) * seg],
                                           mk_ref[sg].astype(BF16), mv_ref[sg].astype(BF16))
                                for sg in range(n_seg)], axis=0)

    mix = [_dot(oa_ref[...], wo_ref[0:A_WIDTH, :]) + _dot(ob_ref[sl, :], wo_ref[A_WIDTH:A_WIDTH + B_WIDTH, :])
           for oa_ref, sl in zip(oa_refs, subs)]
    x1 = [_layer_norm(DEEPNORM_ALPHA * x_ref[sl, :] + m, ln1g_ref[...], ln1b_ref[...]) for sl, m in zip(subs, mix)]
    qm = [(_dot(v.astype(BF16), wmq_ref[...]) * (HD_M ** -0.5)).astype(BF16) for v in x1]
    om = [mem_attend_rows(q) for q in qm]
    mo = [_dot(v, wmo_ref[...]) for v in om]
    x2 = [_layer_norm(DEEPNORM_ALPHA * a + b, ln2g_ref[...], ln2b_ref[...]) for a, b in zip(x1, mo)]
    x2b = [v.astype(BF16) for v in x2]

    def conv(us, col, width):
        cols = slice(col, col + width)
        w0, w1, w2, b = cw_ref[0:1, cols], cw_ref[1:2, cols], cw_ref[2:3, cols], cb_ref[0:1, cols]

        def taps(u, p2, p1):
            row = lax.broadcasted_iota(jnp.int32, u.shape, 0)
            u1 = jnp.where(row == 0, p1, pltpu.roll(u, 1, axis=0))
            u2 = jnp.where(row == 0, p2, jnp.where(row == 1, p1, pltpu.roll(u, 2, axis=0)))
            return u2 * w0 + u1 * w1 + u * w2 + b

        def state(sg):
            base = SUBLANES * sg
            return carry_ref[base + 6:base + 7, cols], carry_ref[base + 7:base + 8, cols]

        if n_seg == 1:
            p2, p1 = state(0)
            outs = []
            for u in us:
                outs.append(taps(u, p2, p1))
                p2, p1 = u[sub - 2:sub - 1], u[sub - 1:sub]
            carry_ref[0:SUBLANES, cols] = us[-1][sub - SUBLANES:sub]
            return outs
        (u,) = us
        outs = []
        for sg in range(n_seg):
            useg = u[sg * seg:(sg + 1) * seg]
            outs.append(taps(useg, *state(sg)))
            carry_ref[SUBLANES * sg:SUBLANES * (sg + 1), cols] = useg[seg - SUBLANES:seg]
        return [jnp.concatenate(outs, axis=0)]

    f = [jnp.zeros((sub, D_MODEL), F32) for _ in subs]
    c0 = 0
    for width in FF_BLOCKS:
        gate = conv([_dot(v, wup_ref[:, c0:c0 + width]) for v in x2b], c0, width)
        val = conv([_dot(v, wup_ref[:, D_FF + c0:D_FF + c0 + width]) for v in x2b], D_FF + c0, width)
        hid = [(g * (1.0 / (1.0 + jnp.exp(-g))) * v).astype(BF16) for g, v in zip(gate, val)]
        f = [a + _dot(h, wdn_ref[c0:c0 + width, :]) for a, h in zip(f, hid)]
        c0 += width
    for sl, a, b in zip(subs, x2, f):
        y_ref[sl, :] = _layer_norm(DEEPNORM_ALPHA * a + b, ln3g_ref[...], ln3b_ref[...])


def _tail(x, oas, ob, mk, mv, conv_in, params, tm, seg, sub):
    rows = x.shape[0]
    n_seg = tm // seg
    assert tm % sub == 0 and sub % SUBLANES == 0 and len(oas) == tm // sub
    assert conv_in.shape == (SUBLANES * n_seg, 2 * D_FF)
    assert n_seg == 1 or rows == tm
    assert mk.shape[0] in (1, n_seg)
    row_spec = lambda w: pl.BlockSpec((tm, w), lambda i: (i, 0))
    carry_spec = pl.BlockSpec(conv_in.shape, lambda i: (0, 0))
    return pl.pallas_call(
        functools.partial(_tail_kernel, seg=seg, sub=sub),
        grid=(rows // tm,),
        in_specs=[row_spec(D_MODEL)] + [pl.BlockSpec((sub, A_WIDTH), lambda i: (i, 0)) for _ in oas]
                 + [row_spec(B_WIDTH), _const_spec(mk.shape), _const_spec(mv.shape),
                  _const_spec(conv_in.shape)] + [_const_spec(p.shape) for p in params],
        out_specs=[row_spec(D_MODEL), carry_spec],
        out_shape=[jax.ShapeDtypeStruct((rows, D_MODEL), F32),
                   jax.ShapeDtypeStruct(conv_in.shape, F32)],
        compiler_params=_arbitrary(1),
        name="tail",
    )(x, *oas, ob, mk, mv, conv_in, *params)


def kernel(x_prompt, x_sample, mem_prompt, cache_a_k, cache_a_v, cache_b_k, cache_b_v, cache_mem_k, cache_mem_v,
           state_conv, w_qkv, rel_bias, lambda_q1, lambda_k1, lambda_q2, lambda_k2, subln_g, w_o, ln1_g, ln1_b,
           w_mq, w_mk, w_mv, w_mo, ln2_g, ln2_b, w_up, conv_w, conv_b, w_down, ln3_g, ln3_b):
    assert w_qkv.shape[0] == DEPTH == 1
    batch, seq, _ = x_prompt.shape
    streams, t_new, _ = x_sample.shape
    past_b = cache_b_k.shape[2]
    assert batch == 1 and seq % QKV_TM == 0 and QKV_TM == BAND_PREV * CHUNK

    row = lambda v: v[0].astype(F32)[None, :]
    w_qkv_bf = w_qkv[0].astype(BF16)
    lamp = jnp.stack([lambda_q1[0], lambda_k1[0], lambda_q2[0], lambda_k2[0]]).astype(F32)
    g = row(subln_g)
    gcol = jnp.broadcast_to(subln_g[0].astype(F32)[:, None], (2 * HD_B, B_QW))
    bias = _bias_row(rel_bias[0])

    xp = x_prompt.reshape(seq, D_MODEL)
    qat, ka, vat, qbt, kb, vbt, ka_f, va_f, kb_f, vb_f = _qkv(xp, w_qkv_bf, jnp.arange(seq), QKV_TM, True)
    bias_rev = jnp.roll(bias[:, ::-1], 1, axis=1)
    assert TAIL_TM == 2 * A_TQ and TAIL_SUB == A_TQ
    oa_even, oa_odd = _attn_a_prompt(qat, ka, vat, bias_rev)
    later = [w[0].astype(F32) for w in (w_o, w_mq, w_mo, w_up, w_down, w_mk, w_mv, cache_mem_k, cache_mem_v)]
    ob, (w_o_bf, w_mq_bf, w_mo_bf, w_up_bf, w_down_bf, w_mk_bf, w_mv_bf, mem_k_s, mem_v_s) = _attn_b_prompt(
        qbt, kb, vbt, lamp, gcol, later)
    tail_params = (w_o_bf, row(ln1_g), row(ln1_b), w_mq_bf, w_mo_bf, row(ln2_g), row(ln2_b),
                   w_up_bf, conv_w[0].astype(F32), row(conv_b), w_down_bf, row(ln3_g), row(ln3_b))
    mk_p, mv_p = _mem_kv(mem_prompt.reshape(N_MEM, D_MODEL), w_mk_bf, w_mv_bf)
    conv0 = jnp.zeros((SUBLANES, 2 * D_FF), F32)
    y_p, conv_p = _tail(xp, [oa_even, oa_odd], ob, mk_p[None], mv_p[None], conv0, tail_params,
                        TAIL_TM, TAIL_TM, TAIL_SUB)

    rows_s = streams * t_new
    xs = x_sample.reshape(rows_s, D_MODEL)
    pos_s = jnp.tile(past_b + jnp.arange(t_new), streams)
    qa_s, ka_s, va_s, qb_s, kb_s, vb_s, ka_sf, va_sf, kb_sf, vb_sf = _qkv(xs, w_qkv_bf, pos_s, rows_s, False)
    oa_s = _attn_a_sample(qa_s, ka_s, va_s,
                          cache_a_k[0].reshape(streams, -1, A_WIDTH), cache_a_v[0].reshape(streams, -1, A_WIDTH),
                          bias, t_new)
    ob_s = _attn_b_sample(qb_s, kb_s, vb_s,
                          cache_b_k[0].reshape(streams, past_b, B_WIDTH),
                          cache_b_v[0], lamp, g, t_new)
    conv_in_s = jnp.pad(state_conv[0].astype(F32), ((0, 0), (SUBLANES - (CONV_W - 1), 0), (0, 0)))
    y_s, conv_s = _tail(xs, [oa_s], ob_s, mem_k_s, mem_v_s,
                        conv_in_s.reshape(streams * SUBLANES, 2 * D_FF), tail_params, rows_s, t_new, rows_s)

    keep = CONV_W - 1
    return (
        y_p.reshape(batch, seq, D_MODEL),
        y_s.reshape(streams, t_new, D_MODEL),
        ka_f.reshape(1, batch, QKV_TM, H_A, HD_A),
        va_f.reshape(1, batch, QKV_TM, H_A, HD_A),
        kb_f.reshape(1, batch, seq, H_B, 2, HD_B),
        vb_f.reshape(1, batch, seq, H_B, 2 * HD_B),
        mk_p.reshape(1, batch, N_MEM, H_M, HD_M),
        mv_p.reshape(1, batch, N_MEM, H_M, HD_M),
        conv_p[SUBLANES - keep:].reshape(1, batch, keep, 2 * D_FF),
        ka_sf.reshape(1, streams, t_new, H_A, HD_A),
        va_sf.reshape(1, streams, t_new, H_A, HD_A),
        kb_sf.reshape(1, streams, t_new, H_B, 2, HD_B),
        vb_sf.reshape(1, streams, t_new, H_B, 2 * HD_B),
        conv_s.reshape(streams, SUBLANES, 2 * D_FF)[:, SUBLANES - keep:].reshape(1, streams, keep, 2 * D_FF),
    )
```
